```python
import math
import jax, jax.numpy as jnp
from jax import lax
import numpy as np

D_MODEL = 1024
BATCH = 2
SEQ = 8192
DEPTH = 2
DEC_BATCH = 32
DEC_SEQ = 4
PAST_LEN = 16384
PAGE_SIZE = 128

N_A_LAYERS = DEPTH // 2
N_B_LAYERS = DEPTH - N_A_LAYERS

POOL_WINDOWS = (2, 4, 8, 16)
N_POOL_GROUPS = len(POOL_WINDOWS)
POOL_CH = D_MODEL // N_POOL_GROUPS
POOL_BUF = max(POOL_WINDOWS) - 1

HEAD_DIM = 64
ATTN_WINDOWS = (128, 512, 2048)
ATTN_DILATIONS = (1, 4, 16)
N_ATTN_GROUPS = len(ATTN_WINDOWS)
KV_PER_GROUP = 2
Q_PER_KV = 3
HEADS_PER_GROUP = KV_PER_GROUP * Q_PER_KV
N_Q_HEADS = N_ATTN_GROUPS * HEADS_PER_GROUP
N_KV_HEADS = N_ATTN_GROUPS * KV_PER_GROUP
Q_WIDTH = N_Q_HEADS * HEAD_DIM
KV_WIDTH = N_KV_HEADS * HEAD_DIM
KV_WINDOW = max(ATTN_WINDOWS)
Q_BLOCK = 128
ROPE_THETA = 10000.0

D_FF = 4 * D_MODEL
EPS = 1e-6

kernel_name = "yoco_pool_dilated_swa_decoder_step"


def _rmsnorm(x, g):
    xf = x.astype(jnp.float32)
    y = xf * lax.rsqrt(jnp.mean(xf * xf, axis=-1, keepdims=True) + EPS) * g.astype(jnp.float32)
    return y.astype(x.dtype)


def _rope(x, pos):
    inv = ROPE_THETA ** (-jnp.arange(0, HEAD_DIM, 2, dtype=jnp.float32) / HEAD_DIM)
    ang = pos.astype(jnp.float32)[:, None] * inv[None, :]
    cos = jnp.cos(ang)[None, :, None, :]
    sin = jnp.sin(ang)[None, :, None, :]
    xf = x.astype(jnp.float32)
    x1, x2 = xf[..., : HEAD_DIM // 2], xf[..., HEAD_DIM // 2:]
    return jnp.concatenate([x1 * cos - x2 * sin, x2 * cos + x1 * sin], axis=-1).astype(x.dtype)


def _pool_mix(u, buf, start_pos, w_pool, pool_scale):
    B, T, D = u.shape
    P = buf.shape[1]
    ue = jnp.concatenate([buf, u], axis=1).astype(jnp.float32)
    cs = jnp.concatenate([jnp.zeros((B, 1, D), jnp.float32), jnp.cumsum(ue, axis=1)], axis=1)
    cs = cs.reshape(B, P + T + 1, N_POOL_GROUPS, POOL_CH)
    hi = P + 1 + jnp.arange(T)
    pos = start_pos + jnp.arange(T)
    means = []
    for g, w in enumerate(POOL_WINDOWS):
        lo = jnp.maximum(hi - w, 0)
        cnt = jnp.minimum(w, pos + 1).astype(jnp.float32)
        means.append((cs[:, hi, g] - cs[:, lo, g]) / cnt[None, :, None])
    mean = jnp.stack(means, axis=2)
    pooled = (mean - u.reshape(B, T, N_POOL_GROUPS, POOL_CH).astype(jnp.float32)).astype(u.dtype)
    y = jnp.einsum('btgc,gce->btge', pooled, w_pool).reshape(B, T, D)
    return y * pool_scale


def _dilated_group_attn(q, k, v, key_offset, window, dilation):
    B, Tq = q.shape[0], q.shape[1]
    qb = math.gcd(Tq, Q_BLOCK)
    nb = Tq // qb
    steps = jnp.arange(window // dilation + 1) * dilation
    scale = HEAD_DIM ** -0.5

    def block(args):
        qi, start = args
        kpos = start + key_offset + jnp.arange(qb)[:, None] - steps[None, :]
        valid = kpos >= 0
        kpos = jnp.maximum(kpos, 0)
        kg = k[:, kpos]
        vg = v[:, kpos]
        s = jnp.einsum('bqgrd,bqkgd->bqgrk', qi, kg, preferred_element_type=jnp.float32) * scale
        s = jnp.where(valid[None, :, None, None, :], s, -jnp.inf)
        lse = jax.nn.logsumexp(s, axis=-1)
        p = jnp.exp(s - lse[..., None]).astype(vg.dtype)
        o = jnp.einsum('bqgrk,bqkgd->bqgrd', p, vg)
        return o, lse

    qs = jnp.swapaxes(q.reshape((B, nb, qb) + q.shape[2:]), 0, 1)
    starts = jnp.arange(nb) * qb
    o, lse = lax.map(block, (qs, starts))
    o = jnp.swapaxes(o, 0, 1).reshape(q.shape)
    lse = jnp.swapaxes(lse, 0, 1).reshape(q.shape[:-1])
    return o, lse


def _dilated_mixer(u, pos, k_all, v_all, key_offset, w_q, w_o):
    B, T, _ = u.shape
    q = _rope((u @ w_q).reshape(B, T, N_Q_HEADS, HEAD_DIM), pos)
    q = q.reshape(B, T, N_ATTN_GROUPS, KV_PER_GROUP, Q_PER_KV, HEAD_DIM)
    outs, lses = [], []
    for g in range(N_ATTN_GROUPS):
        kg = k_all[:, :, g * KV_PER_GROUP:(g + 1) * KV_PER_GROUP]
        vg = v_all[:, :, g * KV_PER_GROUP:(g + 1) * KV_PER_GROUP]
        o, l = _dilated_group_attn(q[:, :, g], kg, vg, key_offset, ATTN_WINDOWS[g], ATTN_DILATIONS[g])
        outs.append(o.reshape(B, T, HEADS_PER_GROUP, HEAD_DIM))
        lses.append(l.reshape(B, T, HEADS_PER_GROUP))
    alpha = jax.nn.softmax(jnp.stack(lses, axis=2), axis=2)
    o = jnp.stack(outs, axis=2) * alpha[..., None].astype(u.dtype)
    return o.reshape(B, T, Q_WIDTH) @ w_o


def _trunk(x, start_pos, pool_bufs, cache_k, cache_v, norm_gains, kv_norm_gain, w_pool, pool_scale,
           w_q, w_o, w_kv, w_up, w_down):
    B, T, _ = x.shape
    pos = start_pos + jnp.arange(T)
    new_pool = []
    k_new = v_new = k_all = v_all = None
    for layer in range(DEPTH):
        g = norm_gains[layer]
        u = _rmsnorm(x, g[0])
        if layer < N_A_LAYERS:
            buf = pool_bufs[layer]
            mix = _pool_mix(u, buf, start_pos, w_pool[layer], pool_scale[layer])
            new_pool.append(jnp.concatenate([buf, u], axis=1)[:, -POOL_BUF:])
        else:
            if layer == N_A_LAYERS:
                kv = _rmsnorm(x, kv_norm_gain) @ w_kv
                k_new = _rope(kv[..., :KV_WIDTH].reshape(B, T, N_KV_HEADS, HEAD_DIM), pos)
                v_new = kv[..., KV_WIDTH:].reshape(B, T, N_KV_HEADS, HEAD_DIM)
                k_all = jnp.concatenate([cache_k, k_new], axis=1)
                v_all = jnp.concatenate([cache_v, v_new], axis=1)
            b = layer - N_A_LAYERS
            mix = _dilated_mixer(u, pos, k_all, v_all, cache_k.shape[1], w_q[b], w_o[b])
        x = x + _rmsnorm(mix, g[1])
        h = _rmsnorm(x, g[2])
        ff = jnp.square(jax.nn.relu(h @ w_up[layer])) @ w_down[layer]
        x = x + _rmsnorm(ff, g[3])
    return x, jnp.stack(new_pool, axis=0), k_new, v_new


def setup_inputs(seed: int = 0) -> dict:
    key = jax.random.key(seed)
    ks = jax.random.split(key, 14)
    f32 = jnp.float32
    kv_buf = min(KV_WINDOW, PAST_LEN)
    nrm = lambda k, s: jax.random.normal(k, s, f32)
    return {
        "x_prompt": nrm(ks[0], (BATCH, SEQ, D_MODEL)),
        "x_sample": nrm(ks[1], (DEC_BATCH, DEC_SEQ, D_MODEL)),
        "cache_pool": nrm(ks[2], (N_A_LAYERS, DEC_BATCH, POOL_BUF, D_MODEL)),
        "cache_k": nrm(ks[3], (DEC_BATCH, kv_buf, N_KV_HEADS, HEAD_DIM)),
        "cache_v": nrm(ks[4], (DEC_BATCH, kv_buf, N_KV_HEADS, HEAD_DIM)),
        "norm_gains": 1.0 + 0.05 * nrm(ks[5], (DEPTH, 4, D_MODEL)),
        "kv_norm_gain": 1.0 + 0.05 * nrm(ks[6], (D_MODEL,)),
        "w_pool": nrm(ks[7], (N_A_LAYERS, N_POOL_GROUPS, POOL_CH, POOL_CH)) * POOL_CH ** -0.5,
        "pool_scale": 1.0 + 0.1 * nrm(ks[8], (N_A_LAYERS, D_MODEL)),
        "w_q": nrm(ks[9], (N_B_LAYERS, D_MODEL, Q_WIDTH)) * D_MODEL ** -0.5,
        "w_o": nrm(ks[10], (N_B_LAYERS, Q_WIDTH, D_MODEL)) * Q_WIDTH ** -0.5,
        "w_kv": nrm(ks[11], (D_MODEL, 2 * KV_WIDTH)) * D_MODEL ** -0.5,
        "w_up": nrm(ks[12], (DEPTH, D_MODEL, D_FF)) * D_MODEL ** -0.5,
        "w_down": nrm(ks[13], (DEPTH, D_FF, D_MODEL)) * D_FF ** -0.5,
    }


def reference(x_prompt, x_sample, cache_pool, cache_k, cache_v, norm_gains, kv_norm_gain, w_pool,
              pool_scale, w_q, w_o, w_kv, w_up, w_down):
    bp = x_prompt.shape[0]
    dt = x_prompt.dtype
    empty_pool = jnp.zeros((N_A_LAYERS, bp, 0, D_MODEL), dt)
    empty_kv = jnp.zeros((bp, 0, N_KV_HEADS, HEAD_DIM), dt)
    y_prompt, pool_prompt, k_p, v_p = _trunk(
        x_prompt, 0, empty_pool, empty_kv, empty_kv, norm_gains, kv_norm_gain, w_pool, pool_scale,
        w_q, w_o, w_kv, w_up, w_down)
    y_sample, pool_sample, k_s, v_s = _trunk(
        x_sample, PAST_LEN, cache_pool, cache_k, cache_v, norm_gains, kv_norm_gain, w_pool, pool_scale,
        w_q, w_o, w_kv, w_up, w_down)
    k_prompt = k_p[:, -KV_WINDOW:]
    v_prompt = v_p[:, -KV_WINDOW:]
    return (y_prompt, y_sample, pool_prompt, k_prompt, v_prompt, pool_sample, k_s, v_s)
```

```python
import functools

import jax
import jax.numpy as jnp
from jax import lax
from jax.experimental import pallas as pl
from jax.experimental.pallas import tpu as pltpu

F32 = jnp.float32
BF16 = jnp.bfloat16

EPS = 1e-6
ROPE_THETA = 10000.0
POOL_WINDOWS = (2, 4, 8, 16)
POOL_BUF = max(POOL_WINDOWS) - 1
POOL_HALO = 16
HEAD_DIM = 64
ATTN_WINDOWS = (128, 512, 2048)
ATTN_DILATIONS = (1, 4, 16)
N_GROUPS = len(ATTN_WINDOWS)
KV_PER_GROUP = 2
Q_PER_KV = 3
HEADS_PER_GROUP = KV_PER_GROUP * Q_PER_KV
GROUP_Q_WIDTH = HEADS_PER_GROUP * HEAD_DIM
GROUP_KV_WIDTH = KV_PER_GROUP * HEAD_DIM
KV_WINDOW = max(ATTN_WINDOWS)
BAND = 128
LSE_LANES = 128
PAST_LEN = 16384
NEG_BIG = -1e30

LANES = 128
VMEM_LIMIT = 56 * 1024 * 1024
FF_CHUNK = 1024

for _w, _d in zip(ATTN_WINDOWS, ATTN_DILATIONS):
    assert _w // _d == BAND and _w % _d == 0


def _rms(x, g):
    return x * lax.rsqrt(jnp.mean(x * x, axis=-1, keepdims=True) + EPS) * g


def _mlp_residual(x1, g_in, g_out, wup_ref, wdn_ref):
    h = _rms(x1, g_in).astype(BF16)
    d_ff = wup_ref.shape[1]
    acc = None
    for c in range(d_ff // FF_CHUNK):
        cs = slice(c * FF_CHUNK, (c + 1) * FF_CHUNK)
        a = jnp.dot(h, wup_ref[:, cs], preferred_element_type=F32)
        a = jnp.square(jnp.maximum(a, 0.0)).astype(BF16)
        part = jnp.dot(a, wdn_ref[cs, :], preferred_element_type=F32)
        acc = part if acc is None else acc + part
    return x1 + _rms(acc, g_out)


def _const_spec(shape):
    zeros = (0,) * len(shape)
    return pl.BlockSpec(shape, lambda *_: zeros, pipeline_mode=pl.Buffered(1))


def _layer0_prompt_kernel(x_ref, xh_ref, g_ref, wp_ref, ps_ref, wup_ref, wdn_ref,
                          y_ref, utail_ref, ext_scr):
    tm = x_ref.shape[1]
    pool_ch = wp_ref.shape[1]
    i = pl.program_id(1)
    g = g_ref[...]
    x = x_ref[0]
    u = _rms(x, g[0:1])
    uh = _rms(xh_ref[0], g[0:1]) * (i > 0).astype(F32)
    ext_scr[0:POOL_HALO, :] = uh
    ext_scr[POOL_HALO:, :] = u
    pos = i * tm + lax.broadcasted_iota(jnp.int32, (tm, 1), 0)
    parts = []
    for gi, w in enumerate(POOL_WINDOWS):
        cs = slice(gi * pool_ch, (gi + 1) * pool_ch)
        s = u[:, cs]
        for j in range(1, w):
            s = s + ext_scr[POOL_HALO - j:POOL_HALO - j + tm, cs]
        cnt = jnp.minimum(w, pos + 1).astype(F32)
        pooled = s / cnt - u[:, cs]
        parts.append(jnp.dot(pooled.astype(BF16), wp_ref[gi], preferred_element_type=F32))
    mix = jnp.concatenate(parts, axis=1) * ps_ref[...]
    x1 = x + _rms(mix, g[1:2])
    y_ref[0] = _mlp_residual(x1, g[2:3], g[3:4], wup_ref, wdn_ref)
    utail_ref[0] = u[tm - POOL_HALO:, :]


def _layer0_prompt(x, gains, w_pool, pool_scale, w_up, w_down, tm):
    b, t, d = x.shape
    assert t % tm == 0 and tm % POOL_HALO == 0 and t >= POOL_HALO
    halo_per_tile = tm // POOL_HALO
    return pl.pallas_call(
        _layer0_prompt_kernel,
        grid=(b, t // tm),
        in_specs=[
            pl.BlockSpec((1, tm, d), lambda bi, i: (bi, i, 0)),
            pl.BlockSpec((1, POOL_HALO, d),
                         lambda bi, i: (bi, jnp.maximum(i * halo_per_tile - 1, 0), 0)),
            _const_spec(gains.shape),
            _const_spec(w_pool.shape),
            _const_spec(pool_scale.shape),
            _const_spec(w_up.shape),
            _const_spec(w_down.shape),
        ],
        out_specs=[
            pl.BlockSpec((1, tm, d), lambda bi, i: (bi, i, 0)),
            pl.BlockSpec((1, POOL_HALO, d), lambda bi, i: (bi, 0, 0)),
        ],
        out_shape=[
            jax.ShapeDtypeStruct((b, t, d), F32),
            jax.ShapeDtypeStruct((b, POOL_HALO, d), F32),
        ],
        scratch_shapes=[pltpu.VMEM((tm + POOL_HALO, d), F32)],
        compiler_params=pltpu.CompilerParams(
            dimension_semantics=("arbitrary", "arbitrary"), vmem_limit_bytes=VMEM_LIMIT),
        name="layer0_prompt",
    )(x, x, gains, w_pool, pool_scale, w_up, w_down)


def _layer0_sample_kernel(start_pos, x_ref, buf_ref, g_ref, wp_ref, ps_ref, wup_ref, wdn_ref,
                          y_ref, u_ref):
    n_t, n_b, _ = x_ref.shape
    pool_ch = wp_ref.shape[1]
    g = g_ref[...]
    xs = [x_ref[t] for t in range(n_t)]
    us = [_rms(xt, g[0:1]) for xt in xs]
    ext = [buf_ref[j] for j in range(POOL_BUF)] + us
    parts = []
    for gi, w in enumerate(POOL_WINDOWS):
        cs = slice(gi * pool_ch, (gi + 1) * pool_ch)
        rows = []
        for t in range(n_t):
            s = us[t][:, cs]
            for j in range(1, w):
                s = s + ext[POOL_BUF + t - j][:, cs]
            cnt = float(min(w, start_pos + t + 1))
            rows.append(s / cnt - us[t][:, cs])
        pooled = jnp.concatenate(rows, axis=0)
        parts.append(jnp.dot(pooled.astype(BF16), wp_ref[gi], preferred_element_type=F32))
    mix = jnp.concatenate(parts, axis=1) * ps_ref[...]
    x = jnp.concatenate(xs, axis=0)
    x1 = x + _rms(mix, g[1:2])
    y = _mlp_residual(x1, g[2:3], g[3:4], wup_ref, wdn_ref)
    for t in range(n_t):
        y_ref[t] = y[t * n_b:(t + 1) * n_b]
        u_ref[t] = us[t]


def _layer0_sample(x_tm, buf_tm, start_pos, gains, w_pool, pool_scale, w_up, w_down):
    assert start_pos + 1 >= max(POOL_WINDOWS) and buf_tm.shape[0] == POOL_BUF
    return pl.pallas_call(
        functools.partial(_layer0_sample_kernel, start_pos),
        out_shape=[jax.ShapeDtypeStruct(x_tm.shape, F32), jax.ShapeDtypeStruct(x_tm.shape, F32)],
        compiler_params=pltpu.CompilerParams(vmem_limit_bytes=VMEM_LIMIT),
        name="layer0_sample",
    )(x_tm, buf_tm, gains, w_pool, pool_scale, w_up, w_down)


def _rope(x, cos, sin_signed):
    lane = lax.broadcasted_iota(jnp.int32, (x.shape[0], LANES), 1)
    first_half = (lane & (HEAD_DIM // 2)) == 0
    out = []
    for c in range(x.shape[1] // LANES):
        xc = x[:, c * LANES:(c + 1) * LANES]
        partner = jnp.where(first_half,
                            pltpu.roll(xc, LANES - HEAD_DIM // 2, 1),
                            pltpu.roll(xc, HEAD_DIM // 2, 1))
        out.append(xc * cos + partner * sin_signed)
    return out


def _qkv_kernel(x_ref, gq_ref, gkv_ref, wq_ref, wkv_ref, cos_ref, sin_ref,
                q0_ref, q1_ref, q2_ref, k0_ref, k1_ref, k2_ref, v0_ref, v1_ref, v2_ref,
                kf_ref, vf_ref):
    x = x_ref[0]
    cos = cos_ref[...]
    sin_signed = sin_ref[...]
    kv_width = wkv_ref.shape[1] // 2
    chunks_per_group = GROUP_Q_WIDTH // LANES

    u = _rms(x, gq_ref[...]).astype(BF16)
    q = jnp.dot(u, wq_ref[...], preferred_element_type=F32)
    scale = HEAD_DIM ** -0.5
    q_chunks = _rope(q, cos * scale, sin_signed * scale)
    for gi, q_ref in enumerate((q0_ref, q1_ref, q2_ref)):
        for c in range(chunks_per_group):
            q_ref[0, :, c * LANES:(c + 1) * LANES] = q_chunks[gi * chunks_per_group + c].astype(BF16)

    un = _rms(x, gkv_ref[...]).astype(BF16)
    kv = jnp.dot(un, wkv_ref[...], preferred_element_type=F32)
    k_chunks = _rope(kv[:, :kv_width], cos, sin_signed)
    for gi, (k_ref, v_ref) in enumerate(((k0_ref, v0_ref), (k1_ref, v1_ref), (k2_ref, v2_ref))):
        cs = slice(gi * GROUP_KV_WIDTH, (gi + 1) * GROUP_KV_WIDTH)
        vg = kv[:, kv_width + gi * GROUP_KV_WIDTH:kv_width + (gi + 1) * GROUP_KV_WIDTH]
        k_ref[0] = k_chunks[gi].astype(BF16)
        v_ref[0] = vg.astype(BF16)
        kf_ref[0, :, cs] = k_chunks[gi]
        vf_ref[0, :, cs] = vg


def _qkv(x, g_q, g_kv, w_q, w_kv, cos, sin_signed, tm, keep_rows):
    b, t, d = x.shape
    assert GROUP_KV_WIDTH == LANES and t % tm == 0 and keep_rows % tm == 0
    first_kept_tile = (t - keep_rows) // tm
    tile = lambda width: pl.BlockSpec((1, tm, width), lambda bi, i: (bi, i, 0))
    kept = pl.BlockSpec((1, tm, N_GROUPS * GROUP_KV_WIDTH),
                        lambda bi, i: (bi, jnp.maximum(i - first_kept_tile, 0), 0))
    bf = lambda width: jax.ShapeDtypeStruct((b, t, width), BF16)
    kept_shape = jax.ShapeDtypeStruct((b, keep_rows, N_GROUPS * GROUP_KV_WIDTH), F32)
    return pl.pallas_call(
        _qkv_kernel,
        grid=(b, t // tm),
        in_specs=[
            tile(d),
            _const_spec(g_q.shape), _const_spec(g_kv.shape),
            _const_spec(w_q.shape), _const_spec(w_kv.shape),
            pl.BlockSpec((tm, LANES), lambda bi, i: (i, 0)),
            pl.BlockSpec((tm, LANES), lambda bi, i: (i, 0)),
        ],
        out_specs=[tile(GROUP_Q_WIDTH)] * N_GROUPS + [tile(GROUP_KV_WIDTH)] * (2 * N_GROUPS)
        + [kept, kept],
        out_shape=[bf(GROUP_Q_WIDTH)] * N_GROUPS + [bf(GROUP_KV_WIDTH)] * (2 * N_GROUPS)
        + [kept_shape, kept_shape],
        compiler_params=pltpu.CompilerParams(
            dimension_semantics=("arbitrary", "arbitrary"), vmem_limit_bytes=VMEM_LIMIT),
        name="qkv_rope",
    )(x, g_q, g_kv, w_q, w_kv, cos, sin_signed)


def _rope_tables(positions):
    inv = ROPE_THETA ** (-jnp.arange(0, HEAD_DIM, 2, dtype=F32) / HEAD_DIM)
    ang = positions.astype(F32)[:, None] * inv[None, :]
    cos, sin = jnp.cos(ang), jnp.sin(ang)
    heads_per_period = LANES // HEAD_DIM
    return (jnp.tile(cos, (1, 2 * heads_per_period)),
            jnp.tile(jnp.concatenate([-sin, sin], axis=1), (1, heads_per_period)))


def _band_attention_kernel(q_ref, kp_ref, kc_ref, vp_ref, vc_ref, o_ref, lse_ref, k_scr, v_scr):
    tq = q_ref.shape[1]
    i = pl.program_id(2)
    k_scr[0:BAND, :] = kp_ref[0]
    k_scr[BAND:, :] = kc_ref[0]
    v_scr[0:BAND, :] = vp_ref[0]
    v_scr[BAND:, :] = vc_ref[0]
    lse_ref[...] = jnp.zeros(lse_ref.shape, F32)

    row = lax.broadcasted_iota(jnp.int32, (BAND, 2 * BAND), 0)
    col = lax.broadcasted_iota(jnp.int32, (BAND, 2 * BAND), 1)
    band_bias = jnp.where((col >= row) & (col <= row + BAND), 0.0, NEG_BIG)
    first_bias = band_bias + jnp.where(col < BAND, NEG_BIG, 0.0) * (i == 0).astype(F32)
    for j in range(tq // BAND):
        rows = slice(j * BAND, (j + 1) * BAND)
        bias = jnp.concatenate([first_bias if j == 0 else band_bias] * Q_PER_KV, axis=0)
        for c in range(KV_PER_GROUP):
            kc = k_scr[j * BAND:(j + 2) * BAND, c * HEAD_DIM:(c + 1) * HEAD_DIM]
            vc = v_scr[j * BAND:(j + 2) * BAND, c * HEAD_DIM:(c + 1) * HEAD_DIM]
            heads = [c * Q_PER_KV + h for h in range(Q_PER_KV)]
            q = jnp.concatenate(
                [q_ref[0, rows, h * HEAD_DIM:(h + 1) * HEAD_DIM] for h in heads], axis=0)
            s = lax.dot_general(q, kc, (((1,), (1,)), ((), ())), preferred_element_type=F32)
            s = s + bias
            m = jnp.max(s, axis=1, keepdims=True)
            p = jnp.exp(s - m)
            l = jnp.sum(p, axis=1, keepdims=True)
            o = jnp.dot(p.astype(BF16), vc, preferred_element_type=F32) / l
            lse = m + jnp.log(l)
            for hi, h in enumerate(heads):
                o_ref[0, rows, h * HEAD_DIM:(h + 1) * HEAD_DIM] = (
                    o[hi * BAND:(hi + 1) * BAND].astype(o_ref.dtype))
                lse_ref[0, rows, h:h + 1] = lse[hi * BAND:(hi + 1) * BAND]


def _band_attention(q, k, v, dilation, tq):
    b, t, _ = q.shape
    sub = t // dilation
    assert t % dilation == 0 and sub % tq == 0 and tq % BAND == 0
    qv = q.reshape(b, sub, dilation * GROUP_Q_WIDTH)
    kv = k.reshape(b, sub, dilation * GROUP_KV_WIDTH)
    vv = v.reshape(b, sub, dilation * GROUP_KV_WIDTH)
    bands_per_tile = tq // BAND
    cur = lambda width: pl.BlockSpec((1, tq, width), lambda bi, r, i: (bi, i, r))
    prev = pl.BlockSpec((1, BAND, GROUP_KV_WIDTH),
                        lambda bi, r, i: (bi, jnp.maximum(i * bands_per_tile - 1, 0), r))
    o, lse = pl.pallas_call(
        _band_attention_kernel,
        grid=(b, dilation, sub // tq),
        in_specs=[cur(GROUP_Q_WIDTH), prev, cur(GROUP_KV_WIDTH), prev, cur(GROUP_KV_WIDTH)],
        out_specs=[cur(GROUP_Q_WIDTH), cur(LSE_LANES)],
        out_shape=[jax.ShapeDtypeStruct(qv.shape, BF16),
                   jax.ShapeDtypeStruct((b, sub, dilation * LSE_LANES), F32)],
        scratch_shapes=[pltpu.VMEM((tq + BAND, GROUP_KV_WIDTH), BF16),
                        pltpu.VMEM((tq + BAND, GROUP_KV_WIDTH), BF16)],
        compiler_params=pltpu.CompilerParams(
            dimension_semantics=("arbitrary", "arbitrary", "arbitrary"),
            vmem_limit_bytes=VMEM_LIMIT),
        name=f"band_attention_d{dilation}",
    )(qv, kv, kv, vv, vv)
    return o.reshape(b, t, GROUP_Q_WIDTH), lse.reshape(b, t, LSE_LANES)


SAMPLE_Q_ROWS = 16
NEW_KEY_ROWS = 8


def _sample_attention_kernel(n_new, q_ref, kn_ref, vn_ref, kc0, kc1, kc2, vc0, vc1, vc2,
                             o_ref, lse_ref):
    n_rows = q_ref.shape[2]
    kn_all = kn_ref[0]
    vn_all = vn_ref[0]
    for gi, (kc_ref, vc_ref) in enumerate(((kc0, vc0), (kc1, vc1), (kc2, vc2))):
        dmask = ATTN_DILATIONS[gi] - 1
        n_cache = kc_ref.shape[1]
        assert n_cache == ATTN_WINDOWS[gi]
        t_c = lax.broadcasted_iota(jnp.int32, (n_rows, n_cache), 0) & (n_new - 1)
        j_c = lax.broadcasted_iota(jnp.int32, (n_rows, n_cache), 1)
        valid_c = (j_c >= t_c) & (((j_c - t_c) & dmask) == 0)
        t_n = lax.broadcasted_iota(jnp.int32, (n_rows, NEW_KEY_ROWS), 0) & (n_new - 1)
        j_n = lax.broadcasted_iota(jnp.int32, (n_rows, NEW_KEY_ROWS), 1)
        valid_n = (j_n <= t_n) & (((t_n - j_n) & dmask) == 0)
        kc_all = kc_ref[0].astype(BF16)
        vc_all = vc_ref[0].astype(BF16)
        for c in range(KV_PER_GROUP):
            kvh = gi * KV_PER_GROUP + c
            hs = slice(c * HEAD_DIM, (c + 1) * HEAD_DIM)
            ns = slice(kvh * HEAD_DIM, (kvh + 1) * HEAD_DIM)
            q = q_ref[0, kvh]
            contract_last = (((1,), (1,)), ((), ()))
            s_c = lax.dot_general(q, kc_all[:, hs], contract_last, preferred_element_type=F32)
            s_n = lax.dot_general(q, kn_all[:, ns].astype(BF16), contract_last,
                                  preferred_element_type=F32)
            s_c = jnp.where(valid_c, s_c, NEG_BIG)
            s_n = jnp.where(valid_n, s_n, NEG_BIG)
            m = jnp.maximum(jnp.max(s_c, axis=1, keepdims=True),
                            jnp.max(s_n, axis=1, keepdims=True))
            p_c = jnp.exp(s_c - m)
            p_n = jnp.exp(s_n - m)
            l = jnp.sum(p_c, axis=1, keepdims=True) + jnp.sum(p_n, axis=1, keepdims=True)
            o = (jnp.dot(p_c.astype(BF16), vc_all[:, hs], preferred_element_type=F32)
                 + jnp.dot(p_n.astype(BF16), vn_all[:, ns].astype(BF16),
                           preferred_element_type=F32))
            o_ref[0, kvh] = o / l
            lse_ref[0, kvh] = m + jnp.log(l)


def _sample_attention(q16, k_new, v_new, cache_k, cache_v, n_new):
    b = q16.shape[0]
    n_kv = q16.shape[1]
    assert cache_k.shape[1] == KV_WINDOW and n_new & (n_new - 1) == 0 and n_new <= NEW_KEY_ROWS

    def cache_spec(gi):
        rows = ATTN_WINDOWS[gi]
        last = KV_WINDOW // rows - 1
        return pl.BlockSpec((1, rows, GROUP_KV_WIDTH), lambda bi: (bi, last, gi))

    whole = lambda a: pl.BlockSpec((1,) + a.shape[1:], lambda bi: (bi,) + (0,) * (a.ndim - 1))
    o_shape = jax.ShapeDtypeStruct((b, n_kv, SAMPLE_Q_ROWS, HEAD_DIM), F32)
    lse_shape = jax.ShapeDtypeStruct((b, n_kv, SAMPLE_Q_ROWS, 1), F32)
    return pl.pallas_call(
        functools.partial(_sample_attention_kernel, n_new),
        grid=(b,),
        in_specs=[whole(q16), whole(k_new), whole(v_new)]
        + [cache_spec(gi) for gi in range(N_GROUPS)] * 2,
        out_specs=[whole(o_shape), whole(lse_shape)],
        out_shape=[o_shape, lse_shape],
        compiler_params=pltpu.CompilerParams(
            dimension_semantics=("arbitrary",), vmem_limit_bytes=VMEM_LIMIT),
        name="sample_attention",
    )(q16, k_new, v_new, cache_k, cache_k, cache_k, cache_v, cache_v, cache_v)


def _layer1_kernel(x_ref, o0_ref, o1_ref, o2_ref, l0_ref, l1_ref, l2_ref, g_ref, wo_ref,
                   wup_ref, wdn_ref, y_ref, a_scr):
    tm = x_ref.shape[1]
    g = g_ref[...]
    x = x_ref[0]
    lses = [l0_ref[0], l1_ref[0], l2_ref[0]]
    top = jnp.maximum(jnp.maximum(lses[0], lses[1]), lses[2])
    es = [jnp.exp(l - top) for l in lses]
    den = es[0] + es[1] + es[2]
    head_id = lax.broadcasted_iota(jnp.int32, (tm, GROUP_Q_WIDTH), 1) // HEAD_DIM
    for gi, o_ref in enumerate((o0_ref, o1_ref, o2_ref)):
        alpha = es[gi] / den
        wide = jnp.zeros((tm, GROUP_Q_WIDTH), F32)
        for h in range(HEADS_PER_GROUP):
            wide = jnp.where(head_id == h, alpha[:, h:h + 1], wide)
        a_scr[:, gi * GROUP_Q_WIDTH:(gi + 1) * GROUP_Q_WIDTH] = (
            o_ref[0].astype(F32) * wide).astype(BF16)
    mix = jnp.dot(a_scr[...], wo_ref[...], preferred_element_type=F32)
    x1 = x + _rms(mix, g[1:2])
    y_ref[0] = _mlp_residual(x1, g[2:3], g[3:4], wup_ref, wdn_ref)


def _layer1(x, os, lses, gains, w_o, w_up, w_down, tm):
    b, t, d = x.shape
    assert t % tm == 0
    tile = lambda width: pl.BlockSpec((1, tm, width), lambda bi, i: (bi, i, 0))
    return pl.pallas_call(
        _layer1_kernel,
        grid=(b, t // tm),
        in_specs=[tile(d)] + [tile(GROUP_Q_WIDTH)] * N_GROUPS + [tile(LSE_LANES)] * N_GROUPS
        + [_const_spec(gains.shape), _const_spec(w_o.shape),
           _const_spec(w_up.shape), _const_spec(w_down.shape)],
        out_specs=tile(d),
        out_shape=jax.ShapeDtypeStruct((b, t, d), F32),
        scratch_shapes=[pltpu.VMEM((tm, N_GROUPS * GROUP_Q_WIDTH), BF16)],
        compiler_params=pltpu.CompilerParams(
            dimension_semantics=("arbitrary", "arbitrary"), vmem_limit_bytes=VMEM_LIMIT),
        name="layer1_tail",
    )(x, *os, *lses, gains, w_o, w_up, w_down)


PROMPT_TILE = 512


def kernel(x_prompt, x_sample, cache_pool, cache_k, cache_v, norm_gains, kv_norm_gain, w_pool,
           pool_scale, w_q, w_o, w_kv, w_up, w_down):
    depth = norm_gains.shape[0]
    assert depth == 2 and cache_pool.shape[0] == 1 and w_q.shape[0] == 1
    bp, tp, d = x_prompt.shape
    bs, ts, _ = x_sample.shape
    past_len = PAST_LEN
    n_kv_heads = N_GROUPS * KV_PER_GROUP

    g0, g1 = norm_gains[0], norm_gains[1]
    g1_q = g1[0:1]
    g_kv = kv_norm_gain[None, :]
    wp = w_pool[0].astype(BF16)
    ps = pool_scale[0][None, :]
    wq, wo, wkv = w_q[0].astype(BF16), w_o[0].astype(BF16), w_kv.astype(BF16)
    wup0, wup1 = w_up[0].astype(BF16), w_up[1].astype(BF16)
    wdn0, wdn1 = w_down[0].astype(BF16), w_down[1].astype(BF16)

    keep = min(KV_WINDOW, tp)
    xp1, utail = _layer0_prompt(x_prompt, g0, wp, ps, wup0, wdn0, PROMPT_TILE)
    cos_p, sin_p = _rope_tables(jnp.arange(tp))
    qkv_p = _qkv(xp1, g1_q, g_kv, wq, wkv, cos_p, sin_p, PROMPT_TILE, keep)
    os_p, lses_p = [], []
    for gi, dil in enumerate(ATTN_DILATIONS):
        o, lse = _band_attention(qkv_p[gi], qkv_p[N_GROUPS + gi], qkv_p[2 * N_GROUPS + gi],
                                 dil, min(1024, tp // dil))
        os_p.append(o)
        lses_p.append(lse)
    y_prompt = _layer1(xp1, os_p, lses_p, g1, wo, wup1, wdn1, PROMPT_TILE)
    pool_prompt = utail[:, POOL_HALO - POOL_BUF:][None]
    k_prompt = qkv_p[-2].reshape(bp, keep, n_kv_heads, HEAD_DIM)
    v_prompt = qkv_p[-1].reshape(bp, keep, n_kv_heads, HEAD_DIM)

    n_tok = ts * bs
    xs_tm = jnp.swapaxes(x_sample, 0, 1)
    buf_tm = jnp.swapaxes(cache_pool[0], 0, 1)
    xs1_tm, us_tm = _layer0_sample(xs_tm, buf_tm, past_len, g0, wp, ps, wup0, wdn0)
    xs1 = xs1_tm.reshape(1, n_tok, d)
    cos_s, sin_s = _rope_tables(past_len + jnp.arange(n_tok) // bs)
    qkv_s = _qkv(xs1, g1_q, g_kv, wq, wkv, cos_s, sin_s, n_tok, n_tok)
    q_s = jnp.concatenate(qkv_s[:N_GROUPS], axis=-1)
    q16 = q_s.reshape(ts, bs, n_kv_heads, Q_PER_KV, HEAD_DIM).transpose(1, 2, 3, 0, 4)
    q16 = jnp.pad(q16, ((0, 0), (0, 0), (0, SAMPLE_Q_ROWS // ts - Q_PER_KV), (0, 0), (0, 0)))
    q16 = q16.reshape(bs, n_kv_heads, SAMPLE_Q_ROWS, HEAD_DIM)
    kv_width = n_kv_heads * HEAD_DIM
    k_s = jnp.swapaxes(qkv_s[-2].reshape(ts, bs, kv_width), 0, 1)
    v_s = jnp.swapaxes(qkv_s[-1].reshape(ts, bs, kv_width), 0, 1)
    pad_new = ((0, 0), (0, NEW_KEY_ROWS - ts), (0, 0))
    o16, lse16 = _sample_attention(
        q16, jnp.pad(k_s, pad_new), jnp.pad(v_s, pad_new),
        cache_k.reshape(bs, -1, kv_width), cache_v.reshape(bs, -1, kv_width), ts)
    heads_padded = SAMPLE_Q_ROWS // ts
    o_s = o16.reshape(bs, n_kv_heads, heads_padded, ts, HEAD_DIM)[:, :, :Q_PER_KV]
    o_s = o_s.transpose(3, 0, 1, 2, 4).reshape(1, n_tok, N_GROUPS, GROUP_Q_WIDTH).astype(BF16)
    lse_s = lse16.reshape(bs, n_kv_heads, heads_padded, ts)[:, :, :Q_PER_KV]
    lse_s = lse_s.transpose(3, 0, 1, 2).reshape(1, n_tok, N_GROUPS, HEADS_PER_GROUP)
    lse_s = jnp.pad(lse_s, ((0, 0), (0, 0), (0, 0), (0, LSE_LANES - HEADS_PER_GROUP)))
    ys_tm = _layer1(xs1, [o_s[:, :, gi] for gi in range(N_GROUPS)],
                    [lse_s[:, :, gi] for gi in range(N_GROUPS)], g1, wo, wup1, wdn1, n_tok)
    y_sample = jnp.swapaxes(ys_tm.reshape(ts, bs, d), 0, 1)
    u_s = jnp.swapaxes(us_tm, 0, 1)
    pool_sample = jnp.concatenate([cache_pool[0], u_s], axis=1)[:, -POOL_BUF:][None]
    k_sample = k_s.reshape(bs, ts, n_kv_heads, HEAD_DIM)
    v_sample = v_s.reshape(bs, ts, n_kv_heads, HEAD_DIM)

    return (y_prompt, y_sample, pool_prompt, k_prompt, v_prompt, pool_sample, k_sample, v_sample)
```

```python
import functools

import jax
import jax.numpy as jnp
from jax import lax
from jax.experimental import pallas as pl
from jax.experimental.pallas import tpu as pltpu

F32 = jnp.float32
BF16 = jnp.bfloat16

EPS = 1e-6
ROPE_THETA = 10000.0
PAST_LEN = 16384
POOL_WINDOWS = (2, 4, 8, 16)
POOL_BUF = max(POOL_WINDOWS) - 1
POOL_HALO = 16
HEAD_DIM = 64
ATTN_WINDOWS = (128, 512, 2048)
ATTN_DILATIONS = (1, 4, 16)
N_GROUPS = len(ATTN_WINDOWS)
KV_PER_GROUP = 2
Q_PER_KV = 3
HEADS_PER_GROUP = KV_PER_GROUP * Q_PER_KV
N_KV_HEADS = N_GROUPS * KV_PER_GROUP
GROUP_Q_WIDTH = HEADS_PER_GROUP * HEAD_DIM
GROUP_KV_WIDTH = KV_PER_GROUP * HEAD_DIM
Q_WIDTH = N_GROUPS * GROUP_Q_WIDTH
KV_WIDTH = N_GROUPS * GROUP_KV_WIDTH
KV_WINDOW = max(ATTN_WINDOWS)
BAND = 128
NEG_BIG = -1e30

LANES = 128
LSE_LANES = LANES
VMEM_LIMIT = 56 * 1024 * 1024
FF_CHUNK = 1024

for _w, _d in zip(ATTN_WINDOWS, ATTN_DILATIONS):
    assert _w // _d == BAND and _w % _d == 0
assert GROUP_KV_WIDTH == LANES


def _rms(x, g):
    return x * lax.rsqrt(jnp.mean(x * x, axis=-1, keepdims=True) + EPS) * g


def _mlp_residual(x1, g_in, g_out, wup_ref, wdn_ref):
    h = _rms(x1, g_in).astype(BF16)
    d_ff = wup_ref.shape[1]
    acc = None
    for c in range(d_ff // FF_CHUNK):
        cs = slice(c * FF_CHUNK, (c + 1) * FF_CHUNK)
        a = jnp.dot(h, wup_ref[:, cs], preferred_element_type=F32)
        a = jnp.square(jnp.maximum(a, 0.0)).astype(BF16)
        part = jnp.dot(a, wdn_ref[cs, :], preferred_element_type=F32)
        acc = part if acc is None else acc + part
    return x1 + _rms(acc, g_out)


def _const_spec(shape):
    zeros = (0,) * len(shape)
    return pl.BlockSpec(shape, lambda *_: zeros, pipeline_mode=pl.Buffered(1))


def _subseq_spec(dilation, rows, width):
    return pl.BlockSpec((None, dilation, rows // dilation, width), lambda bi, i: (bi, 0, i, 0))


def _subseq_rows(n_rows, r, dilation):
    n = n_rows // dilation
    return pl.ds(r, n, stride=dilation) if dilation > 1 else pl.ds(0, n)


def _layer0_prompt_kernel(x_ref, xh_ref, g_ref, wp_ref, ps_ref, wup_ref, wdn_ref,
                          y_ref, utail_ref, ext_scr):
    tm = x_ref.shape[1]
    pool_ch = wp_ref.shape[1]
    i = pl.program_id(1)
    g = g_ref[...]
    x = x_ref[0]
    u = _rms(x, g[0:1])
    uh = _rms(xh_ref[0], g[0:1]) * (i > 0).astype(F32)
    ext_scr[0:POOL_HALO, :] = uh
    ext_scr[POOL_HALO:, :] = u
    pos = i * tm + lax.broadcasted_iota(jnp.int32, (tm, 1), 0)
    parts = []
    for gi, w in enumerate(POOL_WINDOWS):
        cs = slice(gi * pool_ch, (gi + 1) * pool_ch)
        s = u[:, cs]
        for j in range(1, w):
            s = s + ext_scr[POOL_HALO - j:POOL_HALO - j + tm, cs]
        cnt = jnp.minimum(w, pos + 1).astype(F32)
        pooled = s / cnt - u[:, cs]
        parts.append(jnp.dot(pooled.astype(BF16), wp_ref[gi], preferred_element_type=F32))
    mix = jnp.concatenate(parts, axis=1) * ps_ref[...]
    x1 = x + _rms(mix, g[1:2])
    y_ref[0] = _mlp_residual(x1, g[2:3], g[3:4], wup_ref, wdn_ref)
    utail_ref[0] = u[tm - POOL_HALO:, :]


def _layer0_prompt(x, gains, w_pool, pool_scale, w_up, w_down, tm):
    b, t, d = x.shape
    assert t % tm == 0 and tm % POOL_HALO == 0 and t >= POOL_HALO
    halo_per_tile = tm // POOL_HALO
    return pl.pallas_call(
        _layer0_prompt_kernel,
        grid=(b, t // tm),
        in_specs=[
            pl.BlockSpec((1, tm, d), lambda bi, i: (bi, i, 0)),
            pl.BlockSpec((1, POOL_HALO, d),
                         lambda bi, i: (bi, jnp.maximum(i * halo_per_tile - 1, 0), 0)),
            _const_spec(gains.shape),
            _const_spec(w_pool.shape),
            _const_spec(pool_scale.shape),
            _const_spec(w_up.shape),
            _const_spec(w_down.shape),
        ],
        out_specs=[
            pl.BlockSpec((1, tm, d), lambda bi, i: (bi, i, 0)),
            pl.BlockSpec((1, POOL_HALO, d), lambda bi, i: (bi, 0, 0)),
        ],
        out_shape=[
            jax.ShapeDtypeStruct((b, t, d), F32),
            jax.ShapeDtypeStruct((b, POOL_HALO, d), F32),
        ],
        scratch_shapes=[pltpu.VMEM((tm + POOL_HALO, d), F32)],
        compiler_params=pltpu.CompilerParams(
            dimension_semantics=("arbitrary", "arbitrary"), vmem_limit_bytes=VMEM_LIMIT),
        name="layer0_prompt",
    )(x, x, gains, w_pool, pool_scale, w_up, w_down)


def _layer0_sample_kernel(start_pos, x_ref, buf_ref, g_ref, wp_ref, ps_ref, wup_ref, wdn_ref,
                          y_ref, u_ref):
    n_t, n_b, _ = x_ref.shape
    pool_ch = wp_ref.shape[1]
    g = g_ref[...]
    xs = [x_ref[t] for t in range(n_t)]
    us = [_rms(xt, g[0:1]) for xt in xs]
    ext = [buf_ref[j] for j in range(POOL_BUF)] + us
    parts = []
    for gi, w in enumerate(POOL_WINDOWS):
        cs = slice(gi * pool_ch, (gi + 1) * pool_ch)
        rows = []
        for t in range(n_t):
            s = us[t][:, cs]
            for j in range(1, w):
                s = s + ext[POOL_BUF + t - j][:, cs]
            cnt = float(min(w, start_pos + t + 1))
            rows.append(s / cnt - us[t][:, cs])
        pooled = jnp.concatenate(rows, axis=0)
        parts.append(jnp.dot(pooled.astype(BF16), wp_ref[gi], preferred_element_type=F32))
    mix = jnp.concatenate(parts, axis=1) * ps_ref[...]
    x = jnp.concatenate(xs, axis=0)
    x1 = x + _rms(mix, g[1:2])
    y = _mlp_residual(x1, g[2:3], g[3:4], wup_ref, wdn_ref)
    for t in range(n_t):
        y_ref[t] = y[t * n_b:(t + 1) * n_b]
        u_ref[t] = us[t]


def _layer0_sample(x_tm, buf_tm, start_pos, gains, w_pool, pool_scale, w_up, w_down):
    assert start_pos + 1 >= max(POOL_WINDOWS) and buf_tm.shape[0] == POOL_BUF
    return pl.pallas_call(
        functools.partial(_layer0_sample_kernel, start_pos),
        out_shape=[jax.ShapeDtypeStruct(x_tm.shape, F32), jax.ShapeDtypeStruct(x_tm.shape, F32)],
        compiler_params=pltpu.CompilerParams(vmem_limit_bytes=VMEM_LIMIT),
        name="layer0_sample",
    )(x_tm, buf_tm, gains, w_pool, pool_scale, w_up, w_down)


def _rope(x, cos, sin_signed):
    lane = lax.broadcasted_iota(jnp.int32, (x.shape[0], LANES), 1)
    first_half = (lane & (HEAD_DIM // 2)) == 0
    out = []
    for c in range(x.shape[1] // LANES):
        xc = x[:, c * LANES:(c + 1) * LANES]
        partner = jnp.where(first_half,
                            pltpu.roll(xc, LANES - HEAD_DIM // 2, 1),
                            pltpu.roll(xc, HEAD_DIM // 2, 1))
        out.append(xc * cos + partner * sin_signed)
    return out


def _qkv_kernel(dilations, first_kept_tile, transpose_kept,
                x_ref, gq_ref, gkv_ref, wq_ref, wkv_ref, cos_ref, sin_ref,
                q0_ref, q1_ref, q2_ref, k0_ref, k1_ref, k2_ref, v0_ref, v1_ref, v2_ref,
                kf_ref, vf_ref, stage_scr):
    i = pl.program_id(1)
    x = x_ref[0]
    cos = cos_ref[...]
    sin_signed = sin_ref[...]
    chunks_per_group = GROUP_Q_WIDTH // LANES

    def emit(out_ref, dilation, chunks):
        if dilation == 1:
            for c, chunk in enumerate(chunks):
                out_ref[0, :, c * LANES:(c + 1) * LANES] = chunk.astype(BF16)
            return
        for c, chunk in enumerate(chunks):
            stage_scr[c] = chunk
        for r in range(dilation):
            rows = _subseq_rows(stage_scr.shape[1], r, dilation)
            for c in range(len(chunks)):
                out_ref[r, :, c * LANES:(c + 1) * LANES] = stage_scr[c, rows, :].astype(BF16)

    u = _rms(x, gq_ref[...]).astype(BF16)
    q = jnp.dot(u, wq_ref[...], preferred_element_type=F32)
    scale = HEAD_DIM ** -0.5
    q_chunks = _rope(q, cos * scale, sin_signed * scale)
    for gi, q_ref in enumerate((q0_ref, q1_ref, q2_ref)):
        emit(q_ref, dilations[gi], q_chunks[gi * chunks_per_group:(gi + 1) * chunks_per_group])

    un = _rms(x, gkv_ref[...]).astype(BF16)
    kv = jnp.dot(un, wkv_ref[...], preferred_element_type=F32)
    k_chunks = _rope(kv[:, :KV_WIDTH], cos, sin_signed)
    v_chunks = [kv[:, KV_WIDTH + gi * LANES:KV_WIDTH + (gi + 1) * LANES] for gi in range(N_GROUPS)]
    for gi, (k_ref, v_ref) in enumerate(((k0_ref, v0_ref), (k1_ref, v1_ref), (k2_ref, v2_ref))):
        emit(k_ref, dilations[gi], [k_chunks[gi]])
        emit(v_ref, dilations[gi], [v_chunks[gi]])

    @pl.when(i >= first_kept_tile)
    def _():
        for gi in range(N_GROUPS):
            cs = slice(gi * LANES, (gi + 1) * LANES)
            if transpose_kept:
                kf_ref[0, cs, :] = k_chunks[gi].T
                vf_ref[0, cs, :] = v_chunks[gi].T
            else:
                kf_ref[0, :, cs] = k_chunks[gi]
                vf_ref[0, :, cs] = v_chunks[gi]


def _qkv(x, g_q, g_kv, w_q, w_kv, cos, sin_signed, tm, keep_rows, dilations, transpose_kept):
    b, t, d = x.shape
    assert t % tm == 0 and keep_rows % tm == 0 and all(tm % (16 * dil) == 0 for dil in dilations)
    first_kept_tile = (t - keep_rows) // tm
    kept_block = lambda bi, i: jnp.maximum(i - first_kept_tile, 0)
    if transpose_kept:
        kept = pl.BlockSpec((1, KV_WIDTH, tm), lambda bi, i: (bi, 0, kept_block(bi, i)))
        kept_shape = jax.ShapeDtypeStruct((b, KV_WIDTH, keep_rows), F32)
    else:
        kept = pl.BlockSpec((1, tm, KV_WIDTH), lambda bi, i: (bi, kept_block(bi, i), 0))
        kept_shape = jax.ShapeDtypeStruct((b, keep_rows, KV_WIDTH), F32)
    sub_shape = lambda dil, width: jax.ShapeDtypeStruct((b, dil, t // dil, width), BF16)
    q_specs = [_subseq_spec(dil, tm, GROUP_Q_WIDTH) for dil in dilations]
    kv_specs = [_subseq_spec(dil, tm, GROUP_KV_WIDTH) for dil in dilations]
    q_shapes = [sub_shape(dil, GROUP_Q_WIDTH) for dil in dilations]
    kv_shapes = [sub_shape(dil, GROUP_KV_WIDTH) for dil in dilations]
    return pl.pallas_call(
        functools.partial(_qkv_kernel, tuple(dilations), first_kept_tile, transpose_kept),
        grid=(b, t // tm),
        in_specs=[
            pl.BlockSpec((1, tm, d), lambda bi, i: (bi, i, 0)),
            _const_spec(g_q.shape), _const_spec(g_kv.shape),
            _const_spec(w_q.shape), _const_spec(w_kv.shape),
            pl.BlockSpec((tm, LANES), lambda bi, i: (i, 0)),
            pl.BlockSpec((tm, LANES), lambda bi, i: (i, 0)),
        ],
        out_specs=q_specs + kv_specs + kv_specs + [kept, kept],
        out_shape=q_shapes + kv_shapes + kv_shapes + [kept_shape, kept_shape],
        scratch_shapes=[pltpu.VMEM((GROUP_Q_WIDTH // LANES, tm, LANES), F32)],
        compiler_params=pltpu.CompilerParams(
            dimension_semantics=("arbitrary", "arbitrary"), vmem_limit_bytes=VMEM_LIMIT),
        name="qkv_rope",
    )(x, g_q, g_kv, w_q, w_kv, cos, sin_signed)


def _rope_tables(positions):
    inv = ROPE_THETA ** (-jnp.arange(0, HEAD_DIM, 2, dtype=F32) / HEAD_DIM)
    ang = positions.astype(F32)[:, None] * inv[None, :]
    cos, sin = jnp.cos(ang), jnp.sin(ang)
    heads_per_period = LANES // HEAD_DIM
    return (jnp.tile(cos, (1, 2 * heads_per_period)),
            jnp.tile(jnp.concatenate([-sin, sin], axis=1), (1, heads_per_period)))


def _band_attention_kernel(q_ref, kp_ref, kc_ref, vp_ref, vc_ref, o_ref, lse_ref, k_scr, v_scr):
    tq = q_ref.shape[0]
    i = pl.program_id(2)
    k_scr[0:BAND, :] = kp_ref[...]
    k_scr[BAND:, :] = kc_ref[...]
    v_scr[0:BAND, :] = vp_ref[...]
    v_scr[BAND:, :] = vc_ref[...]
    lse_ref[...] = jnp.zeros(lse_ref.shape, F32)

    row = lax.broadcasted_iota(jnp.int32, (BAND, 2 * BAND), 0)
    col = lax.broadcasted_iota(jnp.int32, (BAND, 2 * BAND), 1)
    band_bias = jnp.where((col >= row) & (col <= row + BAND), 0.0, NEG_BIG)
    first_bias = band_bias + jnp.where(col < BAND, NEG_BIG, 0.0) * (i == 0).astype(F32)
    for j in range(tq // BAND):
        rows = slice(j * BAND, (j + 1) * BAND)
        bias = jnp.concatenate([first_bias if j == 0 else band_bias] * Q_PER_KV, axis=0)
        for c in range(KV_PER_GROUP):
            kc = k_scr[j * BAND:(j + 2) * BAND, c * HEAD_DIM:(c + 1) * HEAD_DIM]
            vc = v_scr[j * BAND:(j + 2) * BAND, c * HEAD_DIM:(c + 1) * HEAD_DIM]
            heads = [c * Q_PER_KV + h for h in range(Q_PER_KV)]
            q = jnp.concatenate(
                [q_ref[rows, h * HEAD_DIM:(h + 1) * HEAD_DIM] for h in heads], axis=0)
            s = lax.dot_general(q, kc, (((1,), (1,)), ((), ())), preferred_element_type=F32)
            s = s + bias
            m = jnp.max(s, axis=1, keepdims=True)
            p = jnp.exp(s - m)
            l = jnp.sum(p, axis=1, keepdims=True)
            o = jnp.dot(p.astype(BF16), vc, preferred_element_type=F32) / l
            lse = m + jnp.log(l)
            for hi, h in enumerate(heads):
                o_ref[rows, h * HEAD_DIM:(h + 1) * HEAD_DIM] = (
                    o[hi * BAND:(hi + 1) * BAND].astype(o_ref.dtype))
                lse_ref[rows, h:h + 1] = lse[hi * BAND:(hi + 1) * BAND]


def _band_attention(q, k, v, tq):
    b, dilation, sub, _ = q.shape
    assert sub % tq == 0 and tq % BAND == 0
    bands_per_tile = tq // BAND
    cur = lambda width: pl.BlockSpec((None, None, tq, width), lambda bi, r, i: (bi, r, i, 0))
    prev = pl.BlockSpec((None, None, BAND, GROUP_KV_WIDTH),
                        lambda bi, r, i: (bi, r, jnp.maximum(i * bands_per_tile - 1, 0), 0))
    return pl.pallas_call(
        _band_attention_kernel,
        grid=(b, dilation, sub // tq),
        in_specs=[cur(GROUP_Q_WIDTH), prev, cur(GROUP_KV_WIDTH), prev, cur(GROUP_KV_WIDTH)],
        out_specs=[cur(GROUP_Q_WIDTH), cur(LSE_LANES)],
        out_shape=[jax.ShapeDtypeStruct(q.shape, BF16),
                   jax.ShapeDtypeStruct((b, dilation, sub, LSE_LANES), F32)],
        scratch_shapes=[pltpu.VMEM((tq + BAND, GROUP_KV_WIDTH), BF16),
                        pltpu.VMEM((tq + BAND, GROUP_KV_WIDTH), BF16)],
        compiler_params=pltpu.CompilerParams(
            dimension_semantics=("arbitrary", "arbitrary", "arbitrary"),
            vmem_limit_bytes=VMEM_LIMIT),
        name=f"band_attention_d{dilation}",
    )(q, k, k, v, v)


SAMPLE_Q_ROWS = 16
NEW_KEY_ROWS = 8


def _sample_attention_kernel(n_new, q_ref, kn_ref, vn_ref, kc0, kc1, kc2, vc0, vc1, vc2,
                             o_ref, lse_ref):
    n_rows = q_ref.shape[2]
    kn_all = kn_ref[0]
    vn_all = vn_ref[0]
    contract_last = (((1,), (1,)), ((), ()))
    for gi, (kc_ref, vc_ref) in enumerate(((kc0, vc0), (kc1, vc1), (kc2, vc2))):
        dmask = ATTN_DILATIONS[gi] - 1
        n_cache = kc_ref.shape[3]
        assert n_cache == ATTN_WINDOWS[gi]
        t_c = lax.broadcasted_iota(jnp.int32, (n_rows, n_cache), 0) & (n_new - 1)
        j_c = lax.broadcasted_iota(jnp.int32, (n_rows, n_cache), 1)
        valid_c = (j_c >= t_c) & (((j_c - t_c) & dmask) == 0)
        t_n = lax.broadcasted_iota(jnp.int32, (n_rows, NEW_KEY_ROWS), 0) & (n_new - 1)
        j_n = lax.broadcasted_iota(jnp.int32, (n_rows, NEW_KEY_ROWS), 1)
        valid_n = (j_n <= t_n) & (((t_n - j_n) & dmask) == 0)
        for c in range(KV_PER_GROUP):
            kvh = gi * KV_PER_GROUP + c
            ns = slice(kvh * HEAD_DIM, (kvh + 1) * HEAD_DIM)
            q = q_ref[0, kvh]
            kt = kc_ref[0, c].astype(BF16)
            vt = vc_ref[0, c].astype(BF16)
            s_c = jnp.dot(q, kt, preferred_element_type=F32)
            s_n = lax.dot_general(q, kn_all[:, ns].astype(BF16), contract_last,
                                  preferred_element_type=F32)
            s_c = jnp.where(valid_c, s_c, NEG_BIG)
            s_n = jnp.where(valid_n, s_n, NEG_BIG)
            m = jnp.maximum(jnp.max(s_c, axis=1, keepdims=True),
                            jnp.max(s_n, axis=1, keepdims=True))
            p_c = jnp.exp(s_c - m)
            p_n = jnp.exp(s_n - m)
            l = jnp.sum(p_c, axis=1, keepdims=True) + jnp.sum(p_n, axis=1, keepdims=True)
            o = (lax.dot_general(p_c.astype(BF16), vt, contract_last, preferred_element_type=F32)
                 + jnp.dot(p_n.astype(BF16), vn_all[:, ns].astype(BF16),
                           preferred_element_type=F32))
            o_ref[0, kvh] = o / l
            lse_ref[0, kvh] = m + jnp.log(l)


def _sample_attention(q16, k_new, v_new, cache_kt, cache_vt, n_new):
    b = q16.shape[0]
    n_kv = q16.shape[1]
    assert cache_kt.shape[3] == KV_WINDOW and n_new & (n_new - 1) == 0 and n_new <= NEW_KEY_ROWS

    def cache_spec(gi):
        cols = ATTN_WINDOWS[gi]
        last = KV_WINDOW // cols - 1
        return pl.BlockSpec((1, KV_PER_GROUP, HEAD_DIM, cols), lambda bi: (bi, gi, 0, last))

    whole = lambda a: pl.BlockSpec((1,) + a.shape[1:], lambda bi: (bi,) + (0,) * (a.ndim - 1))
    o_shape = jax.ShapeDtypeStruct((b, n_kv, SAMPLE_Q_ROWS, HEAD_DIM), F32)
    lse_shape = jax.ShapeDtypeStruct((b, n_kv, SAMPLE_Q_ROWS, 1), F32)
    return pl.pallas_call(
        functools.partial(_sample_attention_kernel, n_new),
        grid=(b,),
        in_specs=[whole(q16), whole(k_new), whole(v_new)]
        + [cache_spec(gi) for gi in range(N_GROUPS)] * 2,
        out_specs=[whole(o_shape), whole(lse_shape)],
        out_shape=[o_shape, lse_shape],
        compiler_params=pltpu.CompilerParams(
            dimension_semantics=("arbitrary",), vmem_limit_bytes=VMEM_LIMIT),
        name="sample_attention",
    )(q16, k_new, v_new, cache_kt, cache_kt, cache_kt, cache_vt, cache_vt, cache_vt)


def _layer1_kernel(dilations, x_ref, o0_ref, o1_ref, o2_ref, l0_ref, l1_ref, l2_ref, g_ref,
                   wo_ref, wup_ref, wdn_ref, y_ref, o_scr, lse_scr, a_scr):
    tm = x_ref.shape[1]
    g = g_ref[...]
    x = x_ref[0]
    chunks_per_group = GROUP_Q_WIDTH // LANES
    for gi, (o_ref, l_ref) in enumerate(((o0_ref, l0_ref), (o1_ref, l1_ref), (o2_ref, l2_ref))):
        for r in range(dilations[gi]):
            rows = _subseq_rows(tm, r, dilations[gi])
            lse_scr[gi, rows, :] = l_ref[r]
            for c in range(chunks_per_group):
                o_scr[gi * chunks_per_group + c, rows, :] = (
                    o_ref[r, :, c * LANES:(c + 1) * LANES].astype(F32))
    lses = [lse_scr[gi] for gi in range(N_GROUPS)]
    top = jnp.maximum(jnp.maximum(lses[0], lses[1]), lses[2])
    es = [jnp.exp(l - top) for l in lses]
    den = es[0] + es[1] + es[2]
    head_id = lax.broadcasted_iota(jnp.int32, (tm, GROUP_Q_WIDTH), 1) // HEAD_DIM
    for gi in range(N_GROUPS):
        alpha = es[gi] / den
        wide = jnp.zeros((tm, GROUP_Q_WIDTH), F32)
        for h in range(HEADS_PER_GROUP):
            wide = jnp.where(head_id == h, alpha[:, h:h + 1], wide)
        for c in range(chunks_per_group):
            ci = gi * chunks_per_group + c
            a_scr[:, ci * LANES:(ci + 1) * LANES] = (
                o_scr[ci] * wide[:, c * LANES:(c + 1) * LANES]).astype(BF16)
    mix = jnp.dot(a_scr[...], wo_ref[...], preferred_element_type=F32)
    x1 = x + _rms(mix, g[1:2])
    y_ref[0] = _mlp_residual(x1, g[2:3], g[3:4], wup_ref, wdn_ref)


def _layer1(x, os, lses, gains, w_o, w_up, w_down, tm):
    b, t, d = x.shape
    dilations = tuple(o.shape[1] for o in os)
    assert t % tm == 0 and all(tm % (16 * dil) == 0 for dil in dilations)
    tile = pl.BlockSpec((1, tm, d), lambda bi, i: (bi, i, 0))
    return pl.pallas_call(
        functools.partial(_layer1_kernel, dilations),
        grid=(b, t // tm),
        in_specs=[tile]
        + [_subseq_spec(dil, tm, GROUP_Q_WIDTH) for dil in dilations]
        + [_subseq_spec(dil, tm, LSE_LANES) for dil in dilations]
        + [_const_spec(gains.shape), _const_spec(w_o.shape),
           _const_spec(w_up.shape), _const_spec(w_down.shape)],
        out_specs=tile,
        out_shape=jax.ShapeDtypeStruct((b, t, d), F32),
        scratch_shapes=[pltpu.VMEM((Q_WIDTH // LANES, tm, LANES), F32),
                        pltpu.VMEM((N_GROUPS, tm, LSE_LANES), F32),
                        pltpu.VMEM((tm, Q_WIDTH), BF16)],
        compiler_params=pltpu.CompilerParams(
            dimension_semantics=("arbitrary", "arbitrary"), vmem_limit_bytes=VMEM_LIMIT),
        name="layer1_tail",
    )(x, *os, *lses, gains, w_o, w_up, w_down)


PROMPT_TILE = 512
ATTN_TILE = 1024


def kernel(x_prompt, x_sample, cache_pool, cache_k, cache_v, norm_gains, kv_norm_gain, w_pool,
           pool_scale, w_q, w_o, w_kv, w_up, w_down):
    depth = norm_gains.shape[0]
    assert depth == 2 and cache_pool.shape[0] == 1 and w_q.shape[0] == 1
    bp, tp, d = x_prompt.shape
    bs, ts, _ = x_sample.shape

    g0, g1 = norm_gains[0], norm_gains[1]
    g1_q = g1[0:1]
    g_kv = kv_norm_gain[None, :]
    wp = w_pool[0].astype(BF16)
    ps = pool_scale[0][None, :]
    wq, wo, wkv = w_q[0].astype(BF16), w_o[0].astype(BF16), w_kv.astype(BF16)
    wup0, wup1 = w_up[0].astype(BF16), w_up[1].astype(BF16)
    wdn0, wdn1 = w_down[0].astype(BF16), w_down[1].astype(BF16)

    keep = min(KV_WINDOW, tp)
    xp1, utail = _layer0_prompt(x_prompt, g0, wp, ps, wup0, wdn0, PROMPT_TILE)
    cos_p, sin_p = _rope_tables(jnp.arange(tp))
    qkv_p = _qkv(xp1, g1_q, g_kv, wq, wkv, cos_p, sin_p, PROMPT_TILE, keep, ATTN_DILATIONS, True)
    os_p, lses_p = [], []
    for gi, dil in enumerate(ATTN_DILATIONS):
        o, lse = _band_attention(qkv_p[gi], qkv_p[N_GROUPS + gi], qkv_p[2 * N_GROUPS + gi],
                                 min(ATTN_TILE, tp // dil))
        os_p.append(o)
        lses_p.append(lse)
    y_prompt = _layer1(xp1, os_p, lses_p, g1, wo, wup1, wdn1, PROMPT_TILE)
    pool_prompt = utail[:, POOL_HALO - POOL_BUF:][None]
    k_prompt = qkv_p[-2].reshape(bp, N_KV_HEADS, HEAD_DIM, keep).transpose(0, 3, 1, 2)
    v_prompt = qkv_p[-1].reshape(bp, N_KV_HEADS, HEAD_DIM, keep).transpose(0, 3, 1, 2)

    n_tok = ts * bs
    xs_tm = jnp.swapaxes(x_sample, 0, 1)
    buf_tm = jnp.swapaxes(cache_pool[0], 0, 1)
    xs1_tm, us_tm = _layer0_sample(xs_tm, buf_tm, PAST_LEN, g0, wp, ps, wup0, wdn0)
    xs1 = xs1_tm.reshape(1, n_tok, d)
    cos_s, sin_s = _rope_tables(PAST_LEN + jnp.arange(n_tok) // bs)
    qkv_s = _qkv(xs1, g1_q, g_kv, wq, wkv, cos_s, sin_s, n_tok, n_tok, (1,) * N_GROUPS, False)
    q_s = jnp.concatenate([q[0, 0] for q in qkv_s[:N_GROUPS]], axis=-1)
    q16 = q_s.reshape(ts, bs, N_KV_HEADS, Q_PER_KV, HEAD_DIM).transpose(1, 2, 3, 0, 4)
    q16 = jnp.pad(q16, ((0, 0), (0, 0), (0, SAMPLE_Q_ROWS // ts - Q_PER_KV), (0, 0), (0, 0)))
    q16 = q16.reshape(bs, N_KV_HEADS, SAMPLE_Q_ROWS, HEAD_DIM)
    k_s = jnp.swapaxes(qkv_s[-2].reshape(ts, bs, KV_WIDTH), 0, 1)
    v_s = jnp.swapaxes(qkv_s[-1].reshape(ts, bs, KV_WIDTH), 0, 1)
    pad_new = ((0, 0), (0, NEW_KEY_ROWS - ts), (0, 0))
    o16, lse16 = _sample_attention(
        q16, jnp.pad(k_s, pad_new), jnp.pad(v_s, pad_new),
        cache_k.transpose(0, 2, 3, 1), cache_v.transpose(0, 2, 3, 1), ts)
    heads_padded = SAMPLE_Q_ROWS // ts
    o_s = o16.reshape(bs, N_KV_HEADS, heads_padded, ts, HEAD_DIM)[:, :, :Q_PER_KV]
    o_s = o_s.transpose(3, 0, 1, 2, 4).reshape(1, 1, n_tok, N_GROUPS, GROUP_Q_WIDTH).astype(BF16)
    lse_s = lse16.reshape(bs, N_KV_HEADS, heads_padded, ts)[:, :, :Q_PER_KV]
    lse_s = lse_s.transpose(3, 0, 1, 2).reshape(1, 1, n_tok, N_GROUPS, HEADS_PER_GROUP)
    lse_s = jnp.pad(lse_s, ((0, 0),) * 4 + ((0, LSE_LANES - HEADS_PER_GROUP),))
    ys_tm = _layer1(xs1, [o_s[:, :, :, gi] for gi in range(N_GROUPS)],
                    [lse_s[:, :, :, gi] for gi in range(N_GROUPS)], g1, wo, wup1, wdn1, n_tok)
    y_sample = jnp.swapaxes(ys_tm.reshape(ts, bs, d), 0, 1)
    u_s = jnp.swapaxes(us_tm, 0, 1)
    pool_sample = jnp.concatenate([cache_pool[0], u_s], axis=1)[:, -POOL_BUF:][None]
    k_sample = k_s.reshape(bs, ts, N_KV_HEADS, HEAD_DIM)
    v_sample = v_s.reshape(bs, ts, N_KV_HEADS, HEAD_DIM)

    return (y_prompt, y_sample, pool_prompt, k_prompt, v_prompt, pool_sample, k_sample, v_sample)
```

```python
import functools

import jax
import jax.numpy as jnp
from jax import lax
from jax.experimental import pallas as pl
from jax.experimental.pallas import tpu as pltpu

F32 = jnp.float32
BF16 = jnp.bfloat16

EPS = 1e-6
ROPE_THETA = 10000.0
PAST_LEN = 16384
POOL_WINDOWS = (2, 4, 8, 16)
POOL_BUF = max(POOL_WINDOWS) - 1
POOL_HALO = 16
HEAD_DIM = 64
ATTN_WINDOWS = (128, 512, 2048)
ATTN_DILATIONS = (1, 4, 16)
N_GROUPS = len(ATTN_WINDOWS)
KV_PER_GROUP = 2
Q_PER_KV = 3
HEADS_PER_GROUP = KV_PER_GROUP * Q_PER_KV
N_KV_HEADS = N_GROUPS * KV_PER_GROUP
GROUP_Q_WIDTH = HEADS_PER_GROUP * HEAD_DIM
GROUP_KV_WIDTH = KV_PER_GROUP * HEAD_DIM
Q_WIDTH = N_GROUPS * GROUP_Q_WIDTH
KV_WIDTH = N_GROUPS * GROUP_KV_WIDTH
KV_WINDOW = max(ATTN_WINDOWS)
BAND = 128
NEG_BIG = -1e30

LANES = 128
STAT_LANES = LANES
STAT_MAX_OFFSET = 8
HEAD_SLOT_ORDER = (0, 3, 1, 4, 2, 5)
VMEM_LIMIT = 56 * 1024 * 1024
FF_CHUNK = 1024

for _w, _d in zip(ATTN_WINDOWS, ATTN_DILATIONS):
    assert _w // _d == BAND and _w % _d == 0
assert GROUP_KV_WIDTH == LANES


def _rms(x, g):
    return x * lax.rsqrt(jnp.mean(x * x, axis=-1, keepdims=True) + EPS) * g


def _mlp_residual(x1, g_in, g_out, wup_ref, wdn_ref):
    h = _rms(x1, g_in).astype(BF16)
    d_ff = wup_ref.shape[1]
    acc = None
    for c in range(d_ff // FF_CHUNK):
        cs = slice(c * FF_CHUNK, (c + 1) * FF_CHUNK)
        a = jnp.dot(h, wup_ref[:, cs], preferred_element_type=F32)
        a = jnp.square(jnp.maximum(a, 0.0)).astype(BF16)
        part = jnp.dot(a, wdn_ref[cs, :], preferred_element_type=F32)
        acc = part if acc is None else acc + part
    return x1 + _rms(acc, g_out)


def _const_spec(shape):
    zeros = (0,) * len(shape)
    return pl.BlockSpec(shape, lambda *_: zeros, pipeline_mode=pl.Buffered(1))


def _subseq_spec(dilation, rows, width):
    return pl.BlockSpec((None, dilation, rows // dilation, width), lambda bi, i: (bi, 0, i, 0))


def _subseq_rows(n_rows, r, dilation):
    n = n_rows // dilation
    return pl.ds(r, n, stride=dilation) if dilation > 1 else pl.ds(0, n)


def _layer0_prompt_kernel(x_ref, xh_ref, g_ref, wp_ref, ps_ref, wup_ref, wdn_ref,
                          y_ref, utail_ref, ext_scr):
    tm = x_ref.shape[1]
    pool_ch = wp_ref.shape[1]
    i = pl.program_id(1)
    g = g_ref[...]
    x = x_ref[0]
    u = _rms(x, g[0:1])
    uh = _rms(xh_ref[0], g[0:1]) * (i > 0).astype(F32)
    ext_scr[0:POOL_HALO, :] = uh
    ext_scr[POOL_HALO:, :] = u
    pos = i * tm + lax.broadcasted_iota(jnp.int32, (tm, 1), 0)
    parts = []
    for gi, w in enumerate(POOL_WINDOWS):
        cs = slice(gi * pool_ch, (gi + 1) * pool_ch)
        s = u[:, cs]
        for j in range(1, w):
            s = s + ext_scr[POOL_HALO - j:POOL_HALO - j + tm, cs]
        cnt = jnp.minimum(w, pos + 1).astype(F32)
        pooled = s / cnt - u[:, cs]
        parts.append(jnp.dot(pooled.astype(BF16), wp_ref[gi], preferred_element_type=F32))
    mix = jnp.concatenate(parts, axis=1) * ps_ref[...]
    x1 = x + _rms(mix, g[1:2])
    y_ref[0] = _mlp_residual(x1, g[2:3], g[3:4], wup_ref, wdn_ref)
    utail_ref[0] = u[tm - POOL_HALO:, :]


def _layer0_prompt(x, gains, w_pool, pool_scale, w_up, w_down, tm):
    b, t, d = x.shape
    assert t % tm == 0 and tm % POOL_HALO == 0 and t >= POOL_HALO
    halo_per_tile = tm // POOL_HALO
    return pl.pallas_call(
        _layer0_prompt_kernel,
        grid=(b, t // tm),
        in_specs=[
            pl.BlockSpec((1, tm, d), lambda bi, i: (bi, i, 0)),
            pl.BlockSpec((1, POOL_HALO, d),
                         lambda bi, i: (bi, jnp.maximum(i * halo_per_tile - 1, 0), 0)),
            _const_spec(gains.shape),
            _const_spec(w_pool.shape),
            _const_spec(pool_scale.shape),
            _const_spec(w_up.shape),
            _const_spec(w_down.shape),
        ],
        out_specs=[
            pl.BlockSpec((1, tm, d), lambda bi, i: (bi, i, 0)),
            pl.BlockSpec((1, POOL_HALO, d), lambda bi, i: (bi, 0, 0)),
        ],
        out_shape=[
            jax.ShapeDtypeStruct((b, t, d), F32),
            jax.ShapeDtypeStruct((b, POOL_HALO, d), F32),
        ],
        scratch_shapes=[pltpu.VMEM((tm + POOL_HALO, d), F32)],
        compiler_params=pltpu.CompilerParams(
            dimension_semantics=("arbitrary", "arbitrary"), vmem_limit_bytes=VMEM_LIMIT),
        name="layer0_prompt",
    )(x, x, gains, w_pool, pool_scale, w_up, w_down)


def _layer0_sample_kernel(start_pos, x_ref, buf_ref, g_ref, wp_ref, ps_ref, wup_ref, wdn_ref,
                          y_ref, u_ref):
    n_t, n_b, _ = x_ref.shape
    pool_ch = wp_ref.shape[1]
    g = g_ref[...]
    xs = [x_ref[t] for t in range(n_t)]
    us = [_rms(xt, g[0:1]) for xt in xs]
    ext = [buf_ref[j] for j in range(POOL_BUF)] + us
    parts = []
    for gi, w in enumerate(POOL_WINDOWS):
        cs = slice(gi * pool_ch, (gi + 1) * pool_ch)
        rows = []
        for t in range(n_t):
            s = us[t][:, cs]
            for j in range(1, w):
                s = s + ext[POOL_BUF + t - j][:, cs]
            cnt = float(min(w, start_pos + t + 1))
            rows.append(s / cnt - us[t][:, cs])
        pooled = jnp.concatenate(rows, axis=0)
        parts.append(jnp.dot(pooled.astype(BF16), wp_ref[gi], preferred_element_type=F32))
    mix = jnp.concatenate(parts, axis=1) * ps_ref[...]
    x = jnp.concatenate(xs, axis=0)
    x1 = x + _rms(mix, g[1:2])
    y = _mlp_residual(x1, g[2:3], g[3:4], wup_ref, wdn_ref)
    for t in range(n_t):
        y_ref[t] = y[t * n_b:(t + 1) * n_b]
        u_ref[t] = us[t]


def _layer0_sample(x_tm, buf_tm, start_pos, gains, w_pool, pool_scale, w_up, w_down):
    assert start_pos + 1 >= max(POOL_WINDOWS) and buf_tm.shape[0] == POOL_BUF
    return pl.pallas_call(
        functools.partial(_layer0_sample_kernel, start_pos),
        out_shape=[jax.ShapeDtypeStruct(x_tm.shape, F32), jax.ShapeDtypeStruct(x_tm.shape, F32)],
        compiler_params=pltpu.CompilerParams(vmem_limit_bytes=VMEM_LIMIT),
        name="layer0_sample",
    )(x_tm, buf_tm, gains, w_pool, pool_scale, w_up, w_down)


def _rope(x, cos, sin_signed):
    lane = lax.broadcasted_iota(jnp.int32, (x.shape[0], LANES), 1)
    first_half = (lane & (HEAD_DIM // 2)) == 0
    out = []
    for c in range(x.shape[1] // LANES):
        xc = x[:, c * LANES:(c + 1) * LANES]
        partner = jnp.where(first_half,
                            pltpu.roll(xc, LANES - HEAD_DIM // 2, 1),
                            pltpu.roll(xc, HEAD_DIM // 2, 1))
        out.append(xc * cos + partner * sin_signed)
    return out


def _qkv_kernel(dilations, first_kept_tile, transpose_kept,
                x_ref, gq_ref, gkv_ref, wq_ref, wkv_ref, cos_ref, sin_ref,
                q0_ref, q1_ref, q2_ref, k0_ref, k1_ref, k2_ref, v0_ref, v1_ref, v2_ref,
                kf_ref, vf_ref, stage_scr):
    i = pl.program_id(1)
    x = x_ref[0]
    cos = cos_ref[...]
    sin_signed = sin_ref[...]
    chunks_per_group = GROUP_Q_WIDTH // LANES

    def emit(out_ref, dilation, chunks):
        if dilation == 1:
            for c, chunk in enumerate(chunks):
                out_ref[0, :, c * LANES:(c + 1) * LANES] = chunk.astype(BF16)
            return
        for c, chunk in enumerate(chunks):
            stage_scr[c] = chunk
        for r in range(dilation):
            rows = _subseq_rows(stage_scr.shape[1], r, dilation)
            for c in range(len(chunks)):
                out_ref[r, :, c * LANES:(c + 1) * LANES] = stage_scr[c, rows, :].astype(BF16)

    u = _rms(x, gq_ref[...]).astype(BF16)
    q = jnp.dot(u, wq_ref[...], preferred_element_type=F32)
    scale = HEAD_DIM ** -0.5
    q_chunks = _rope(q, cos * scale, sin_signed * scale)
    for gi, q_ref in enumerate((q0_ref, q1_ref, q2_ref)):
        emit(q_ref, dilations[gi], q_chunks[gi * chunks_per_group:(gi + 1) * chunks_per_group])

    un = _rms(x, gkv_ref[...]).astype(BF16)
    kv = jnp.dot(un, wkv_ref[...], preferred_element_type=F32)
    k_chunks = _rope(kv[:, :KV_WIDTH], cos, sin_signed)
    v_chunks = [kv[:, KV_WIDTH + gi * LANES:KV_WIDTH + (gi + 1) * LANES] for gi in range(N_GROUPS)]
    for gi, (k_ref, v_ref) in enumerate(((k0_ref, v0_ref), (k1_ref, v1_ref), (k2_ref, v2_ref))):
        emit(k_ref, dilations[gi], [k_chunks[gi]])
        emit(v_ref, dilations[gi], [v_chunks[gi]])

    @pl.when(i >= first_kept_tile)
    def _():
        for gi in range(N_GROUPS):
            cs = slice(gi * LANES, (gi + 1) * LANES)
            if transpose_kept:
                kf_ref[0, cs, :] = k_chunks[gi].T
                vf_ref[0, cs, :] = v_chunks[gi].T
            else:
                kf_ref[0, :, cs] = k_chunks[gi]
                vf_ref[0, :, cs] = v_chunks[gi]


def _qkv(x, g_q, g_kv, w_q, w_kv, cos, sin_signed, tm, keep_rows, dilations, transpose_kept):
    b, t, d = x.shape
    assert t % tm == 0 and keep_rows % tm == 0 and all(tm % (16 * dil) == 0 for dil in dilations)
    first_kept_tile = (t - keep_rows) // tm
    kept_block = lambda bi, i: jnp.maximum(i - first_kept_tile, 0)
    if transpose_kept:
        kept = pl.BlockSpec((1, KV_WIDTH, tm), lambda bi, i: (bi, 0, kept_block(bi, i)))
        kept_shape = jax.ShapeDtypeStruct((b, KV_WIDTH, keep_rows), F32)
    else:
        kept = pl.BlockSpec((1, tm, KV_WIDTH), lambda bi, i: (bi, kept_block(bi, i), 0))
        kept_shape = jax.ShapeDtypeStruct((b, keep_rows, KV_WIDTH), F32)
    sub_shape = lambda dil, width: jax.ShapeDtypeStruct((b, dil, t // dil, width), BF16)
    q_specs = [_subseq_spec(dil, tm, GROUP_Q_WIDTH) for dil in dilations]
    kv_specs = [_subseq_spec(dil, tm, GROUP_KV_WIDTH) for dil in dilations]
    q_shapes = [sub_shape(dil, GROUP_Q_WIDTH) for dil in dilations]
    kv_shapes = [sub_shape(dil, GROUP_KV_WIDTH) for dil in dilations]
    return pl.pallas_call(
        functools.partial(_qkv_kernel, tuple(dilations), first_kept_tile, transpose_kept),
        grid=(b, t // tm),
        in_specs=[
            pl.BlockSpec((1, tm, d), lambda bi, i: (bi, i, 0)),
            _const_spec(g_q.shape), _const_spec(g_kv.shape),
            _const_spec(w_q.shape), _const_spec(w_kv.shape),
            pl.BlockSpec((tm, LANES), lambda bi, i: (i, 0)),
            pl.BlockSpec((tm, LANES), lambda bi, i: (i, 0)),
        ],
        out_specs=q_specs + kv_specs + kv_specs + [kept, kept],
        out_shape=q_shapes + kv_shapes + kv_shapes + [kept_shape, kept_shape],
        scratch_shapes=[pltpu.VMEM((GROUP_Q_WIDTH // LANES, tm, LANES), F32)],
        compiler_params=pltpu.CompilerParams(
            dimension_semantics=("arbitrary", "arbitrary"), vmem_limit_bytes=VMEM_LIMIT),
        name="qkv_rope",
    )(x, g_q, g_kv, w_q, w_kv, cos, sin_signed)


def _rope_tables(positions):
    inv = ROPE_THETA ** (-jnp.arange(0, HEAD_DIM, 2, dtype=F32) / HEAD_DIM)
    ang = positions.astype(F32)[:, None] * inv[None, :]
    cos, sin = jnp.cos(ang), jnp.sin(ang)
    heads_per_period = LANES // HEAD_DIM
    return (jnp.tile(cos, (1, 2 * heads_per_period)),
            jnp.tile(jnp.concatenate([-sin, sin], axis=1), (1, heads_per_period)))


def _stat_lane(c, k):
    return (1 - c) * HEAD_DIM + k


def _band_attention_kernel(q_ref, kp_ref, kc_ref, vp_ref, vc_ref, o_ref, stat_ref,
                           k_scr, v_scr, bias_scr, s_scr, p_scr):
    tq = q_ref.shape[0]
    n_bands = tq // BAND
    i = pl.program_id(2)
    lane = lax.broadcasted_iota(jnp.int32, (1, LANES), 1)
    for c in range(KV_PER_GROUP):
        own = (lane // HEAD_DIM) == c
        for lo, k_src, v_src in ((0, kp_ref, vp_ref), (BAND, kc_ref, vc_ref)):
            rows = slice(lo, lo + k_src.shape[0])
            k_scr[c, rows, :] = jnp.where(own, k_src[...], jnp.zeros((), BF16))
            v_scr[c, rows, :] = jnp.where(own, v_src[...], jnp.ones((), BF16))
    stat_ref[...] = jnp.ones(stat_ref.shape, F32)

    row = lax.broadcasted_iota(jnp.int32, (BAND, 2 * BAND), 0)
    col = lax.broadcasted_iota(jnp.int32, (BAND, 2 * BAND), 1)
    band_bias = jnp.where((col >= row) & (col <= row + BAND), 0.0, NEG_BIG)
    bias_scr[0] = band_bias
    bias_scr[1] = band_bias + jnp.where(col < BAND, NEG_BIG, 0.0)

    def start_of(j):
        return j * BAND if isinstance(j, int) else pl.multiple_of(j * BAND, BAND)

    def scores(j, slot):
        q = jnp.concatenate(
            [q_ref[pl.ds(start_of(j), BAND), k * LANES:(k + 1) * LANES] for k in range(Q_PER_KV)],
            axis=0)
        for c in range(KV_PER_GROUP):
            keys = k_scr[c, pl.ds(start_of(j), 2 * BAND), :]
            s_scr[slot, c] = lax.dot_general(q, keys, (((1,), (1,)), ((), ())),
                                             preferred_element_type=F32)

    def softmax(j, slot):
        first = jnp.logical_and(i == 0, j == 0).astype(jnp.int32)
        bias = bias_scr[first]
        bias = jnp.concatenate([bias] * Q_PER_KV, axis=0)
        for c in range(KV_PER_GROUP):
            s = s_scr[slot, c] + bias
            m = jnp.max(s, axis=1, keepdims=True)
            p_scr[slot, c] = jnp.exp(s - m).astype(BF16)
            for k in range(Q_PER_KV):
                stat_ref[pl.ds(start_of(j), BAND), pl.ds(_stat_lane(c, k) + STAT_MAX_OFFSET, 1)] = (
                    m[k * BAND:(k + 1) * BAND])

    def weighted_values(j, slot):
        rows = pl.ds(start_of(j), BAND)
        ov = [jnp.dot(p_scr[slot, c], v_scr[c, pl.ds(start_of(j), 2 * BAND), :],
                      preferred_element_type=F32) for c in range(KV_PER_GROUP)]
        low = lane < HEAD_DIM
        sums = None
        for k in range(Q_PER_KV):
            part = slice(k * BAND, (k + 1) * BAND)
            o_ref[rows, k * LANES:(k + 1) * LANES] = (
                jnp.where(low, ov[0][part], ov[1][part]).astype(o_ref.dtype))
            sums_k = jnp.where(low, ov[1][part], ov[0][part])
            sums = sums_k if sums is None else jnp.where((lane % HEAD_DIM) == k, sums_k, sums)
        for c in range(KV_PER_GROUP):
            lanes = pl.ds(_stat_lane(c, 0), Q_PER_KV)
            stat_ref[rows, lanes] = sums[:, _stat_lane(c, 0):_stat_lane(c, 0) + Q_PER_KV]

    scores(0, 0)
    softmax(0, 0)
    scores(1, 1)

    def steady(j, carry):
        slot = lax.rem(j, 2)
        weighted_values(j - 1, 1 - slot)
        softmax(j, slot)
        scores(j + 1, 1 - slot)
        return carry

    lax.fori_loop(1, n_bands - 1, steady, 0)
    last = n_bands - 1
    weighted_values(last - 1, (last - 1) % 2)
    softmax(last, last % 2)
    weighted_values(last, last % 2)


def _band_attention(q, k, v, tq):
    b, dilation, sub, _ = q.shape
    assert sub % tq == 0 and tq % BAND == 0 and tq >= 2 * BAND
    bands_per_tile = tq // BAND
    cur = lambda width: pl.BlockSpec((None, None, tq, width), lambda bi, r, i: (bi, r, i, 0))
    prev = pl.BlockSpec((None, None, BAND, GROUP_KV_WIDTH),
                        lambda bi, r, i: (bi, r, jnp.maximum(i * bands_per_tile - 1, 0), 0))
    return pl.pallas_call(
        _band_attention_kernel,
        grid=(b, dilation, sub // tq),
        in_specs=[cur(GROUP_Q_WIDTH), prev, cur(GROUP_KV_WIDTH), prev, cur(GROUP_KV_WIDTH)],
        out_specs=[cur(GROUP_Q_WIDTH), cur(STAT_LANES)],
        out_shape=[jax.ShapeDtypeStruct(q.shape, BF16),
                   jax.ShapeDtypeStruct((b, dilation, sub, STAT_LANES), F32)],
        scratch_shapes=[pltpu.VMEM((KV_PER_GROUP, tq + BAND, GROUP_KV_WIDTH), BF16),
                        pltpu.VMEM((KV_PER_GROUP, tq + BAND, GROUP_KV_WIDTH), BF16),
                        pltpu.VMEM((2, BAND, 2 * BAND), F32),
                        pltpu.VMEM((2, KV_PER_GROUP, Q_PER_KV * BAND, 2 * BAND), F32),
                        pltpu.VMEM((2, KV_PER_GROUP, Q_PER_KV * BAND, 2 * BAND), BF16)],
        compiler_params=pltpu.CompilerParams(
            dimension_semantics=("arbitrary", "arbitrary", "arbitrary"),
            vmem_limit_bytes=VMEM_LIMIT),
        name=f"band_attention_d{dilation}",
    )(q, k, k, v, v)


SAMPLE_Q_ROWS = 16
NEW_KEY_ROWS = 8


def _sample_attention_kernel(n_new, q_ref, kn_ref, vn_ref, kc0, kc1, kc2, vc0, vc1, vc2,
                             o_ref, lse_ref):
    n_rows = q_ref.shape[2]
    kn_all = kn_ref[0]
    vn_all = vn_ref[0]
    contract_last = (((1,), (1,)), ((), ()))
    for gi, (kc_ref, vc_ref) in enumerate(((kc0, vc0), (kc1, vc1), (kc2, vc2))):
        dmask = ATTN_DILATIONS[gi] - 1
        n_cache = kc_ref.shape[3]
        assert n_cache == ATTN_WINDOWS[gi]
        t_c = lax.broadcasted_iota(jnp.int32, (n_rows, n_cache), 0) & (n_new - 1)
        j_c = lax.broadcasted_iota(jnp.int32, (n_rows, n_cache), 1)
        valid_c = (j_c >= t_c) & (((j_c - t_c) & dmask) == 0)
        t_n = lax.broadcasted_iota(jnp.int32, (n_rows, NEW_KEY_ROWS), 0) & (n_new - 1)
        j_n = lax.broadcasted_iota(jnp.int32, (n_rows, NEW_KEY_ROWS), 1)
        valid_n = (j_n <= t_n) & (((t_n - j_n) & dmask) == 0)
        for c in range(KV_PER_GROUP):
            kvh = gi * KV_PER_GROUP + c
            ns = slice(kvh * HEAD_DIM, (kvh + 1) * HEAD_DIM)
            q = q_ref[0, kvh]
            kt = kc_ref[0, c].astype(BF16)
            vt = vc_ref[0, c].astype(BF16)
            s_c = jnp.dot(q, kt, preferred_element_type=F32)
            s_n = lax.dot_general(q, kn_all[:, ns].astype(BF16), contract_last,
                                  preferred_element_type=F32)
            s_c = jnp.where(valid_c, s_c, NEG_BIG)
            s_n = jnp.where(valid_n, s_n, NEG_BIG)
            m = jnp.maximum(jnp.max(s_c, axis=1, keepdims=True),
                            jnp.max(s_n, axis=1, keepdims=True))
            p_c = jnp.exp(s_c - m)
            p_n = jnp.exp(s_n - m)
            l = jnp.sum(p_c, axis=1, keepdims=True) + jnp.sum(p_n, axis=1, keepdims=True)
            o = (lax.dot_general(p_c.astype(BF16), vt, contract_last, preferred_element_type=F32)
                 + jnp.dot(p_n.astype(BF16), vn_all[:, ns].astype(BF16),
                           preferred_element_type=F32))
            o_ref[0, kvh] = o / l
            lse_ref[0, kvh] = m + jnp.log(l)


def _sample_attention(q16, k_new, v_new, cache_kt, cache_vt, n_new):
    b = q16.shape[0]
    n_kv = q16.shape[1]
    assert cache_kt.shape[3] == KV_WINDOW and n_new & (n_new - 1) == 0 and n_new <= NEW_KEY_ROWS

    def cache_spec(gi):
        cols = ATTN_WINDOWS[gi]
        last = KV_WINDOW // cols - 1
        return pl.BlockSpec((1, KV_PER_GROUP, HEAD_DIM, cols), lambda bi: (bi, gi, 0, last))

    whole = lambda a: pl.BlockSpec((1,) + a.shape[1:], lambda bi: (bi,) + (0,) * (a.ndim - 1))
    o_shape = jax.ShapeDtypeStruct((b, n_kv, SAMPLE_Q_ROWS, HEAD_DIM), F32)
    lse_shape = jax.ShapeDtypeStruct((b, n_kv, SAMPLE_Q_ROWS, 1), F32)
    return pl.pallas_call(
        functools.partial(_sample_attention_kernel, n_new),
        grid=(b,),
        in_specs=[whole(q16), whole(k_new), whole(v_new)]
        + [cache_spec(gi) for gi in range(N_GROUPS)] * 2,
        out_specs=[whole(o_shape), whole(lse_shape)],
        out_shape=[o_shape, lse_shape],
        compiler_params=pltpu.CompilerParams(
            dimension_semantics=("arbitrary",), vmem_limit_bytes=VMEM_LIMIT),
        name="sample_attention",
    )(q16, k_new, v_new, cache_kt, cache_kt, cache_kt, cache_vt, cache_vt, cache_vt)


def _layer1_kernel(dilations, x_ref, o0_ref, o1_ref, o2_ref, s0_ref, s1_ref, s2_ref, g_ref,
                   wo_ref, wup_ref, wdn_ref, y_ref, o_scr, stat_scr, a_scr):
    tm = x_ref.shape[1]
    g = g_ref[...]
    x = x_ref[0]
    chunks_per_group = GROUP_Q_WIDTH // LANES
    for gi, (o_ref, s_ref) in enumerate(((o0_ref, s0_ref), (o1_ref, s1_ref), (o2_ref, s2_ref))):
        for r in range(dilations[gi]):
            rows = _subseq_rows(tm, r, dilations[gi])
            stat_scr[gi, rows, :] = s_ref[r]
            for c in range(chunks_per_group):
                o_scr[gi * chunks_per_group + c, rows, :] = (
                    o_ref[r, :, c * LANES:(c + 1) * LANES].astype(F32))
    sums = [stat_scr[gi] for gi in range(N_GROUPS)]
    maxes = [pltpu.roll(sm, STAT_LANES - STAT_MAX_OFFSET, 1) for sm in sums]
    top = jnp.maximum(jnp.maximum(maxes[0], maxes[1]), maxes[2])
    es = [jnp.exp(mx - top) for mx in maxes]
    den = sums[0] * es[0] + sums[1] * es[1] + sums[2] * es[2]
    slot_id = lax.broadcasted_iota(jnp.int32, (tm, GROUP_Q_WIDTH), 1) // HEAD_DIM
    for gi in range(N_GROUPS):
        scale = es[gi] / den
        wide = jnp.zeros((tm, GROUP_Q_WIDTH), F32)
        for slot in range(HEADS_PER_GROUP):
            stat_lane = _stat_lane(slot % KV_PER_GROUP, slot // KV_PER_GROUP)
            wide = jnp.where(slot_id == slot, scale[:, stat_lane:stat_lane + 1], wide)
        for c in range(chunks_per_group):
            ci = gi * chunks_per_group + c
            a_scr[:, ci * LANES:(ci + 1) * LANES] = (
                o_scr[ci] * wide[:, c * LANES:(c + 1) * LANES]).astype(BF16)
    mix = jnp.dot(a_scr[...], wo_ref[...], preferred_element_type=F32)
    x1 = x + _rms(mix, g[1:2])
    y_ref[0] = _mlp_residual(x1, g[2:3], g[3:4], wup_ref, wdn_ref)


def _layer1(x, os, stats, gains, w_o, w_up, w_down, tm):
    b, t, d = x.shape
    dilations = tuple(o.shape[1] for o in os)
    assert t % tm == 0 and all(tm % (16 * dil) == 0 for dil in dilations)
    tile = pl.BlockSpec((1, tm, d), lambda bi, i: (bi, i, 0))
    return pl.pallas_call(
        functools.partial(_layer1_kernel, dilations),
        grid=(b, t // tm),
        in_specs=[tile]
        + [_subseq_spec(dil, tm, GROUP_Q_WIDTH) for dil in dilations]
        + [_subseq_spec(dil, tm, STAT_LANES) for dil in dilations]
        + [_const_spec(gains.shape), _const_spec(w_o.shape),
           _const_spec(w_up.shape), _const_spec(w_down.shape)],
        out_specs=tile,
        out_shape=jax.ShapeDtypeStruct((b, t, d), F32),
        scratch_shapes=[pltpu.VMEM((Q_WIDTH // LANES, tm, LANES), F32),
                        pltpu.VMEM((N_GROUPS, tm, STAT_LANES), F32),
                        pltpu.VMEM((tm, Q_WIDTH), BF16)],
        compiler_params=pltpu.CompilerParams(
            dimension_semantics=("arbitrary", "arbitrary"), vmem_limit_bytes=VMEM_LIMIT),
        name="layer1_tail",
    )(x, *os, *stats, gains, w_o, w_up, w_down)


PROMPT_TILE = 512
ATTN_TILE = 1024


def kernel(x_prompt, x_sample, cache_pool, cache_k, cache_v, norm_gains, kv_norm_gain, w_pool,
           pool_scale, w_q, w_o, w_kv, w_up, w_down):
    depth = norm_gains.shape[0]
    assert depth == 2 and cache_pool.shape[0] == 1 and w_q.shape[0] == 1
    bp, tp, d = x_prompt.shape
    bs, ts, _ = x_sample.shape

    g0, g1 = norm_gains[0], norm_gains[1]
    g1_q = g1[0:1]
    g_kv = kv_norm_gain[None, :]
    wp = w_pool[0].astype(BF16)
    ps = pool_scale[0][None, :]
    slot_order = jnp.array(HEAD_SLOT_ORDER)
    wq = w_q[0].reshape(d, N_GROUPS, HEADS_PER_GROUP, HEAD_DIM)[:, :, slot_order]
    wq = wq.reshape(d, Q_WIDTH).astype(BF16)
    wo = w_o[0].reshape(N_GROUPS, HEADS_PER_GROUP, HEAD_DIM, d)[:, slot_order]
    wo = wo.reshape(Q_WIDTH, d).astype(BF16)
    wkv = w_kv.astype(BF16)
    wup0, wup1 = w_up[0].astype(BF16), w_up[1].astype(BF16)
    wdn0, wdn1 = w_down[0].astype(BF16), w_down[1].astype(BF16)

    keep = min(KV_WINDOW, tp)
    xp1, utail = _layer0_prompt(x_prompt, g0, wp, ps, wup0, wdn0, PROMPT_TILE)
    cos_p, sin_p = _rope_tables(jnp.arange(tp))
    qkv_p = _qkv(xp1, g1_q, g_kv, wq, wkv, cos_p, sin_p, PROMPT_TILE, keep, ATTN_DILATIONS, True)
    os_p, stats_p = [], []
    for gi, dil in enumerate(ATTN_DILATIONS):
        o, stat = _band_attention(qkv_p[gi], qkv_p[N_GROUPS + gi], qkv_p[2 * N_GROUPS + gi],
                                  min(ATTN_TILE, tp // dil))
        os_p.append(o)
        stats_p.append(stat)
    y_prompt = _layer1(xp1, os_p, stats_p, g1, wo, wup1, wdn1, PROMPT_TILE)
    pool_prompt = utail[:, POOL_HALO - POOL_BUF:][None]
    k_prompt = qkv_p[-2].reshape(bp, N_KV_HEADS, HEAD_DIM, keep).transpose(0, 3, 1, 2)
    v_prompt = qkv_p[-1].reshape(bp, N_KV_HEADS, HEAD_DIM, keep).transpose(0, 3, 1, 2)

    n_tok = ts * bs
    xs_tm = jnp.swapaxes(x_sample, 0, 1)
    buf_tm = jnp.swapaxes(cache_pool[0], 0, 1)
    xs1_tm, us_tm = _layer0_sample(xs_tm, buf_tm, PAST_LEN, g0, wp, ps, wup0, wdn0)
    xs1 = xs1_tm.reshape(1, n_tok, d)
    cos_s, sin_s = _rope_tables(PAST_LEN + jnp.arange(n_tok) // bs)
    qkv_s = _qkv(xs1, g1_q, g_kv, wq, wkv, cos_s, sin_s, n_tok, n_tok, (1,) * N_GROUPS, False)
    q_s = jnp.concatenate([q[0, 0] for q in qkv_s[:N_GROUPS]], axis=-1)
    q16 = q_s.reshape(ts, bs, N_GROUPS, Q_PER_KV, KV_PER_GROUP, HEAD_DIM)
    q16 = q16.transpose(1, 2, 4, 3, 0, 5).reshape(bs, N_KV_HEADS, Q_PER_KV, ts, HEAD_DIM)
    q16 = jnp.pad(q16, ((0, 0), (0, 0), (0, SAMPLE_Q_ROWS // ts - Q_PER_KV), (0, 0), (0, 0)))
    q16 = q16.reshape(bs, N_KV_HEADS, SAMPLE_Q_ROWS, HEAD_DIM)
    k_s = jnp.swapaxes(qkv_s[-2].reshape(ts, bs, KV_WIDTH), 0, 1)
    v_s = jnp.swapaxes(qkv_s[-1].reshape(ts, bs, KV_WIDTH), 0, 1)
    pad_new = ((0, 0), (0, NEW_KEY_ROWS - ts), (0, 0))
    o16, lse16 = _sample_attention(
        q16, jnp.pad(k_s, pad_new), jnp.pad(v_s, pad_new),
        cache_k.transpose(0, 2, 3, 1), cache_v.transpose(0, 2, 3, 1), ts)
    heads_padded = SAMPLE_Q_ROWS // ts
    o_s = o16.reshape(bs, N_GROUPS, KV_PER_GROUP, heads_padded, ts, HEAD_DIM)[:, :, :, :Q_PER_KV]
    o_s = o_s.transpose(4, 0, 1, 3, 2, 5).reshape(1, 1, n_tok, N_GROUPS, GROUP_Q_WIDTH).astype(BF16)
    lse_s = lse16.reshape(bs, N_GROUPS, KV_PER_GROUP, heads_padded, ts)[:, :, :, :Q_PER_KV]
    lse_s = lse_s.transpose(4, 0, 1, 2, 3).reshape(1, 1, n_tok, N_GROUPS, KV_PER_GROUP, Q_PER_KV)
    ones = lambda n: jnp.ones((1, 1, n_tok, N_GROUPS, n), F32)
    max_lane = [_stat_lane(c, 0) + STAT_MAX_OFFSET for c in range(KV_PER_GROUP)]
    assert max_lane[1] < max_lane[0]
    stat_s = jnp.concatenate(
        [ones(max_lane[1]), lse_s[..., 1, :],
         ones(max_lane[0] - max_lane[1] - Q_PER_KV), lse_s[..., 0, :],
         ones(STAT_LANES - max_lane[0] - Q_PER_KV)], axis=-1)
    ys_tm = _layer1(xs1, [o_s[:, :, :, gi] for gi in range(N_GROUPS)],
                    [stat_s[:, :, :, gi] for gi in range(N_GROUPS)], g1, wo, wup1, wdn1, n_tok)
    y_sample = jnp.swapaxes(ys_tm.reshape(ts, bs, d), 0, 1)
    u_s = jnp.swapaxes(us_tm, 0, 1)
    pool_sample = jnp.concatenate([cache_pool[0], u_s], axis=1)[:, -POOL_BUF:][None]
    k_sample = k_s.reshape(bs, ts, N_KV_HEADS, HEAD_DIM)
    v_sample = v_s.reshape(bs, ts, N_KV_HEADS, HEAD_DIM)

    return (y_prompt, y_sample, pool_prompt, k_prompt, v_prompt, pool_sample, k_sample, v_sample)
```

```python
import functools

import jax
import jax.numpy as jnp
from jax import lax
from jax.experimental import pallas as pl
from jax.experimental.pallas import tpu as pltpu

F32 = jnp.float32
BF16 = jnp.bfloat16

EPS = 1e-6
ROPE_THETA = 10000.0
PAST_LEN = 16384
POOL_WINDOWS = (2, 4, 8, 16)
POOL_BUF = max(POOL_WINDOWS) - 1
POOL_HALO = 16
HEAD_DIM = 64
ATTN_WINDOWS = (128, 512, 2048)
ATTN_DILATIONS = (1, 4, 16)
N_GROUPS = len(ATTN_WINDOWS)
KV_PER_GROUP = 2
Q_PER_KV = 3
HEADS_PER_GROUP = KV_PER_GROUP * Q_PER_KV
N_KV_HEADS = N_GROUPS * KV_PER_GROUP
GROUP_Q_WIDTH = HEADS_PER_GROUP * HEAD_DIM
GROUP_KV_WIDTH = KV_PER_GROUP * HEAD_DIM
Q_WIDTH = N_GROUPS * GROUP_Q_WIDTH
KV_WIDTH = N_GROUPS * GROUP_KV_WIDTH
KV_WINDOW = max(ATTN_WINDOWS)
BAND = 128
NEG_BIG = -1e30

LANES = 128
STAT_LANES = LANES
STAT_MAX_OFFSET = 8
HEAD_SLOT_ORDER = (0, 3, 1, 4, 2, 5)
VMEM_LIMIT = 56 * 1024 * 1024
FF_CHUNK = 512

for _w, _d in zip(ATTN_WINDOWS, ATTN_DILATIONS):
    assert _w // _d == BAND and _w % _d == 0
assert GROUP_KV_WIDTH == LANES


def _rms(x, g):
    return x * lax.rsqrt(jnp.mean(x * x, axis=-1, keepdims=True) + EPS) * g


def _mlp_residual(x1, g_in, g_out, wup_ref, wdn_ref):
    return _mlp_from_hidden(x1, _rms(x1, g_in).astype(BF16), g_out, wup_ref, wdn_ref)


def _mlp_from_hidden(x1, h, g_out, wup_ref, wdn_ref):
    out = []
    for _ in _mlp_steps(lambda: x1, lambda: h, g_out, wup_ref, wdn_ref, out.append):
        pass
    return out[0]


def _mlp_steps(load_x1, load_h, g_out, wup_ref, wdn_ref, emit):
    d_ff = wup_ref.shape[1]
    h = load_h()
    acc = None
    for c in range(d_ff // FF_CHUNK):
        cs = slice(c * FF_CHUNK, (c + 1) * FF_CHUNK)
        a = jnp.dot(h, wup_ref[:, cs], preferred_element_type=F32)
        a = jnp.square(jnp.maximum(a, 0.0)).astype(BF16)
        part = jnp.dot(a, wdn_ref[cs, :], preferred_element_type=F32)
        acc = part if acc is None else acc + part
        yield
    emit(load_x1() + _rms(acc, g_out))
    yield


def _interleave(*generators):
    live = list(generators)
    while live:
        for gen in list(live):
            try:
                next(gen)
            except StopIteration:
                live.remove(gen)


def _const_spec(shape):
    zeros = (0,) * len(shape)
    return pl.BlockSpec(shape, lambda *_: zeros, pipeline_mode=pl.Buffered(1))


def _subseq_spec(dilation, rows, width):
    return pl.BlockSpec((None, dilation, rows // dilation, width), lambda bi, i: (bi, 0, i, 0))


def _subseq_rows(n_rows, r, dilation):
    n = n_rows // dilation
    return pl.ds(r, n, stride=dilation) if dilation > 1 else pl.ds(0, n)


def _pool_mixer_steps(x_ref, xh_ref, tile_in_seq, g, wp_ref, ps_ref, ext_ref, x1_ref, h_ref,
                      utail_ref=None):
    tm = x_ref.shape[0]
    pool_ch = wp_ref.shape[1]
    u = _rms(x_ref[...], g[0:1])
    uh = _rms(xh_ref[0:POOL_HALO, :], g[0:1]) * jnp.where(tile_in_seq > 0, 1.0, 0.0)
    ext_ref[0:POOL_HALO, :] = uh
    ext_ref[POOL_HALO:, :] = u
    if utail_ref is not None:
        utail_ref[0] = u[tm - POOL_HALO:, :]
    yield
    pos = tile_in_seq * tm + lax.broadcasted_iota(jnp.int32, (tm, 1), 0)
    parts = []
    for gi, w in enumerate(POOL_WINDOWS):
        cs = slice(gi * pool_ch, (gi + 1) * pool_ch)
        s = ext_ref[POOL_HALO:, cs]
        for j in range(1, w):
            s = s + ext_ref[POOL_HALO - j:POOL_HALO - j + tm, cs]
        cnt = jnp.minimum(w, pos + 1).astype(F32)
        pooled = s / cnt - ext_ref[POOL_HALO:, cs]
        parts.append(jnp.dot(pooled.astype(BF16), wp_ref[gi], preferred_element_type=F32))
        yield
    mix = jnp.concatenate(parts, axis=1) * ps_ref[...]
    x1 = x_ref[...] + _rms(mix, g[1:2])
    x1_ref[...] = x1
    yield
    h_ref[...] = _rms(x1_ref[...], g[2:3]).astype(BF16)
    yield


def _layer0_first_kernel(x_ref, g_ref, wp_ref, ps_ref, x1_ref, h_ref, ext_scr):
    _interleave(_pool_mixer_steps(x_ref, x_ref, 0, g_ref[...], wp_ref, ps_ref, ext_scr,
                                  x1_ref, h_ref))


def _layer0_prompt_kernel(tiles_per_seq, xa_ref, xha_ref, xb_ref, xhb_ref, x1f_ref, hf_ref, g_ref,
                          wp_ref, ps_ref, wup_ref, wdn_ref, y_ref, utail_ref,
                          x1_scr, h_scr, ext_scr):
    s = pl.program_id(0)
    n_tiles = 2 * pl.num_programs(0)
    tm = xa_ref.shape[0]
    g = g_ref[...]

    @pl.when(s == 0)
    def _():
        x1_scr[0] = x1f_ref[...]
        h_scr[0] = hf_ref[...]

    def prepare(x_ref, xh_ref, tile, slot, tail_ref):
        return _pool_mixer_steps(x_ref, xh_ref, lax.rem(tile, tiles_per_seq), g, wp_ref, ps_ref,
                                 ext_scr.at[slot], x1_scr.at[slot], h_scr.at[slot], tail_ref)

    def mlp(slot, rows):
        def emit(y):
            y_ref[rows, :] = y
        return _mlp_steps(lambda: x1_scr[slot], lambda: h_scr[slot], g[3:4], wup_ref, wdn_ref,
                          emit)

    _interleave(mlp(0, slice(0, tm)), prepare(xa_ref, xha_ref, 2 * s + 1, 1, utail_ref))
    _interleave(mlp(1, slice(tm, 2 * tm)),
                prepare(xb_ref, xhb_ref, jnp.minimum(2 * s + 2, n_tiles - 1), 0, None))


def _layer0_prompt(x, gains, w_pool, pool_scale, w_up, w_down, tm):
    b, t, d = x.shape
    tiles_per_seq = t // tm
    n_tiles = b * tiles_per_seq
    assert t % tm == 0 and tm % POOL_HALO == 0 and tiles_per_seq % 2 == 0
    halo_per_tile = tm // POOL_HALO
    x2 = x.reshape(b * t, d)

    x1_first, h_first = pl.pallas_call(
        _layer0_first_kernel,
        grid=(1,),
        in_specs=[pl.BlockSpec((tm, d), lambda i: (0, 0)), _const_spec(gains.shape),
                  _const_spec(w_pool.shape), _const_spec(pool_scale.shape)],
        out_specs=[pl.BlockSpec((tm, d), lambda i: (0, 0))] * 2,
        out_shape=[jax.ShapeDtypeStruct((tm, d), F32), jax.ShapeDtypeStruct((tm, d), BF16)],
        scratch_shapes=[pltpu.VMEM((tm + POOL_HALO, d), F32)],
        compiler_params=pltpu.CompilerParams(vmem_limit_bytes=VMEM_LIMIT),
        name="layer0_first_tile",
    )(x2, gains, w_pool, pool_scale)

    tile_a = lambda s: 2 * s + 1
    tile_b = lambda s: jnp.minimum(2 * s + 2, n_tiles - 1)
    tile_spec = lambda tile: pl.BlockSpec((tm, d), lambda s: (tile(s), 0))
    halo_spec = lambda tile: pl.BlockSpec((POOL_HALO, d),
                                          lambda s: (tile(s) * halo_per_tile - 1, 0))
    y, utail = pl.pallas_call(
        functools.partial(_layer0_prompt_kernel, tiles_per_seq),
        grid=(n_tiles // 2,),
        in_specs=[
            tile_spec(tile_a), halo_spec(tile_a), tile_spec(tile_b), halo_spec(tile_b),
            _const_spec(x1_first.shape), _const_spec(h_first.shape),
            _const_spec(gains.shape), _const_spec(w_pool.shape), _const_spec(pool_scale.shape),
            _const_spec(w_up.shape), _const_spec(w_down.shape),
        ],
        out_specs=[
            pl.BlockSpec((2 * tm, d), lambda s: (s, 0)),
            pl.BlockSpec((1, POOL_HALO, d), lambda s: (tile_a(s) // tiles_per_seq, 0, 0)),
        ],
        out_shape=[
            jax.ShapeDtypeStruct((b * t, d), F32),
            jax.ShapeDtypeStruct((b, POOL_HALO, d), F32),
        ],
        scratch_shapes=[pltpu.VMEM((2, tm, d), F32), pltpu.VMEM((2, tm, d), BF16),
                        pltpu.VMEM((2, tm + POOL_HALO, d), F32)],
        compiler_params=pltpu.CompilerParams(
            dimension_semantics=("arbitrary",), vmem_limit_bytes=VMEM_LIMIT),
        name="layer0_prompt",
    )(x2, x2, x2, x2, x1_first, h_first, gains, w_pool, pool_scale, w_up, w_down)
    return y.reshape(b, t, d), utail


def _layer0_sample_kernel(start_pos, x_ref, buf_ref, g_ref, wp_ref, ps_ref, wup_ref, wdn_ref,
                          y_ref, u_ref):
    n_t, n_b, _ = x_ref.shape
    pool_ch = wp_ref.shape[1]
    g = g_ref[...]
    xs = [x_ref[t] for t in range(n_t)]
    us = [_rms(xt, g[0:1]) for xt in xs]
    ext = [buf_ref[j] for j in range(POOL_BUF)] + us
    parts = []
    for gi, w in enumerate(POOL_WINDOWS):
        cs = slice(gi * pool_ch, (gi + 1) * pool_ch)
        rows = []
        for t in range(n_t):
            s = us[t][:, cs]
            for j in range(1, w):
                s = s + ext[POOL_BUF + t - j][:, cs]
            cnt = float(min(w, start_pos + t + 1))
            rows.append(s / cnt - us[t][:, cs])
        pooled = jnp.concatenate(rows, axis=0)
        parts.append(jnp.dot(pooled.astype(BF16), wp_ref[gi], preferred_element_type=F32))
    mix = jnp.concatenate(parts, axis=1) * ps_ref[...]
    x = jnp.concatenate(xs, axis=0)
    x1 = x + _rms(mix, g[1:2])
    y = _mlp_residual(x1, g[2:3], g[3:4], wup_ref, wdn_ref)
    for t in range(n_t):
        y_ref[t] = y[t * n_b:(t + 1) * n_b]
        u_ref[t] = us[t]


def _layer0_sample(x_tm, buf_tm, start_pos, gains, w_pool, pool_scale, w_up, w_down):
    assert start_pos + 1 >= max(POOL_WINDOWS) and buf_tm.shape[0] == POOL_BUF
    return pl.pallas_call(
        functools.partial(_layer0_sample_kernel, start_pos),
        out_shape=[jax.ShapeDtypeStruct(x_tm.shape, F32), jax.ShapeDtypeStruct(x_tm.shape, F32)],
        compiler_params=pltpu.CompilerParams(vmem_limit_bytes=VMEM_LIMIT),
        name="layer0_sample",
    )(x_tm, buf_tm, gains, w_pool, pool_scale, w_up, w_down)


def _rope(x, cos, sin_signed):
    lane = lax.broadcasted_iota(jnp.int32, (x.shape[0], LANES), 1)
    first_half = (lane & (HEAD_DIM // 2)) == 0
    out = []
    for c in range(x.shape[1] // LANES):
        xc = x[:, c * LANES:(c + 1) * LANES]
        partner = jnp.where(first_half,
                            pltpu.roll(xc, LANES - HEAD_DIM // 2, 1),
                            pltpu.roll(xc, HEAD_DIM // 2, 1))
        out.append(xc * cos + partner * sin_signed)
    return out


def _qkv_kernel(dilations, first_kept_tile, transpose_kept,
                x_ref, gq_ref, gkv_ref, wq_ref, wkv_ref, cos_ref, sin_ref,
                q0_ref, q1_ref, q2_ref, k0_ref, k1_ref, k2_ref, v0_ref, v1_ref, v2_ref,
                kf_ref, vf_ref, stage_scr):
    i = pl.program_id(1)
    x = x_ref[0]
    cos = cos_ref[...]
    sin_signed = sin_ref[...]
    chunks_per_group = GROUP_Q_WIDTH // LANES

    def emit(out_ref, dilation, chunks):
        if dilation == 1:
            for c, chunk in enumerate(chunks):
                out_ref[0, :, c * LANES:(c + 1) * LANES] = chunk.astype(BF16)
            return
        for c, chunk in enumerate(chunks):
            stage_scr[c] = chunk
        for r in range(dilation):
            rows = _subseq_rows(stage_scr.shape[1], r, dilation)
            for c in range(len(chunks)):
                out_ref[r, :, c * LANES:(c + 1) * LANES] = stage_scr[c, rows, :].astype(BF16)

    u = _rms(x, gq_ref[...]).astype(BF16)
    q = jnp.dot(u, wq_ref[...], preferred_element_type=F32)
    scale = HEAD_DIM ** -0.5
    q_chunks = _rope(q, cos * scale, sin_signed * scale)
    for gi, q_ref in enumerate((q0_ref, q1_ref, q2_ref)):
        emit(q_ref, dilations[gi], q_chunks[gi * chunks_per_group:(gi + 1) * chunks_per_group])

    un = _rms(x, gkv_ref[...]).astype(BF16)
    kv = jnp.dot(un, wkv_ref[...], preferred_element_type=F32)
    k_chunks = _rope(kv[:, :KV_WIDTH], cos, sin_signed)
    v_chunks = [kv[:, KV_WIDTH + gi * LANES:KV_WIDTH + (gi + 1) * LANES] for gi in range(N_GROUPS)]
    for gi, (k_ref, v_ref) in enumerate(((k0_ref, v0_ref), (k1_ref, v1_ref), (k2_ref, v2_ref))):
        emit(k_ref, dilations[gi], [k_chunks[gi]])
        emit(v_ref, dilations[gi], [v_chunks[gi]])

    @pl.when(i >= first_kept_tile)
    def _():
        for gi in range(N_GROUPS):
            cs = slice(gi * LANES, (gi + 1) * LANES)
            if transpose_kept:
                kf_ref[0, cs, :] = k_chunks[gi].T
                vf_ref[0, cs, :] = v_chunks[gi].T
            else:
                kf_ref[0, :, cs] = k_chunks[gi]
                vf_ref[0, :, cs] = v_chunks[gi]


def _qkv(x, g_q, g_kv, w_q, w_kv, cos, sin_signed, tm, keep_rows, dilations, transpose_kept):
    b, t, d = x.shape
    assert t % tm == 0 and keep_rows % tm == 0 and all(tm % (16 * dil) == 0 for dil in dilations)
    first_kept_tile = (t - keep_rows) // tm
    kept_block = lambda bi, i: jnp.maximum(i - first_kept_tile, 0)
    if transpose_kept:
        kept = pl.BlockSpec((1, KV_WIDTH, tm), lambda bi, i: (bi, 0, kept_block(bi, i)))
        kept_shape = jax.ShapeDtypeStruct((b, KV_WIDTH, keep_rows), F32)
    else:
        kept = pl.BlockSpec((1, tm, KV_WIDTH), lambda bi, i: (bi, kept_block(bi, i), 0))
        kept_shape = jax.ShapeDtypeStruct((b, keep_rows, KV_WIDTH), F32)
    sub_shape = lambda dil, width: jax.ShapeDtypeStruct((b, dil, t // dil, width), BF16)
    q_specs = [_subseq_spec(dil, tm, GROUP_Q_WIDTH) for dil in dilations]
    kv_specs = [_subseq_spec(dil, tm, GROUP_KV_WIDTH) for dil in dilations]
    q_shapes = [sub_shape(dil, GROUP_Q_WIDTH) for dil in dilations]
    kv_shapes = [sub_shape(dil, GROUP_KV_WIDTH) for dil in dilations]
    return pl.pallas_call(
        functools.partial(_qkv_kernel, tuple(dilations), first_kept_tile, transpose_kept),
        grid=(b, t // tm),
        in_specs=[
            pl.BlockSpec((1, tm, d), lambda bi, i: (bi, i, 0)),
            _const_spec(g_q.shape), _const_spec(g_kv.shape),
            _const_spec(w_q.shape), _const_spec(w_kv.shape),
            pl.BlockSpec((tm, LANES), lambda bi, i: (i, 0)),
            pl.BlockSpec((tm, LANES), lambda bi, i: (i, 0)),
        ],
        out_specs=q_specs + kv_specs + kv_specs + [kept, kept],
        out_shape=q_shapes + kv_shapes + kv_shapes + [kept_shape, kept_shape],
        scratch_shapes=[pltpu.VMEM((GROUP_Q_WIDTH // LANES, tm, LANES), F32)],
        compiler_params=pltpu.CompilerParams(
            dimension_semantics=("arbitrary", "arbitrary"), vmem_limit_bytes=VMEM_LIMIT),
        name="qkv_rope",
    )(x, g_q, g_kv, w_q, w_kv, cos, sin_signed)


def _rope_tables(positions):
    inv = ROPE_THETA ** (-jnp.arange(0, HEAD_DIM, 2, dtype=F32) / HEAD_DIM)
    ang = positions.astype(F32)[:, None] * inv[None, :]
    cos, sin = jnp.cos(ang), jnp.sin(ang)
    heads_per_period = LANES // HEAD_DIM
    return (jnp.tile(cos, (1, 2 * heads_per_period)),
            jnp.tile(jnp.concatenate([-sin, sin], axis=1), (1, heads_per_period)))


def _stat_lane(c, k):
    return (1 - c) * HEAD_DIM + k


def _band_attention_kernel(q_ref, kp_ref, kc_ref, vp_ref, vc_ref, o_ref, stat_ref,
                           k_scr, v_scr, bias_scr, s_scr, p_scr):
    tq = q_ref.shape[0]
    n_bands = tq // BAND
    i = pl.program_id(2)
    lane = lax.broadcasted_iota(jnp.int32, (1, LANES), 1)
    for c in range(KV_PER_GROUP):
        own = (lane // HEAD_DIM) == c
        for lo, k_src, v_src in ((0, kp_ref, vp_ref), (BAND, kc_ref, vc_ref)):
            rows = slice(lo, lo + k_src.shape[0])
            k_scr[c, rows, :] = jnp.where(own, k_src[...], jnp.zeros((), BF16))
            v_scr[c, rows, :] = jnp.where(own, v_src[...], jnp.ones((), BF16))
    stat_ref[...] = jnp.ones(stat_ref.shape, F32)

    row = lax.broadcasted_iota(jnp.int32, (BAND, 2 * BAND), 0)
    col = lax.broadcasted_iota(jnp.int32, (BAND, 2 * BAND), 1)
    band_bias = jnp.where((col >= row) & (col <= row + BAND), 0.0, NEG_BIG)
    bias_scr[0] = band_bias
    bias_scr[1] = band_bias + jnp.where(col < BAND, NEG_BIG, 0.0)

    def start_of(j):
        return j * BAND if isinstance(j, int) else pl.multiple_of(j * BAND, BAND)

    def scores(j, slot):
        q = jnp.concatenate(
            [q_ref[pl.ds(start_of(j), BAND), k * LANES:(k + 1) * LANES] for k in range(Q_PER_KV)],
            axis=0)
        for c in range(KV_PER_GROUP):
            keys = k_scr[c, pl.ds(start_of(j), 2 * BAND), :]
            s_scr[slot, c] = lax.dot_general(q, keys, (((1,), (1,)), ((), ())),
                                             preferred_element_type=F32)

    def softmax(j, slot):
        first = jnp.logical_and(i == 0, j == 0).astype(jnp.int32)
        bias = bias_scr[first]
        bias = jnp.concatenate([bias] * Q_PER_KV, axis=0)
        for c in range(KV_PER_GROUP):
            s = s_scr[slot, c] + bias
            m = jnp.max(s, axis=1, keepdims=True)
            p_scr[slot, c] = jnp.exp(s - m).astype(BF16)
            for k in range(Q_PER_KV):
                stat_ref[pl.ds(start_of(j), BAND), pl.ds(_stat_lane(c, k) + STAT_MAX_OFFSET, 1)] = (
                    m[k * BAND:(k + 1) * BAND])

    def weighted_values(j, slot):
        rows = pl.ds(start_of(j), BAND)
        ov = [jnp.dot(p_scr[slot, c], v_scr[c, pl.ds(start_of(j), 2 * BAND), :],
                      preferred_element_type=F32) for c in range(KV_PER_GROUP)]
        low = lane < HEAD_DIM
        sums = None
        for k in range(Q_PER_KV):
            part = slice(k * BAND, (k + 1) * BAND)
            o_ref[rows, k * LANES:(k + 1) * LANES] = (
                jnp.where(low, ov[0][part], ov[1][part]).astype(o_ref.dtype))
            sums_k = jnp.where(low, ov[1][part], ov[0][part])
            sums = sums_k if sums is None else jnp.where((lane % HEAD_DIM) == k, sums_k, sums)
        for c in range(KV_PER_GROUP):
            lanes = pl.ds(_stat_lane(c, 0), Q_PER_KV)
            stat_ref[rows, lanes] = sums[:, _stat_lane(c, 0):_stat_lane(c, 0) + Q_PER_KV]

    scores(0, 0)
    softmax(0, 0)
    scores(1, 1)

    def steady(j, carry):
        slot = lax.rem(j, 2)
        weighted_values(j - 1, 1 - slot)
        softmax(j, slot)
        scores(j + 1, 1 - slot)
        return carry

    lax.fori_loop(1, n_bands - 1, steady, 0)
    last = n_bands - 1
    weighted_values(last - 1, (last - 1) % 2)
    softmax(last, last % 2)
    weighted_values(last, last % 2)


def _band_attention(q, k, v, tq):
    b, dilation, sub, _ = q.shape
    assert sub % tq == 0 and tq % BAND == 0 and tq >= 2 * BAND
    bands_per_tile = tq // BAND
    cur = lambda width: pl.BlockSpec((None, None, tq, width), lambda bi, r, i: (bi, r, i, 0))
    prev = pl.BlockSpec((None, None, BAND, GROUP_KV_WIDTH),
                        lambda bi, r, i: (bi, r, jnp.maximum(i * bands_per_tile - 1, 0), 0))
    return pl.pallas_call(
        _band_attention_kernel,
        grid=(b, dilation, sub // tq),
        in_specs=[cur(GROUP_Q_WIDTH), prev, cur(GROUP_KV_WIDTH), prev, cur(GROUP_KV_WIDTH)],
        out_specs=[cur(GROUP_Q_WIDTH), cur(STAT_LANES)],
        out_shape=[jax.ShapeDtypeStruct(q.shape, BF16),
                   jax.ShapeDtypeStruct((b, dilation, sub, STAT_LANES), F32)],
        scratch_shapes=[pltpu.VMEM((KV_PER_GROUP, tq + BAND, GROUP_KV_WIDTH), BF16),
                        pltpu.VMEM((KV_PER_GROUP, tq + BAND, GROUP_KV_WIDTH), BF16),
                        pltpu.VMEM((2, BAND, 2 * BAND), F32),
                        pltpu.VMEM((2, KV_PER_GROUP, Q_PER_KV * BAND, 2 * BAND), F32),
                        pltpu.VMEM((2, KV_PER_GROUP, Q_PER_KV * BAND, 2 * BAND), BF16)],
        compiler_params=pltpu.CompilerParams(
            dimension_semantics=("arbitrary", "arbitrary", "arbitrary"),
            vmem_limit_bytes=VMEM_LIMIT),
        name=f"band_attention_d{dilation}",
    )(q, k, k, v, v)


SAMPLE_Q_ROWS = 16
NEW_KEY_ROWS = 8


def _sample_attention_kernel(n_new, q_ref, kn_ref, vn_ref, kc0, kc1, kc2, vc0, vc1, vc2,
                             o_ref, lse_ref):
    n_rows = q_ref.shape[2]
    kn_all = kn_ref[0]
    vn_all = vn_ref[0]
    contract_last = (((1,), (1,)), ((), ()))
    for gi, (kc_ref, vc_ref) in enumerate(((kc0, vc0), (kc1, vc1), (kc2, vc2))):
        dmask = ATTN_DILATIONS[gi] - 1
        n_cache = kc_ref.shape[3]
        assert n_cache == ATTN_WINDOWS[gi]
        t_c = lax.broadcasted_iota(jnp.int32, (n_rows, n_cache), 0) & (n_new - 1)
        j_c = lax.broadcasted_iota(jnp.int32, (n_rows, n_cache), 1)
        valid_c = (j_c >= t_c) & (((j_c - t_c) & dmask) == 0)
        t_n = lax.broadcasted_iota(jnp.int32, (n_rows, NEW_KEY_ROWS), 0) & (n_new - 1)
        j_n = lax.broadcasted_iota(jnp.int32, (n_rows, NEW_KEY_ROWS), 1)
        valid_n = (j_n <= t_n) & (((t_n - j_n) & dmask) == 0)
        for c in range(KV_PER_GROUP):
            kvh = gi * KV_PER_GROUP + c
            ns = slice(kvh * HEAD_DIM, (kvh + 1) * HEAD_DIM)
            q = q_ref[0, kvh]
            kt = kc_ref[0, c].astype(BF16)
            vt = vc_ref[0, c].astype(BF16)
            s_c = jnp.dot(q, kt, preferred_element_type=F32)
            s_n = lax.dot_general(q, kn_all[:, ns].astype(BF16), contract_last,
                                  preferred_element_type=F32)
            s_c = jnp.where(valid_c, s_c, NEG_BIG)
            s_n = jnp.where(valid_n, s_n, NEG_BIG)
            m = jnp.maximum(jnp.max(s_c, axis=1, keepdims=True),
                            jnp.max(s_n, axis=1, keepdims=True))
            p_c = jnp.exp(s_c - m)
            p_n = jnp.exp(s_n - m)
            l = jnp.sum(p_c, axis=1, keepdims=True) + jnp.sum(p_n, axis=1, keepdims=True)
            o = (lax.dot_general(p_c.astype(BF16), vt, contract_last, preferred_element_type=F32)
                 + jnp.dot(p_n.astype(BF16), vn_all[:, ns].astype(BF16),
                           preferred_element_type=F32))
            o_ref[0, kvh] = o / l
            lse_ref[0, kvh] = m + jnp.log(l)


def _sample_attention(q16, k_new, v_new, cache_kt, cache_vt, n_new):
    b = q16.shape[0]
    n_kv = q16.shape[1]
    assert cache_kt.shape[3] == KV_WINDOW and n_new & (n_new - 1) == 0 and n_new <= NEW_KEY_ROWS

    def cache_spec(gi):
        cols = ATTN_WINDOWS[gi]
        last = KV_WINDOW // cols - 1
        return pl.BlockSpec((1, KV_PER_GROUP, HEAD_DIM, cols), lambda bi: (bi, gi, 0, last))

    whole = lambda a: pl.BlockSpec((1,) + a.shape[1:], lambda bi: (bi,) + (0,) * (a.ndim - 1))
    o_shape = jax.ShapeDtypeStruct((b, n_kv, SAMPLE_Q_ROWS, HEAD_DIM), F32)
    lse_shape = jax.ShapeDtypeStruct((b, n_kv, SAMPLE_Q_ROWS, 1), F32)
    return pl.pallas_call(
        functools.partial(_sample_attention_kernel, n_new),
        grid=(b,),
        in_specs=[whole(q16), whole(k_new), whole(v_new)]
        + [cache_spec(gi) for gi in range(N_GROUPS)] * 2,
        out_specs=[whole(o_shape), whole(lse_shape)],
        out_shape=[o_shape, lse_shape],
        compiler_params=pltpu.CompilerParams(
            dimension_semantics=("arbitrary",), vmem_limit_bytes=VMEM_LIMIT),
        name="sample_attention",
    )(q16, k_new, v_new, cache_kt, cache_kt, cache_kt, cache_vt, cache_vt, cache_vt)


def _attn_mix_steps(dilations, x_ref, o_refs, s_refs, g, wo_ref, o_scr, stat_scr, a_scr,
                    x1_ref, h_ref):
    tm = x_ref.shape[0]
    chunks_per_group = GROUP_Q_WIDTH // LANES
    for gi, (o_ref, s_ref) in enumerate(zip(o_refs, s_refs)):
        for r in range(dilations[gi]):
            rows = _subseq_rows(tm, r, dilations[gi])
            stat_scr[gi, rows, :] = s_ref[r]
            for c in range(chunks_per_group):
                o_scr[gi * chunks_per_group + c, rows, :] = (
                    o_ref[r, :, c * LANES:(c + 1) * LANES].astype(F32))
        yield
    sums = [stat_scr[gi] for gi in range(N_GROUPS)]
    maxes = [pltpu.roll(sm, STAT_LANES - STAT_MAX_OFFSET, 1) for sm in sums]
    top = jnp.maximum(jnp.maximum(maxes[0], maxes[1]), maxes[2])
    es = [jnp.exp(mx - top) for mx in maxes]
    den = sums[0] * es[0] + sums[1] * es[1] + sums[2] * es[2]
    slot_id = lax.broadcasted_iota(jnp.int32, (tm, GROUP_Q_WIDTH), 1) // HEAD_DIM
    for gi in range(N_GROUPS):
        scale = es[gi] / den
        wide = jnp.zeros((tm, GROUP_Q_WIDTH), F32)
        for slot in range(HEADS_PER_GROUP):
            stat_lane = _stat_lane(slot % KV_PER_GROUP, slot // KV_PER_GROUP)
            wide = jnp.where(slot_id == slot, scale[:, stat_lane:stat_lane + 1], wide)
        for c in range(chunks_per_group):
            ci = gi * chunks_per_group + c
            a_scr[:, ci * LANES:(ci + 1) * LANES] = (
                o_scr[ci] * wide[:, c * LANES:(c + 1) * LANES]).astype(BF16)
        yield
    mix = jnp.dot(a_scr[...], wo_ref[...], preferred_element_type=F32)
    x1_ref[...] = x_ref[...] + _rms(mix, g[1:2])
    yield
    h_ref[...] = _rms(x1_ref[...], g[2:3]).astype(BF16)
    yield


def _layer1_scratch(tm, d, slots):
    lead = (slots,) if slots else ()
    return [pltpu.VMEM((Q_WIDTH // LANES, tm, LANES), F32),
            pltpu.VMEM((N_GROUPS, tm, STAT_LANES), F32),
            pltpu.VMEM((tm, Q_WIDTH), BF16),
            pltpu.VMEM(lead + (tm, d), F32),
            pltpu.VMEM(lead + (tm, d), BF16)]


def _layer1_single_kernel(dilations, x_ref, o0_ref, o1_ref, o2_ref, s0_ref, s1_ref, s2_ref, g_ref,
                          wo_ref, wup_ref, wdn_ref, y_ref, o_scr, stat_scr, a_scr, x1_scr, h_scr):
    g = g_ref[...]
    _interleave(_attn_mix_steps(dilations, x_ref, (o0_ref, o1_ref, o2_ref),
                                (s0_ref, s1_ref, s2_ref), g, wo_ref, o_scr, stat_scr, a_scr,
                                x1_scr, h_scr))
    y_ref[...] = _mlp_from_hidden(x1_scr[...], h_scr[...], g[3:4], wup_ref, wdn_ref)


def _layer1_first_kernel(dilations, x_ref, o0_ref, o1_ref, o2_ref, s0_ref, s1_ref, s2_ref, g_ref,
                         wo_ref, x1_ref, h_ref, o_scr, stat_scr, a_scr):
    _interleave(_attn_mix_steps(dilations, x_ref, (o0_ref, o1_ref, o2_ref),
                                (s0_ref, s1_ref, s2_ref), g_ref[...], wo_ref, o_scr, stat_scr,
                                a_scr, x1_ref, h_ref))


def _layer1_prompt_kernel(dilations, xa_ref, oa0, oa1, oa2, sa0, sa1, sa2,
                          xb_ref, ob0, ob1, ob2, sb0, sb1, sb2, x1f_ref, hf_ref, g_ref, wo_ref,
                          wup_ref, wdn_ref, y_ref, o_scr, stat_scr, a_scr, x1_scr, h_scr):
    s = pl.program_id(0)
    tm = xa_ref.shape[0]
    g = g_ref[...]

    @pl.when(s == 0)
    def _():
        x1_scr[0] = x1f_ref[...]
        h_scr[0] = hf_ref[...]

    def prepare(x_ref, o_refs, s_refs, slot):
        return _attn_mix_steps(dilations, x_ref, o_refs, s_refs, g, wo_ref, o_scr, stat_scr,
                               a_scr, x1_scr.at[slot], h_scr.at[slot])

    def mlp(slot, rows):
        def emit(y):
            y_ref[rows, :] = y
        return _mlp_steps(lambda: x1_scr[slot], lambda: h_scr[slot], g[3:4], wup_ref, wdn_ref,
                          emit)

    _interleave(mlp(0, slice(0, tm)), prepare(xa_ref, (oa0, oa1, oa2), (sa0, sa1, sa2), 1))
    _interleave(mlp(1, slice(tm, 2 * tm)), prepare(xb_ref, (ob0, ob1, ob2), (sb0, sb1, sb2), 0))


def _layer1(x, os, stats, gains, w_o, w_up, w_down, tm):
    b, t, d = x.shape
    dilations = tuple(o.shape[1] for o in os)
    tiles_per_seq = t // tm
    n_tiles = b * tiles_per_seq
    assert t % tm == 0 and all(tm % (16 * dil) == 0 for dil in dilations)
    x2 = x.reshape(b * t, d)
    weights = (gains, w_o, w_up, w_down)

    def tile_specs(tile):
        sub = lambda dil, width: pl.BlockSpec(
            (None, dil, tm // dil, width),
            lambda s: (tile(s) // tiles_per_seq, 0, tile(s) % tiles_per_seq, 0))
        return ([pl.BlockSpec((tm, d), lambda s: (tile(s), 0))]
                + [sub(dil, GROUP_Q_WIDTH) for dil in dilations]
                + [sub(dil, STAT_LANES) for dil in dilations])

    if n_tiles == 1:
        y = pl.pallas_call(
            functools.partial(_layer1_single_kernel, dilations),
            grid=(1,),
            in_specs=tile_specs(lambda s: 0) + [_const_spec(w.shape) for w in weights],
            out_specs=pl.BlockSpec((tm, d), lambda s: (0, 0)),
            out_shape=jax.ShapeDtypeStruct((b * t, d), F32),
            scratch_shapes=_layer1_scratch(tm, d, 0),
            compiler_params=pltpu.CompilerParams(vmem_limit_bytes=VMEM_LIMIT),
            name="layer1_single_tile",
        )(x2, *os, *stats, *weights)
        return y.reshape(b, t, d)

    assert n_tiles % 2 == 0
    x1_first, h_first = pl.pallas_call(
        functools.partial(_layer1_first_kernel, dilations),
        grid=(1,),
        in_specs=tile_specs(lambda s: 0) + [_const_spec(gains.shape), _const_spec(w_o.shape)],
        out_specs=[pl.BlockSpec((tm, d), lambda s: (0, 0))] * 2,
        out_shape=[jax.ShapeDtypeStruct((tm, d), F32), jax.ShapeDtypeStruct((tm, d), BF16)],
        scratch_shapes=_layer1_scratch(tm, d, 0)[:3],
        compiler_params=pltpu.CompilerParams(vmem_limit_bytes=VMEM_LIMIT),
        name="layer1_first_tile",
    )(x2, *os, *stats, gains, w_o)

    tile_a = lambda s: 2 * s + 1
    tile_b = lambda s: jnp.minimum(2 * s + 2, n_tiles - 1)
    y = pl.pallas_call(
        functools.partial(_layer1_prompt_kernel, dilations),
        grid=(n_tiles // 2,),
        in_specs=tile_specs(tile_a) + tile_specs(tile_b)
        + [_const_spec(x1_first.shape), _const_spec(h_first.shape)]
        + [_const_spec(w.shape) for w in weights],
        out_specs=pl.BlockSpec((2 * tm, d), lambda s: (s, 0)),
        out_shape=jax.ShapeDtypeStruct((b * t, d), F32),
        scratch_shapes=_layer1_scratch(tm, d, 2),
        compiler_params=pltpu.CompilerParams(
            dimension_semantics=("arbitrary",), vmem_limit_bytes=VMEM_LIMIT),
        name="layer1_prompt",
    )(x2, *os, *stats, x2, *os, *stats, x1_first, h_first, *weights)
    return y.reshape(b, t, d)


PROMPT_TILE = 512
LAYER1_TILE = 256
ATTN_TILE = 1024


def kernel(x_prompt, x_sample, cache_pool, cache_k, cache_v, norm_gains, kv_norm_gain, w_pool,
           pool_scale, w_q, w_o, w_kv, w_up, w_down):
    depth = norm_gains.shape[0]
    assert depth == 2 and cache_pool.shape[0] == 1 and w_q.shape[0] == 1
    bp, tp, d = x_prompt.shape
    bs, ts, _ = x_sample.shape

    g0, g1 = norm_gains[0], norm_gains[1]
    g1_q = g1[0:1]
    g_kv = kv_norm_gain[None, :]
    wp = w_pool[0].astype(BF16)
    ps = pool_scale[0][None, :]
    slot_order = jnp.array(HEAD_SLOT_ORDER)
    wq = w_q[0].reshape(d, N_GROUPS, HEADS_PER_GROUP, HEAD_DIM)[:, :, slot_order]
    wq = wq.reshape(d, Q_WIDTH).astype(BF16)
    wo = w_o[0].reshape(N_GROUPS, HEADS_PER_GROUP, HEAD_DIM, d)[:, slot_order]
    wo = wo.reshape(Q_WIDTH, d).astype(BF16)
    wkv = w_kv.astype(BF16)
    wup0, wup1 = w_up[0].astype(BF16), w_up[1].astype(BF16)
    wdn0, wdn1 = w_down[0].astype(BF16), w_down[1].astype(BF16)

    keep = min(KV_WINDOW, tp)
    xp1, utail = _layer0_prompt(x_prompt, g0, wp, ps, wup0, wdn0, PROMPT_TILE)
    cos_p, sin_p = _rope_tables(jnp.arange(tp))
    qkv_p = _qkv(xp1, g1_q, g_kv, wq, wkv, cos_p, sin_p, PROMPT_TILE, keep, ATTN_DILATIONS, True)
    os_p, stats_p = [], []
    for gi, dil in enumerate(ATTN_DILATIONS):
        o, stat = _band_attention(qkv_p[gi], qkv_p[N_GROUPS + gi], qkv_p[2 * N_GROUPS + gi],
                                  min(ATTN_TILE, tp // dil))
        os_p.append(o)
        stats_p.append(stat)
    y_prompt = _layer1(xp1, os_p, stats_p, g1, wo, wup1, wdn1, LAYER1_TILE)
    pool_prompt = utail[:, POOL_HALO - POOL_BUF:][None]
    k_prompt = qkv_p[-2].reshape(bp, N_KV_HEADS, HEAD_DIM, keep).transpose(0, 3, 1, 2)
    v_prompt = qkv_p[-1].reshape(bp, N_KV_HEADS, HEAD_DIM, keep).transpose(0, 3, 1, 2)

    n_tok = ts * bs
    xs_tm = jnp.swapaxes(x_sample, 0, 1)
    buf_tm = jnp.swapaxes(cache_pool[0], 0, 1)
    xs1_tm, us_tm = _layer0_sample(xs_tm, buf_tm, PAST_LEN, g0, wp, ps, wup0, wdn0)
    xs1 = xs1_tm.reshape(1, n_tok, d)
    cos_s, sin_s = _rope_tables(PAST_LEN + jnp.arange(n_tok) // bs)
    qkv_s = _qkv(xs1, g1_q, g_kv, wq, wkv, cos_s, sin_s, n_tok, n_tok, (1,) * N_GROUPS, False)
    q_s = jnp.concatenate([q[0, 0] for q in qkv_s[:N_GROUPS]], axis=-1)
    q16 = q_s.reshape(ts, bs, N_GROUPS, Q_PER_KV, KV_PER_GROUP, HEAD_DIM)
    q16 = q16.transpose(1, 2, 4, 3, 0, 5).reshape(bs, N_KV_HEADS, Q_PER_KV, ts, HEAD_DIM)
    q16 = jnp.pad(q16, ((0, 0), (0, 0), (0, SAMPLE_Q_ROWS // ts - Q_PER_KV), (0, 0), (0, 0)))
    q16 = q16.reshape(bs, N_KV_HEADS, SAMPLE_Q_ROWS, HEAD_DIM)
    k_s = jnp.swapaxes(qkv_s[-2].reshape(ts, bs, KV_WIDTH), 0, 1)
    v_s = jnp.swapaxes(qkv_s[-1].reshape(ts, bs, KV_WIDTH), 0, 1)
    pad_new = ((0, 0), (0, NEW_KEY_ROWS - ts), (0, 0))
    o16, lse16 = _sample_attention(
        q16, jnp.pad(k_s, pad_new), jnp.pad(v_s, pad_new),
        cache_k.transpose(0, 2, 3, 1), cache_v.transpose(0, 2, 3, 1), ts)
    heads_padded = SAMPLE_Q_ROWS // ts
    o_s = o16.reshape(bs, N_GROUPS, KV_PER_GROUP, heads_padded, ts, HEAD_DIM)[:, :, :, :Q_PER_KV]
    o_s = o_s.transpose(4, 0, 1, 3, 2, 5).reshape(1, 1, n_tok, N_GROUPS, GROUP_Q_WIDTH).astype(BF16)
    lse_s = lse16.reshape(bs, N_GROUPS, KV_PER_GROUP, heads_padded, ts)[:, :, :, :Q_PER_KV]
    lse_s = lse_s.transpose(4, 0, 1, 2, 3).reshape(1, 1, n_tok, N_GROUPS, KV_PER_GROUP, Q_PER_KV)
    ones = lambda n: jnp.ones((1, 1, n_tok, N_GROUPS, n), F32)
    max_lane = [_stat_lane(c, 0) + STAT_MAX_OFFSET for c in range(KV_PER_GROUP)]
    assert max_lane[1] < max_lane[0]
    stat_s = jnp.concatenate(
        [ones(max_lane[1]), lse_s[..., 1, :],
         ones(max_lane[0] - max_lane[1] - Q_PER_KV), lse_s[..., 0, :],
         ones(STAT_LANES - max_lane[0] - Q_PER_KV)], axis=-1)
    ys_tm = _layer1(xs1, [o_s[:, :, :, gi] for gi in range(N_GROUPS)],
                    [stat_s[:, :, :, gi] for gi in range(N_GROUPS)], g1, wo, wup1, wdn1, n_tok)
    y_sample = jnp.swapaxes(ys_tm.reshape(ts, bs, d), 0, 1)
    u_s = jnp.swapaxes(us_tm, 0, 1)
    pool_sample = jnp.concatenate([cache_pool[0], u_s], axis=1)[:, -POOL_BUF:][None]
    k_sample = k_s.reshape(bs, ts, N_KV_HEADS, HEAD_DIM)
    v_sample = v_s.reshape(bs, ts, N_KV_HEADS, HEAD_DIM)

    return (y_prompt, y_sample, pool_prompt, k_prompt, v_prompt, pool_sample, k_sample, v_sample)
```

```python
import functools

import jax
import jax.numpy as jnp
from jax import lax
from jax.experimental import pallas as pl
from jax.experimental.pallas import tpu as pltpu

F32 = jnp.float32
BF16 = jnp.bfloat16

EPS = 1e-6
ROPE_THETA = 10000.0
PAST_LEN = 16384
POOL_WINDOWS = (2, 4, 8, 16)
POOL_BUF = max(POOL_WINDOWS) - 1
POOL_HALO = 16
HEAD_DIM = 64
ATTN_WINDOWS = (128, 512, 2048)
ATTN_DILATIONS = (1, 4, 16)
N_GROUPS = len(ATTN_WINDOWS)
KV_PER_GROUP = 2
Q_PER_KV = 3
HEADS_PER_GROUP = KV_PER_GROUP * Q_PER_KV
N_KV_HEADS = N_GROUPS * KV_PER_GROUP
GROUP_Q_WIDTH = HEADS_PER_GROUP * HEAD_DIM
GROUP_KV_WIDTH = KV_PER_GROUP * HEAD_DIM
Q_WIDTH = N_GROUPS * GROUP_Q_WIDTH
KV_WIDTH = N_GROUPS * GROUP_KV_WIDTH
KV_WINDOW = max(ATTN_WINDOWS)
BAND = 128
NEG_BIG = -1e30

LANES = 128
STAT_LANES = LANES
STAT_MAX_OFFSET = 8
HEAD_SLOT_ORDER = (0, 3, 1, 4, 2, 5)
VMEM_LIMIT = 56 * 1024 * 1024
FF_CHUNK = 512

for _w, _d in zip(ATTN_WINDOWS, ATTN_DILATIONS):
    assert _w // _d == BAND and _w % _d == 0
assert GROUP_KV_WIDTH == LANES


def _rms(x, g):
    return x * lax.rsqrt(jnp.mean(x * x, axis=-1, keepdims=True) + EPS) * g


def _mlp_residual(x1, g_in, g_out, wup_ref, wdn_ref):
    return _mlp_from_hidden(x1, _rms(x1, g_in).astype(BF16), g_out, wup_ref, wdn_ref)


def _mlp_from_hidden(x1, h, g_out, wup_ref, wdn_ref):
    out = []
    for _ in _mlp_steps(lambda: x1, lambda: h, g_out, wup_ref, wdn_ref, out.append):
        pass
    return out[0]


def _mlp_steps(load_x1, load_h, g_out, wup_ref, wdn_ref, emit):
    d_ff = wup_ref.shape[1]
    h = load_h()
    acc = None
    for c in range(d_ff // FF_CHUNK):
        cs = slice(c * FF_CHUNK, (c + 1) * FF_CHUNK)
        a = jnp.dot(h, wup_ref[:, cs], preferred_element_type=F32)
        a = jnp.square(jnp.maximum(a, 0.0)).astype(BF16)
        part = jnp.dot(a, wdn_ref[cs, :], preferred_element_type=F32)
        acc = part if acc is None else acc + part
        yield
    emit(load_x1() + _rms(acc, g_out))
    yield


def _interleave(*generators):
    live = list(generators)
    while live:
        for gen in list(live):
            try:
                next(gen)
            except StopIteration:
                live.remove(gen)


def _const_spec(shape):
    zeros = (0,) * len(shape)
    return pl.BlockSpec(shape, lambda *_: zeros, pipeline_mode=pl.Buffered(1))


def _subseq_spec(dilation, rows, width):
    return pl.BlockSpec((None, dilation, rows // dilation, width), lambda bi, i: (bi, 0, i, 0))


def _subseq_rows(n_rows, r, dilation):
    n = n_rows // dilation
    return pl.ds(r, n, stride=dilation) if dilation > 1 else pl.ds(0, n)


def _pool_mixer_steps(x_ref, xh_ref, tile_in_seq, g, wp_ref, ps_ref, ext_ref, x1_ref, h_ref,
                      utail_ref=None):
    tm = x_ref.shape[0]
    pool_ch = wp_ref.shape[1]
    u = _rms(x_ref[...], g[0:1])
    uh = _rms(xh_ref[0:POOL_HALO, :], g[0:1]) * jnp.where(tile_in_seq > 0, 1.0, 0.0)
    ext_ref[0:POOL_HALO, :] = uh
    ext_ref[POOL_HALO:, :] = u
    if utail_ref is not None:
        utail_ref[0] = u[tm - POOL_HALO:, :]
    yield
    pos = tile_in_seq * tm + lax.broadcasted_iota(jnp.int32, (tm, 1), 0)
    parts = []
    for gi, w in enumerate(POOL_WINDOWS):
        cs = slice(gi * pool_ch, (gi + 1) * pool_ch)
        s = ext_ref[POOL_HALO:, cs]
        for j in range(1, w):
            s = s + ext_ref[POOL_HALO - j:POOL_HALO - j + tm, cs]
        cnt = jnp.minimum(w, pos + 1).astype(F32)
        pooled = s / cnt - ext_ref[POOL_HALO:, cs]
        parts.append(jnp.dot(pooled.astype(BF16), wp_ref[gi], preferred_element_type=F32))
        yield
    mix = jnp.concatenate(parts, axis=1) * ps_ref[...]
    x1 = x_ref[...] + _rms(mix, g[1:2])
    x1_ref[...] = x1
    yield
    h_ref[...] = _rms(x1_ref[...], g[2:3]).astype(BF16)
    yield


def _layer0_first_kernel(x_ref, g_ref, wp_ref, ps_ref, x1_ref, h_ref, ext_scr):
    _interleave(_pool_mixer_steps(x_ref, x_ref, 0, g_ref[...], wp_ref, ps_ref, ext_scr,
                                  x1_ref, h_ref))


def _layer0_prompt_kernel(tiles_per_seq, xa_ref, xha_ref, xb_ref, xhb_ref, x1f_ref, hf_ref, g_ref,
                          wp_ref, ps_ref, wup_ref, wdn_ref, y_ref, utail_ref,
                          x1_scr, h_scr, ext_scr):
    s = pl.program_id(0)
    n_tiles = 2 * pl.num_programs(0)
    tm = xa_ref.shape[0]
    g = g_ref[...]

    @pl.when(s == 0)
    def _():
        x1_scr[0] = x1f_ref[...]
        h_scr[0] = hf_ref[...]

    def prepare(x_ref, xh_ref, tile, slot, tail_ref):
        return _pool_mixer_steps(x_ref, xh_ref, lax.rem(tile, tiles_per_seq), g, wp_ref, ps_ref,
                                 ext_scr.at[slot], x1_scr.at[slot], h_scr.at[slot], tail_ref)

    def mlp(slot, rows):
        def emit(y):
            y_ref[rows, :] = y
        return _mlp_steps(lambda: x1_scr[slot], lambda: h_scr[slot], g[3:4], wup_ref, wdn_ref,
                          emit)

    _interleave(mlp(0, slice(0, tm)), prepare(xa_ref, xha_ref, 2 * s + 1, 1, utail_ref))
    _interleave(mlp(1, slice(tm, 2 * tm)),
                prepare(xb_ref, xhb_ref, jnp.minimum(2 * s + 2, n_tiles - 1), 0, None))


def _layer0_prompt(x, gains, w_pool, pool_scale, w_up, w_down, tm):
    b, t, d = x.shape
    tiles_per_seq = t // tm
    n_tiles = b * tiles_per_seq
    assert t % tm == 0 and tm % POOL_HALO == 0 and tiles_per_seq % 2 == 0
    halo_per_tile = tm // POOL_HALO
    x2 = x.reshape(b * t, d)

    x1_first, h_first = pl.pallas_call(
        _layer0_first_kernel,
        grid=(1,),
        in_specs=[pl.BlockSpec((tm, d), lambda i: (0, 0)), _const_spec(gains.shape),
                  _const_spec(w_pool.shape), _const_spec(pool_scale.shape)],
        out_specs=[pl.BlockSpec((tm, d), lambda i: (0, 0))] * 2,
        out_shape=[jax.ShapeDtypeStruct((tm, d), F32), jax.ShapeDtypeStruct((tm, d), BF16)],
        scratch_shapes=[pltpu.VMEM((tm + POOL_HALO, d), F32)],
        compiler_params=pltpu.CompilerParams(vmem_limit_bytes=VMEM_LIMIT),
        name="layer0_first_tile",
    )(x2, gains, w_pool, pool_scale)

    tile_a = lambda s: 2 * s + 1
    tile_b = lambda s: jnp.minimum(2 * s + 2, n_tiles - 1)
    tile_spec = lambda tile: pl.BlockSpec((tm, d), lambda s: (tile(s), 0))
    halo_spec = lambda tile: pl.BlockSpec((POOL_HALO, d),
                                          lambda s: (tile(s) * halo_per_tile - 1, 0))
    y, utail = pl.pallas_call(
        functools.partial(_layer0_prompt_kernel, tiles_per_seq),
        grid=(n_tiles // 2,),
        in_specs=[
            tile_spec(tile_a), halo_spec(tile_a), tile_spec(tile_b), halo_spec(tile_b),
            _const_spec(x1_first.shape), _const_spec(h_first.shape),
            _const_spec(gains.shape), _const_spec(w_pool.shape), _const_spec(pool_scale.shape),
            _const_spec(w_up.shape), _const_spec(w_down.shape),
        ],
        out_specs=[
            pl.BlockSpec((2 * tm, d), lambda s: (s, 0)),
            pl.BlockSpec((1, POOL_HALO, d), lambda s: (tile_a(s) // tiles_per_seq, 0, 0)),
        ],
        out_shape=[
            jax.ShapeDtypeStruct((b * t, d), F32),
            jax.ShapeDtypeStruct((b, POOL_HALO, d), F32),
        ],
        scratch_shapes=[pltpu.VMEM((2, tm, d), F32), pltpu.VMEM((2, tm, d), BF16),
                        pltpu.VMEM((2, tm + POOL_HALO, d), F32)],
        compiler_params=pltpu.CompilerParams(
            dimension_semantics=("arbitrary",), vmem_limit_bytes=VMEM_LIMIT),
        name="layer0_prompt",
    )(x2, x2, x2, x2, x1_first, h_first, gains, w_pool, pool_scale, w_up, w_down)
    return y.reshape(b, t, d), utail


def _layer0_sample_kernel(start_pos, x_ref, buf_ref, g_ref, wp_ref, ps_ref, wup_ref, wdn_ref,
                          y_ref, u_ref):
    n_t, n_b, _ = x_ref.shape
    pool_ch = wp_ref.shape[1]
    g = g_ref[...]
    xs = [x_ref[t] for t in range(n_t)]
    us = [_rms(xt, g[0:1]) for xt in xs]
    ext = [buf_ref[j] for j in range(POOL_BUF)] + us
    parts = []
    for gi, w in enumerate(POOL_WINDOWS):
        cs = slice(gi * pool_ch, (gi + 1) * pool_ch)
        rows = []
        for t in range(n_t):
            s = us[t][:, cs]
            for j in range(1, w):
                s = s + ext[POOL_BUF + t - j][:, cs]
            cnt = float(min(w, start_pos + t + 1))
            rows.append(s / cnt - us[t][:, cs])
        pooled = jnp.concatenate(rows, axis=0)
        parts.append(jnp.dot(pooled.astype(BF16), wp_ref[gi], preferred_element_type=F32))
    mix = jnp.concatenate(parts, axis=1) * ps_ref[...]
    x = jnp.concatenate(xs, axis=0)
    x1 = x + _rms(mix, g[1:2])
    y = _mlp_residual(x1, g[2:3], g[3:4], wup_ref, wdn_ref)
    for t in range(n_t):
        y_ref[t] = y[t * n_b:(t + 1) * n_b]
        u_ref[t] = us[t]


def _layer0_sample(x_tm, buf_tm, start_pos, gains, w_pool, pool_scale, w_up, w_down):
    assert start_pos + 1 >= max(POOL_WINDOWS) and buf_tm.shape[0] == POOL_BUF
    return pl.pallas_call(
        functools.partial(_layer0_sample_kernel, start_pos),
        out_shape=[jax.ShapeDtypeStruct(x_tm.shape, F32), jax.ShapeDtypeStruct(x_tm.shape, F32)],
        compiler_params=pltpu.CompilerParams(vmem_limit_bytes=VMEM_LIMIT),
        name="layer0_sample",
    )(x_tm, buf_tm, gains, w_pool, pool_scale, w_up, w_down)


def _rope(x, cos, sin_signed):
    lane = lax.broadcasted_iota(jnp.int32, (x.shape[0], LANES), 1)
    first_half = (lane & (HEAD_DIM // 2)) == 0
    out = []
    for c in range(x.shape[1] // LANES):
        xc = x[:, c * LANES:(c + 1) * LANES]
        partner = jnp.where(first_half,
                            pltpu.roll(xc, LANES - HEAD_DIM // 2, 1),
                            pltpu.roll(xc, HEAD_DIM // 2, 1))
        out.append(xc * cos + partner * sin_signed)
    return out


def _qkv_kernel(dilations, first_kept_tile, transpose_kept,
                x_ref, gq_ref, gkv_ref, wq_ref, wkv_ref, cos_ref, sin_ref,
                q0_ref, q1_ref, q2_ref, k0_ref, k1_ref, k2_ref, v0_ref, v1_ref, v2_ref,
                kf_ref, vf_ref, stage_scr):
    i = pl.program_id(1)
    x = x_ref[0]
    cos = cos_ref[...]
    sin_signed = sin_ref[...]
    chunks_per_group = GROUP_Q_WIDTH // LANES

    def emit(out_ref, dilation, chunks):
        if dilation == 1:
            for c, chunk in enumerate(chunks):
                out_ref[0, :, c * LANES:(c + 1) * LANES] = chunk.astype(BF16)
            return
        for c, chunk in enumerate(chunks):
            stage_scr[c] = chunk
        for r in range(dilation):
            rows = _subseq_rows(stage_scr.shape[1], r, dilation)
            for c in range(len(chunks)):
                out_ref[r, :, c * LANES:(c + 1) * LANES] = stage_scr[c, rows, :].astype(BF16)

    u = _rms(x, gq_ref[...]).astype(BF16)
    q = jnp.dot(u, wq_ref[...], preferred_element_type=F32)
    scale = HEAD_DIM ** -0.5
    q_chunks = _rope(q, cos * scale, sin_signed * scale)
    for gi, q_ref in enumerate((q0_ref, q1_ref, q2_ref)):
        emit(q_ref, dilations[gi], q_chunks[gi * chunks_per_group:(gi + 1) * chunks_per_group])

    un = _rms(x, gkv_ref[...]).astype(BF16)
    kv = jnp.dot(un, wkv_ref[...], preferred_element_type=F32)
    k_chunks = _rope(kv[:, :KV_WIDTH], cos, sin_signed)
    v_chunks = [kv[:, KV_WIDTH + gi * LANES:KV_WIDTH + (gi + 1) * LANES] for gi in range(N_GROUPS)]
    for gi, (k_ref, v_ref) in enumerate(((k0_ref, v0_ref), (k1_ref, v1_ref), (k2_ref, v2_ref))):
        emit(k_ref, dilations[gi], [k_chunks[gi]])
        emit(v_ref, dilations[gi], [v_chunks[gi]])

    @pl.when(i >= first_kept_tile)
    def _():
        for gi in range(N_GROUPS):
            cs = slice(gi * LANES, (gi + 1) * LANES)
            if transpose_kept:
                kf_ref[0, cs, :] = k_chunks[gi].T
                vf_ref[0, cs, :] = v_chunks[gi].T
            else:
                kf_ref[0, :, cs] = k_chunks[gi]
                vf_ref[0, :, cs] = v_chunks[gi]


def _qkv(x, g_q, g_kv, w_q, w_kv, cos, sin_signed, tm, keep_rows, dilations, transpose_kept):
    b, t, d = x.shape
    assert t % tm == 0 and keep_rows % tm == 0 and all(tm % (16 * dil) == 0 for dil in dilations)
    first_kept_tile = (t - keep_rows) // tm
    kept_block = lambda bi, i: jnp.maximum(i - first_kept_tile, 0)
    if transpose_kept:
        kept = pl.BlockSpec((1, KV_WIDTH, tm), lambda bi, i: (bi, 0, kept_block(bi, i)))
        kept_shape = jax.ShapeDtypeStruct((b, KV_WIDTH, keep_rows), F32)
    else:
        kept = pl.BlockSpec((1, tm, KV_WIDTH), lambda bi, i: (bi, kept_block(bi, i), 0))
        kept_shape = jax.ShapeDtypeStruct((b, keep_rows, KV_WIDTH), F32)
    sub_shape = lambda dil, width: jax.ShapeDtypeStruct((b, dil, t // dil, width), BF16)
    q_specs = [_subseq_spec(dil, tm, GROUP_Q_WIDTH) for dil in dilations]
    kv_specs = [_subseq_spec(dil, tm, GROUP_KV_WIDTH) for dil in dilations]
    q_shapes = [sub_shape(dil, GROUP_Q_WIDTH) for dil in dilations]
    kv_shapes = [sub_shape(dil, GROUP_KV_WIDTH) for dil in dilations]
    return pl.pallas_call(
        functools.partial(_qkv_kernel, tuple(dilations), first_kept_tile, transpose_kept),
        grid=(b, t // tm),
        in_specs=[
            pl.BlockSpec((1, tm, d), lambda bi, i: (bi, i, 0)),
            _const_spec(g_q.shape), _const_spec(g_kv.shape),
            _const_spec(w_q.shape), _const_spec(w_kv.shape),
            pl.BlockSpec((tm, LANES), lambda bi, i: (i, 0)),
            pl.BlockSpec((tm, LANES), lambda bi, i: (i, 0)),
        ],
        out_specs=q_specs + kv_specs + kv_specs + [kept, kept],
        out_shape=q_shapes + kv_shapes + kv_shapes + [kept_shape, kept_shape],
        scratch_shapes=[pltpu.VMEM((GROUP_Q_WIDTH // LANES, tm, LANES), F32)],
        compiler_params=pltpu.CompilerParams(
            dimension_semantics=("arbitrary", "arbitrary"), vmem_limit_bytes=VMEM_LIMIT),
        name="qkv_rope",
    )(x, g_q, g_kv, w_q, w_kv, cos, sin_signed)


def _rope_tables(positions):
    inv = ROPE_THETA ** (-jnp.arange(0, HEAD_DIM, 2, dtype=F32) / HEAD_DIM)
    ang = positions.astype(F32)[:, None] * inv[None, :]
    cos, sin = jnp.cos(ang), jnp.sin(ang)
    heads_per_period = LANES // HEAD_DIM
    return (jnp.tile(cos, (1, 2 * heads_per_period)),
            jnp.tile(jnp.concatenate([-sin, sin], axis=1), (1, heads_per_period)))


def _stat_lane(c, k):
    return (1 - c) * HEAD_DIM + k


def _band_attention_kernel(seg_rows, tiles_per_subseq, q_ref, kp_ref, kc_ref, vp_ref, vc_ref,
                           o_ref, stat_ref, k_scr, v_scr, bias_scr, s_scr, p_scr):
    n_seg = q_ref.shape[0] // seg_rows
    bands_per_seg = seg_rows // BAND
    n_bands = n_seg * bands_per_seg
    step = pl.program_id(1)
    lane = lax.broadcasted_iota(jnp.int32, (1, LANES), 1)
    for c in range(KV_PER_GROUP):
        own = (lane // HEAD_DIM) == c
        for u in range(n_seg):
            base = u * (seg_rows + BAND)
            for dst, k_rows, v_rows in (
                    (slice(base, base + BAND), kp_ref[...], vp_ref[...]),
                    (slice(base + BAND, base + BAND + seg_rows),
                     kc_ref[u * seg_rows:(u + 1) * seg_rows, :],
                     vc_ref[u * seg_rows:(u + 1) * seg_rows, :])):
                k_scr[c, dst, :] = jnp.where(own, k_rows, jnp.zeros((), BF16))
                v_scr[c, dst, :] = jnp.where(own, v_rows, jnp.ones((), BF16))
    stat_ref[...] = jnp.ones(stat_ref.shape, F32)

    row = lax.broadcasted_iota(jnp.int32, (BAND, 2 * BAND), 0)
    col = lax.broadcasted_iota(jnp.int32, (BAND, 2 * BAND), 1)
    band_bias = jnp.where((col >= row) & (col <= row + BAND), 0.0, NEG_BIG)
    bias_scr[0] = band_bias
    bias_scr[1] = band_bias + jnp.where(col < BAND, NEG_BIG, 0.0)
    starts_subseq = True if n_seg > 1 else lax.rem(step, tiles_per_subseq) == 0

    def rows_of(g):
        return pl.ds(g * BAND if isinstance(g, int) else pl.multiple_of(g * BAND, BAND), BAND)

    def key_rows_of(g):
        staged = g + g // bands_per_seg
        start = staged * BAND if isinstance(g, int) else pl.multiple_of(staged * BAND, BAND)
        return pl.ds(start, 2 * BAND)

    def scores(g, slot):
        q = jnp.concatenate(
            [q_ref[rows_of(g), k * LANES:(k + 1) * LANES] for k in range(Q_PER_KV)], axis=0)
        for c in range(KV_PER_GROUP):
            s_scr[slot, c] = lax.dot_general(q, k_scr[c, key_rows_of(g), :],
                                             (((1,), (1,)), ((), ())),
                                             preferred_element_type=F32)
            yield

    def softmax(g, slot):
        first = jnp.logical_and(starts_subseq, g % bands_per_seg == 0).astype(jnp.int32)
        for c in range(KV_PER_GROUP):
            for k in range(Q_PER_KV):
                part = pl.ds(k * BAND, BAND)
                s = s_scr[slot, c, part, :] + bias_scr[first]
                m = jnp.max(s, axis=1, keepdims=True)
                p_scr[slot, c, part, :] = jnp.exp(s - m).astype(BF16)
                stat_ref[rows_of(g), pl.ds(_stat_lane(c, k) + STAT_MAX_OFFSET, 1)] = m
                yield

    def weighted_values(g, slot):
        ov = []
        for c in range(KV_PER_GROUP):
            ov.append(jnp.dot(p_scr[slot, c], v_scr[c, key_rows_of(g), :],
                              preferred_element_type=F32))
            yield
        low = lane < HEAD_DIM
        sums = None
        for k in range(Q_PER_KV):
            part = slice(k * BAND, (k + 1) * BAND)
            o_ref[rows_of(g), k * LANES:(k + 1) * LANES] = (
                jnp.where(low, ov[0][part], ov[1][part]).astype(o_ref.dtype))
            sums_k = jnp.where(low, ov[1][part], ov[0][part])
            sums = sums_k if sums is None else jnp.where((lane % HEAD_DIM) == k, sums_k, sums)
            yield
        for c in range(KV_PER_GROUP):
            lanes = pl.ds(_stat_lane(c, 0), Q_PER_KV)
            stat_ref[rows_of(g), lanes] = sums[:, _stat_lane(c, 0):_stat_lane(c, 0) + Q_PER_KV]
        yield

    _interleave(scores(0, 0))
    _interleave(softmax(0, 0), scores(1, 1))

    def steady(t, carry):
        for g, slot in ((2 * t + 1, 1), (2 * t + 2, 0)):
            _interleave(weighted_values(g - 1, 1 - slot), softmax(g, slot),
                        scores(g + 1, 1 - slot))
        return carry

    assert n_bands % 2 == 0
    lax.fori_loop(0, (n_bands - 2) // 2, steady, 0)
    last = n_bands - 1
    _interleave(weighted_values(last - 1, (last - 1) % 2), softmax(last, last % 2))
    _interleave(weighted_values(last, last % 2))


def _band_attention(q, k, v, block_rows):
    b, dilation, sub, _ = q.shape
    seg_rows = min(sub, block_rows)
    total = dilation * sub
    assert total % block_rows == 0 and block_rows % seg_rows == 0 and sub % seg_rows == 0
    assert seg_rows % BAND == 0 and block_rows >= 2 * BAND
    n_seg = block_rows // seg_rows
    flat = lambda a: a.reshape(b, total, a.shape[-1])
    cur = lambda width: pl.BlockSpec((None, block_rows, width), lambda bi, i: (bi, i, 0))
    prev = pl.BlockSpec((None, BAND, GROUP_KV_WIDTH),
                        lambda bi, i: (bi, jnp.maximum(i * (block_rows // BAND) - 1, 0), 0))
    staged_rows = n_seg * (seg_rows + BAND)
    o, stat = pl.pallas_call(
        functools.partial(_band_attention_kernel, seg_rows, sub // seg_rows),
        grid=(b, total // block_rows),
        in_specs=[cur(GROUP_Q_WIDTH), prev, cur(GROUP_KV_WIDTH), prev, cur(GROUP_KV_WIDTH)],
        out_specs=[cur(GROUP_Q_WIDTH), cur(STAT_LANES)],
        out_shape=[jax.ShapeDtypeStruct((b, total, GROUP_Q_WIDTH), BF16),
                   jax.ShapeDtypeStruct((b, total, STAT_LANES), F32)],
        scratch_shapes=[pltpu.VMEM((KV_PER_GROUP, staged_rows, GROUP_KV_WIDTH), BF16),
                        pltpu.VMEM((KV_PER_GROUP, staged_rows, GROUP_KV_WIDTH), BF16),
                        pltpu.VMEM((2, BAND, 2 * BAND), F32),
                        pltpu.VMEM((2, KV_PER_GROUP, Q_PER_KV * BAND, 2 * BAND), F32),
                        pltpu.VMEM((2, KV_PER_GROUP, Q_PER_KV * BAND, 2 * BAND), BF16)],
        compiler_params=pltpu.CompilerParams(
            dimension_semantics=("arbitrary", "arbitrary"), vmem_limit_bytes=VMEM_LIMIT),
        name=f"band_attention_d{dilation}",
    )(flat(q), flat(k), flat(k), flat(v), flat(v))
    return (o.reshape(b, dilation, sub, GROUP_Q_WIDTH), stat.reshape(b, dilation, sub, STAT_LANES))


SAMPLE_Q_ROWS = 16
NEW_KEY_ROWS = 8


def _sample_attention_kernel(n_new, q_ref, kn_ref, vn_ref, kc0, kc1, kc2, vc0, vc1, vc2,
                             o_ref, lse_ref):
    n_rows = q_ref.shape[2]
    kn_all = kn_ref[0]
    vn_all = vn_ref[0]
    contract_last = (((1,), (1,)), ((), ()))
    for gi, (kc_ref, vc_ref) in enumerate(((kc0, vc0), (kc1, vc1), (kc2, vc2))):
        dmask = ATTN_DILATIONS[gi] - 1
        n_cache = kc_ref.shape[3]
        assert n_cache == ATTN_WINDOWS[gi]
        t_c = lax.broadcasted_iota(jnp.int32, (n_rows, n_cache), 0) & (n_new - 1)
        j_c = lax.broadcasted_iota(jnp.int32, (n_rows, n_cache), 1)
        valid_c = (j_c >= t_c) & (((j_c - t_c) & dmask) == 0)
        t_n = lax.broadcasted_iota(jnp.int32, (n_rows, NEW_KEY_ROWS), 0) & (n_new - 1)
        j_n = lax.broadcasted_iota(jnp.int32, (n_rows, NEW_KEY_ROWS), 1)
        valid_n = (j_n <= t_n) & (((t_n - j_n) & dmask) == 0)
        for c in range(KV_PER_GROUP):
            kvh = gi * KV_PER_GROUP + c
            ns = slice(kvh * HEAD_DIM, (kvh + 1) * HEAD_DIM)
            q = q_ref[0, kvh]
            kt = kc_ref[0, c].astype(BF16)
            vt = vc_ref[0, c].astype(BF16)
            s_c = jnp.dot(q, kt, preferred_element_type=F32)
            s_n = lax.dot_general(q, kn_all[:, ns].astype(BF16), contract_last,
                                  preferred_element_type=F32)
            s_c = jnp.where(valid_c, s_c, NEG_BIG)
            s_n = jnp.where(valid_n, s_n, NEG_BIG)
            m = jnp.maximum(jnp.max(s_c, axis=1, keepdims=True),
                            jnp.max(s_n, axis=1, keepdims=True))
            p_c = jnp.exp(s_c - m)
            p_n = jnp.exp(s_n - m)
            l = jnp.sum(p_c, axis=1, keepdims=True) + jnp.sum(p_n, axis=1, keepdims=True)
            o = (lax.dot_general(p_c.astype(BF16), vt, contract_last, preferred_element_type=F32)
                 + jnp.dot(p_n.astype(BF16), vn_all[:, ns].astype(BF16),
                           preferred_element_type=F32))
            o_ref[0, kvh] = o / l
            lse_ref[0, kvh] = m + jnp.log(l)


def _sample_attention(q16, k_new, v_new, cache_kt, cache_vt, n_new):
    b = q16.shape[0]
    n_kv = q16.shape[1]
    assert cache_kt.shape[3] == KV_WINDOW and n_new & (n_new - 1) == 0 and n_new <= NEW_KEY_ROWS

    def cache_spec(gi):
        cols = ATTN_WINDOWS[gi]
        last = KV_WINDOW // cols - 1
        return pl.BlockSpec((1, KV_PER_GROUP, HEAD_DIM, cols), lambda bi: (bi, gi, 0, last))

    whole = lambda a: pl.BlockSpec((1,) + a.shape[1:], lambda bi: (bi,) + (0,) * (a.ndim - 1))
    o_shape = jax.ShapeDtypeStruct((b, n_kv, SAMPLE_Q_ROWS, HEAD_DIM), F32)
    lse_shape = jax.ShapeDtypeStruct((b, n_kv, SAMPLE_Q_ROWS, 1), F32)
    return pl.pallas_call(
        functools.partial(_sample_attention_kernel, n_new),
        grid=(b,),
        in_specs=[whole(q16), whole(k_new), whole(v_new)]
        + [cache_spec(gi) for gi in range(N_GROUPS)] * 2,
        out_specs=[whole(o_shape), whole(lse_shape)],
        out_shape=[o_shape, lse_shape],
        compiler_params=pltpu.CompilerParams(
            dimension_semantics=("arbitrary",), vmem_limit_bytes=VMEM_LIMIT),
        name="sample_attention",
    )(q16, k_new, v_new, cache_kt, cache_kt, cache_kt, cache_vt, cache_vt, cache_vt)


def _attn_mix_steps(dilations, x_ref, o_refs, s_refs, g, wo_ref, o_scr, stat_scr, a_scr,
                    x1_ref, h_ref):
    tm = x_ref.shape[0]
    chunks_per_group = GROUP_Q_WIDTH // LANES
    for gi, (o_ref, s_ref) in enumerate(zip(o_refs, s_refs)):
        for r in range(dilations[gi]):
            rows = _subseq_rows(tm, r, dilations[gi])
            stat_scr[gi, rows, :] = s_ref[r]
            for c in range(chunks_per_group):
                o_scr[gi * chunks_per_group + c, rows, :] = (
                    o_ref[r, :, c * LANES:(c + 1) * LANES].astype(F32))
        yield
    sums = [stat_scr[gi] for gi in range(N_GROUPS)]
    maxes = [pltpu.roll(sm, STAT_LANES - STAT_MAX_OFFSET, 1) for sm in sums]
    top = jnp.maximum(jnp.maximum(maxes[0], maxes[1]), maxes[2])
    es = [jnp.exp(mx - top) for mx in maxes]
    den = sums[0] * es[0] + sums[1] * es[1] + sums[2] * es[2]
    slot_id = lax.broadcasted_iota(jnp.int32, (tm, GROUP_Q_WIDTH), 1) // HEAD_DIM
    for gi in range(N_GROUPS):
        scale = es[gi] / den
        wide = jnp.zeros((tm, GROUP_Q_WIDTH), F32)
        for slot in range(HEADS_PER_GROUP):
            stat_lane = _stat_lane(slot % KV_PER_GROUP, slot // KV_PER_GROUP)
            wide = jnp.where(slot_id == slot, scale[:, stat_lane:stat_lane + 1], wide)
        for c in range(chunks_per_group):
            ci = gi * chunks_per_group + c
            a_scr[:, ci * LANES:(ci + 1) * LANES] = (
                o_scr[ci] * wide[:, c * LANES:(c + 1) * LANES]).astype(BF16)
        yield
    mix = jnp.dot(a_scr[...], wo_ref[...], preferred_element_type=F32)
    x1_ref[...] = x_ref[...] + _rms(mix, g[1:2])
    yield
    h_ref[...] = _rms(x1_ref[...], g[2:3]).astype(BF16)
    yield


def _layer1_scratch(tm, d, slots):
    lead = (slots,) if slots else ()
    return [pltpu.VMEM((Q_WIDTH // LANES, tm, LANES), F32),
            pltpu.VMEM((N_GROUPS, tm, STAT_LANES), F32),
            pltpu.VMEM((tm, Q_WIDTH), BF16),
            pltpu.VMEM(lead + (tm, d), F32),
            pltpu.VMEM(lead + (tm, d), BF16)]


def _layer1_single_kernel(dilations, x_ref, o0_ref, o1_ref, o2_ref, s0_ref, s1_ref, s2_ref, g_ref,
                          wo_ref, wup_ref, wdn_ref, y_ref, o_scr, stat_scr, a_scr, x1_scr, h_scr):
    g = g_ref[...]
    _interleave(_attn_mix_steps(dilations, x_ref, (o0_ref, o1_ref, o2_ref),
                                (s0_ref, s1_ref, s2_ref), g, wo_ref, o_scr, stat_scr, a_scr,
                                x1_scr, h_scr))
    y_ref[...] = _mlp_from_hidden(x1_scr[...], h_scr[...], g[3:4], wup_ref, wdn_ref)


def _layer1_first_kernel(dilations, x_ref, o0_ref, o1_ref, o2_ref, s0_ref, s1_ref, s2_ref, g_ref,
                         wo_ref, x1_ref, h_ref, o_scr, stat_scr, a_scr):
    _interleave(_attn_mix_steps(dilations, x_ref, (o0_ref, o1_ref, o2_ref),
                                (s0_ref, s1_ref, s2_ref), g_ref[...], wo_ref, o_scr, stat_scr,
                                a_scr, x1_ref, h_ref))


def _layer1_prompt_kernel(dilations, xa_ref, oa0, oa1, oa2, sa0, sa1, sa2,
                          xb_ref, ob0, ob1, ob2, sb0, sb1, sb2, x1f_ref, hf_ref, g_ref, wo_ref,
                          wup_ref, wdn_ref, y_ref, o_scr, stat_scr, a_scr, x1_scr, h_scr):
    s = pl.program_id(0)
    tm = xa_ref.shape[0]
    g = g_ref[...]

    @pl.when(s == 0)
    def _():
        x1_scr[0] = x1f_ref[...]
        h_scr[0] = hf_ref[...]

    def prepare(x_ref, o_refs, s_refs, slot):
        return _attn_mix_steps(dilations, x_ref, o_refs, s_refs, g, wo_ref, o_scr, stat_scr,
                               a_scr, x1_scr.at[slot], h_scr.at[slot])

    def mlp(slot, rows):
        def emit(y):
            y_ref[rows, :] = y
        return _mlp_steps(lambda: x1_scr[slot], lambda: h_scr[slot], g[3:4], wup_ref, wdn_ref,
                          emit)

    _interleave(mlp(0, slice(0, tm)), prepare(xa_ref, (oa0, oa1, oa2), (sa0, sa1, sa2), 1))
    _interleave(mlp(1, slice(tm, 2 * tm)), prepare(xb_ref, (ob0, ob1, ob2), (sb0, sb1, sb2), 0))


def _layer1(x, os, stats, gains, w_o, w_up, w_down, tm):
    b, t, d = x.shape
    dilations = tuple(o.shape[1] for o in os)
    tiles_per_seq = t // tm
    n_tiles = b * tiles_per_seq
    assert t % tm == 0 and all(tm % (16 * dil) == 0 for dil in dilations)
    x2 = x.reshape(b * t, d)
    weights = (gains, w_o, w_up, w_down)

    def tile_specs(tile):
        sub = lambda dil, width: pl.BlockSpec(
            (None, dil, tm // dil, width),
            lambda s: (tile(s) // tiles_per_seq, 0, tile(s) % tiles_per_seq, 0))
        return ([pl.BlockSpec((tm, d), lambda s: (tile(s), 0))]
                + [sub(dil, GROUP_Q_WIDTH) for dil in dilations]
                + [sub(dil, STAT_LANES) for dil in dilations])

    if n_tiles == 1:
        y = pl.pallas_call(
            functools.partial(_layer1_single_kernel, dilations),
            grid=(1,),
            in_specs=tile_specs(lambda s: 0) + [_const_spec(w.shape) for w in weights],
            out_specs=pl.BlockSpec((tm, d), lambda s: (0, 0)),
            out_shape=jax.ShapeDtypeStruct((b * t, d), F32),
            scratch_shapes=_layer1_scratch(tm, d, 0),
            compiler_params=pltpu.CompilerParams(vmem_limit_bytes=VMEM_LIMIT),
            name="layer1_single_tile",
        )(x2, *os, *stats, *weights)
        return y.reshape(b, t, d)

    assert n_tiles % 2 == 0
    x1_first, h_first = pl.pallas_call(
        functools.partial(_layer1_first_kernel, dilations),
        grid=(1,),
        in_specs=tile_specs(lambda s: 0) + [_const_spec(gains.shape), _const_spec(w_o.shape)],
        out_specs=[pl.BlockSpec((tm, d), lambda s: (0, 0))] * 2,
        out_shape=[jax.ShapeDtypeStruct((tm, d), F32), jax.ShapeDtypeStruct((tm, d), BF16)],
        scratch_shapes=_layer1_scratch(tm, d, 0)[:3],
        compiler_params=pltpu.CompilerParams(vmem_limit_bytes=VMEM_LIMIT),
        name="layer1_first_tile",
    )(x2, *os, *stats, gains, w_o)

    tile_a = lambda s: 2 * s + 1
    tile_b = lambda s: jnp.minimum(2 * s + 2, n_tiles - 1)
    y = pl.pallas_call(
        functools.partial(_layer1_prompt_kernel, dilations),
        grid=(n_tiles // 2,),
        in_specs=tile_specs(tile_a) + tile_specs(tile_b)
        + [_const_spec(x1_first.shape), _const_spec(h_first.shape)]
        + [_const_spec(w.shape) for w in weights],
        out_specs=pl.BlockSpec((2 * tm, d), lambda s: (s, 0)),
        out_shape=jax.ShapeDtypeStruct((b * t, d), F32),
        scratch_shapes=_layer1_scratch(tm, d, 2),
        compiler_params=pltpu.CompilerParams(
            dimension_semantics=("arbitrary",), vmem_limit_bytes=VMEM_LIMIT),
        name="layer1_prompt",
    )(x2, *os, *stats, x2, *os, *stats, x1_first, h_first, *weights)
    return y.reshape(b, t, d)


PROMPT_TILE = 512
LAYER1_TILE = 256
ATTN_BLOCK_ROWS = 2048


def kernel(x_prompt, x_sample, cache_pool, cache_k, cache_v, norm_gains, kv_norm_gain, w_pool,
           pool_scale, w_q, w_o, w_kv, w_up, w_down):
    depth = norm_gains.shape[0]
    assert depth == 2 and cache_pool.shape[0] == 1 and w_q.shape[0] == 1
    bp, tp, d = x_prompt.shape
    bs, ts, _ = x_sample.shape

    g0, g1 = norm_gains[0], norm_gains[1]
    g1_q = g1[0:1]
    g_kv = kv_norm_gain[None, :]
    wp = w_pool[0].astype(BF16)
    ps = pool_scale[0][None, :]
    slot_order = jnp.array(HEAD_SLOT_ORDER)
    wq = w_q[0].reshape(d, N_GROUPS, HEADS_PER_GROUP, HEAD_DIM)[:, :, slot_order]
    wq = wq.reshape(d, Q_WIDTH).astype(BF16)
    wo = w_o[0].reshape(N_GROUPS, HEADS_PER_GROUP, HEAD_DIM, d)[:, slot_order]
    wo = wo.reshape(Q_WIDTH, d).astype(BF16)
    wkv = w_kv.astype(BF16)
    wup0, wup1 = w_up[0].astype(BF16), w_up[1].astype(BF16)
    wdn0, wdn1 = w_down[0].astype(BF16), w_down[1].astype(BF16)

    keep = min(KV_WINDOW, tp)
    xp1, utail = _layer0_prompt(x_prompt, g0, wp, ps, wup0, wdn0, PROMPT_TILE)
    cos_p, sin_p = _rope_tables(jnp.arange(tp))
    qkv_p = _qkv(xp1, g1_q, g_kv, wq, wkv, cos_p, sin_p, PROMPT_TILE, keep, ATTN_DILATIONS, True)
    os_p, stats_p = [], []
    for gi, dil in enumerate(ATTN_DILATIONS):
        o, stat = _band_attention(qkv_p[gi], qkv_p[N_GROUPS + gi], qkv_p[2 * N_GROUPS + gi],
                                  ATTN_BLOCK_ROWS)
        os_p.append(o)
        stats_p.append(stat)
    y_prompt = _layer1(xp1, os_p, stats_p, g1, wo, wup1, wdn1, LAYER1_TILE)
    pool_prompt = utail[:, POOL_HALO - POOL_BUF:][None]
    k_prompt = qkv_p[-2].reshape(bp, N_KV_HEADS, HEAD_DIM, keep).transpose(0, 3, 1, 2)
    v_prompt = qkv_p[-1].reshape(bp, N_KV_HEADS, HEAD_DIM, keep).transpose(0, 3, 1, 2)

    n_tok = ts * bs
    xs_tm = jnp.swapaxes(x_sample, 0, 1)
    buf_tm = jnp.swapaxes(cache_pool[0], 0, 1)
    xs1_tm, us_tm = _layer0_sample(xs_tm, buf_tm, PAST_LEN, g0, wp, ps, wup0, wdn0)
    xs1 = xs1_tm.reshape(1, n_tok, d)
    cos_s, sin_s = _rope_tables(PAST_LEN + jnp.arange(n_tok) // bs)
    qkv_s = _qkv(xs1, g1_q, g_kv, wq, wkv, cos_s, sin_s, n_tok, n_tok, (1,) * N_GROUPS, False)
    q_s = jnp.concatenate([q[0, 0] for q in qkv_s[:N_GROUPS]], axis=-1)
    q16 = q_s.reshape(ts, bs, N_GROUPS, Q_PER_KV, KV_PER_GROUP, HEAD_DIM)
    q16 = q16.transpose(1, 2, 4, 3, 0, 5).reshape(bs, N_KV_HEADS, Q_PER_KV, ts, HEAD_DIM)
    q16 = jnp.pad(q16, ((0, 0), (0, 0), (0, SAMPLE_Q_ROWS // ts - Q_PER_KV), (0, 0), (0, 0)))
    q16 = q16.reshape(bs, N_KV_HEADS, SAMPLE_Q_ROWS, HEAD_DIM)
    k_s = jnp.swapaxes(qkv_s[-2].reshape(ts, bs, KV_WIDTH), 0, 1)
    v_s = jnp.swapaxes(qkv_s[-1].reshape(ts, bs, KV_WIDTH), 0, 1)
    pad_new = ((0, 0), (0, NEW_KEY_ROWS - ts), (0, 0))
    o16, lse16 = _sample_attention(
        q16, jnp.pad(k_s, pad_new), jnp.pad(v_s, pad_new),
        cache_k.transpose(0, 2, 3, 1), cache_v.transpose(0, 2, 3, 1), ts)
    heads_padded = SAMPLE_Q_ROWS // ts
    o_s = o16.reshape(bs, N_GROUPS, KV_PER_GROUP, heads_padded, ts, HEAD_DIM)[:, :, :, :Q_PER_KV]
    o_s = o_s.transpose(4, 0, 1, 3, 2, 5).reshape(1, 1, n_tok, N_GROUPS, GROUP_Q_WIDTH).astype(BF16)
    lse_s = lse16.reshape(bs, N_GROUPS, KV_PER_GROUP, heads_padded, ts)[:, :, :, :Q_PER_KV]
    lse_s = lse_s.transpose(4, 0, 1, 2, 3).reshape(1, 1, n_tok, N_GROUPS, KV_PER_GROUP, Q_PER_KV)
    ones = lambda n: jnp.ones((1, 1, n_tok, N_GROUPS, n), F32)
    max_lane = [_stat_lane(c, 0) + STAT_MAX_OFFSET for c in range(KV_PER_GROUP)]
    assert max_lane[1] < max_lane[0]
    stat_s = jnp.concatenate(
        [ones(max_lane[1]), lse_s[..., 1, :],
         ones(max_lane[0] - max_lane[1] - Q_PER_KV), lse_s[..., 0, :],
         ones(STAT_LANES - max_lane[0] - Q_PER_KV)], axis=-1)
    ys_tm = _layer1(xs1, [o_s[:, :, :, gi] for gi in range(N_GROUPS)],
                    [stat_s[:, :, :, gi] for gi in range(N_GROUPS)], g1, wo, wup1, wdn1, n_tok)
    y_sample = jnp.swapaxes(ys_tm.reshape(ts, bs, d), 0, 1)
    u_s = jnp.swapaxes(us_tm, 0, 1)
    pool_sample = jnp.concatenate([cache_pool[0], u_s], axis=1)[:, -POOL_BUF:][None]
    k_sample = k_s.reshape(bs, ts, N_KV_HEADS, HEAD_DIM)
    v_sample = v_s.reshape(bs, ts, N_KV_HEADS, HEAD_DIM)

    return (y_prompt, y_sample, pool_prompt, k_prompt, v_prompt, pool_sample, k_sample, v_sample)
```

```python
import functools

import jax
import jax.numpy as jnp
from jax import lax
from jax.experimental import pallas as pl
from jax.experimental.pallas import tpu as pltpu

F32 = jnp.float32
BF16 = jnp.bfloat16

EPS = 1e-6
ROPE_THETA = 10000.0
PAST_LEN = 16384
POOL_WINDOWS = (2, 4, 8, 16)
POOL_BUF = max(POOL_WINDOWS) - 1
POOL_HALO = 16
HEAD_DIM = 64
ATTN_WINDOWS = (128, 512, 2048)
ATTN_DILATIONS = (1, 4, 16)
N_GROUPS = len(ATTN_WINDOWS)
KV_PER_GROUP = 2
Q_PER_KV = 3
HEADS_PER_GROUP = KV_PER_GROUP * Q_PER_KV
N_KV_HEADS = N_GROUPS * KV_PER_GROUP
GROUP_Q_WIDTH = HEADS_PER_GROUP * HEAD_DIM
GROUP_KV_WIDTH = KV_PER_GROUP * HEAD_DIM
Q_WIDTH = N_GROUPS * GROUP_Q_WIDTH
KV_WIDTH = N_GROUPS * GROUP_KV_WIDTH
KV_WINDOW = max(ATTN_WINDOWS)
BAND = 128
NEG_BIG = -1e30

LANES = 128
STAT_LANES = LANES
STAT_MAX_OFFSET = 8
VMEM_LIMIT = 56 * 1024 * 1024
FF_CHUNK = 512

for _w, _d in zip(ATTN_WINDOWS, ATTN_DILATIONS):
    assert _w // _d == BAND and _w % _d == 0
assert GROUP_KV_WIDTH == LANES


def _rms(x, g):
    return x * lax.rsqrt(jnp.mean(x * x, axis=-1, keepdims=True) + EPS) * g


def _mlp_residual(x1, g_in, g_out, wup_ref, wdn_ref):
    return _mlp_from_hidden(x1, _rms(x1, g_in).astype(BF16), g_out, wup_ref, wdn_ref)


def _mlp_from_hidden(x1, h, g_out, wup_ref, wdn_ref):
    out = []
    for _ in _mlp_steps(lambda: x1, lambda: h, g_out, wup_ref, wdn_ref, out.append):
        pass
    return out[0]


def _mlp_steps(load_x1, load_h, g_out, wup_ref, wdn_ref, emit):
    d_ff = wup_ref.shape[1]
    h = load_h()
    acc = None
    for c in range(d_ff // FF_CHUNK):
        cs = slice(c * FF_CHUNK, (c + 1) * FF_CHUNK)
        a = jnp.dot(h, wup_ref[:, cs], preferred_element_type=F32)
        a = jnp.square(jnp.maximum(a, 0.0)).astype(BF16)
        part = jnp.dot(a, wdn_ref[cs, :], preferred_element_type=F32)
        acc = part if acc is None else acc + part
        yield
    emit(load_x1() + _rms(acc, g_out))
    yield


def _interleave(*generators):
    live = list(generators)
    while live:
        for gen in list(live):
            try:
                next(gen)
            except StopIteration:
                live.remove(gen)


def _const_spec(shape):
    zeros = (0,) * len(shape)
    return pl.BlockSpec(shape, lambda *_: zeros, pipeline_mode=pl.Buffered(1))


def _layer_spec(w, layer):
    index = (layer,) + (0,) * (w.ndim - 1)
    return pl.BlockSpec((None,) + w.shape[1:], lambda *_: index, pipeline_mode=pl.Buffered(1))


def _subseq_spec(dilation, rows, width):
    return pl.BlockSpec((None, dilation, rows // dilation, width), lambda bi, i: (bi, 0, i, 0))


def _subseq_rows(n_rows, r, dilation):
    n = n_rows // dilation
    return pl.ds(r, n, stride=dilation) if dilation > 1 else pl.ds(0, n)


def _pool_mixer_steps(x_ref, xh_ref, tile_in_seq, g, wp_ref, ps_ref, ext_ref, x1_ref, h_ref,
                      utail_ref=None):
    tm = x_ref.shape[0]
    pool_ch = wp_ref.shape[1]
    u = _rms(x_ref[...], g[0:1])
    uh = _rms(xh_ref[0:POOL_HALO, :], g[0:1]) * jnp.where(tile_in_seq > 0, 1.0, 0.0)
    ext_ref[0:POOL_HALO, :] = uh
    ext_ref[POOL_HALO:, :] = u
    if utail_ref is not None:
        utail_ref[0] = u[tm - POOL_HALO:, :]
    yield
    pos = tile_in_seq * tm + lax.broadcasted_iota(jnp.int32, (tm, 1), 0)
    parts = []
    for gi, w in enumerate(POOL_WINDOWS):
        cs = slice(gi * pool_ch, (gi + 1) * pool_ch)
        s = ext_ref[POOL_HALO:, cs]
        for j in range(1, w):
            s = s + ext_ref[POOL_HALO - j:POOL_HALO - j + tm, cs]
        cnt = jnp.minimum(w, pos + 1).astype(F32)
        pooled = s / cnt - ext_ref[POOL_HALO:, cs]
        parts.append(jnp.dot(pooled.astype(BF16), wp_ref[gi], preferred_element_type=F32))
        yield
    mix = jnp.concatenate(parts, axis=1) * ps_ref[...]
    x1 = x_ref[...] + _rms(mix, g[1:2])
    x1_ref[...] = x1
    yield
    h_ref[...] = _rms(x1_ref[...], g[2:3]).astype(BF16)
    yield


def _layer0_first_kernel(x_ref, g_ref, wp_ref, ps_ref, x1_ref, h_ref, ext_scr):
    _interleave(_pool_mixer_steps(x_ref, x_ref, 0, g_ref[...], wp_ref, ps_ref, ext_scr,
                                  x1_ref, h_ref))


def _layer0_prompt_kernel(tiles_per_seq, xa_ref, xha_ref, xb_ref, xhb_ref, x1f_ref, hf_ref, g_ref,
                          wp_ref, ps_ref, wup_ref, wdn_ref, y_ref, utail_ref,
                          x1_scr, h_scr, ext_scr):
    s = pl.program_id(0)
    n_tiles = 2 * pl.num_programs(0)
    tm = xa_ref.shape[0]
    g = g_ref[...]

    @pl.when(s == 0)
    def _():
        x1_scr[0] = x1f_ref[...]
        h_scr[0] = hf_ref[...]

    def prepare(x_ref, xh_ref, tile, slot, tail_ref):
        return _pool_mixer_steps(x_ref, xh_ref, lax.rem(tile, tiles_per_seq), g, wp_ref, ps_ref,
                                 ext_scr.at[slot], x1_scr.at[slot], h_scr.at[slot], tail_ref)

    def mlp(slot, rows):
        def emit(y):
            y_ref[rows, :] = y
        return _mlp_steps(lambda: x1_scr[slot], lambda: h_scr[slot], g[3:4], wup_ref, wdn_ref,
                          emit)

    _interleave(mlp(0, slice(0, tm)), prepare(xa_ref, xha_ref, 2 * s + 1, 1, utail_ref))
    _interleave(mlp(1, slice(tm, 2 * tm)),
                prepare(xb_ref, xhb_ref, jnp.minimum(2 * s + 2, n_tiles - 1), 0, None))


def _layer0_prompt(x, gains, w_pool, pool_scale, w_up, w_down, layer, tm):
    b, t, d = x.shape
    tiles_per_seq = t // tm
    n_tiles = b * tiles_per_seq
    assert t % tm == 0 and tm % POOL_HALO == 0 and tiles_per_seq % 2 == 0
    halo_per_tile = tm // POOL_HALO
    x2 = x.reshape(b * t, d)

    x1_first, h_first = pl.pallas_call(
        _layer0_first_kernel,
        grid=(1,),
        in_specs=[pl.BlockSpec((tm, d), lambda i: (0, 0)), _const_spec(gains.shape),
                  _const_spec(w_pool.shape), _const_spec(pool_scale.shape)],
        out_specs=[pl.BlockSpec((tm, d), lambda i: (0, 0))] * 2,
        out_shape=[jax.ShapeDtypeStruct((tm, d), F32), jax.ShapeDtypeStruct((tm, d), BF16)],
        scratch_shapes=[pltpu.VMEM((tm + POOL_HALO, d), F32)],
        compiler_params=pltpu.CompilerParams(vmem_limit_bytes=VMEM_LIMIT),
        name="layer0_first_tile",
    )(x2, gains, w_pool, pool_scale)

    tile_a = lambda s: 2 * s + 1
    tile_b = lambda s: jnp.minimum(2 * s + 2, n_tiles - 1)
    tile_spec = lambda tile: pl.BlockSpec((tm, d), lambda s: (tile(s), 0))
    halo_spec = lambda tile: pl.BlockSpec((POOL_HALO, d),
                                          lambda s: (tile(s) * halo_per_tile - 1, 0))
    y, utail = pl.pallas_call(
        functools.partial(_layer0_prompt_kernel, tiles_per_seq),
        grid=(n_tiles // 2,),
        in_specs=[
            tile_spec(tile_a), halo_spec(tile_a), tile_spec(tile_b), halo_spec(tile_b),
            _const_spec(x1_first.shape), _const_spec(h_first.shape),
            _const_spec(gains.shape), _const_spec(w_pool.shape), _const_spec(pool_scale.shape),
            _layer_spec(w_up, layer), _layer_spec(w_down, layer),
        ],
        out_specs=[
            pl.BlockSpec((2 * tm, d), lambda s: (s, 0)),
            pl.BlockSpec((1, POOL_HALO, d), lambda s: (tile_a(s) // tiles_per_seq, 0, 0)),
        ],
        out_shape=[
            jax.ShapeDtypeStruct((b * t, d), F32),
            jax.ShapeDtypeStruct((b, POOL_HALO, d), F32),
        ],
        scratch_shapes=[pltpu.VMEM((2, tm, d), F32), pltpu.VMEM((2, tm, d), BF16),
                        pltpu.VMEM((2, tm + POOL_HALO, d), F32)],
        compiler_params=pltpu.CompilerParams(
            dimension_semantics=("arbitrary",), vmem_limit_bytes=VMEM_LIMIT),
        name="layer0_prompt",
    )(x2, x2, x2, x2, x1_first, h_first, gains, w_pool, pool_scale, w_up, w_down)
    return y.reshape(b, t, d), utail


def _layer0_sample_kernel(start_pos, x_ref, buf_ref, g_ref, wp_ref, ps_ref, wup_ref, wdn_ref,
                          y_ref, u_ref):
    n_t, n_b, _ = x_ref.shape
    pool_ch = wp_ref.shape[1]
    g = g_ref[...]
    xs = [x_ref[t] for t in range(n_t)]
    us = [_rms(xt, g[0:1]) for xt in xs]
    ext = [buf_ref[j] for j in range(POOL_BUF)] + us
    parts = []
    for gi, w in enumerate(POOL_WINDOWS):
        cs = slice(gi * pool_ch, (gi + 1) * pool_ch)
        rows = []
        for t in range(n_t):
            s = us[t][:, cs]
            for j in range(1, w):
                s = s + ext[POOL_BUF + t - j][:, cs]
            cnt = float(min(w, start_pos + t + 1))
            rows.append(s / cnt - us[t][:, cs])
        pooled = jnp.concatenate(rows, axis=0)
        parts.append(jnp.dot(pooled.astype(BF16), wp_ref[gi], preferred_element_type=F32))
    mix = jnp.concatenate(parts, axis=1) * ps_ref[...]
    x = jnp.concatenate(xs, axis=0)
    x1 = x + _rms(mix, g[1:2])
    y = _mlp_residual(x1, g[2:3], g[3:4], wup_ref, wdn_ref)
    for t in range(n_t):
        y_ref[t] = y[t * n_b:(t + 1) * n_b]
        u_ref[t] = us[t]


def _layer0_sample(x_tm, buf_tm, start_pos, gains, w_pool, pool_scale, w_up, w_down, layer):
    assert start_pos + 1 >= max(POOL_WINDOWS) and buf_tm.shape[0] == POOL_BUF
    return pl.pallas_call(
        functools.partial(_layer0_sample_kernel, start_pos),
        grid=(1,),
        in_specs=[_const_spec(a.shape) for a in (x_tm, buf_tm, gains, w_pool, pool_scale)]
        + [_layer_spec(w_up, layer), _layer_spec(w_down, layer)],
        out_specs=[pl.BlockSpec(x_tm.shape, lambda i: (0, 0, 0))] * 2,
        out_shape=[jax.ShapeDtypeStruct(x_tm.shape, F32), jax.ShapeDtypeStruct(x_tm.shape, F32)],
        compiler_params=pltpu.CompilerParams(vmem_limit_bytes=VMEM_LIMIT),
        name="layer0_sample",
    )(x_tm, buf_tm, gains, w_pool, pool_scale, w_up, w_down)


def _rope(x, cos, sin_signed):
    lane = lax.broadcasted_iota(jnp.int32, (x.shape[0], LANES), 1)
    first_half = (lane & (HEAD_DIM // 2)) == 0
    out = []
    for c in range(x.shape[1] // LANES):
        xc = x[:, c * LANES:(c + 1) * LANES]
        partner = jnp.where(first_half,
                            pltpu.roll(xc, LANES - HEAD_DIM // 2, 1),
                            pltpu.roll(xc, HEAD_DIM // 2, 1))
        out.append(xc * cos + partner * sin_signed)
    return out


def _qkv_kernel(dilations, first_kept_tile, transpose_kept,
                x_ref, gq_ref, gkv_ref, wq_ref, wkv_ref, cos_ref, sin_ref,
                q0_ref, q1_ref, q2_ref, k0_ref, k1_ref, k2_ref, v0_ref, v1_ref, v2_ref,
                kf_ref, vf_ref, stage_scr):
    i = pl.program_id(1)
    x = x_ref[0]
    cos = cos_ref[...]
    sin_signed = sin_ref[...]
    chunks_per_group = GROUP_Q_WIDTH // LANES

    def emit(out_ref, dilation, chunks):
        if dilation == 1:
            for c, chunk in enumerate(chunks):
                out_ref[0, :, c * LANES:(c + 1) * LANES] = chunk.astype(BF16)
            return
        for c, chunk in enumerate(chunks):
            stage_scr[c] = chunk
        for r in range(dilation):
            rows = _subseq_rows(stage_scr.shape[1], r, dilation)
            for c in range(len(chunks)):
                out_ref[r, :, c * LANES:(c + 1) * LANES] = stage_scr[c, rows, :].astype(BF16)

    u = _rms(x, gq_ref[...]).astype(BF16)
    q = jnp.dot(u, wq_ref[...], preferred_element_type=F32)
    scale = HEAD_DIM ** -0.5
    q_chunks = _rope(q, cos * scale, sin_signed * scale)
    for gi, q_ref in enumerate((q0_ref, q1_ref, q2_ref)):
        emit(q_ref, dilations[gi], q_chunks[gi * chunks_per_group:(gi + 1) * chunks_per_group])

    un = _rms(x, gkv_ref[...]).astype(BF16)
    kv = jnp.dot(un, wkv_ref[...], preferred_element_type=F32)
    k_chunks = _rope(kv[:, :KV_WIDTH], cos, sin_signed)
    v_chunks = [kv[:, KV_WIDTH + gi * LANES:KV_WIDTH + (gi + 1) * LANES] for gi in range(N_GROUPS)]
    for gi, (k_ref, v_ref) in enumerate(((k0_ref, v0_ref), (k1_ref, v1_ref), (k2_ref, v2_ref))):
        emit(k_ref, dilations[gi], [k_chunks[gi]])
        emit(v_ref, dilations[gi], [v_chunks[gi]])

    @pl.when(i >= first_kept_tile)
    def _():
        for gi in range(N_GROUPS):
            cs = slice(gi * LANES, (gi + 1) * LANES)
            if transpose_kept:
                kf_ref[0, cs, :] = k_chunks[gi].T
                vf_ref[0, cs, :] = v_chunks[gi].T
            else:
                kf_ref[0, :, cs] = k_chunks[gi]
                vf_ref[0, :, cs] = v_chunks[gi]


def _qkv(x, g_q, g_kv, w_q, w_kv, cos, sin_signed, tm, keep_rows, dilations, transpose_kept):
    b, t, d = x.shape
    assert t % tm == 0 and keep_rows % tm == 0 and all(tm % (16 * dil) == 0 for dil in dilations)
    first_kept_tile = (t - keep_rows) // tm
    kept_block = lambda bi, i: jnp.maximum(i - first_kept_tile, 0)
    if transpose_kept:
        kept = pl.BlockSpec((1, KV_WIDTH, tm), lambda bi, i: (bi, 0, kept_block(bi, i)))
        kept_shape = jax.ShapeDtypeStruct((b, KV_WIDTH, keep_rows), F32)
    else:
        kept = pl.BlockSpec((1, tm, KV_WIDTH), lambda bi, i: (bi, kept_block(bi, i), 0))
        kept_shape = jax.ShapeDtypeStruct((b, keep_rows, KV_WIDTH), F32)
    sub_shape = lambda dil, width: jax.ShapeDtypeStruct((b, dil, t // dil, width), BF16)
    q_specs = [_subseq_spec(dil, tm, GROUP_Q_WIDTH) for dil in dilations]
    kv_specs = [_subseq_spec(dil, tm, GROUP_KV_WIDTH) for dil in dilations]
    q_shapes = [sub_shape(dil, GROUP_Q_WIDTH) for dil in dilations]
    kv_shapes = [sub_shape(dil, GROUP_KV_WIDTH) for dil in dilations]
    return pl.pallas_call(
        functools.partial(_qkv_kernel, tuple(dilations), first_kept_tile, transpose_kept),
        grid=(b, t // tm),
        in_specs=[
            pl.BlockSpec((1, tm, d), lambda bi, i: (bi, i, 0)),
            _const_spec(g_q.shape), _const_spec(g_kv.shape),
            _const_spec(w_q.shape), _const_spec(w_kv.shape),
            pl.BlockSpec((tm, LANES), lambda bi, i: (i, 0)),
            pl.BlockSpec((tm, LANES), lambda bi, i: (i, 0)),
        ],
        out_specs=q_specs + kv_specs + kv_specs + [kept, kept],
        out_shape=q_shapes + kv_shapes + kv_shapes + [kept_shape, kept_shape],
        scratch_shapes=[pltpu.VMEM((GROUP_Q_WIDTH // LANES, tm, LANES), F32)],
        compiler_params=pltpu.CompilerParams(
            dimension_semantics=("arbitrary", "arbitrary"), vmem_limit_bytes=VMEM_LIMIT),
        name="qkv_rope",
    )(x, g_q, g_kv, w_q, w_kv, cos, sin_signed)


def _rope_tables(positions):
    half = HEAD_DIM // 2
    inv = ROPE_THETA ** (-jnp.arange(0, HEAD_DIM, 2, dtype=F32) / HEAD_DIM)
    ang = positions.astype(F32)[:, None] * jnp.tile(inv, LANES // half)[None, :]
    sign = jnp.where((jnp.arange(LANES) // half) % 2 == 0, -1.0, 1.0).astype(F32)
    return jnp.cos(ang), jnp.sin(ang) * sign[None, :]


def _stat_lane(c, k):
    return (1 - c) * HEAD_DIM + k


def _band_attention_kernel(seg_rows, tiles_per_subseq, q_ref, kp_ref, kc_ref, vp_ref, vc_ref,
                           o_ref, stat_ref, k_scr, v_scr, bias_scr, s_scr, p_scr):
    n_seg = q_ref.shape[0] // seg_rows
    bands_per_seg = seg_rows // BAND
    n_bands = n_seg * bands_per_seg
    step = pl.program_id(1)
    lane = lax.broadcasted_iota(jnp.int32, (1, LANES), 1)
    for c in range(KV_PER_GROUP):
        own = (lane // HEAD_DIM) == c
        for u in range(n_seg):
            base = u * (seg_rows + BAND)
            for dst, k_rows, v_rows in (
                    (slice(base, base + BAND), kp_ref[...], vp_ref[...]),
                    (slice(base + BAND, base + BAND + seg_rows),
                     kc_ref[u * seg_rows:(u + 1) * seg_rows, :],
                     vc_ref[u * seg_rows:(u + 1) * seg_rows, :])):
                k_scr[c, dst, :] = jnp.where(own, k_rows, jnp.zeros((), BF16))
                v_scr[c, dst, :] = jnp.where(own, v_rows, jnp.ones((), BF16))
    stat_ref[...] = jnp.ones(stat_ref.shape, F32)

    row = lax.broadcasted_iota(jnp.int32, (BAND, 2 * BAND), 0)
    col = lax.broadcasted_iota(jnp.int32, (BAND, 2 * BAND), 1)
    band_bias = jnp.where((col >= row) & (col <= row + BAND), 0.0, NEG_BIG)
    bias_scr[0] = band_bias
    bias_scr[1] = band_bias + jnp.where(col < BAND, NEG_BIG, 0.0)
    starts_subseq = True if n_seg > 1 else lax.rem(step, tiles_per_subseq) == 0

    def rows_of(g):
        return pl.ds(g * BAND if isinstance(g, int) else pl.multiple_of(g * BAND, BAND), BAND)

    def key_rows_of(g):
        staged = g + g // bands_per_seg
        start = staged * BAND if isinstance(g, int) else pl.multiple_of(staged * BAND, BAND)
        return pl.ds(start, 2 * BAND)

    def scores(g, slot):
        q = jnp.concatenate(
            [q_ref[rows_of(g), k * LANES:(k + 1) * LANES] for k in range(Q_PER_KV)], axis=0)
        for c in range(KV_PER_GROUP):
            s_scr[slot, c] = lax.dot_general(q, k_scr[c, key_rows_of(g), :],
                                             (((1,), (1,)), ((), ())),
                                             preferred_element_type=F32)
            yield

    def softmax(g, slot):
        first = jnp.logical_and(starts_subseq, g % bands_per_seg == 0).astype(jnp.int32)
        for c in range(KV_PER_GROUP):
            for k in range(Q_PER_KV):
                part = pl.ds(k * BAND, BAND)
                s = s_scr[slot, c, part, :] + bias_scr[first]
                m = jnp.max(s, axis=1, keepdims=True)
                p_scr[slot, c, part, :] = jnp.exp(s - m).astype(BF16)
                stat_ref[rows_of(g), pl.ds(_stat_lane(c, k) + STAT_MAX_OFFSET, 1)] = m
                yield

    def weighted_values(g, slot):
        ov = []
        for c in range(KV_PER_GROUP):
            ov.append(jnp.dot(p_scr[slot, c], v_scr[c, key_rows_of(g), :],
                              preferred_element_type=F32))
            yield
        low = lane < HEAD_DIM
        sums = None
        for k in range(Q_PER_KV):
            part = slice(k * BAND, (k + 1) * BAND)
            o_ref[rows_of(g), k * LANES:(k + 1) * LANES] = (
                jnp.where(low, ov[0][part], ov[1][part]).astype(o_ref.dtype))
            sums_k = jnp.where(low, ov[1][part], ov[0][part])
            sums = sums_k if sums is None else jnp.where((lane % HEAD_DIM) == k, sums_k, sums)
            yield
        for c in range(KV_PER_GROUP):
            lanes = pl.ds(_stat_lane(c, 0), Q_PER_KV)
            stat_ref[rows_of(g), lanes] = sums[:, _stat_lane(c, 0):_stat_lane(c, 0) + Q_PER_KV]
        yield

    _interleave(scores(0, 0))
    _interleave(softmax(0, 0), scores(1, 1))

    def steady(t, carry):
        for g, slot in ((2 * t + 1, 1), (2 * t + 2, 0)):
            _interleave(weighted_values(g - 1, 1 - slot), softmax(g, slot),
                        scores(g + 1, 1 - slot))
        return carry

    assert n_bands % 2 == 0
    lax.fori_loop(0, (n_bands - 2) // 2, steady, 0)
    last = n_bands - 1
    _interleave(weighted_values(last - 1, (last - 1) % 2), softmax(last, last % 2))
    _interleave(weighted_values(last, last % 2))


def _band_attention(q, k, v, block_rows):
    b, dilation, sub, _ = q.shape
    seg_rows = min(sub, block_rows)
    total = dilation * sub
    assert total % block_rows == 0 and block_rows % seg_rows == 0 and sub % seg_rows == 0
    assert seg_rows % BAND == 0 and block_rows >= 2 * BAND
    n_seg = block_rows // seg_rows
    flat = lambda a: a.reshape(b, total, a.shape[-1])
    cur = lambda width: pl.BlockSpec((None, block_rows, width), lambda bi, i: (bi, i, 0))
    prev = pl.BlockSpec((None, BAND, GROUP_KV_WIDTH),
                        lambda bi, i: (bi, jnp.maximum(i * (block_rows // BAND) - 1, 0), 0))
    staged_rows = n_seg * (seg_rows + BAND)
    o, stat = pl.pallas_call(
        functools.partial(_band_attention_kernel, seg_rows, sub // seg_rows),
        grid=(b, total // block_rows),
        in_specs=[cur(GROUP_Q_WIDTH), prev, cur(GROUP_KV_WIDTH), prev, cur(GROUP_KV_WIDTH)],
        out_specs=[cur(GROUP_Q_WIDTH), cur(STAT_LANES)],
        out_shape=[jax.ShapeDtypeStruct((b, total, GROUP_Q_WIDTH), BF16),
                   jax.ShapeDtypeStruct((b, total, STAT_LANES), F32)],
        scratch_shapes=[pltpu.VMEM((KV_PER_GROUP, staged_rows, GROUP_KV_WIDTH), BF16),
                        pltpu.VMEM((KV_PER_GROUP, staged_rows, GROUP_KV_WIDTH), BF16),
                        pltpu.VMEM((2, BAND, 2 * BAND), F32),
                        pltpu.VMEM((2, KV_PER_GROUP, Q_PER_KV * BAND, 2 * BAND), F32),
                        pltpu.VMEM((2, KV_PER_GROUP, Q_PER_KV * BAND, 2 * BAND), BF16)],
        compiler_params=pltpu.CompilerParams(
            dimension_semantics=("arbitrary", "arbitrary"), vmem_limit_bytes=VMEM_LIMIT),
        name=f"band_attention_d{dilation}",
    )(flat(q), flat(k), flat(k), flat(v), flat(v))
    return (o.reshape(b, dilation, sub, GROUP_Q_WIDTH), stat.reshape(b, dilation, sub, STAT_LANES))


SAMPLE_Q_ROWS = 16
NEW_KEY_ROWS = 8
SAMPLE_SEQS_PER_STEP = 4


def _sample_attention_kernel(n_new, q_ref, kn_ref, vn_ref, kc0, kc1, kc2, vc0, vc1, vc2,
                             o_ref, lse_ref):
    n_seq = q_ref.shape[0]
    n_rows = q_ref.shape[2]
    contract_last = (((1,), (1,)), ((), ()))
    cache_refs = ((kc0, vc0), (kc1, vc1), (kc2, vc2))
    bias_c, bias_n = [], []
    for gi, (kc_ref, _) in enumerate(cache_refs):
        dmask = ATTN_DILATIONS[gi] - 1
        n_cache = kc_ref.shape[3]
        assert n_cache == ATTN_WINDOWS[gi]
        t_c = lax.broadcasted_iota(jnp.int32, (n_rows, n_cache), 0) & (n_new - 1)
        j_c = lax.broadcasted_iota(jnp.int32, (n_rows, n_cache), 1)
        valid_c = (j_c >= t_c) & (((j_c - t_c) & dmask) == 0)
        t_n = lax.broadcasted_iota(jnp.int32, (n_rows, NEW_KEY_ROWS), 0) & (n_new - 1)
        j_n = lax.broadcasted_iota(jnp.int32, (n_rows, NEW_KEY_ROWS), 1)
        valid_n = (j_n <= t_n) & (((t_n - j_n) & dmask) == 0)
        bias_c.append(jnp.where(valid_c, 0.0, NEG_BIG))
        bias_n.append(jnp.where(valid_n, 0.0, NEG_BIG))

    def one_sequence(b):
        kn_all = kn_ref[b]
        vn_all = vn_ref[b]
        for gi, (kc_ref, vc_ref) in enumerate(cache_refs):
            for c in range(KV_PER_GROUP):
                kvh = gi * KV_PER_GROUP + c
                ns = slice(kvh * HEAD_DIM, (kvh + 1) * HEAD_DIM)
                q = q_ref[b, kvh]
                kt = kc_ref[b, c].astype(BF16)
                vt = vc_ref[b, c].astype(BF16)
                s_c = jnp.dot(q, kt, preferred_element_type=F32) + bias_c[gi]
                s_n = lax.dot_general(q, kn_all[:, ns].astype(BF16), contract_last,
                                      preferred_element_type=F32) + bias_n[gi]
                m = jnp.maximum(jnp.max(s_c, axis=1, keepdims=True),
                                jnp.max(s_n, axis=1, keepdims=True))
                p_c = jnp.exp(s_c - m)
                p_n = jnp.exp(s_n - m)
                l = jnp.sum(p_c, axis=1, keepdims=True) + jnp.sum(p_n, axis=1, keepdims=True)
                o = (lax.dot_general(p_c.astype(BF16), vt, contract_last,
                                     preferred_element_type=F32)
                     + jnp.dot(p_n.astype(BF16), vn_all[:, ns].astype(BF16),
                               preferred_element_type=F32))
                o_ref[b, kvh] = o / l
                lse_ref[b, kvh] = m + jnp.log(l)
                yield

    _interleave(*[one_sequence(b) for b in range(n_seq)])


def _sample_attention(q16, k_new, v_new, cache_kt, cache_vt, n_new):
    b = q16.shape[0]
    n_kv = q16.shape[1]
    assert cache_kt.shape[3] == KV_WINDOW and n_new & (n_new - 1) == 0 and n_new <= NEW_KEY_ROWS

    per_step = SAMPLE_SEQS_PER_STEP
    assert b % per_step == 0

    def cache_spec(gi):
        cols = ATTN_WINDOWS[gi]
        last = KV_WINDOW // cols - 1
        return pl.BlockSpec((per_step, KV_PER_GROUP, HEAD_DIM, cols),
                            lambda bi: (bi, gi, 0, last))

    whole = lambda a: pl.BlockSpec((per_step,) + a.shape[1:],
                                   lambda bi: (bi,) + (0,) * (a.ndim - 1))
    o_shape = jax.ShapeDtypeStruct((b, n_kv, SAMPLE_Q_ROWS, HEAD_DIM), F32)
    lse_shape = jax.ShapeDtypeStruct((b, n_kv, SAMPLE_Q_ROWS, 1), F32)
    return pl.pallas_call(
        functools.partial(_sample_attention_kernel, n_new),
        grid=(b // per_step,),
        in_specs=[whole(q16), whole(k_new), whole(v_new)]
        + [cache_spec(gi) for gi in range(N_GROUPS)] * 2,
        out_specs=[whole(o_shape), whole(lse_shape)],
        out_shape=[o_shape, lse_shape],
        compiler_params=pltpu.CompilerParams(
            dimension_semantics=("arbitrary",), vmem_limit_bytes=VMEM_LIMIT),
        name="sample_attention",
    )(q16, k_new, v_new, cache_kt, cache_kt, cache_kt, cache_vt, cache_vt, cache_vt)


def _attn_mix_steps(dilations, x_ref, o_refs, s_refs, g, wo_ref, o_scr, stat_scr, a_scr,
                    x1_ref, h_ref):
    tm = x_ref.shape[0]
    chunks_per_group = GROUP_Q_WIDTH // LANES
    for gi, (o_ref, s_ref) in enumerate(zip(o_refs, s_refs)):
        for r in range(dilations[gi]):
            rows = _subseq_rows(tm, r, dilations[gi])
            stat_scr[gi, rows, :] = s_ref[r]
            for c in range(chunks_per_group):
                o_scr[gi * chunks_per_group + c, rows, :] = (
                    o_ref[r, :, c * LANES:(c + 1) * LANES].astype(F32))
        yield
    sums = [stat_scr[gi] for gi in range(N_GROUPS)]
    maxes = [pltpu.roll(sm, STAT_LANES - STAT_MAX_OFFSET, 1) for sm in sums]
    top = jnp.maximum(jnp.maximum(maxes[0], maxes[1]), maxes[2])
    es = [jnp.exp(mx - top) for mx in maxes]
    den = sums[0] * es[0] + sums[1] * es[1] + sums[2] * es[2]
    slot_id = lax.broadcasted_iota(jnp.int32, (tm, GROUP_Q_WIDTH), 1) // HEAD_DIM
    for gi in range(N_GROUPS):
        scale = es[gi] / den
        wide = jnp.zeros((tm, GROUP_Q_WIDTH), F32)
        for slot in range(HEADS_PER_GROUP):
            stat_lane = _stat_lane(slot % KV_PER_GROUP, slot // KV_PER_GROUP)
            wide = jnp.where(slot_id == slot, scale[:, stat_lane:stat_lane + 1], wide)
        for c in range(chunks_per_group):
            ci = gi * chunks_per_group + c
            a_scr[:, ci * LANES:(ci + 1) * LANES] = (
                o_scr[ci] * wide[:, c * LANES:(c + 1) * LANES]).astype(BF16)
        yield
    mix = jnp.dot(a_scr[...], wo_ref[...], preferred_element_type=F32)
    x1_ref[...] = x_ref[...] + _rms(mix, g[1:2])
    yield
    h_ref[...] = _rms(x1_ref[...], g[2:3]).astype(BF16)
    yield


def _layer1_scratch(tm, d, slots):
    lead = (slots,) if slots else ()
    return [pltpu.VMEM((Q_WIDTH // LANES, tm, LANES), F32),
            pltpu.VMEM((N_GROUPS, tm, STAT_LANES), F32),
            pltpu.VMEM((tm, Q_WIDTH), BF16),
            pltpu.VMEM(lead + (tm, d), F32),
            pltpu.VMEM(lead + (tm, d), BF16)]


def _layer1_single_kernel(dilations, x_ref, o0_ref, o1_ref, o2_ref, s0_ref, s1_ref, s2_ref, g_ref,
                          wo_ref, wup_ref, wdn_ref, y_ref, o_scr, stat_scr, a_scr, x1_scr, h_scr):
    g = g_ref[...]
    _interleave(_attn_mix_steps(dilations, x_ref, (o0_ref, o1_ref, o2_ref),
                                (s0_ref, s1_ref, s2_ref), g, wo_ref, o_scr, stat_scr, a_scr,
                                x1_scr, h_scr))
    y_ref[...] = _mlp_from_hidden(x1_scr[...], h_scr[...], g[3:4], wup_ref, wdn_ref)


def _layer1_first_kernel(dilations, x_ref, o0_ref, o1_ref, o2_ref, s0_ref, s1_ref, s2_ref, g_ref,
                         wo_ref, x1_ref, h_ref, o_scr, stat_scr, a_scr):
    _interleave(_attn_mix_steps(dilations, x_ref, (o0_ref, o1_ref, o2_ref),
                                (s0_ref, s1_ref, s2_ref), g_ref[...], wo_ref, o_scr, stat_scr,
                                a_scr, x1_ref, h_ref))


def _layer1_prompt_kernel(dilations, xa_ref, oa0, oa1, oa2, sa0, sa1, sa2,
                          xb_ref, ob0, ob1, ob2, sb0, sb1, sb2, x1f_ref, hf_ref, g_ref, wo_ref,
                          wup_ref, wdn_ref, y_ref, o_scr, stat_scr, a_scr, x1_scr, h_scr):
    s = pl.program_id(0)
    tm = xa_ref.shape[0]
    g = g_ref[...]

    @pl.when(s == 0)
    def _():
        x1_scr[0] = x1f_ref[...]
        h_scr[0] = hf_ref[...]

    def prepare(x_ref, o_refs, s_refs, slot):
        return _attn_mix_steps(dilations, x_ref, o_refs, s_refs, g, wo_ref, o_scr, stat_scr,
                               a_scr, x1_scr.at[slot], h_scr.at[slot])

    def mlp(slot, rows):
        def emit(y):
            y_ref[rows, :] = y
        return _mlp_steps(lambda: x1_scr[slot], lambda: h_scr[slot], g[3:4], wup_ref, wdn_ref,
                          emit)

    _interleave(mlp(0, slice(0, tm)), prepare(xa_ref, (oa0, oa1, oa2), (sa0, sa1, sa2), 1))
    _interleave(mlp(1, slice(tm, 2 * tm)), prepare(xb_ref, (ob0, ob1, ob2), (sb0, sb1, sb2), 0))


def _layer1(x, os, stats, gains, w_o, w_up, w_down, layer, tm):
    b, t, d = x.shape
    dilations = tuple(o.shape[1] for o in os)
    tiles_per_seq = t // tm
    n_tiles = b * tiles_per_seq
    assert t % tm == 0 and all(tm % (16 * dil) == 0 for dil in dilations)
    x2 = x.reshape(b * t, d)
    weights = (gains, w_o, w_up, w_down)
    weight_specs = [_const_spec(gains.shape), _const_spec(w_o.shape),
                    _layer_spec(w_up, layer), _layer_spec(w_down, layer)]

    def tile_specs(tile):
        sub = lambda dil, width: pl.BlockSpec(
            (None, dil, tm // dil, width),
            lambda s: (tile(s) // tiles_per_seq, 0, tile(s) % tiles_per_seq, 0))
        return ([pl.BlockSpec((tm, d), lambda s: (tile(s), 0))]
                + [sub(dil, GROUP_Q_WIDTH) for dil in dilations]
                + [sub(dil, STAT_LANES) for dil in dilations])

    if n_tiles == 1:
        y = pl.pallas_call(
            functools.partial(_layer1_single_kernel, dilations),
            grid=(1,),
            in_specs=tile_specs(lambda s: 0) + weight_specs,
            out_specs=pl.BlockSpec((tm, d), lambda s: (0, 0)),
            out_shape=jax.ShapeDtypeStruct((b * t, d), F32),
            scratch_shapes=_layer1_scratch(tm, d, 0),
            compiler_params=pltpu.CompilerParams(vmem_limit_bytes=VMEM_LIMIT),
            name="layer1_single_tile",
        )(x2, *os, *stats, *weights)
        return y.reshape(b, t, d)

    assert n_tiles % 2 == 0
    x1_first, h_first = pl.pallas_call(
        functools.partial(_layer1_first_kernel, dilations),
        grid=(1,),
        in_specs=tile_specs(lambda s: 0) + [_const_spec(gains.shape), _const_spec(w_o.shape)],
        out_specs=[pl.BlockSpec((tm, d), lambda s: (0, 0))] * 2,
        out_shape=[jax.ShapeDtypeStruct((tm, d), F32), jax.ShapeDtypeStruct((tm, d), BF16)],
        scratch_shapes=_layer1_scratch(tm, d, 0)[:3],
        compiler_params=pltpu.CompilerParams(vmem_limit_bytes=VMEM_LIMIT),
        name="layer1_first_tile",
    )(x2, *os, *stats, gains, w_o)

    tile_a = lambda s: 2 * s + 1
    tile_b = lambda s: jnp.minimum(2 * s + 2, n_tiles - 1)
    y = pl.pallas_call(
        functools.partial(_layer1_prompt_kernel, dilations),
        grid=(n_tiles // 2,),
        in_specs=tile_specs(tile_a) + tile_specs(tile_b)
        + [_const_spec(x1_first.shape), _const_spec(h_first.shape)]
        + weight_specs,
        out_specs=pl.BlockSpec((2 * tm, d), lambda s: (s, 0)),
        out_shape=jax.ShapeDtypeStruct((b * t, d), F32),
        scratch_shapes=_layer1_scratch(tm, d, 2),
        compiler_params=pltpu.CompilerParams(
            dimension_semantics=("arbitrary",), vmem_limit_bytes=VMEM_LIMIT),
        name="layer1_prompt",
    )(x2, *os, *stats, x2, *os, *stats, x1_first, h_first, *weights)
    return y.reshape(b, t, d)


PROMPT_TILE = 512
LAYER1_TILE = 256
ATTN_BLOCK_ROWS = 2048


def kernel(x_prompt, x_sample, cache_pool, cache_k, cache_v, norm_gains, kv_norm_gain, w_pool,
           pool_scale, w_q, w_o, w_kv, w_up, w_down):
    depth = norm_gains.shape[0]
    assert depth == 2 and cache_pool.shape[0] == 1 and w_q.shape[0] == 1
    bp, tp, d = x_prompt.shape
    bs, ts, _ = x_sample.shape

    g0, g1 = norm_gains[0], norm_gains[1]
    g1_q = g1[0:1]
    g_kv = kv_norm_gain[None, :]
    wp = w_pool[0].astype(BF16)
    ps = pool_scale[0][None, :]
    wq = w_q[0].reshape(d, N_GROUPS, KV_PER_GROUP, Q_PER_KV, HEAD_DIM).transpose(0, 1, 3, 2, 4)
    wq = wq.reshape(d, Q_WIDTH).astype(BF16)
    wo = w_o[0].reshape(N_GROUPS, KV_PER_GROUP, Q_PER_KV, HEAD_DIM, d).transpose(0, 2, 1, 3, 4)
    wo = wo.reshape(Q_WIDTH, d).astype(BF16)
    wkv = w_kv.astype(BF16)
    wup, wdn = w_up.astype(BF16), w_down.astype(BF16)

    keep = min(KV_WINDOW, tp)
    xp1, utail = _layer0_prompt(x_prompt, g0, wp, ps, wup, wdn, 0, PROMPT_TILE)
    cos_p, sin_p = _rope_tables(jnp.arange(tp))
    qkv_p = _qkv(xp1, g1_q, g_kv, wq, wkv, cos_p, sin_p, PROMPT_TILE, keep, ATTN_DILATIONS, True)
    os_p, stats_p = [], []
    for gi, dil in enumerate(ATTN_DILATIONS):
        o, stat = _band_attention(qkv_p[gi], qkv_p[N_GROUPS + gi], qkv_p[2 * N_GROUPS + gi],
                                  ATTN_BLOCK_ROWS)
        os_p.append(o)
        stats_p.append(stat)
    y_prompt = _layer1(xp1, os_p, stats_p, g1, wo, wup, wdn, 1, LAYER1_TILE)
    pool_prompt = utail[:, POOL_HALO - POOL_BUF:][None]
    k_prompt = qkv_p[-2].reshape(bp, N_KV_HEADS, HEAD_DIM, keep).transpose(0, 3, 1, 2)
    v_prompt = qkv_p[-1].reshape(bp, N_KV_HEADS, HEAD_DIM, keep).transpose(0, 3, 1, 2)

    n_tok = ts * bs
    xs_tm = jnp.swapaxes(x_sample, 0, 1)
    buf_tm = jnp.swapaxes(cache_pool[0], 0, 1)
    xs1_tm, us_tm = _layer0_sample(xs_tm, buf_tm, PAST_LEN, g0, wp, ps, wup, wdn, 0)
    xs1 = xs1_tm.reshape(1, n_tok, d)
    cos_s, sin_s = _rope_tables(PAST_LEN + jnp.arange(n_tok) // bs)
    qkv_s = _qkv(xs1, g1_q, g_kv, wq, wkv, cos_s, sin_s, n_tok, n_tok, (1,) * N_GROUPS, False)
    q_s = jnp.concatenate([q[0, 0] for q in qkv_s[:N_GROUPS]], axis=-1)
    q16 = q_s.reshape(ts, bs, N_GROUPS, Q_PER_KV, KV_PER_GROUP, HEAD_DIM)
    q16 = q16.transpose(1, 2, 4, 3, 0, 5).reshape(bs, N_KV_HEADS, Q_PER_KV, ts, HEAD_DIM)
    q16 = jnp.pad(q16, ((0, 0), (0, 0), (0, SAMPLE_Q_ROWS // ts - Q_PER_KV), (0, 0), (0, 0)))
    q16 = q16.reshape(bs, N_KV_HEADS, SAMPLE_Q_ROWS, HEAD_DIM)
    k_s = jnp.swapaxes(qkv_s[-2].reshape(ts, bs, KV_WIDTH), 0, 1)
    v_s = jnp.swapaxes(qkv_s[-1].reshape(ts, bs, KV_WIDTH), 0, 1)
    pad_new = ((0, 0), (0, NEW_KEY_ROWS - ts), (0, 0))
    o16, lse16 = _sample_attention(
        q16, jnp.pad(k_s, pad_new), jnp.pad(v_s, pad_new),
        cache_k.transpose(0, 2, 3, 1), cache_v.transpose(0, 2, 3, 1), ts)
    heads_padded = SAMPLE_Q_ROWS // ts
    o_s = o16.reshape(bs, N_GROUPS, KV_PER_GROUP, heads_padded, ts, HEAD_DIM)[:, :, :, :Q_PER_KV]
    o_s = o_s.transpose(4, 0, 1, 3, 2, 5).reshape(1, 1, n_tok, N_GROUPS, GROUP_Q_WIDTH).astype(BF16)
    lse_s = lse16.reshape(bs, N_GROUPS, KV_PER_GROUP, heads_padded, ts)[:, :, :, :Q_PER_KV]
    lse_s = lse_s.transpose(4, 0, 1, 2, 3).reshape(1, 1, n_tok, N_GROUPS, KV_PER_GROUP, Q_PER_KV)
    ones = lambda n: jnp.ones((1, 1, n_tok, N_GROUPS, n), F32)
    max_lane = [_stat_lane(c, 0) + STAT_MAX_OFFSET for c in range(KV_PER_GROUP)]
    assert max_lane[1] < max_lane[0]
    stat_s = jnp.concatenate(
        [ones(max_lane[1]), lse_s[..., 1, :],
         ones(max_lane[0] - max_lane[1] - Q_PER_KV), lse_s[..., 0, :],
         ones(STAT_LANES - max_lane[0] - Q_PER_KV)], axis=-1)
    ys_tm = _layer1(xs1, [o_s[:, :, :, gi] for gi in range(N_GROUPS)],
                    [stat_s[:, :, :, gi] for gi in range(N_GROUPS)], g1, wo, wup, wdn, 1, n_tok)
    y_sample = jnp.swapaxes(ys_tm.reshape(ts, bs, d), 0, 1)
    u_s = jnp.swapaxes(us_tm, 0, 1)
    pool_sample = jnp.concatenate([cache_pool[0], u_s], axis=1)[:, -POOL_BUF:][None]
    k_sample = k_s.reshape(bs, ts, N_KV_HEADS, HEAD_DIM)
    v_sample = v_s.reshape(bs, ts, N_KV_HEADS, HEAD_DIM)

    return (y_prompt, y_sample, pool_prompt, k_prompt, v_prompt, pool_sample, k_sample, v_sample)
```

```python
import functools

import jax
import jax.numpy as jnp
from jax import lax
from jax.experimental import pallas as pl
from jax.experimental.pallas import tpu as pltpu

F32 = jnp.float32
BF16 = jnp.bfloat16

EPS = 1e-6
ROPE_THETA = 10000.0
PAST_LEN = 16384
POOL_WINDOWS = (2, 4, 8, 16)
POOL_BUF = max(POOL_WINDOWS) - 1
POOL_HALO = 16
HEAD_DIM = 64
ATTN_WINDOWS = (128, 512, 2048)
ATTN_DILATIONS = (1, 4, 16)
N_GROUPS = len(ATTN_WINDOWS)
KV_PER_GROUP = 2
Q_PER_KV = 3
HEADS_PER_GROUP = KV_PER_GROUP * Q_PER_KV
N_KV_HEADS = N_GROUPS * KV_PER_GROUP
GROUP_Q_WIDTH = HEADS_PER_GROUP * HEAD_DIM
GROUP_KV_WIDTH = KV_PER_GROUP * HEAD_DIM
Q_WIDTH = N_GROUPS * GROUP_Q_WIDTH
KV_WIDTH = N_GROUPS * GROUP_KV_WIDTH
KV_WINDOW = max(ATTN_WINDOWS)
BAND = 128
NEG_BIG = -1e30

LANES = 128
STAT_LANES = LANES
STAT_MAX_OFFSET = 8
VMEM_LIMIT = 56 * 1024 * 1024
FF_CHUNK = 512

for _w, _d in zip(ATTN_WINDOWS, ATTN_DILATIONS):
    assert _w // _d == BAND and _w % _d == 0
assert GROUP_KV_WIDTH == LANES


def _rms(x, g):
    return x * lax.rsqrt(jnp.mean(x * x, axis=-1, keepdims=True) + EPS) * g


def _mlp_residual(x1, g_in, g_out, wup_ref, wdn_ref):
    return _mlp_from_hidden(x1, _rms(x1, g_in).astype(BF16), g_out, wup_ref, wdn_ref)


def _mlp_from_hidden(x1, h, g_out, wup_ref, wdn_ref):
    out = []
    for _ in _mlp_steps(lambda: x1, lambda: h, g_out, wup_ref, wdn_ref, out.append):
        pass
    return out[0]


def _mlp_steps(load_x1, load_h, g_out, wup_ref, wdn_ref, emit):
    acc = []
    yield from _mlp_matmul_steps(load_h, wup_ref, wdn_ref, acc.append)
    emit(load_x1() + _rms(acc[0], g_out))
    yield


def _mlp_matmul_steps(load_h, wup_ref, wdn_ref, emit_acc):
    d_ff = wup_ref.shape[1]
    h = load_h()
    acc = None
    for c in range(d_ff // FF_CHUNK):
        cs = slice(c * FF_CHUNK, (c + 1) * FF_CHUNK)
        a = jnp.dot(h, wup_ref[:, cs], preferred_element_type=F32)
        a = jnp.square(jnp.maximum(a, 0.0)).astype(BF16)
        part = jnp.dot(a, wdn_ref[cs, :], preferred_element_type=F32)
        acc = part if acc is None else acc + part
        if c + 1 == d_ff // FF_CHUNK:
            emit_acc(acc)
        yield


def _interleave(*generators):
    live = list(generators)
    while live:
        for gen in list(live):
            try:
                next(gen)
            except StopIteration:
                live.remove(gen)


def _const_spec(shape):
    zeros = (0,) * len(shape)
    return pl.BlockSpec(shape, lambda *_: zeros, pipeline_mode=pl.Buffered(1))


def _layer_spec(w, layer):
    index = (layer,) + (0,) * (w.ndim - 1)
    return pl.BlockSpec((None,) + w.shape[1:], lambda *_: index, pipeline_mode=pl.Buffered(1))


def _subseq_spec(dilation, rows, width):
    return pl.BlockSpec((None, dilation, rows // dilation, width), lambda bi, i: (bi, 0, i, 0))


def _subseq_rows(n_rows, r, dilation):
    n = n_rows // dilation
    return pl.ds(r, n, stride=dilation) if dilation > 1 else pl.ds(0, n)


def _pool_mixer_steps(x_ref, xh_ref, tile_in_seq, g, wp_ref, ps_ref, ext_ref, x1_ref, h_ref,
                      utail_ref=None):
    tm = x_ref.shape[0]
    pool_ch = wp_ref.shape[1]
    u = _rms(x_ref[...], g[0:1])
    uh = _rms(xh_ref[0:POOL_HALO, :], g[0:1]) * jnp.where(tile_in_seq > 0, 1.0, 0.0)
    ext_ref[0:POOL_HALO, :] = uh
    ext_ref[POOL_HALO:, :] = u
    if utail_ref is not None:
        utail_ref[0] = u[tm - POOL_HALO:, :]
    yield
    pos = tile_in_seq * tm + lax.broadcasted_iota(jnp.int32, (tm, 1), 0)
    parts = []
    for gi, w in enumerate(POOL_WINDOWS):
        cs = slice(gi * pool_ch, (gi + 1) * pool_ch)
        s = ext_ref[POOL_HALO:, cs]
        for j in range(1, w):
            s = s + ext_ref[POOL_HALO - j:POOL_HALO - j + tm, cs]
        cnt = jnp.minimum(w, pos + 1).astype(F32)
        pooled = s / cnt - ext_ref[POOL_HALO:, cs]
        parts.append(jnp.dot(pooled.astype(BF16), wp_ref[gi], preferred_element_type=F32))
        yield
    mix = jnp.concatenate(parts, axis=1) * ps_ref[...]
    x1 = x_ref[...] + _rms(mix, g[1:2])
    x1_ref[...] = x1
    yield
    h_ref[...] = _rms(x1_ref[...], g[2:3]).astype(BF16)
    yield


def _layer0_first_kernel(x_ref, g_ref, wp_ref, ps_ref, x1_ref, h_ref, ext_scr):
    _interleave(_pool_mixer_steps(x_ref, x_ref, 0, g_ref[...], wp_ref, ps_ref, ext_scr,
                                  x1_ref, h_ref))


def _layer0_prompt_kernel(tiles_per_seq, xa_ref, xha_ref, xb_ref, xhb_ref, x1f_ref, hf_ref, g_ref,
                          wp_ref, ps_ref, wup_ref, wdn_ref, y_ref, utail_ref,
                          x1_scr, h_scr, ext_scr):
    s = pl.program_id(0)
    n_tiles = 2 * pl.num_programs(0)
    tm = xa_ref.shape[0]
    g = g_ref[...]

    @pl.when(s == 0)
    def _():
        x1_scr[0] = x1f_ref[...]
        h_scr[0] = hf_ref[...]

    def prepare(x_ref, xh_ref, tile, slot, tail_ref):
        return _pool_mixer_steps(x_ref, xh_ref, lax.rem(tile, tiles_per_seq), g, wp_ref, ps_ref,
                                 ext_scr.at[slot], x1_scr.at[slot], h_scr.at[slot], tail_ref)

    def mlp(slot, rows):
        def emit(y):
            y_ref[rows, :] = y
        return _mlp_steps(lambda: x1_scr[slot], lambda: h_scr[slot], g[3:4], wup_ref, wdn_ref,
                          emit)

    _interleave(mlp(0, slice(0, tm)), prepare(xa_ref, xha_ref, 2 * s + 1, 1, utail_ref))
    _interleave(mlp(1, slice(tm, 2 * tm)),
                prepare(xb_ref, xhb_ref, jnp.minimum(2 * s + 2, n_tiles - 1), 0, None))


def _layer0_prompt(x, gains, w_pool, pool_scale, w_up, w_down, layer, tm):
    b, t, d = x.shape
    tiles_per_seq = t // tm
    n_tiles = b * tiles_per_seq
    assert t % tm == 0 and tm % POOL_HALO == 0 and tiles_per_seq % 2 == 0
    halo_per_tile = tm // POOL_HALO
    x2 = x.reshape(b * t, d)

    x1_first, h_first = pl.pallas_call(
        _layer0_first_kernel,
        grid=(1,),
        in_specs=[pl.BlockSpec((tm, d), lambda i: (0, 0)), _const_spec(gains.shape),
                  _const_spec(w_pool.shape), _const_spec(pool_scale.shape)],
        out_specs=[pl.BlockSpec((tm, d), lambda i: (0, 0))] * 2,
        out_shape=[jax.ShapeDtypeStruct((tm, d), F32), jax.ShapeDtypeStruct((tm, d), BF16)],
        scratch_shapes=[pltpu.VMEM((tm + POOL_HALO, d), F32)],
        compiler_params=pltpu.CompilerParams(vmem_limit_bytes=VMEM_LIMIT),
        name="layer0_first_tile",
    )(x2, gains, w_pool, pool_scale)

    tile_a = lambda s: 2 * s + 1
    tile_b = lambda s: jnp.minimum(2 * s + 2, n_tiles - 1)
    tile_spec = lambda tile: pl.BlockSpec((tm, d), lambda s: (tile(s), 0))
    halo_spec = lambda tile: pl.BlockSpec((POOL_HALO, d),
                                          lambda s: (tile(s) * halo_per_tile - 1, 0))
    y, utail = pl.pallas_call(
        functools.partial(_layer0_prompt_kernel, tiles_per_seq),
        grid=(n_tiles // 2,),
        in_specs=[
            tile_spec(tile_a), halo_spec(tile_a), tile_spec(tile_b), halo_spec(tile_b),
            _const_spec(x1_first.shape), _const_spec(h_first.shape),
            _const_spec(gains.shape), _const_spec(w_pool.shape), _const_spec(pool_scale.shape),
            _layer_spec(w_up, layer), _layer_spec(w_down, layer),
        ],
        out_specs=[
            pl.BlockSpec((2 * tm, d), lambda s: (s, 0)),
            pl.BlockSpec((1, POOL_HALO, d), lambda s: (tile_a(s) // tiles_per_seq, 0, 0)),
        ],
        out_shape=[
            jax.ShapeDtypeStruct((b * t, d), F32),
            jax.ShapeDtypeStruct((b, POOL_HALO, d), F32),
        ],
        scratch_shapes=[pltpu.VMEM((2, tm, d), F32), pltpu.VMEM((2, tm, d), BF16),
                        pltpu.VMEM((2, tm + POOL_HALO, d), F32)],
        compiler_params=pltpu.CompilerParams(
            dimension_semantics=("arbitrary",), vmem_limit_bytes=VMEM_LIMIT),
        name="layer0_prompt",
    )(x2, x2, x2, x2, x1_first, h_first, gains, w_pool, pool_scale, w_up, w_down)
    return y.reshape(b, t, d), utail


def _layer0_sample_kernel(start_pos, x_ref, buf_ref, g_ref, wp_ref, ps_ref, wup_ref, wdn_ref,
                          y_ref, u_ref):
    n_t, n_b, _ = x_ref.shape
    pool_ch = wp_ref.shape[1]
    g = g_ref[...]
    xs = [x_ref[t] for t in range(n_t)]
    us = [_rms(xt, g[0:1]) for xt in xs]
    ext = [buf_ref[j] for j in range(POOL_BUF)] + us
    parts = []
    for gi, w in enumerate(POOL_WINDOWS):
        cs = slice(gi * pool_ch, (gi + 1) * pool_ch)
        rows = []
        for t in range(n_t):
            s = us[t][:, cs]
            for j in range(1, w):
                s = s + ext[POOL_BUF + t - j][:, cs]
            cnt = float(min(w, start_pos + t + 1))
            rows.append(s / cnt - us[t][:, cs])
        pooled = jnp.concatenate(rows, axis=0)
        parts.append(jnp.dot(pooled.astype(BF16), wp_ref[gi], preferred_element_type=F32))
    mix = jnp.concatenate(parts, axis=1) * ps_ref[...]
    x = jnp.concatenate(xs, axis=0)
    x1 = x + _rms(mix, g[1:2])
    y = _mlp_residual(x1, g[2:3], g[3:4], wup_ref, wdn_ref)
    for t in range(n_t):
        y_ref[t] = y[t * n_b:(t + 1) * n_b]
        u_ref[t] = us[t]


def _layer0_sample(x_tm, buf_tm, start_pos, gains, w_pool, pool_scale, w_up, w_down, layer):
    assert start_pos + 1 >= max(POOL_WINDOWS) and buf_tm.shape[0] == POOL_BUF
    return pl.pallas_call(
        functools.partial(_layer0_sample_kernel, start_pos),
        grid=(1,),
        in_specs=[_const_spec(a.shape) for a in (x_tm, buf_tm, gains, w_pool, pool_scale)]
        + [_layer_spec(w_up, layer), _layer_spec(w_down, layer)],
        out_specs=[pl.BlockSpec(x_tm.shape, lambda i: (0, 0, 0))] * 2,
        out_shape=[jax.ShapeDtypeStruct(x_tm.shape, F32), jax.ShapeDtypeStruct(x_tm.shape, F32)],
        compiler_params=pltpu.CompilerParams(vmem_limit_bytes=VMEM_LIMIT),
        name="layer0_sample",
    )(x_tm, buf_tm, gains, w_pool, pool_scale, w_up, w_down)


def _rope(x, cos, sin_signed):
    lane = lax.broadcasted_iota(jnp.int32, (x.shape[0], LANES), 1)
    first_half = (lane & (HEAD_DIM // 2)) == 0
    out = []
    for c in range(x.shape[1] // LANES):
        xc = x[:, c * LANES:(c + 1) * LANES]
        partner = jnp.where(first_half,
                            pltpu.roll(xc, LANES - HEAD_DIM // 2, 1),
                            pltpu.roll(xc, HEAD_DIM // 2, 1))
        out.append(xc * cos + partner * sin_signed)
    return out


def _qkv_kernel(dilations, first_kept_tile, transpose_kept,
                x_ref, gq_ref, gkv_ref, wq_ref, wkv_ref, cos_ref, sin_ref,
                q0_ref, q1_ref, q2_ref, k0_ref, k1_ref, k2_ref, v0_ref, v1_ref, v2_ref,
                kf_ref, vf_ref, stage_scr):
    i = pl.program_id(1)
    x = x_ref[0]
    cos = cos_ref[...]
    sin_signed = sin_ref[...]
    chunks_per_group = GROUP_Q_WIDTH // LANES

    def emit(out_ref, dilation, chunks):
        if dilation == 1:
            for c, chunk in enumerate(chunks):
                out_ref[0, :, c * LANES:(c + 1) * LANES] = chunk.astype(BF16)
            return
        for c, chunk in enumerate(chunks):
            stage_scr[c] = chunk
        for r in range(dilation):
            rows = _subseq_rows(stage_scr.shape[1], r, dilation)
            for c in range(len(chunks)):
                out_ref[r, :, c * LANES:(c + 1) * LANES] = stage_scr[c, rows, :].astype(BF16)

    u = _rms(x, gq_ref[...]).astype(BF16)
    q = jnp.dot(u, wq_ref[...], preferred_element_type=F32)
    scale = HEAD_DIM ** -0.5
    q_chunks = _rope(q, cos * scale, sin_signed * scale)
    for gi, q_ref in enumerate((q0_ref, q1_ref, q2_ref)):
        emit(q_ref, dilations[gi], q_chunks[gi * chunks_per_group:(gi + 1) * chunks_per_group])

    un = _rms(x, gkv_ref[...]).astype(BF16)
    kv = jnp.dot(un, wkv_ref[...], preferred_element_type=F32)
    k_chunks = _rope(kv[:, :KV_WIDTH], cos, sin_signed)
    v_chunks = [kv[:, KV_WIDTH + gi * LANES:KV_WIDTH + (gi + 1) * LANES] for gi in range(N_GROUPS)]
    for gi, (k_ref, v_ref) in enumerate(((k0_ref, v0_ref), (k1_ref, v1_ref), (k2_ref, v2_ref))):
        emit(k_ref, dilations[gi], [k_chunks[gi]])
        emit(v_ref, dilations[gi], [v_chunks[gi]])

    @pl.when(i >= first_kept_tile)
    def _():
        for gi in range(N_GROUPS):
            cs = slice(gi * LANES, (gi + 1) * LANES)
            if transpose_kept:
                kf_ref[0, cs, :] = k_chunks[gi].T
                vf_ref[0, cs, :] = v_chunks[gi].T
            else:
                kf_ref[0, :, cs] = k_chunks[gi]
                vf_ref[0, :, cs] = v_chunks[gi]


def _qkv(x, g_q, g_kv, w_q, w_kv, cos, sin_signed, tm, keep_rows, dilations, transpose_kept):
    b, t, d = x.shape
    assert t % tm == 0 and keep_rows % tm == 0 and all(tm % (16 * dil) == 0 for dil in dilations)
    first_kept_tile = (t - keep_rows) // tm
    kept_block = lambda bi, i: jnp.maximum(i - first_kept_tile, 0)
    if transpose_kept:
        kept = pl.BlockSpec((1, KV_WIDTH, tm), lambda bi, i: (bi, 0, kept_block(bi, i)))
        kept_shape = jax.ShapeDtypeStruct((b, KV_WIDTH, keep_rows), F32)
    else:
        kept = pl.BlockSpec((1, tm, KV_WIDTH), lambda bi, i: (bi, kept_block(bi, i), 0))
        kept_shape = jax.ShapeDtypeStruct((b, keep_rows, KV_WIDTH), F32)
    sub_shape = lambda dil, width: jax.ShapeDtypeStruct((b, dil, t // dil, width), BF16)
    q_specs = [_subseq_spec(dil, tm, GROUP_Q_WIDTH) for dil in dilations]
    kv_specs = [_subseq_spec(dil, tm, GROUP_KV_WIDTH) for dil in dilations]
    q_shapes = [sub_shape(dil, GROUP_Q_WIDTH) for dil in dilations]
    kv_shapes = [sub_shape(dil, GROUP_KV_WIDTH) for dil in dilations]
    return pl.pallas_call(
        functools.partial(_qkv_kernel, tuple(dilations), first_kept_tile, transpose_kept),
        grid=(b, t // tm),
        in_specs=[
            pl.BlockSpec((1, tm, d), lambda bi, i: (bi, i, 0)),
            _const_spec(g_q.shape), _const_spec(g_kv.shape),
            _const_spec(w_q.shape), _const_spec(w_kv.shape),
            pl.BlockSpec((tm, LANES), lambda bi, i: (i, 0)),
            pl.BlockSpec((tm, LANES), lambda bi, i: (i, 0)),
        ],
        out_specs=q_specs + kv_specs + kv_specs + [kept, kept],
        out_shape=q_shapes + kv_shapes + kv_shapes + [kept_shape, kept_shape],
        scratch_shapes=[pltpu.VMEM((GROUP_Q_WIDTH // LANES, tm, LANES), F32)],
        compiler_params=pltpu.CompilerParams(
            dimension_semantics=("arbitrary", "arbitrary"), vmem_limit_bytes=VMEM_LIMIT),
        name="qkv_rope",
    )(x, g_q, g_kv, w_q, w_kv, cos, sin_signed)


def _rope_tables(positions):
    half = HEAD_DIM // 2
    inv = ROPE_THETA ** (-jnp.arange(0, HEAD_DIM, 2, dtype=F32) / HEAD_DIM)
    ang = positions.astype(F32)[:, None] * jnp.tile(inv, LANES // half)[None, :]
    sign = jnp.where((jnp.arange(LANES) // half) % 2 == 0, -1.0, 1.0).astype(F32)
    return jnp.cos(ang), jnp.sin(ang) * sign[None, :]


def _stat_lane(c, k):
    return (1 - c) * HEAD_DIM + k


def _band_attention_kernel(seg_rows, tiles_per_subseq, q_ref, kp_ref, kc_ref, vp_ref, vc_ref,
                           o_ref, stat_ref, k_scr, v_scr, bias_scr, s_scr, p_scr):
    n_seg = q_ref.shape[0] // seg_rows
    bands_per_seg = seg_rows // BAND
    n_bands = n_seg * bands_per_seg
    step = pl.program_id(1)
    lane = lax.broadcasted_iota(jnp.int32, (1, LANES), 1)
    for c in range(KV_PER_GROUP):
        own = (lane // HEAD_DIM) == c
        for u in range(n_seg):
            base = u * (seg_rows + BAND)
            for dst, k_rows, v_rows in (
                    (slice(base, base + BAND), kp_ref[...], vp_ref[...]),
                    (slice(base + BAND, base + BAND + seg_rows),
                     kc_ref[u * seg_rows:(u + 1) * seg_rows, :],
                     vc_ref[u * seg_rows:(u + 1) * seg_rows, :])):
                k_scr[c, dst, :] = jnp.where(own, k_rows, jnp.zeros((), BF16))
                v_scr[c, dst, :] = jnp.where(own, v_rows, jnp.ones((), BF16))
    stat_ref[...] = jnp.ones(stat_ref.shape, F32)

    row = lax.broadcasted_iota(jnp.int32, (BAND, 2 * BAND), 0)
    col = lax.broadcasted_iota(jnp.int32, (BAND, 2 * BAND), 1)
    band_bias = jnp.where((col >= row) & (col <= row + BAND), 0.0, NEG_BIG)
    bias_scr[0] = band_bias
    bias_scr[1] = band_bias + jnp.where(col < BAND, NEG_BIG, 0.0)
    starts_subseq = True if n_seg > 1 else lax.rem(step, tiles_per_subseq) == 0

    def rows_of(g):
        return pl.ds(g * BAND if isinstance(g, int) else pl.multiple_of(g * BAND, BAND), BAND)

    def key_rows_of(g):
        staged = g + g // bands_per_seg
        start = staged * BAND if isinstance(g, int) else pl.multiple_of(staged * BAND, BAND)
        return pl.ds(start, 2 * BAND)

    def scores(g, slot):
        q = jnp.concatenate(
            [q_ref[rows_of(g), k * LANES:(k + 1) * LANES] for k in range(Q_PER_KV)], axis=0)
        for c in range(KV_PER_GROUP):
            s_scr[slot, c] = lax.dot_general(q, k_scr[c, key_rows_of(g), :],
                                             (((1,), (1,)), ((), ())),
                                             preferred_element_type=F32)
            yield

    def softmax(g, slot):
        first = jnp.logical_and(starts_subseq, g % bands_per_seg == 0).astype(jnp.int32)
        for c in range(KV_PER_GROUP):
            for k in range(Q_PER_KV):
                part = pl.ds(k * BAND, BAND)
                s = s_scr[slot, c, part, :] + bias_scr[first]
                m = jnp.max(s, axis=1, keepdims=True)
                p_scr[slot, c, part, :] = jnp.exp(s - m).astype(BF16)
                stat_ref[rows_of(g), pl.ds(_stat_lane(c, k) + STAT_MAX_OFFSET, 1)] = m
                yield

    def weighted_values(g, slot):
        ov = []
        for c in range(KV_PER_GROUP):
            ov.append(jnp.dot(p_scr[slot, c], v_scr[c, key_rows_of(g), :],
                              preferred_element_type=F32))
            yield
        low = lane < HEAD_DIM
        sums = None
        for k in range(Q_PER_KV):
            part = slice(k * BAND, (k + 1) * BAND)
            o_ref[rows_of(g), k * LANES:(k + 1) * LANES] = (
                jnp.where(low, ov[0][part], ov[1][part]).astype(o_ref.dtype))
            sums_k = jnp.where(low, ov[1][part], ov[0][part])
            sums = sums_k if sums is None else jnp.where((lane % HEAD_DIM) == k, sums_k, sums)
            yield
        for c in range(KV_PER_GROUP):
            lanes = pl.ds(_stat_lane(c, 0), Q_PER_KV)
            stat_ref[rows_of(g), lanes] = sums[:, _stat_lane(c, 0):_stat_lane(c, 0) + Q_PER_KV]
        yield

    _interleave(scores(0, 0))
    _interleave(softmax(0, 0), scores(1, 1))

    def steady(t, carry):
        for g, slot in ((2 * t + 1, 1), (2 * t + 2, 0)):
            _interleave(weighted_values(g - 1, 1 - slot), softmax(g, slot),
                        scores(g + 1, 1 - slot))
        return carry

    assert n_bands % 2 == 0
    lax.fori_loop(0, (n_bands - 2) // 2, steady, 0)
    last = n_bands - 1
    _interleave(weighted_values(last - 1, (last - 1) % 2), softmax(last, last % 2))
    _interleave(weighted_values(last, last % 2))


def _band_attention(q, k, v, block_rows):
    b, dilation, sub, _ = q.shape
    seg_rows = min(sub, block_rows)
    total = dilation * sub
    assert total % block_rows == 0 and block_rows % seg_rows == 0 and sub % seg_rows == 0
    assert seg_rows % BAND == 0 and block_rows >= 2 * BAND
    n_seg = block_rows // seg_rows
    flat = lambda a: a.reshape(b, total, a.shape[-1])
    cur = lambda width: pl.BlockSpec((None, block_rows, width), lambda bi, i: (bi, i, 0))
    prev = pl.BlockSpec((None, BAND, GROUP_KV_WIDTH),
                        lambda bi, i: (bi, jnp.maximum(i * (block_rows // BAND) - 1, 0), 0))
    staged_rows = n_seg * (seg_rows + BAND)
    o, stat = pl.pallas_call(
        functools.partial(_band_attention_kernel, seg_rows, sub // seg_rows),
        grid=(b, total // block_rows),
        in_specs=[cur(GROUP_Q_WIDTH), prev, cur(GROUP_KV_WIDTH), prev, cur(GROUP_KV_WIDTH)],
        out_specs=[cur(GROUP_Q_WIDTH), cur(STAT_LANES)],
        out_shape=[jax.ShapeDtypeStruct((b, total, GROUP_Q_WIDTH), BF16),
                   jax.ShapeDtypeStruct((b, total, STAT_LANES), F32)],
        scratch_shapes=[pltpu.VMEM((KV_PER_GROUP, staged_rows, GROUP_KV_WIDTH), BF16),
                        pltpu.VMEM((KV_PER_GROUP, staged_rows, GROUP_KV_WIDTH), BF16),
                        pltpu.VMEM((2, BAND, 2 * BAND), F32),
                        pltpu.VMEM((2, KV_PER_GROUP, Q_PER_KV * BAND, 2 * BAND), F32),
                        pltpu.VMEM((2, KV_PER_GROUP, Q_PER_KV * BAND, 2 * BAND), BF16)],
        compiler_params=pltpu.CompilerParams(
            dimension_semantics=("arbitrary", "arbitrary"), vmem_limit_bytes=VMEM_LIMIT),
        name=f"band_attention_d{dilation}",
    )(flat(q), flat(k), flat(k), flat(v), flat(v))
    return (o.reshape(b, dilation, sub, GROUP_Q_WIDTH), stat.reshape(b, dilation, sub, STAT_LANES))


SAMPLE_Q_ROWS = 16
NEW_KEY_ROWS = 8
SAMPLE_SEQS_PER_STEP = 4


def _sample_attention_kernel(n_new, q_ref, kn_ref, vn_ref, kc0, kc1, kc2, vc0, vc1, vc2,
                             o_ref, lse_ref):
    n_seq = q_ref.shape[0]
    n_rows = q_ref.shape[2]
    contract_last = (((1,), (1,)), ((), ()))
    cache_refs = ((kc0, vc0), (kc1, vc1), (kc2, vc2))
    bias_c, bias_n = [], []
    for gi, (kc_ref, _) in enumerate(cache_refs):
        dmask = ATTN_DILATIONS[gi] - 1
        n_cache = kc_ref.shape[3]
        assert n_cache == ATTN_WINDOWS[gi]
        t_c = lax.broadcasted_iota(jnp.int32, (n_rows, n_cache), 0) & (n_new - 1)
        j_c = lax.broadcasted_iota(jnp.int32, (n_rows, n_cache), 1)
        valid_c = (j_c >= t_c) & (((j_c - t_c) & dmask) == 0)
        t_n = lax.broadcasted_iota(jnp.int32, (n_rows, NEW_KEY_ROWS), 0) & (n_new - 1)
        j_n = lax.broadcasted_iota(jnp.int32, (n_rows, NEW_KEY_ROWS), 1)
        valid_n = (j_n <= t_n) & (((t_n - j_n) & dmask) == 0)
        bias_c.append(jnp.where(valid_c, 0.0, NEG_BIG))
        bias_n.append(jnp.where(valid_n, 0.0, NEG_BIG))

    def one_sequence(b):
        kn_all = kn_ref[b]
        vn_all = vn_ref[b]
        for gi, (kc_ref, vc_ref) in enumerate(cache_refs):
            for c in range(KV_PER_GROUP):
                kvh = gi * KV_PER_GROUP + c
                ns = slice(kvh * HEAD_DIM, (kvh + 1) * HEAD_DIM)
                q = q_ref[b, kvh]
                kt = kc_ref[b, c].astype(BF16)
                vt = vc_ref[b, c].astype(BF16)
                s_c = jnp.dot(q, kt, preferred_element_type=F32) + bias_c[gi]
                s_n = lax.dot_general(q, kn_all[:, ns].astype(BF16), contract_last,
                                      preferred_element_type=F32) + bias_n[gi]
                m = jnp.maximum(jnp.max(s_c, axis=1, keepdims=True),
                                jnp.max(s_n, axis=1, keepdims=True))
                p_c = jnp.exp(s_c - m)
                p_n = jnp.exp(s_n - m)
                l = jnp.sum(p_c, axis=1, keepdims=True) + jnp.sum(p_n, axis=1, keepdims=True)
                o = (lax.dot_general(p_c.astype(BF16), vt, contract_last,
                                     preferred_element_type=F32)
                     + jnp.dot(p_n.astype(BF16), vn_all[:, ns].astype(BF16),
                               preferred_element_type=F32))
                o_ref[b, kvh] = o / l
                lse_ref[b, kvh] = m + jnp.log(l)
                yield

    _interleave(*[one_sequence(b) for b in range(n_seq)])


def _sample_attention(q16, k_new, v_new, cache_kt, cache_vt, n_new):
    b = q16.shape[0]
    n_kv = q16.shape[1]
    assert cache_kt.shape[3] == KV_WINDOW and n_new & (n_new - 1) == 0 and n_new <= NEW_KEY_ROWS

    per_step = SAMPLE_SEQS_PER_STEP
    assert b % per_step == 0

    def cache_spec(gi):
        cols = ATTN_WINDOWS[gi]
        last = KV_WINDOW // cols - 1
        return pl.BlockSpec((per_step, KV_PER_GROUP, HEAD_DIM, cols),
                            lambda bi: (bi, gi, 0, last))

    whole = lambda a: pl.BlockSpec((per_step,) + a.shape[1:],
                                   lambda bi: (bi,) + (0,) * (a.ndim - 1))
    o_shape = jax.ShapeDtypeStruct((b, n_kv, SAMPLE_Q_ROWS, HEAD_DIM), F32)
    lse_shape = jax.ShapeDtypeStruct((b, n_kv, SAMPLE_Q_ROWS, 1), F32)
    return pl.pallas_call(
        functools.partial(_sample_attention_kernel, n_new),
        grid=(b // per_step,),
        in_specs=[whole(q16), whole(k_new), whole(v_new)]
        + [cache_spec(gi) for gi in range(N_GROUPS)] * 2,
        out_specs=[whole(o_shape), whole(lse_shape)],
        out_shape=[o_shape, lse_shape],
        compiler_params=pltpu.CompilerParams(
            dimension_semantics=("arbitrary",), vmem_limit_bytes=VMEM_LIMIT),
        name="sample_attention",
    )(q16, k_new, v_new, cache_kt, cache_kt, cache_kt, cache_vt, cache_vt, cache_vt)


def _attn_mix_steps(dilations, x_ref, o_refs, s_refs, g, wo_ref, o_scr, stat_scr, a_scr,
                    x1_ref, h_ref):
    tm = x_ref.shape[0]
    chunks_per_group = GROUP_Q_WIDTH // LANES
    for gi, (o_ref, s_ref) in enumerate(zip(o_refs, s_refs)):
        for r in range(dilations[gi]):
            rows = _subseq_rows(tm, r, dilations[gi])
            stat_scr[gi, rows, :] = s_ref[r]
            for c in range(chunks_per_group):
                o_scr[gi * chunks_per_group + c, rows, :] = (
                    o_ref[r, :, c * LANES:(c + 1) * LANES].astype(F32))
        yield
    sums = [stat_scr[gi] for gi in range(N_GROUPS)]
    maxes = [pltpu.roll(sm, STAT_LANES - STAT_MAX_OFFSET, 1) for sm in sums]
    top = jnp.maximum(jnp.maximum(maxes[0], maxes[1]), maxes[2])
    es = [jnp.exp(mx - top) for mx in maxes]
    den = sums[0] * es[0] + sums[1] * es[1] + sums[2] * es[2]
    slot_id = lax.broadcasted_iota(jnp.int32, (tm, GROUP_Q_WIDTH), 1) // HEAD_DIM
    for gi in range(N_GROUPS):
        scale = es[gi] / den
        wide = jnp.zeros((tm, GROUP_Q_WIDTH), F32)
        for slot in range(HEADS_PER_GROUP):
            stat_lane = _stat_lane(slot % KV_PER_GROUP, slot // KV_PER_GROUP)
            wide = jnp.where(slot_id == slot, scale[:, stat_lane:stat_lane + 1], wide)
        for c in range(chunks_per_group):
            ci = gi * chunks_per_group + c
            a_scr[:, ci * LANES:(ci + 1) * LANES] = (
                o_scr[ci] * wide[:, c * LANES:(c + 1) * LANES]).astype(BF16)
        yield
    mix = jnp.dot(a_scr[...], wo_ref[...], preferred_element_type=F32)
    x1_ref[...] = x_ref[...] + _rms(mix, g[1:2])
    yield
    h_ref[...] = _rms(x1_ref[...], g[2:3]).astype(BF16)
    yield


def _layer1_scratch(tm, d, slots):
    lead = (slots,) if slots else ()
    return [pltpu.VMEM((Q_WIDTH // LANES, tm, LANES), F32),
            pltpu.VMEM((N_GROUPS, tm, STAT_LANES), F32),
            pltpu.VMEM((tm, Q_WIDTH), BF16),
            pltpu.VMEM(lead + (tm, d), F32),
            pltpu.VMEM(lead + (tm, d), BF16)]


def _layer1_single_kernel(dilations, x_ref, o0_ref, o1_ref, o2_ref, s0_ref, s1_ref, s2_ref, g_ref,
                          wo_ref, wup_ref, wdn_ref, y_ref, o_scr, stat_scr, a_scr, x1_scr, h_scr):
    g = g_ref[...]
    _interleave(_attn_mix_steps(dilations, x_ref, (o0_ref, o1_ref, o2_ref),
                                (s0_ref, s1_ref, s2_ref), g, wo_ref, o_scr, stat_scr, a_scr,
                                x1_scr, h_scr))
    y_ref[...] = _mlp_from_hidden(x1_scr[...], h_scr[...], g[3:4], wup_ref, wdn_ref)


def _layer1_first_kernel(dilations, x_ref, o0_ref, o1_ref, o2_ref, s0_ref, s1_ref, s2_ref, g_ref,
                         wo_ref, x1_ref, h_ref, o_scr, stat_scr, a_scr):
    _interleave(_attn_mix_steps(dilations, x_ref, (o0_ref, o1_ref, o2_ref),
                                (s0_ref, s1_ref, s2_ref), g_ref[...], wo_ref, o_scr, stat_scr,
                                a_scr, x1_ref, h_ref))


def _layer1_prompt_kernel(dilations, x_ref, o0_ref, o1_ref, o2_ref, s0_ref, s1_ref, s2_ref,
                          x1f_ref, hf_ref, g_ref, wo_ref, wup_ref, wdn_ref, y_ref,
                          o_scr, stat_scr, a_scr, x1_scr, h_scr, acc_scr):
    s = pl.program_id(0)
    n_tiles = pl.num_programs(0) - 1
    g = g_ref[...]

    @pl.when(s == 0)
    def _():
        x1_scr[0] = x1f_ref[...]
        h_scr[0] = hf_ref[...]
        x1_scr[2] = jnp.zeros(x1_scr.shape[1:], F32)
        acc_scr[1] = jnp.zeros(acc_scr.shape[1:], F32)

    def finish_previous():
        y_ref[...] = x1_scr[lax.rem(s + 2, 3)] + _rms(acc_scr[lax.rem(s + 1, 2)], g[3:4])
        yield

    @pl.when(s < n_tiles)
    def _():
        cur = lax.rem(s, 2)

        def store_acc(acc):
            acc_scr[cur] = acc

        _interleave(
            finish_previous(),
            _mlp_matmul_steps(lambda: h_scr[cur], wup_ref, wdn_ref, store_acc),
            _attn_mix_steps(dilations, x_ref, (o0_ref, o1_ref, o2_ref), (s0_ref, s1_ref, s2_ref),
                            g, wo_ref, o_scr, stat_scr, a_scr,
                            x1_scr.at[lax.rem(s + 1, 3)], h_scr.at[1 - cur]))

    @pl.when(s == n_tiles)
    def _():
        _interleave(finish_previous())


def _layer1(x, os, stats, gains, w_o, w_up, w_down, layer, tm):
    b, t, d = x.shape
    dilations = tuple(o.shape[1] for o in os)
    tiles_per_seq = t // tm
    n_tiles = b * tiles_per_seq
    assert t % tm == 0 and all(tm % (16 * dil) == 0 for dil in dilations)
    x2 = x.reshape(b * t, d)
    weights = (gains, w_o, w_up, w_down)
    weight_specs = [_const_spec(gains.shape), _const_spec(w_o.shape),
                    _layer_spec(w_up, layer), _layer_spec(w_down, layer)]

    def tile_specs(tile):
        sub = lambda dil, width: pl.BlockSpec(
            (None, dil, tm // dil, width),
            lambda s: (tile(s) // tiles_per_seq, 0, tile(s) % tiles_per_seq, 0))
        return ([pl.BlockSpec((tm, d), lambda s: (tile(s), 0))]
                + [sub(dil, GROUP_Q_WIDTH) for dil in dilations]
                + [sub(dil, STAT_LANES) for dil in dilations])

    if n_tiles == 1:
        y = pl.pallas_call(
            functools.partial(_layer1_single_kernel, dilations),
            grid=(1,),
            in_specs=tile_specs(lambda s: 0) + weight_specs,
            out_specs=pl.BlockSpec((tm, d), lambda s: (0, 0)),
            out_shape=jax.ShapeDtypeStruct((b * t, d), F32),
            scratch_shapes=_layer1_scratch(tm, d, 0),
            compiler_params=pltpu.CompilerParams(vmem_limit_bytes=VMEM_LIMIT),
            name="layer1_single_tile",
        )(x2, *os, *stats, *weights)
        return y.reshape(b, t, d)

    x1_first, h_first = pl.pallas_call(
        functools.partial(_layer1_first_kernel, dilations),
        grid=(1,),
        in_specs=tile_specs(lambda s: 0) + [_const_spec(gains.shape), _const_spec(w_o.shape)],
        out_specs=[pl.BlockSpec((tm, d), lambda s: (0, 0))] * 2,
        out_shape=[jax.ShapeDtypeStruct((tm, d), F32), jax.ShapeDtypeStruct((tm, d), BF16)],
        scratch_shapes=_layer1_scratch(tm, d, 0)[:3],
        compiler_params=pltpu.CompilerParams(vmem_limit_bytes=VMEM_LIMIT),
        name="layer1_first_tile",
    )(x2, *os, *stats, gains, w_o)

    scratch = _layer1_scratch(tm, d, 0)[:3] + [
        pltpu.VMEM((3, tm, d), F32), pltpu.VMEM((2, tm, d), BF16), pltpu.VMEM((2, tm, d), F32)]
    y = pl.pallas_call(
        functools.partial(_layer1_prompt_kernel, dilations),
        grid=(n_tiles + 1,),
        in_specs=tile_specs(lambda s: jnp.minimum(s + 1, n_tiles - 1))
        + [_const_spec(x1_first.shape), _const_spec(h_first.shape)]
        + weight_specs,
        out_specs=pl.BlockSpec((tm, d), lambda s: (jnp.maximum(s - 1, 0), 0)),
        out_shape=jax.ShapeDtypeStruct((b * t, d), F32),
        scratch_shapes=scratch,
        compiler_params=pltpu.CompilerParams(
            dimension_semantics=("arbitrary",), vmem_limit_bytes=VMEM_LIMIT),
        name="layer1_prompt",
    )(x2, *os, *stats, x1_first, h_first, *weights)
    return y.reshape(b, t, d)


PROMPT_TILE = 512
LAYER1_TILE = 512
ATTN_BLOCK_ROWS = 2048


def kernel(x_prompt, x_sample, cache_pool, cache_k, cache_v, norm_gains, kv_norm_gain, w_pool,
           pool_scale, w_q, w_o, w_kv, w_up, w_down):
    depth = norm_gains.shape[0]
    assert depth == 2 and cache_pool.shape[0] == 1 and w_q.shape[0] == 1
    bp, tp, d = x_prompt.shape
    bs, ts, _ = x_sample.shape

    g0, g1 = norm_gains[0], norm_gains[1]
    g1_q = g1[0:1]
    g_kv = kv_norm_gain[None, :]
    wp = w_pool[0].astype(BF16)
    ps = pool_scale[0][None, :]
    wq = w_q[0].reshape(d, N_GROUPS, KV_PER_GROUP, Q_PER_KV, HEAD_DIM).transpose(0, 1, 3, 2, 4)
    wq = wq.reshape(d, Q_WIDTH).astype(BF16)
    wo = w_o[0].reshape(N_GROUPS, KV_PER_GROUP, Q_PER_KV, HEAD_DIM, d).transpose(0, 2, 1, 3, 4)
    wo = wo.reshape(Q_WIDTH, d).astype(BF16)
    wkv = w_kv.astype(BF16)
    wup, wdn = w_up.astype(BF16), w_down.astype(BF16)

    keep = min(KV_WINDOW, tp)
    xp1, utail = _layer0_prompt(x_prompt, g0, wp, ps, wup, wdn, 0, PROMPT_TILE)
    cos_p, sin_p = _rope_tables(jnp.arange(tp))
    qkv_p = _qkv(xp1, g1_q, g_kv, wq, wkv, cos_p, sin_p, PROMPT_TILE, keep, ATTN_DILATIONS, True)
    os_p, stats_p = [], []
    for gi, dil in enumerate(ATTN_DILATIONS):
        o, stat = _band_attention(qkv_p[gi], qkv_p[N_GROUPS + gi], qkv_p[2 * N_GROUPS + gi],
                                  ATTN_BLOCK_ROWS)
        os_p.append(o)
        stats_p.append(stat)
    y_prompt = _layer1(xp1, os_p, stats_p, g1, wo, wup, wdn, 1, LAYER1_TILE)
    pool_prompt = utail[:, POOL_HALO - POOL_BUF:][None]
    k_prompt = qkv_p[-2].reshape(bp, N_KV_HEADS, HEAD_DIM, keep).transpose(0, 3, 1, 2)
    v_prompt = qkv_p[-1].reshape(bp, N_KV_HEADS, HEAD_DIM, keep).transpose(0, 3, 1, 2)

    n_tok = ts * bs
    xs_tm = jnp.swapaxes(x_sample, 0, 1)
    buf_tm = jnp.swapaxes(cache_pool[0], 0, 1)
    xs1_tm, us_tm = _layer0_sample(xs_tm, buf_tm, PAST_LEN, g0, wp, ps, wup, wdn, 0)
    xs1 = xs1_tm.reshape(1, n_tok, d)
    cos_s, sin_s = _rope_tables(PAST_LEN + jnp.arange(n_tok) // bs)
    qkv_s = _qkv(xs1, g1_q, g_kv, wq, wkv, cos_s, sin_s, n_tok, n_tok, (1,) * N_GROUPS, False)
    q_s = jnp.concatenate([q[0, 0] for q in qkv_s[:N_GROUPS]], axis=-1)
    q16 = q_s.reshape(ts, bs, N_GROUPS, Q_PER_KV, KV_PER_GROUP, HEAD_DIM)
    q16 = q16.transpose(1, 2, 4, 3, 0, 5).reshape(bs, N_KV_HEADS, Q_PER_KV, ts, HEAD_DIM)
    q16 = jnp.pad(q16, ((0, 0), (0, 0), (0, SAMPLE_Q_ROWS // ts - Q_PER_KV), (0, 0), (0, 0)))
    q16 = q16.reshape(bs, N_KV_HEADS, SAMPLE_Q_ROWS, HEAD_DIM)
    k_s = jnp.swapaxes(qkv_s[-2].reshape(ts, bs, KV_WIDTH), 0, 1)
    v_s = jnp.swapaxes(qkv_s[-1].reshape(ts, bs, KV_WIDTH), 0, 1)
    pad_new = ((0, 0), (0, NEW_KEY_ROWS - ts), (0, 0))
    o16, lse16 = _sample_attention(
        q16, jnp.pad(k_s, pad_new), jnp.pad(v_s, pad_new),
        cache_k.transpose(0, 2, 3, 1), cache_v.transpose(0, 2, 3, 1), ts)
    heads_padded = SAMPLE_Q_ROWS // ts
    o_s = o16.reshape(bs, N_GROUPS, KV_PER_GROUP, heads_padded, ts, HEAD_DIM)[:, :, :, :Q_PER_KV]
    o_s = o_s.transpose(4, 0, 1, 3, 2, 5).reshape(1, 1, n_tok, N_GROUPS, GROUP_Q_WIDTH).astype(BF16)
    lse_s = lse16.reshape(bs, N_GROUPS, KV_PER_GROUP, heads_padded, ts)[:, :, :, :Q_PER_KV]
    lse_s = lse_s.transpose(4, 0, 1, 2, 3).reshape(1, 1, n_tok, N_GROUPS, KV_PER_GROUP, Q_PER_KV)
    ones = lambda n: jnp.ones((1, 1, n_tok, N_GROUPS, n), F32)
    max_lane = [_stat_lane(c, 0) + STAT_MAX_OFFSET for c in range(KV_PER_GROUP)]
    assert max_lane[1] < max_lane[0]
    stat_s = jnp.concatenate(
        [ones(max_lane[1]), lse_s[..., 1, :],
         ones(max_lane[0] - max_lane[1] - Q_PER_KV), lse_s[..., 0, :],
         ones(STAT_LANES - max_lane[0] - Q_PER_KV)], axis=-1)
    ys_tm = _layer1(xs1, [o_s[:, :, :, gi] for gi in range(N_GROUPS)],
                    [stat_s[:, :, :, gi] for gi in range(N_GROUPS)], g1, wo, wup, wdn, 1, n_tok)
    y_sample = jnp.swapaxes(ys_tm.reshape(ts, bs, d), 0, 1)
    u_s = jnp.swapaxes(us_tm, 0, 1)
    pool_sample = jnp.concatenate([cache_pool[0], u_s], axis=1)[:, -POOL_BUF:][None]
    k_sample = k_s.reshape(bs, ts, N_KV_HEADS, HEAD_DIM)
    v_sample = v_s.reshape(bs, ts, N_KV_HEADS, HEAD_DIM)

    return (y_prompt, y_sample, pool_prompt, k_prompt, v_prompt, pool_sample, k_sample, v_sample)
```

```python
import functools

import jax
import jax.numpy as jnp
from jax import lax
from jax.experimental import pallas as pl
from jax.experimental.pallas import tpu as pltpu

F32 = jnp.float32
BF16 = jnp.bfloat16

EPS = 1e-6
ROPE_THETA = 10000.0
PAST_LEN = 16384
POOL_WINDOWS = (2, 4, 8, 16)
POOL_BUF = max(POOL_WINDOWS) - 1
POOL_HALO = 16
HEAD_DIM = 64
ATTN_WINDOWS = (128, 512, 2048)
ATTN_DILATIONS = (1, 4, 16)
N_GROUPS = len(ATTN_WINDOWS)
KV_PER_GROUP = 2
Q_PER_KV = 3
HEADS_PER_GROUP = KV_PER_GROUP * Q_PER_KV
N_KV_HEADS = N_GROUPS * KV_PER_GROUP
GROUP_Q_WIDTH = HEADS_PER_GROUP * HEAD_DIM
GROUP_KV_WIDTH = KV_PER_GROUP * HEAD_DIM
Q_WIDTH = N_GROUPS * GROUP_Q_WIDTH
KV_WIDTH = N_GROUPS * GROUP_KV_WIDTH
KV_WINDOW = max(ATTN_WINDOWS)
BAND = 128
NEG_BIG = -1e30

LANES = 128
STAT_LANES = LANES
STAT_MAX_OFFSET = 8
VMEM_LIMIT = 56 * 1024 * 1024
FF_CHUNK = 512
NORM_ROW_BLOCKS = 4

for _w, _d in zip(ATTN_WINDOWS, ATTN_DILATIONS):
    assert _w // _d == BAND and _w % _d == 0
assert GROUP_KV_WIDTH == LANES


def _rms(x, g):
    return x * lax.rsqrt(jnp.mean(x * x, axis=-1, keepdims=True) + EPS) * g


def _mlp_residual(x1, g_in, g_out, wup_ref, wdn_ref):
    return _mlp_from_hidden(x1, _rms(x1, g_in).astype(BF16), g_out, wup_ref, wdn_ref)


def _mlp_from_hidden(x1, h, g_out, wup_ref, wdn_ref):
    acc = []
    _interleave(_mlp_matmul_steps(lambda: h, wup_ref, wdn_ref, acc.append))
    return x1 + _rms(acc[0], g_out)


def _mlp_matmul_steps(load_h, wup_ref, wdn_ref, emit_acc):
    n_chunks = wup_ref.shape[1] // FF_CHUNK
    yield n_chunks
    h = load_h()
    acc = None
    for c in range(n_chunks):
        cs = slice(c * FF_CHUNK, (c + 1) * FF_CHUNK)
        a = jnp.dot(h, wup_ref[:, cs], preferred_element_type=F32)
        a = jnp.square(jnp.maximum(a, 0.0)).astype(BF16)
        part = jnp.dot(a, wdn_ref[cs, :], preferred_element_type=F32)
        acc = part if acc is None else acc + part
        if c + 1 == n_chunks:
            emit_acc(acc)
        yield


def _row_blocks(n_rows):
    size = n_rows // NORM_ROW_BLOCKS if n_rows % (8 * NORM_ROW_BLOCKS) == 0 else n_rows
    return [slice(lo, lo + size) for lo in range(0, n_rows, size)]


def _residual_norm_steps(dst_ref, load_x, load_v, g):
    blocks = _row_blocks(dst_ref.shape[0])
    yield len(blocks)
    for rows in blocks:
        dst_ref[rows, :] = load_x(rows) + _rms(load_v(rows), g)
        yield


def _norm_cast_steps(dst_ref, load_v, g):
    blocks = _row_blocks(dst_ref.shape[0])
    yield len(blocks)
    for rows in blocks:
        dst_ref[rows, :] = _rms(load_v(rows), g).astype(dst_ref.dtype)
        yield


def _chain(*generators):
    counts = [next(gen) for gen in generators]
    yield sum(counts)
    for gen in generators:
        yield from gen


def _interleave(*generators):
    totals = [next(gen) for gen in generators]
    done = [0] * len(generators)
    while any(d < t for d, t in zip(done, totals)):
        i = min((i for i in range(len(generators)) if done[i] < totals[i]),
                key=lambda i: (done[i] + 1) / totals[i])
        next(generators[i])
        done[i] += 1


def _const_spec(shape):
    zeros = (0,) * len(shape)
    return pl.BlockSpec(shape, lambda *_: zeros, pipeline_mode=pl.Buffered(1))


def _layer_spec(w, layer):
    index = (layer,) + (0,) * (w.ndim - 1)
    return pl.BlockSpec((None,) + w.shape[1:], lambda *_: index, pipeline_mode=pl.Buffered(1))


def _subseq_spec(dilation, rows, width):
    return pl.BlockSpec((None, dilation, rows // dilation, width), lambda bi, i: (bi, 0, i, 0))


def _subseq_rows(n_rows, r, dilation):
    n = n_rows // dilation
    return pl.ds(r, n, stride=dilation) if dilation > 1 else pl.ds(0, n)


def _pool_mixer_steps(x_ref, xh_ref, tile_in_seq, g, wp_ref, ps_ref, ext_ref, x1_ref, h_ref,
                      utail_ref=None):
    tm = x_ref.shape[0]
    pool_ch = wp_ref.shape[1]
    blocks = _row_blocks(tm)
    yield 3 * len(blocks) + len(POOL_WINDOWS)
    uh = _rms(xh_ref[0:POOL_HALO, :], g[0:1]) * jnp.where(tile_in_seq > 0, 1.0, 0.0)
    ext_ref[0:POOL_HALO, :] = uh
    for rows in blocks:
        u = _rms(x_ref[rows, :], g[0:1])
        ext_ref[POOL_HALO + rows.start:POOL_HALO + rows.stop, :] = u
        if utail_ref is not None and rows.stop == tm:
            utail_ref[0] = u[u.shape[0] - POOL_HALO:, :]
        yield
    pos = tile_in_seq * tm + lax.broadcasted_iota(jnp.int32, (tm, 1), 0)
    parts = []
    for gi, w in enumerate(POOL_WINDOWS):
        cs = slice(gi * pool_ch, (gi + 1) * pool_ch)
        s = ext_ref[POOL_HALO:, cs]
        for j in range(1, w):
            s = s + ext_ref[POOL_HALO - j:POOL_HALO - j + tm, cs]
        cnt = jnp.minimum(w, pos + 1).astype(F32)
        pooled = s / cnt - ext_ref[POOL_HALO:, cs]
        parts.append(jnp.dot(pooled.astype(BF16), wp_ref[gi], preferred_element_type=F32))
        yield
    mix = jnp.concatenate(parts, axis=1) * ps_ref[...]
    tail = _chain(
        _residual_norm_steps(x1_ref, lambda rows: x_ref[rows, :], lambda rows: mix[rows], g[1:2]),
        _norm_cast_steps(h_ref, lambda rows: x1_ref[rows, :], g[2:3]))
    next(tail)
    yield from tail


def _layer0_first_kernel(x_ref, g_ref, wp_ref, ps_ref, x1_ref, h_ref, ext_scr):
    _interleave(_pool_mixer_steps(x_ref, x_ref, 0, g_ref[...], wp_ref, ps_ref, ext_scr,
                                  x1_ref, h_ref))


def _layer0_prompt_kernel(tiles_per_seq, xa_ref, xha_ref, xb_ref, xhb_ref, x1f_ref, hf_ref, g_ref,
                          wp_ref, ps_ref, wup_ref, wdn_ref, y_ref, utail_ref,
                          x1_scr, h_scr, ext_scr):
    s = pl.program_id(0)
    n_tiles = 2 * pl.num_programs(0)
    tm = xa_ref.shape[0]
    g = g_ref[...]

    @pl.when(s == 0)
    def _():
        x1_scr[0] = x1f_ref[...]
        h_scr[0] = hf_ref[...]

    def prepare(x_ref, xh_ref, tile, slot, tail_ref):
        return _pool_mixer_steps(x_ref, xh_ref, lax.rem(tile, tiles_per_seq), g, wp_ref, ps_ref,
                                 ext_scr.at[slot], x1_scr.at[slot], h_scr.at[slot], tail_ref)

    def mlp(slot):
        acc = []
        return _chain(
            _mlp_matmul_steps(lambda: h_scr[slot], wup_ref, wdn_ref, acc.append),
            _residual_norm_steps(y_ref.at[pl.ds(slot * tm, tm)],
                                 lambda rows: x1_scr[slot, rows, :],
                                 lambda rows: acc[0][rows], g[3:4]))

    _interleave(mlp(0), prepare(xa_ref, xha_ref, 2 * s + 1, 1, utail_ref))
    _interleave(mlp(1), prepare(xb_ref, xhb_ref, jnp.minimum(2 * s + 2, n_tiles - 1), 0, None))


def _layer0_prompt(x, gains, w_pool, pool_scale, w_up, w_down, layer, tm):
    b, t, d = x.shape
    tiles_per_seq = t // tm
    n_tiles = b * tiles_per_seq
    assert t % tm == 0 and tm % POOL_HALO == 0 and tiles_per_seq % 2 == 0
    halo_per_tile = tm // POOL_HALO
    x2 = x.reshape(b * t, d)

    x1_first, h_first = pl.pallas_call(
        _layer0_first_kernel,
        grid=(1,),
        in_specs=[pl.BlockSpec((tm, d), lambda i: (0, 0)), _const_spec(gains.shape),
                  _const_spec(w_pool.shape), _const_spec(pool_scale.shape)],
        out_specs=[pl.BlockSpec((tm, d), lambda i: (0, 0))] * 2,
        out_shape=[jax.ShapeDtypeStruct((tm, d), F32), jax.ShapeDtypeStruct((tm, d), BF16)],
        scratch_shapes=[pltpu.VMEM((tm + POOL_HALO, d), F32)],
        compiler_params=pltpu.CompilerParams(vmem_limit_bytes=VMEM_LIMIT),
        name="layer0_first_tile",
    )(x2, gains, w_pool, pool_scale)

    tile_a = lambda s: 2 * s + 1
    tile_b = lambda s: jnp.minimum(2 * s + 2, n_tiles - 1)
    tile_spec = lambda tile: pl.BlockSpec((tm, d), lambda s: (tile(s), 0))
    halo_spec = lambda tile: pl.BlockSpec((POOL_HALO, d),
                                          lambda s: (tile(s) * halo_per_tile - 1, 0))
    y, utail = pl.pallas_call(
        functools.partial(_layer0_prompt_kernel, tiles_per_seq),
        grid=(n_tiles // 2,),
        in_specs=[
            tile_spec(tile_a), halo_spec(tile_a), tile_spec(tile_b), halo_spec(tile_b),
            _const_spec(x1_first.shape), _const_spec(h_first.shape),
            _const_spec(gains.shape), _const_spec(w_pool.shape), _const_spec(pool_scale.shape),
            _layer_spec(w_up, layer), _layer_spec(w_down, layer),
        ],
        out_specs=[
            pl.BlockSpec((2 * tm, d), lambda s: (s, 0)),
            pl.BlockSpec((1, POOL_HALO, d), lambda s: (tile_a(s) // tiles_per_seq, 0, 0)),
        ],
        out_shape=[
            jax.ShapeDtypeStruct((b * t, d), F32),
            jax.ShapeDtypeStruct((b, POOL_HALO, d), F32),
        ],
        scratch_shapes=[pltpu.VMEM((2, tm, d), F32), pltpu.VMEM((2, tm, d), BF16),
                        pltpu.VMEM((2, tm + POOL_HALO, d), F32)],
        compiler_params=pltpu.CompilerParams(
            dimension_semantics=("arbitrary",), vmem_limit_bytes=VMEM_LIMIT),
        name="layer0_prompt",
    )(x2, x2, x2, x2, x1_first, h_first, gains, w_pool, pool_scale, w_up, w_down)
    return y.reshape(b, t, d), utail


def _layer0_sample_kernel(start_pos, x_ref, buf_ref, g_ref, wp_ref, ps_ref, wup_ref, wdn_ref,
                          y_ref, u_ref):
    n_t, n_b, _ = x_ref.shape
    pool_ch = wp_ref.shape[1]
    g = g_ref[...]
    xs = [x_ref[t] for t in range(n_t)]
    us = [_rms(xt, g[0:1]) for xt in xs]
    ext = [buf_ref[j] for j in range(POOL_BUF)] + us
    parts = []
    for gi, w in enumerate(POOL_WINDOWS):
        cs = slice(gi * pool_ch, (gi + 1) * pool_ch)
        rows = []
        for t in range(n_t):
            s = us[t][:, cs]
            for j in range(1, w):
                s = s + ext[POOL_BUF + t - j][:, cs]
            cnt = float(min(w, start_pos + t + 1))
            rows.append(s / cnt - us[t][:, cs])
        pooled = jnp.concatenate(rows, axis=0)
        parts.append(jnp.dot(pooled.astype(BF16), wp_ref[gi], preferred_element_type=F32))
    mix = jnp.concatenate(parts, axis=1) * ps_ref[...]
    x = jnp.concatenate(xs, axis=0)
    x1 = x + _rms(mix, g[1:2])
    y = _mlp_residual(x1, g[2:3], g[3:4], wup_ref, wdn_ref)
    for t in range(n_t):
        y_ref[t] = y[t * n_b:(t + 1) * n_b]
        u_ref[t] = us[t]


def _layer0_sample(x_tm, buf_tm, start_pos, gains, w_pool, pool_scale, w_up, w_down, layer):
    assert start_pos + 1 >= max(POOL_WINDOWS) and buf_tm.shape[0] == POOL_BUF
    return pl.pallas_call(
        functools.partial(_layer0_sample_kernel, start_pos),
        grid=(1,),
        in_specs=[_const_spec(a.shape) for a in (x_tm, buf_tm, gains, w_pool, pool_scale)]
        + [_layer_spec(w_up, layer), _layer_spec(w_down, layer)],
        out_specs=[pl.BlockSpec(x_tm.shape, lambda i: (0, 0, 0))] * 2,
        out_shape=[jax.ShapeDtypeStruct(x_tm.shape, F32), jax.ShapeDtypeStruct(x_tm.shape, F32)],
        compiler_params=pltpu.CompilerParams(vmem_limit_bytes=VMEM_LIMIT),
        name="layer0_sample",
    )(x_tm, buf_tm, gains, w_pool, pool_scale, w_up, w_down)


N_PROJ_CHUNKS = (Q_WIDTH + 2 * KV_WIDTH) // LANES
K_CHUNK0 = Q_WIDTH // LANES
V_CHUNK0 = K_CHUNK0 + KV_WIDTH // LANES
PROJ_PIECE = 512


def _rope_chunk(xc, cos, sin_signed):
    lane = lax.broadcasted_iota(jnp.int32, xc.shape, 1)
    first_half = (lane & (HEAD_DIM // 2)) == 0
    partner = jnp.where(first_half,
                        pltpu.roll(xc, LANES - HEAD_DIM // 2, 1),
                        pltpu.roll(xc, HEAD_DIM // 2, 1))
    return xc * cos + partner * sin_signed


def _project_steps(x_ref, gq, gkv, wq_ref, wkv_ref, cos_ref, sin_ref, dst):
    pieces = []
    for lhs_id, w_ref, chunk0 in ((0, wq_ref, 0), (1, wkv_ref, K_CHUNK0)):
        width = w_ref.shape[1]
        for lo in range(0, width, PROJ_PIECE):
            pieces.append((lhs_id, w_ref, lo, min(lo + PROJ_PIECE, width), chunk0 + lo // LANES))
    yield 1 + 2 * len(pieces)
    x = x_ref[...]
    lhs = (_rms(x, gq).astype(BF16), _rms(x, gkv).astype(BF16))
    yield
    cos = cos_ref[...]
    sin_signed = sin_ref[...]
    scale = HEAD_DIM ** -0.5

    def post(value, chunk0):
        for c in range(value.shape[1] // LANES):
            xc = value[:, c * LANES:(c + 1) * LANES]
            ci = chunk0 + c
            if ci < K_CHUNK0:
                xc = _rope_chunk(xc, cos * scale, sin_signed * scale)
            elif ci < V_CHUNK0:
                xc = _rope_chunk(xc, cos, sin_signed)
            dst[ci] = xc

    pending = None
    for lhs_id, w_ref, lo, hi, chunk0 in pieces:
        value = jnp.dot(lhs[lhs_id], w_ref[:, lo:hi], preferred_element_type=F32)
        yield
        if pending is not None:
            post(*pending)
            yield
        pending = (value, chunk0)
    post(*pending)
    yield


def _emit_steps(src, dilations, q_refs, k_refs, v_refs, part):
    tm = src.shape[1]
    chunks_per_group = GROUP_Q_WIDTH // LANES
    yield N_PROJ_CHUNKS
    for ci in range(N_PROJ_CHUNKS):
        if ci < K_CHUNK0:
            gi, c = divmod(ci, chunks_per_group)
            out_ref = q_refs[gi]
        else:
            gi, c = (ci - K_CHUNK0) % N_GROUPS, 0
            out_ref = (k_refs if ci < V_CHUNK0 else v_refs)[gi]
        n = tm // dilations[gi]
        for r in range(dilations[gi]):
            out_ref[r, pl.ds(part * n, n), c * LANES:(c + 1) * LANES] = (
                src[ci, _subseq_rows(tm, r, dilations[gi]), :].astype(BF16))
        yield


def _write_kept(src, kf_ref, vf_ref, part, transposed):
    tm = src.shape[1]
    for gi in range(N_GROUPS):
        cs = slice(gi * LANES, (gi + 1) * LANES)
        for out_ref, chunk0 in ((kf_ref, K_CHUNK0), (vf_ref, V_CHUNK0)):
            if transposed:
                out_ref[0, cs, part * tm:(part + 1) * tm] = src[chunk0 + gi].T
            else:
                out_ref[0, part * tm:(part + 1) * tm, cs] = src[chunk0 + gi]


def _qkv_single_kernel(dilations, x_ref, gq_ref, gkv_ref, wq_ref, wkv_ref, cos_ref, sin_ref,
                       q0_ref, q1_ref, q2_ref, k0_ref, k1_ref, k2_ref, v0_ref, v1_ref, v2_ref,
                       kf_ref, vf_ref, proj_scr):
    _interleave(_project_steps(x_ref, gq_ref[...], gkv_ref[...], wq_ref, wkv_ref, cos_ref, sin_ref,
                               proj_scr))
    _interleave(_emit_steps(proj_scr, dilations, (q0_ref, q1_ref, q2_ref),
                            (k0_ref, k1_ref, k2_ref), (v0_ref, v1_ref, v2_ref), 0))
    _write_kept(proj_scr, kf_ref, vf_ref, 0, False)


def _qkv_prompt_kernel(dilations, tiles_per_seq, first_kept_tile,
                       x0_ref, xa_ref, xb_ref, gq_ref, gkv_ref, wq_ref, wkv_ref,
                       cos0_ref, sin0_ref, cosa_ref, sina_ref, cosb_ref, sinb_ref,
                       q0_ref, q1_ref, q2_ref, k0_ref, k1_ref, k2_ref, v0_ref, v1_ref, v2_ref,
                       kf_ref, vf_ref, proj_scr):
    s = pl.program_id(0)
    gq, gkv = gq_ref[...], gkv_ref[...]
    outs = ((q0_ref, q1_ref, q2_ref), (k0_ref, k1_ref, k2_ref), (v0_ref, v1_ref, v2_ref))

    def project(x_ref, cos_ref, sin_ref, slot):
        return _project_steps(x_ref, gq, gkv, wq_ref, wkv_ref, cos_ref, sin_ref, proj_scr.at[slot])

    @pl.when(s == 0)
    def _():
        _interleave(project(x0_ref, cos0_ref, sin0_ref, 0))

    kept = lax.rem(2 * s, tiles_per_seq) >= first_kept_tile
    for part, x_ref, cos_ref, sin_ref in ((0, xa_ref, cosa_ref, sina_ref),
                                          (1, xb_ref, cosb_ref, sinb_ref)):
        _interleave(project(x_ref, cos_ref, sin_ref, 1 - part),
                    _emit_steps(proj_scr.at[part], dilations, *outs, part))

        @pl.when(kept)
        def _():
            _write_kept(proj_scr.at[part], kf_ref, vf_ref, part, True)


def _qkv_shapes(b, t, dilations):
    sub = lambda dil, width: jax.ShapeDtypeStruct((b, dil, t // dil, width), BF16)
    return ([sub(dil, GROUP_Q_WIDTH) for dil in dilations]
            + [sub(dil, GROUP_KV_WIDTH) for dil in dilations] * 2)


def _qkv_single(x, g_q, g_kv, w_q, w_kv, cos, sin_signed):
    t, _ = x.shape
    dilations = (1,) * N_GROUPS
    shapes = _qkv_shapes(1, t, dilations)
    kept_shape = jax.ShapeDtypeStruct((1, t, KV_WIDTH), F32)
    whole = lambda shape: pl.BlockSpec(shape, lambda i: (0,) * len(shape))
    return pl.pallas_call(
        functools.partial(_qkv_single_kernel, dilations),
        grid=(1,),
        in_specs=[whole(a.shape) for a in (x, g_q, g_kv, w_q, w_kv, cos, sin_signed)],
        out_specs=[pl.BlockSpec((None,) + sh.shape[1:], lambda i: (0, 0, 0, 0)) for sh in shapes]
        + [whole(kept_shape.shape)] * 2,
        out_shape=shapes + [kept_shape, kept_shape],
        scratch_shapes=[pltpu.VMEM((N_PROJ_CHUNKS, t, LANES), F32)],
        compiler_params=pltpu.CompilerParams(vmem_limit_bytes=VMEM_LIMIT),
        name="qkv_rope_single_tile",
    )(x, g_q, g_kv, w_q, w_kv, cos, sin_signed)


def _qkv_prompt(x, g_q, g_kv, w_q, w_kv, cos, sin_signed, tm, keep_rows, dilations):
    b, t, d = x.shape
    tiles_per_seq = t // tm
    n_tiles = b * tiles_per_seq
    first_kept_tile = (t - keep_rows) // tm
    assert t % tm == 0 and keep_rows % (2 * tm) == 0 and tiles_per_seq % 2 == 0
    assert first_kept_tile % 2 == 0 and all(tm % (16 * dil) == 0 for dil in dilations)
    x2 = x.reshape(b * t, d)
    tile_0 = lambda s: 0
    tile_a = lambda s: 2 * s + 1
    tile_b = lambda s: jnp.minimum(2 * s + 2, n_tiles - 1)
    x_spec = lambda tile: pl.BlockSpec((tm, d), lambda s: (tile(s), 0))
    table_spec = lambda tile: pl.BlockSpec((tm, LANES), lambda s: (tile(s) % tiles_per_seq, 0))
    seq_of = lambda s: (2 * s) // tiles_per_seq
    pair_of = lambda s: ((2 * s) % tiles_per_seq) // 2
    out_spec = lambda dil, width: pl.BlockSpec(
        (None, dil, 2 * tm // dil, width), lambda s: (seq_of(s), 0, pair_of(s), 0))
    kept = pl.BlockSpec(
        (1, KV_WIDTH, 2 * tm),
        lambda s: (seq_of(s), 0, jnp.maximum(pair_of(s) - first_kept_tile // 2, 0)))
    kept_shape = jax.ShapeDtypeStruct((b, KV_WIDTH, keep_rows), F32)
    tables = [spec for tile in (tile_0, tile_a, tile_b) for spec in (table_spec(tile),) * 2]
    return pl.pallas_call(
        functools.partial(_qkv_prompt_kernel, tuple(dilations), tiles_per_seq, first_kept_tile),
        grid=(n_tiles // 2,),
        in_specs=[x_spec(tile_0), x_spec(tile_a), x_spec(tile_b),
                  _const_spec(g_q.shape), _const_spec(g_kv.shape),
                  _const_spec(w_q.shape), _const_spec(w_kv.shape)] + tables,
        out_specs=[out_spec(dil, GROUP_Q_WIDTH) for dil in dilations]
        + [out_spec(dil, GROUP_KV_WIDTH) for dil in dilations] * 2 + [kept, kept],
        out_shape=_qkv_shapes(b, t, dilations) + [kept_shape, kept_shape],
        scratch_shapes=[pltpu.VMEM((2, N_PROJ_CHUNKS, tm, LANES), F32)],
        compiler_params=pltpu.CompilerParams(
            dimension_semantics=("arbitrary",), vmem_limit_bytes=VMEM_LIMIT),
        name="qkv_rope",
    )(x2, x2, x2, g_q, g_kv, w_q, w_kv, *([cos, sin_signed] * 3))


def _rope_tables(positions):
    half = HEAD_DIM // 2
    inv = ROPE_THETA ** (-jnp.arange(0, HEAD_DIM, 2, dtype=F32) / HEAD_DIM)
    ang = positions.astype(F32)[:, None] * jnp.tile(inv, LANES // half)[None, :]
    sign = jnp.where((jnp.arange(LANES) // half) % 2 == 0, -1.0, 1.0).astype(F32)
    return jnp.cos(ang), jnp.sin(ang) * sign[None, :]


def _stat_lane(c, k):
    return (1 - c) * HEAD_DIM + k


def _band_attention_kernel(seg_rows, tiles_per_subseq, q_ref, kp_ref, kc_ref, vp_ref, vc_ref,
                           o_ref, stat_ref, k_scr, v_scr, bias_scr, s_scr, p_scr):
    n_seg = q_ref.shape[0] // seg_rows
    bands_per_seg = seg_rows // BAND
    n_bands = n_seg * bands_per_seg
    step = pl.program_id(1)
    lane = lax.broadcasted_iota(jnp.int32, (1, LANES), 1)
    for c in range(KV_PER_GROUP):
        own = (lane // HEAD_DIM) == c
        for u in range(n_seg):
            base = u * (seg_rows + BAND)
            for dst, k_rows, v_rows in (
                    (slice(base, base + BAND), kp_ref[...], vp_ref[...]),
                    (slice(base + BAND, base + BAND + seg_rows),
                     kc_ref[u * seg_rows:(u + 1) * seg_rows, :],
                     vc_ref[u * seg_rows:(u + 1) * seg_rows, :])):
                k_scr[c, dst, :] = jnp.where(own, k_rows, jnp.zeros((), BF16))
                v_scr[c, dst, :] = jnp.where(own, v_rows, jnp.ones((), BF16))
    stat_ref[...] = jnp.ones(stat_ref.shape, F32)

    row = lax.broadcasted_iota(jnp.int32, (BAND, 2 * BAND), 0)
    col = lax.broadcasted_iota(jnp.int32, (BAND, 2 * BAND), 1)
    band_bias = jnp.where((col >= row) & (col <= row + BAND), 0.0, NEG_BIG)
    bias_scr[0] = band_bias
    bias_scr[1] = band_bias + jnp.where(col < BAND, NEG_BIG, 0.0)
    starts_subseq = True if n_seg > 1 else lax.rem(step, tiles_per_subseq) == 0

    def rows_of(g):
        return pl.ds(g * BAND if isinstance(g, int) else pl.multiple_of(g * BAND, BAND), BAND)

    def key_rows_of(g):
        staged = g + g // bands_per_seg
        start = staged * BAND if isinstance(g, int) else pl.multiple_of(staged * BAND, BAND)
        return pl.ds(start, 2 * BAND)

    def scores(g, slot):
        yield KV_PER_GROUP
        q = jnp.concatenate(
            [q_ref[rows_of(g), k * LANES:(k + 1) * LANES] for k in range(Q_PER_KV)], axis=0)
        for c in range(KV_PER_GROUP):
            s_scr[slot, c] = lax.dot_general(q, k_scr[c, key_rows_of(g), :],
                                             (((1,), (1,)), ((), ())),
                                             preferred_element_type=F32)
            yield

    def softmax(g, slot):
        yield KV_PER_GROUP * Q_PER_KV
        first = jnp.logical_and(starts_subseq, g % bands_per_seg == 0).astype(jnp.int32)
        for c in range(KV_PER_GROUP):
            for k in range(Q_PER_KV):
                part = pl.ds(k * BAND, BAND)
                s = s_scr[slot, c, part, :] + bias_scr[first]
                m = jnp.max(s, axis=1, keepdims=True)
                p_scr[slot, c, part, :] = jnp.exp(s - m).astype(BF16)
                stat_ref[rows_of(g), pl.ds(_stat_lane(c, k) + STAT_MAX_OFFSET, 1)] = m
                yield

    def weighted_values(g, slot):
        yield KV_PER_GROUP + Q_PER_KV + 1
        ov = []
        for c in range(KV_PER_GROUP):
            ov.append(jnp.dot(p_scr[slot, c], v_scr[c, key_rows_of(g), :],
                              preferred_element_type=F32))
            yield
        low = lane < HEAD_DIM
        sums = None
        for k in range(Q_PER_KV):
            part = slice(k * BAND, (k + 1) * BAND)
            o_ref[rows_of(g), k * LANES:(k + 1) * LANES] = (
                jnp.where(low, ov[0][part], ov[1][part]).astype(o_ref.dtype))
            sums_k = jnp.where(low, ov[1][part], ov[0][part])
            sums = sums_k if sums is None else jnp.where((lane % HEAD_DIM) == k, sums_k, sums)
            yield
        for c in range(KV_PER_GROUP):
            lanes = pl.ds(_stat_lane(c, 0), Q_PER_KV)
            stat_ref[rows_of(g), lanes] = sums[:, _stat_lane(c, 0):_stat_lane(c, 0) + Q_PER_KV]
        yield

    _interleave(scores(0, 0))
    _interleave(softmax(0, 0), scores(1, 1))

    def steady(t, carry):
        for g, slot in ((2 * t + 1, 1), (2 * t + 2, 0)):
            _interleave(weighted_values(g - 1, 1 - slot), softmax(g, slot),
                        scores(g + 1, 1 - slot))
        return carry

    assert n_bands % 2 == 0
    lax.fori_loop(0, (n_bands - 2) // 2, steady, 0)
    last = n_bands - 1
    _interleave(weighted_values(last - 1, (last - 1) % 2), softmax(last, last % 2))
    _interleave(weighted_values(last, last % 2))


def _band_attention(q, k, v, block_rows):
    b, dilation, sub, _ = q.shape
    seg_rows = min(sub, block_rows)
    total = dilation * sub
    assert total % block_rows == 0 and block_rows % seg_rows == 0 and sub % seg_rows == 0
    assert seg_rows % BAND == 0 and block_rows >= 2 * BAND
    n_seg = block_rows // seg_rows
    flat = lambda a: a.reshape(b, total, a.shape[-1])
    cur = lambda width: pl.BlockSpec((None, block_rows, width), lambda bi, i: (bi, i, 0))
    prev = pl.BlockSpec((None, BAND, GROUP_KV_WIDTH),
                        lambda bi, i: (bi, jnp.maximum(i * (block_rows // BAND) - 1, 0), 0))
    staged_rows = n_seg * (seg_rows + BAND)
    o, stat = pl.pallas_call(
        functools.partial(_band_attention_kernel, seg_rows, sub // seg_rows),
        grid=(b, total // block_rows),
        in_specs=[cur(GROUP_Q_WIDTH), prev, cur(GROUP_KV_WIDTH), prev, cur(GROUP_KV_WIDTH)],
        out_specs=[cur(GROUP_Q_WIDTH), cur(STAT_LANES)],
        out_shape=[jax.ShapeDtypeStruct((b, total, GROUP_Q_WIDTH), BF16),
                   jax.ShapeDtypeStruct((b, total, STAT_LANES), F32)],
        scratch_shapes=[pltpu.VMEM((KV_PER_GROUP, staged_rows, GROUP_KV_WIDTH), BF16),
                        pltpu.VMEM((KV_PER_GROUP, staged_rows, GROUP_KV_WIDTH), BF16),
                        pltpu.VMEM((2, BAND, 2 * BAND), F32),
                        pltpu.VMEM((2, KV_PER_GROUP, Q_PER_KV * BAND, 2 * BAND), F32),
                        pltpu.VMEM((2, KV_PER_GROUP, Q_PER_KV * BAND, 2 * BAND), BF16)],
        compiler_params=pltpu.CompilerParams(
            dimension_semantics=("arbitrary", "arbitrary"), vmem_limit_bytes=VMEM_LIMIT),
        name=f"band_attention_d{dilation}",
    )(flat(q), flat(k), flat(k), flat(v), flat(v))
    return (o.reshape(b, dilation, sub, GROUP_Q_WIDTH), stat.reshape(b, dilation, sub, STAT_LANES))


SAMPLE_Q_ROWS = 16
NEW_KEY_ROWS = 8
SAMPLE_SEQS_PER_STEP = 4


def _sample_attention_kernel(n_new, q_ref, kn_ref, vn_ref, kc0, kc1, kc2, vc0, vc1, vc2,
                             o_ref, lse_ref):
    n_seq = q_ref.shape[0]
    n_rows = q_ref.shape[2]
    contract_last = (((1,), (1,)), ((), ()))
    cache_refs = ((kc0, vc0), (kc1, vc1), (kc2, vc2))
    bias_c, bias_n = [], []
    for gi, (kc_ref, _) in enumerate(cache_refs):
        dmask = ATTN_DILATIONS[gi] - 1
        n_cache = kc_ref.shape[3]
        assert n_cache == ATTN_WINDOWS[gi]
        t_c = lax.broadcasted_iota(jnp.int32, (n_rows, n_cache), 0) & (n_new - 1)
        j_c = lax.broadcasted_iota(jnp.int32, (n_rows, n_cache), 1)
        valid_c = (j_c >= t_c) & (((j_c - t_c) & dmask) == 0)
        t_n = lax.broadcasted_iota(jnp.int32, (n_rows, NEW_KEY_ROWS), 0) & (n_new - 1)
        j_n = lax.broadcasted_iota(jnp.int32, (n_rows, NEW_KEY_ROWS), 1)
        valid_n = (j_n <= t_n) & (((t_n - j_n) & dmask) == 0)
        bias_c.append(jnp.where(valid_c, 0.0, NEG_BIG))
        bias_n.append(jnp.where(valid_n, 0.0, NEG_BIG))

    def one_sequence(b):
        yield len(cache_refs) * KV_PER_GROUP
        kn_all = kn_ref[b]
        vn_all = vn_ref[b]
        for gi, (kc_ref, vc_ref) in enumerate(cache_refs):
            for c in range(KV_PER_GROUP):
                kvh = gi * KV_PER_GROUP + c
                ns = slice(kvh * HEAD_DIM, (kvh + 1) * HEAD_DIM)
                q = q_ref[b, kvh]
                kt = kc_ref[b, c].astype(BF16)
                vt = vc_ref[b, c].astype(BF16)
                s_c = jnp.dot(q, kt, preferred_element_type=F32) + bias_c[gi]
                s_n = lax.dot_general(q, kn_all[:, ns].astype(BF16), contract_last,
                                      preferred_element_type=F32) + bias_n[gi]
                m = jnp.maximum(jnp.max(s_c, axis=1, keepdims=True),
                                jnp.max(s_n, axis=1, keepdims=True))
                p_c = jnp.exp(s_c - m)
                p_n = jnp.exp(s_n - m)
                l = jnp.sum(p_c, axis=1, keepdims=True) + jnp.sum(p_n, axis=1, keepdims=True)
                o = (lax.dot_general(p_c.astype(BF16), vt, contract_last,
                                     preferred_element_type=F32)
                     + jnp.dot(p_n.astype(BF16), vn_all[:, ns].astype(BF16),
                               preferred_element_type=F32))
                o_ref[b, kvh] = o / l
                lse_ref[b, kvh] = m + jnp.log(l)
                yield

    _interleave(*[one_sequence(b) for b in range(n_seq)])


def _sample_attention(q16, k_new, v_new, cache_kt, cache_vt, n_new):
    b = q16.shape[0]
    n_kv = q16.shape[1]
    assert cache_kt.shape[3] == KV_WINDOW and n_new & (n_new - 1) == 0 and n_new <= NEW_KEY_ROWS

    per_step = SAMPLE_SEQS_PER_STEP
    assert b % per_step == 0

    def cache_spec(gi):
        cols = ATTN_WINDOWS[gi]
        last = KV_WINDOW // cols - 1
        return pl.BlockSpec((per_step, KV_PER_GROUP, HEAD_DIM, cols),
                            lambda bi: (bi, gi, 0, last))

    whole = lambda a: pl.BlockSpec((per_step,) + a.shape[1:],
                                   lambda bi: (bi,) + (0,) * (a.ndim - 1))
    o_shape = jax.ShapeDtypeStruct((b, n_kv, SAMPLE_Q_ROWS, HEAD_DIM), F32)
    lse_shape = jax.ShapeDtypeStruct((b, n_kv, SAMPLE_Q_ROWS, 1), F32)
    return pl.pallas_call(
        functools.partial(_sample_attention_kernel, n_new),
        grid=(b // per_step,),
        in_specs=[whole(q16), whole(k_new), whole(v_new)]
        + [cache_spec(gi) for gi in range(N_GROUPS)] * 2,
        out_specs=[whole(o_shape), whole(lse_shape)],
        out_shape=[o_shape, lse_shape],
        compiler_params=pltpu.CompilerParams(
            dimension_semantics=("arbitrary",), vmem_limit_bytes=VMEM_LIMIT),
        name="sample_attention",
    )(q16, k_new, v_new, cache_kt, cache_kt, cache_kt, cache_vt, cache_vt, cache_vt)


def _attn_mix_steps(dilations, x_ref, o_refs, s_refs, g, wo_ref, o_scr, stat_scr, a_scr,
                    x1_ref, h_ref):
    tm = x_ref.shape[0]
    chunks_per_group = GROUP_Q_WIDTH // LANES
    yield 2 * N_GROUPS + 1 + 2 * len(_row_blocks(tm))
    for gi, (o_ref, s_ref) in enumerate(zip(o_refs, s_refs)):
        for r in range(dilations[gi]):
            rows = _subseq_rows(tm, r, dilations[gi])
            stat_scr[gi, rows, :] = s_ref[r]
            for c in range(chunks_per_group):
                o_scr[gi * chunks_per_group + c, rows, :] = (
                    o_ref[r, :, c * LANES:(c + 1) * LANES].astype(F32))
        yield
    sums = [stat_scr[gi] for gi in range(N_GROUPS)]
    maxes = [pltpu.roll(sm, STAT_LANES - STAT_MAX_OFFSET, 1) for sm in sums]
    top = jnp.maximum(jnp.maximum(maxes[0], maxes[1]), maxes[2])
    es = [jnp.exp(mx - top) for mx in maxes]
    den = sums[0] * es[0] + sums[1] * es[1] + sums[2] * es[2]
    slot_id = lax.broadcasted_iota(jnp.int32, (tm, GROUP_Q_WIDTH), 1) // HEAD_DIM
    for gi in range(N_GROUPS):
        scale = es[gi] / den
        wide = jnp.zeros((tm, GROUP_Q_WIDTH), F32)
        for slot in range(HEADS_PER_GROUP):
            stat_lane = _stat_lane(slot % KV_PER_GROUP, slot // KV_PER_GROUP)
            wide = jnp.where(slot_id == slot, scale[:, stat_lane:stat_lane + 1], wide)
        for c in range(chunks_per_group):
            ci = gi * chunks_per_group + c
            a_scr[:, ci * LANES:(ci + 1) * LANES] = (
                o_scr[ci] * wide[:, c * LANES:(c + 1) * LANES]).astype(BF16)
        yield
    mix = jnp.dot(a_scr[...], wo_ref[...], preferred_element_type=F32)
    yield
    tail = _chain(
        _residual_norm_steps(x1_ref, lambda rows: x_ref[rows, :], lambda rows: mix[rows], g[1:2]),
        _norm_cast_steps(h_ref, lambda rows: x1_ref[rows, :], g[2:3]))
    next(tail)
    yield from tail


def _layer1_scratch(tm, d, slots):
    lead = (slots,) if slots else ()
    return [pltpu.VMEM((Q_WIDTH // LANES, tm, LANES), F32),
            pltpu.VMEM((N_GROUPS, tm, STAT_LANES), F32),
            pltpu.VMEM((tm, Q_WIDTH), BF16),
            pltpu.VMEM(lead + (tm, d), F32),
            pltpu.VMEM(lead + (tm, d), BF16)]


def _layer1_single_kernel(dilations, x_ref, o0_ref, o1_ref, o2_ref, s0_ref, s1_ref, s2_ref, g_ref,
                          wo_ref, wup_ref, wdn_ref, y_ref, o_scr, stat_scr, a_scr, x1_scr, h_scr):
    g = g_ref[...]
    _interleave(_attn_mix_steps(dilations, x_ref, (o0_ref, o1_ref, o2_ref),
                                (s0_ref, s1_ref, s2_ref), g, wo_ref, o_scr, stat_scr, a_scr,
                                x1_scr, h_scr))
    y_ref[...] = _mlp_from_hidden(x1_scr[...], h_scr[...], g[3:4], wup_ref, wdn_ref)


def _layer1_first_kernel(dilations, x_ref, o0_ref, o1_ref, o2_ref, s0_ref, s1_ref, s2_ref, g_ref,
                         wo_ref, x1_ref, h_ref, o_scr, stat_scr, a_scr):
    _interleave(_attn_mix_steps(dilations, x_ref, (o0_ref, o1_ref, o2_ref),
                                (s0_ref, s1_ref, s2_ref), g_ref[...], wo_ref, o_scr, stat_scr,
                                a_scr, x1_ref, h_ref))


def _layer1_prompt_kernel(dilations, x_ref, o0_ref, o1_ref, o2_ref, s0_ref, s1_ref, s2_ref,
                          x1f_ref, hf_ref, g_ref, wo_ref, wup_ref, wdn_ref, y_ref,
                          o_scr, stat_scr, a_scr, x1_scr, h_scr, acc_scr):
    s = pl.program_id(0)
    n_tiles = pl.num_programs(0) - 1
    g = g_ref[...]

    @pl.when(s == 0)
    def _():
        x1_scr[0] = x1f_ref[...]
        h_scr[0] = hf_ref[...]
        x1_scr[2] = jnp.zeros(x1_scr.shape[1:], F32)
        acc_scr[1] = jnp.zeros(acc_scr.shape[1:], F32)

    def finish_previous():
        x1_slot, acc_slot = lax.rem(s + 2, 3), lax.rem(s + 1, 2)
        return _residual_norm_steps(y_ref, lambda rows: x1_scr[x1_slot, rows, :],
                                    lambda rows: acc_scr[acc_slot, rows, :], g[3:4])

    @pl.when(s < n_tiles)
    def _():
        cur = lax.rem(s, 2)

        def store_acc(acc):
            acc_scr[cur] = acc

        _interleave(
            finish_previous(),
            _mlp_matmul_steps(lambda: h_scr[cur], wup_ref, wdn_ref, store_acc),
            _attn_mix_steps(dilations, x_ref, (o0_ref, o1_ref, o2_ref), (s0_ref, s1_ref, s2_ref),
                            g, wo_ref, o_scr, stat_scr, a_scr,
                            x1_scr.at[lax.rem(s + 1, 3)], h_scr.at[1 - cur]))

    @pl.when(s == n_tiles)
    def _():
        _interleave(finish_previous())


def _layer1(x, os, stats, gains, w_o, w_up, w_down, layer, tm):
    b, t, d = x.shape
    dilations = tuple(o.shape[1] for o in os)
    tiles_per_seq = t // tm
    n_tiles = b * tiles_per_seq
    assert t % tm == 0 and all(tm % (16 * dil) == 0 for dil in dilations)
    x2 = x.reshape(b * t, d)
    weights = (gains, w_o, w_up, w_down)
    weight_specs = [_const_spec(gains.shape), _const_spec(w_o.shape),
                    _layer_spec(w_up, layer), _layer_spec(w_down, layer)]

    def tile_specs(tile):
        sub = lambda dil, width: pl.BlockSpec(
            (None, dil, tm // dil, width),
            lambda s: (tile(s) // tiles_per_seq, 0, tile(s) % tiles_per_seq, 0))
        return ([pl.BlockSpec((tm, d), lambda s: (tile(s), 0))]
                + [sub(dil, GROUP_Q_WIDTH) for dil in dilations]
                + [sub(dil, STAT_LANES) for dil in dilations])

    if n_tiles == 1:
        y = pl.pallas_call(
            functools.partial(_layer1_single_kernel, dilations),
            grid=(1,),
            in_specs=tile_specs(lambda s: 0) + weight_specs,
            out_specs=pl.BlockSpec((tm, d), lambda s: (0, 0)),
            out_shape=jax.ShapeDtypeStruct((b * t, d), F32),
            scratch_shapes=_layer1_scratch(tm, d, 0),
            compiler_params=pltpu.CompilerParams(vmem_limit_bytes=VMEM_LIMIT),
            name="layer1_single_tile",
        )(x2, *os, *stats, *weights)
        return y.reshape(b, t, d)

    x1_first, h_first = pl.pallas_call(
        functools.partial(_layer1_first_kernel, dilations),
        grid=(1,),
        in_specs=tile_specs(lambda s: 0) + [_const_spec(gains.shape), _const_spec(w_o.shape)],
        out_specs=[pl.BlockSpec((tm, d), lambda s: (0, 0))] * 2,
        out_shape=[jax.ShapeDtypeStruct((tm, d), F32), jax.ShapeDtypeStruct((tm, d), BF16)],
        scratch_shapes=_layer1_scratch(tm, d, 0)[:3],
        compiler_params=pltpu.CompilerParams(vmem_limit_bytes=VMEM_LIMIT),
        name="layer1_first_tile",
    )(x2, *os, *stats, gains, w_o)

    scratch = _layer1_scratch(tm, d, 0)[:3] + [
        pltpu.VMEM((3, tm, d), F32), pltpu.VMEM((2, tm, d), BF16), pltpu.VMEM((2, tm, d), F32)]
    y = pl.pallas_call(
        functools.partial(_layer1_prompt_kernel, dilations),
        grid=(n_tiles + 1,),
        in_specs=tile_specs(lambda s: jnp.minimum(s + 1, n_tiles - 1))
        + [_const_spec(x1_first.shape), _const_spec(h_first.shape)]
        + weight_specs,
        out_specs=pl.BlockSpec((tm, d), lambda s: (jnp.maximum(s - 1, 0), 0)),
        out_shape=jax.ShapeDtypeStruct((b * t, d), F32),
        scratch_shapes=scratch,
        compiler_params=pltpu.CompilerParams(
            dimension_semantics=("arbitrary",), vmem_limit_bytes=VMEM_LIMIT),
        name="layer1_prompt",
    )(x2, *os, *stats, x1_first, h_first, *weights)
    return y.reshape(b, t, d)


PROMPT_TILE = 512
LAYER1_TILE = 512
ATTN_BLOCK_ROWS = 2048


def kernel(x_prompt, x_sample, cache_pool, cache_k, cache_v, norm_gains, kv_norm_gain, w_pool,
           pool_scale, w_q, w_o, w_kv, w_up, w_down):
    depth = norm_gains.shape[0]
    assert depth == 2 and cache_pool.shape[0] == 1 and w_q.shape[0] == 1
    bp, tp, d = x_prompt.shape
    bs, ts, _ = x_sample.shape

    g0, g1 = norm_gains[0], norm_gains[1]
    g1_q = g1[0:1]
    g_kv = kv_norm_gain[None, :]
    wp = w_pool[0].astype(BF16)
    ps = pool_scale[0][None, :]
    wq = w_q[0].reshape(d, N_GROUPS, KV_PER_GROUP, Q_PER_KV, HEAD_DIM).transpose(0, 1, 3, 2, 4)
    wq = wq.reshape(d, Q_WIDTH).astype(BF16)
    wo = w_o[0].reshape(N_GROUPS, KV_PER_GROUP, Q_PER_KV, HEAD_DIM, d).transpose(0, 2, 1, 3, 4)
    wo = wo.reshape(Q_WIDTH, d).astype(BF16)
    wkv = w_kv.astype(BF16)
    wup, wdn = w_up.astype(BF16), w_down.astype(BF16)

    keep = min(KV_WINDOW, tp)
    xp1, utail = _layer0_prompt(x_prompt, g0, wp, ps, wup, wdn, 0, PROMPT_TILE)
    cos_p, sin_p = _rope_tables(jnp.arange(tp))
    qkv_p = _qkv_prompt(xp1, g1_q, g_kv, wq, wkv, cos_p, sin_p, PROMPT_TILE, keep, ATTN_DILATIONS)
    os_p, stats_p = [], []
    for gi, dil in enumerate(ATTN_DILATIONS):
        o, stat = _band_attention(qkv_p[gi], qkv_p[N_GROUPS + gi], qkv_p[2 * N_GROUPS + gi],
                                  ATTN_BLOCK_ROWS)
        os_p.append(o)
        stats_p.append(stat)
    y_prompt = _layer1(xp1, os_p, stats_p, g1, wo, wup, wdn, 1, LAYER1_TILE)
    pool_prompt = utail[:, POOL_HALO - POOL_BUF:][None]
    k_prompt = qkv_p[-2].reshape(bp, N_KV_HEADS, HEAD_DIM, keep).transpose(0, 3, 1, 2)
    v_prompt = qkv_p[-1].reshape(bp, N_KV_HEADS, HEAD_DIM, keep).transpose(0, 3, 1, 2)

    n_tok = ts * bs
    xs_tm = jnp.swapaxes(x_sample, 0, 1)
    buf_tm = jnp.swapaxes(cache_pool[0], 0, 1)
    xs1_tm, us_tm = _layer0_sample(xs_tm, buf_tm, PAST_LEN, g0, wp, ps, wup, wdn, 0)
    xs1 = xs1_tm.reshape(1, n_tok, d)
    cos_s, sin_s = _rope_tables(PAST_LEN + jnp.arange(n_tok) // bs)
    qkv_s = _qkv_single(xs1[0], g1_q, g_kv, wq, wkv, cos_s, sin_s)
    q_s = jnp.concatenate([q[0, 0] for q in qkv_s[:N_GROUPS]], axis=-1)
    q16 = q_s.reshape(ts, bs, N_GROUPS, Q_PER_KV, KV_PER_GROUP, HEAD_DIM)
    q16 = q16.transpose(1, 2, 4, 3, 0, 5).reshape(bs, N_KV_HEADS, Q_PER_KV, ts, HEAD_DIM)
    q16 = jnp.pad(q16, ((0, 0), (0, 0), (0, SAMPLE_Q_ROWS // ts - Q_PER_KV), (0, 0), (0, 0)))
    q16 = q16.reshape(bs, N_KV_HEADS, SAMPLE_Q_ROWS, HEAD_DIM)
    k_s = jnp.swapaxes(qkv_s[-2].reshape(ts, bs, KV_WIDTH), 0, 1)
    v_s = jnp.swapaxes(qkv_s[-1].reshape(ts, bs, KV_WIDTH), 0, 1)
    pad_new = ((0, 0), (0, NEW_KEY_ROWS - ts), (0, 0))
    o16, lse16 = _sample_attention(
        q16, jnp.pad(k_s, pad_new), jnp.pad(v_s, pad_new),
        cache_k.transpose(0, 2, 3, 1), cache_v.transpose(0, 2, 3, 1), ts)
    heads_padded = SAMPLE_Q_ROWS // ts
    o_s = o16.reshape(bs, N_GROUPS, KV_PER_GROUP, heads_padded, ts, HEAD_DIM)[:, :, :, :Q_PER_KV]
    o_s = o_s.transpose(4, 0, 1, 3, 2, 5).reshape(1, 1, n_tok, N_GROUPS, GROUP_Q_WIDTH).astype(BF16)
    lse_s = lse16.reshape(bs, N_GROUPS, KV_PER_GROUP, heads_padded, ts)[:, :, :, :Q_PER_KV]
    lse_s = lse_s.transpose(4, 0, 1, 2, 3).reshape(1, 1, n_tok, N_GROUPS, KV_PER_GROUP, Q_PER_KV)
    ones = lambda n: jnp.ones((1, 1, n_tok, N_GROUPS, n), F32)
    max_lane = [_stat_lane(c, 0) + STAT_MAX_OFFSET for c in range(KV_PER_GROUP)]
    assert max_lane[1] < max_lane[0]
    stat_s = jnp.concatenate(
        [ones(max_lane[1]), lse_s[..., 1, :],
         ones(max_lane[0] - max_lane[1] - Q_PER_KV), lse_s[..., 0, :],
         ones(STAT_LANES - max_lane[0] - Q_PER_KV)], axis=-1)
    ys_tm = _layer1(xs1, [o_s[:, :, :, gi] for gi in range(N_GROUPS)],
                    [stat_s[:, :, :, gi] for gi in range(N_GROUPS)], g1, wo, wup, wdn, 1, n_tok)
    y_sample = jnp.swapaxes(ys_tm.reshape(ts, bs, d), 0, 1)
    u_s = jnp.swapaxes(us_tm, 0, 1)
    pool_sample = jnp.concatenate([cache_pool[0], u_s], axis=1)[:, -POOL_BUF:][None]
    k_sample = k_s.reshape(bs, ts, N_KV_HEADS, HEAD_DIM)
    v_sample = v_s.reshape(bs, ts, N_KV_HEADS, HEAD_DIM)

    return (y_prompt, y_sample, pool_prompt, k_prompt, v_prompt, pool_sample, k_sample, v_sample)
```

```python
import functools

import jax
import jax.numpy as jnp
from jax import lax
from jax.experimental import pallas as pl
from jax.experimental.pallas import tpu as pltpu

F32 = jnp.float32
BF16 = jnp.bfloat16

EPS = 1e-6
ROPE_THETA = 10000.0
PAST_LEN = 16384
POOL_WINDOWS = (2, 4, 8, 16)
POOL_BUF = max(POOL_WINDOWS) - 1
POOL_HALO = 16
HEAD_DIM = 64
ATTN_WINDOWS = (128, 512, 2048)
ATTN_DILATIONS = (1, 4, 16)
N_GROUPS = len(ATTN_WINDOWS)
KV_PER_GROUP = 2
Q_PER_KV = 3
HEADS_PER_GROUP = KV_PER_GROUP * Q_PER_KV
N_KV_HEADS = N_GROUPS * KV_PER_GROUP
GROUP_Q_WIDTH = HEADS_PER_GROUP * HEAD_DIM
GROUP_KV_WIDTH = KV_PER_GROUP * HEAD_DIM
Q_WIDTH = N_GROUPS * GROUP_Q_WIDTH
KV_WIDTH = N_GROUPS * GROUP_KV_WIDTH
KV_WINDOW = max(ATTN_WINDOWS)
BAND = 128
NEG_BIG = -1e30

LANES = 128
STAT_LANES = LANES
STAT_MAX_OFFSET = 8
VMEM_LIMIT = 56 * 1024 * 1024
FF_CHUNK = 512
NORM_ROW_BLOCKS = 4
POOL_NORM_BLOCKS = 1

for _w, _d in zip(ATTN_WINDOWS, ATTN_DILATIONS):
    assert _w // _d == BAND and _w % _d == 0
assert GROUP_KV_WIDTH == LANES


def _rms(x, g):
    return x * lax.rsqrt(jnp.mean(x * x, axis=-1, keepdims=True) + EPS) * g


def _mlp_residual(x1, g_in, g_out, wup_ref, wdn_ref):
    return _mlp_from_hidden(x1, _rms(x1, g_in).astype(BF16), g_out, wup_ref, wdn_ref)


def _mlp_from_hidden(x1, h, g_out, wup_ref, wdn_ref):
    acc = []
    _interleave(_mlp_matmul_steps(lambda: h, wup_ref, wdn_ref, acc.append))
    return x1 + _rms(acc[0], g_out)


def _mlp_matmul_steps(load_h, wup_ref, wdn_ref, emit_acc):
    n_chunks = wup_ref.shape[1] // FF_CHUNK
    yield n_chunks
    h = load_h()
    acc = None
    for c in range(n_chunks):
        cs = slice(c * FF_CHUNK, (c + 1) * FF_CHUNK)
        a = jnp.dot(h, wup_ref[:, cs], preferred_element_type=F32)
        a = jnp.square(jnp.maximum(a, 0.0)).astype(BF16)
        part = jnp.dot(a, wdn_ref[cs, :], preferred_element_type=F32)
        acc = part if acc is None else acc + part
        if c + 1 == n_chunks:
            emit_acc(acc)
        yield


def _row_blocks(n_rows, n_blocks=NORM_ROW_BLOCKS):
    size = n_rows // n_blocks if n_rows % (8 * n_blocks) == 0 else n_rows
    return [slice(lo, lo + size) for lo in range(0, n_rows, size)]


def _residual_norm_steps(dst_ref, load_x, load_v, g, n_blocks=NORM_ROW_BLOCKS):
    blocks = _row_blocks(dst_ref.shape[0], n_blocks)
    yield len(blocks)
    for rows in blocks:
        dst_ref[rows, :] = load_x(rows) + _rms(load_v(rows), g)
        yield


def _norm_cast_steps(dst_ref, load_v, g, n_blocks=NORM_ROW_BLOCKS):
    blocks = _row_blocks(dst_ref.shape[0], n_blocks)
    yield len(blocks)
    for rows in blocks:
        dst_ref[rows, :] = _rms(load_v(rows), g).astype(dst_ref.dtype)
        yield


def _chain(*generators):
    counts = [next(gen) for gen in generators]
    yield sum(counts)
    for gen in generators:
        yield from gen


def _interleave(*generators):
    totals = [next(gen) for gen in generators]
    done = [0] * len(generators)
    while any(d < t for d, t in zip(done, totals)):
        i = min((i for i in range(len(generators)) if done[i] < totals[i]),
                key=lambda i: (done[i] + 1) / totals[i])
        next(generators[i])
        done[i] += 1


def _const_spec(shape):
    zeros = (0,) * len(shape)
    return pl.BlockSpec(shape, lambda *_: zeros, pipeline_mode=pl.Buffered(1))


def _layer_spec(w, layer):
    index = (layer,) + (0,) * (w.ndim - 1)
    return pl.BlockSpec((None,) + w.shape[1:], lambda *_: index, pipeline_mode=pl.Buffered(1))


def _subseq_spec(dilation, rows, width):
    return pl.BlockSpec((None, dilation, rows // dilation, width), lambda bi, i: (bi, 0, i, 0))


def _subseq_rows(n_rows, r, dilation):
    n = n_rows // dilation
    return pl.ds(r, n, stride=dilation) if dilation > 1 else pl.ds(0, n)


def _pool_mixer_steps(x_ref, xh_ref, tile_in_seq, g, wp_ref, ps_ref, ext_ref, x1_ref, h_ref,
                      utail_ref=None):
    tm = x_ref.shape[0]
    pool_ch = wp_ref.shape[1]
    blocks = _row_blocks(tm, POOL_NORM_BLOCKS)
    yield 3 * len(blocks) + len(POOL_WINDOWS)
    uh = _rms(xh_ref[0:POOL_HALO, :], g[0:1]) * jnp.where(tile_in_seq > 0, 1.0, 0.0)
    ext_ref[0:POOL_HALO, :] = uh
    for rows in blocks:
        u = _rms(x_ref[rows, :], g[0:1])
        ext_ref[POOL_HALO + rows.start:POOL_HALO + rows.stop, :] = u
        if utail_ref is not None and rows.stop == tm:
            utail_ref[0] = u[u.shape[0] - POOL_HALO:, :]
        yield
    pos = tile_in_seq * tm + lax.broadcasted_iota(jnp.int32, (tm, 1), 0)
    parts = []
    for gi, w in enumerate(POOL_WINDOWS):
        cs = slice(gi * pool_ch, (gi + 1) * pool_ch)
        s = ext_ref[POOL_HALO:, cs]
        for j in range(1, w):
            s = s + ext_ref[POOL_HALO - j:POOL_HALO - j + tm, cs]
        cnt = jnp.minimum(w, pos + 1).astype(F32)
        pooled = s / cnt - ext_ref[POOL_HALO:, cs]
        parts.append(jnp.dot(pooled.astype(BF16), wp_ref[gi], preferred_element_type=F32))
        yield
    mix = jnp.concatenate(parts, axis=1) * ps_ref[...]
    tail = _chain(
        _residual_norm_steps(x1_ref, lambda rows: x_ref[rows, :], lambda rows: mix[rows], g[1:2],
                             POOL_NORM_BLOCKS),
        _norm_cast_steps(h_ref, lambda rows: x1_ref[rows, :], g[2:3], POOL_NORM_BLOCKS))
    next(tail)
    yield from tail


def _layer0_first_kernel(x_ref, g_ref, wp_ref, ps_ref, x1_ref, h_ref, ext_scr):
    _interleave(_pool_mixer_steps(x_ref, x_ref, 0, g_ref[...], wp_ref, ps_ref, ext_scr,
                                  x1_ref, h_ref))


def _layer0_prompt_kernel(tiles_per_seq, xa_ref, xha_ref, xb_ref, xhb_ref, x1f_ref, hf_ref, g_ref,
                          wp_ref, ps_ref, wup_ref, wdn_ref, y_ref, utail_ref,
                          x1_scr, h_scr, ext_scr):
    s = pl.program_id(0)
    n_tiles = 2 * pl.num_programs(0)
    tm = xa_ref.shape[0]
    g = g_ref[...]

    @pl.when(s == 0)
    def _():
        x1_scr[0] = x1f_ref[...]
        h_scr[0] = hf_ref[...]

    def prepare(x_ref, xh_ref, tile, slot, tail_ref):
        return _pool_mixer_steps(x_ref, xh_ref, lax.rem(tile, tiles_per_seq), g, wp_ref, ps_ref,
                                 ext_scr.at[slot], x1_scr.at[slot], h_scr.at[slot], tail_ref)

    def mlp(slot):
        acc = []
        return _chain(
            _mlp_matmul_steps(lambda: h_scr[slot], wup_ref, wdn_ref, acc.append),
            _residual_norm_steps(y_ref.at[pl.ds(slot * tm, tm)],
                                 lambda rows: x1_scr[slot, rows, :],
                                 lambda rows: acc[0][rows], g[3:4], POOL_NORM_BLOCKS))

    _interleave(mlp(0), prepare(xa_ref, xha_ref, 2 * s + 1, 1, utail_ref))
    _interleave(mlp(1), prepare(xb_ref, xhb_ref, jnp.minimum(2 * s + 2, n_tiles - 1), 0, None))


def _layer0_prompt(x, gains, w_pool, pool_scale, w_up, w_down, layer, tm):
    b, t, d = x.shape
    tiles_per_seq = t // tm
    n_tiles = b * tiles_per_seq
    assert t % tm == 0 and tm % POOL_HALO == 0 and tiles_per_seq % 2 == 0
    halo_per_tile = tm // POOL_HALO
    x2 = x.reshape(b * t, d)

    x1_first, h_first = pl.pallas_call(
        _layer0_first_kernel,
        grid=(1,),
        in_specs=[pl.BlockSpec((tm, d), lambda i: (0, 0)), _const_spec(gains.shape),
                  _const_spec(w_pool.shape), _const_spec(pool_scale.shape)],
        out_specs=[pl.BlockSpec((tm, d), lambda i: (0, 0))] * 2,
        out_shape=[jax.ShapeDtypeStruct((tm, d), F32), jax.ShapeDtypeStruct((tm, d), BF16)],
        scratch_shapes=[pltpu.VMEM((tm + POOL_HALO, d), F32)],
        compiler_params=pltpu.CompilerParams(vmem_limit_bytes=VMEM_LIMIT),
        name="layer0_first_tile",
    )(x2, gains, w_pool, pool_scale)

    tile_a = lambda s: 2 * s + 1
    tile_b = lambda s: jnp.minimum(2 * s + 2, n_tiles - 1)
    tile_spec = lambda tile: pl.BlockSpec((tm, d), lambda s: (tile(s), 0))
    halo_spec = lambda tile: pl.BlockSpec((POOL_HALO, d),
                                          lambda s: (tile(s) * halo_per_tile - 1, 0))
    y, utail = pl.pallas_call(
        functools.partial(_layer0_prompt_kernel, tiles_per_seq),
        grid=(n_tiles // 2,),
        in_specs=[
            tile_spec(tile_a), halo_spec(tile_a), tile_spec(tile_b), halo_spec(tile_b),
            _const_spec(x1_first.shape), _const_spec(h_first.shape),
            _const_spec(gains.shape), _const_spec(w_pool.shape), _const_spec(pool_scale.shape),
            _layer_spec(w_up, layer), _layer_spec(w_down, layer),
        ],
        out_specs=[
            pl.BlockSpec((2 * tm, d), lambda s: (s, 0)),
            pl.BlockSpec((1, POOL_HALO, d), lambda s: (tile_a(s) // tiles_per_seq, 0, 0)),
        ],
        out_shape=[
            jax.ShapeDtypeStruct((b * t, d), F32),
            jax.ShapeDtypeStruct((b, POOL_HALO, d), F32),
        ],
        scratch_shapes=[pltpu.VMEM((2, tm, d), F32), pltpu.VMEM((2, tm, d), BF16),
                        pltpu.VMEM((2, tm + POOL_HALO, d), F32)],
        compiler_params=pltpu.CompilerParams(
            dimension_semantics=("arbitrary",), vmem_limit_bytes=VMEM_LIMIT),
        name="layer0_prompt",
    )(x2, x2, x2, x2, x1_first, h_first, gains, w_pool, pool_scale, w_up, w_down)
    return y.reshape(b, t, d), utail


def _layer0_sample_kernel(start_pos, x_ref, buf_ref, g_ref, wp_ref, ps_ref, wup_ref, wdn_ref,
                          y_ref, u_ref):
    n_t, n_b, _ = x_ref.shape
    pool_ch = wp_ref.shape[1]
    g = g_ref[...]
    xs = [x_ref[t] for t in range(n_t)]
    us = [_rms(xt, g[0:1]) for xt in xs]
    ext = [buf_ref[j] for j in range(POOL_BUF)] + us
    parts = []
    for gi, w in enumerate(POOL_WINDOWS):
        cs = slice(gi * pool_ch, (gi + 1) * pool_ch)
        rows = []
        for t in range(n_t):
            s = us[t][:, cs]
            for j in range(1, w):
                s = s + ext[POOL_BUF + t - j][:, cs]
            cnt = float(min(w, start_pos + t + 1))
            rows.append(s / cnt - us[t][:, cs])
        pooled = jnp.concatenate(rows, axis=0)
        parts.append(jnp.dot(pooled.astype(BF16), wp_ref[gi], preferred_element_type=F32))
    mix = jnp.concatenate(parts, axis=1) * ps_ref[...]
    x = jnp.concatenate(xs, axis=0)
    x1 = x + _rms(mix, g[1:2])
    y = _mlp_residual(x1, g[2:3], g[3:4], wup_ref, wdn_ref)
    for t in range(n_t):
        y_ref[t] = y[t * n_b:(t + 1) * n_b]
        u_ref[t] = us[t]


def _layer0_sample(x_tm, buf_tm, start_pos, gains, w_pool, pool_scale, w_up, w_down, layer):
    assert start_pos + 1 >= max(POOL_WINDOWS) and buf_tm.shape[0] == POOL_BUF
    return pl.pallas_call(
        functools.partial(_layer0_sample_kernel, start_pos),
        grid=(1,),
        in_specs=[_const_spec(a.shape) for a in (x_tm, buf_tm, gains, w_pool, pool_scale)]
        + [_layer_spec(w_up, layer), _layer_spec(w_down, layer)],
        out_specs=[pl.BlockSpec(x_tm.shape, lambda i: (0, 0, 0))] * 2,
        out_shape=[jax.ShapeDtypeStruct(x_tm.shape, F32), jax.ShapeDtypeStruct(x_tm.shape, F32)],
        compiler_params=pltpu.CompilerParams(vmem_limit_bytes=VMEM_LIMIT),
        name="layer0_sample",
    )(x_tm, buf_tm, gains, w_pool, pool_scale, w_up, w_down)


def _rope(x, cos, sin_signed):
    lane = lax.broadcasted_iota(jnp.int32, (x.shape[0], LANES), 1)
    first_half = (lane & (HEAD_DIM // 2)) == 0
    out = []
    for c in range(x.shape[1] // LANES):
        xc = x[:, c * LANES:(c + 1) * LANES]
        partner = jnp.where(first_half,
                            pltpu.roll(xc, LANES - HEAD_DIM // 2, 1),
                            pltpu.roll(xc, HEAD_DIM // 2, 1))
        out.append(xc * cos + partner * sin_signed)
    return out


def _qkv_kernel(dilations, first_kept_tile, transpose_kept,
                x_ref, gq_ref, gkv_ref, wq_ref, wkv_ref, cos_ref, sin_ref, cos_step_ref, sin_step_ref,
                q0_ref, q1_ref, q2_ref, k0_ref, k1_ref, k2_ref, v0_ref, v1_ref, v2_ref,
                kf_ref, vf_ref, stage_scr):
    i = pl.program_id(1)
    x = x_ref[0]
    cos_in, sin_in = cos_ref[...], sin_ref[...]
    cos_at, sin_at = cos_step_ref[0:1, :], sin_step_ref[0:1, :]
    cos = cos_in * cos_at - sin_in * sin_at
    sin_signed = sin_in * cos_at + cos_in * sin_at
    chunks_per_group = GROUP_Q_WIDTH // LANES

    def emit(out_ref, dilation, chunks):
        if dilation == 1:
            for c, chunk in enumerate(chunks):
                out_ref[0, :, c * LANES:(c + 1) * LANES] = chunk.astype(BF16)
            return
        for c, chunk in enumerate(chunks):
            stage_scr[c] = chunk
        for r in range(dilation):
            rows = _subseq_rows(stage_scr.shape[1], r, dilation)
            for c in range(len(chunks)):
                out_ref[r, :, c * LANES:(c + 1) * LANES] = stage_scr[c, rows, :].astype(BF16)

    u = _rms(x, gq_ref[...]).astype(BF16)
    q = jnp.dot(u, wq_ref[...], preferred_element_type=F32)
    scale = HEAD_DIM ** -0.5
    q_chunks = _rope(q, cos * scale, sin_signed * scale)
    for gi, q_ref in enumerate((q0_ref, q1_ref, q2_ref)):
        emit(q_ref, dilations[gi], q_chunks[gi * chunks_per_group:(gi + 1) * chunks_per_group])

    un = _rms(x, gkv_ref[...]).astype(BF16)
    kv = jnp.dot(un, wkv_ref[...], preferred_element_type=F32)
    k_chunks = _rope(kv[:, :KV_WIDTH], cos, sin_signed)
    v_chunks = [kv[:, KV_WIDTH + gi * LANES:KV_WIDTH + (gi + 1) * LANES] for gi in range(N_GROUPS)]
    for gi, (k_ref, v_ref) in enumerate(((k0_ref, v0_ref), (k1_ref, v1_ref), (k2_ref, v2_ref))):
        emit(k_ref, dilations[gi], [k_chunks[gi]])
        emit(v_ref, dilations[gi], [v_chunks[gi]])

    @pl.when(i >= first_kept_tile)
    def _():
        for gi in range(N_GROUPS):
            cs = slice(gi * LANES, (gi + 1) * LANES)
            if transpose_kept:
                kf_ref[0, cs, :] = k_chunks[gi].T
                vf_ref[0, cs, :] = v_chunks[gi].T
            else:
                kf_ref[0, :, cs] = k_chunks[gi]
                vf_ref[0, :, cs] = v_chunks[gi]


def _qkv(x, g_q, g_kv, w_q, w_kv, rope, tm, keep_rows, dilations, transpose_kept):
    b, t, d = x.shape
    cos, sin_signed, cos_step, sin_step = rope
    assert t % tm == 0 and keep_rows % tm == 0 and all(tm % (16 * dil) == 0 for dil in dilations)
    first_kept_tile = (t - keep_rows) // tm
    kept_block = lambda bi, i: jnp.maximum(i - first_kept_tile, 0)
    if transpose_kept:
        kept = pl.BlockSpec((1, KV_WIDTH, tm), lambda bi, i: (bi, 0, kept_block(bi, i)))
        kept_shape = jax.ShapeDtypeStruct((b, KV_WIDTH, keep_rows), F32)
    else:
        kept = pl.BlockSpec((1, tm, KV_WIDTH), lambda bi, i: (bi, kept_block(bi, i), 0))
        kept_shape = jax.ShapeDtypeStruct((b, keep_rows, KV_WIDTH), F32)
    sub_shape = lambda dil, width: jax.ShapeDtypeStruct((b, dil, t // dil, width), BF16)
    q_specs = [_subseq_spec(dil, tm, GROUP_Q_WIDTH) for dil in dilations]
    kv_specs = [_subseq_spec(dil, tm, GROUP_KV_WIDTH) for dil in dilations]
    q_shapes = [sub_shape(dil, GROUP_Q_WIDTH) for dil in dilations]
    kv_shapes = [sub_shape(dil, GROUP_KV_WIDTH) for dil in dilations]
    return pl.pallas_call(
        functools.partial(_qkv_kernel, tuple(dilations), first_kept_tile, transpose_kept),
        grid=(b, t // tm),
        in_specs=[
            pl.BlockSpec((1, tm, d), lambda bi, i: (bi, i, 0)),
            _const_spec(g_q.shape), _const_spec(g_kv.shape),
            _const_spec(w_q.shape), _const_spec(w_kv.shape),
            _const_spec(cos.shape), _const_spec(sin_signed.shape),
            pl.BlockSpec((None,) + cos_step.shape[1:], lambda bi, i: (i, 0, 0)),
            pl.BlockSpec((None,) + sin_step.shape[1:], lambda bi, i: (i, 0, 0)),
        ],
        out_specs=q_specs + kv_specs + kv_specs + [kept, kept],
        out_shape=q_shapes + kv_shapes + kv_shapes + [kept_shape, kept_shape],
        scratch_shapes=[pltpu.VMEM((GROUP_Q_WIDTH // LANES, tm, LANES), F32)],
        compiler_params=pltpu.CompilerParams(
            dimension_semantics=("arbitrary", "arbitrary"), vmem_limit_bytes=VMEM_LIMIT),
        name="qkv_rope",
    )(x, g_q, g_kv, w_q, w_kv, cos, sin_signed, cos_step, sin_step)


def _rope_tables(offsets, tile_starts):
    half = HEAD_DIM // 2
    inv = jnp.tile(ROPE_THETA ** (-jnp.arange(0, HEAD_DIM, 2, dtype=F32) / HEAD_DIM), LANES // half)
    sign = jnp.where((jnp.arange(LANES) // half) % 2 == 0, -1.0, 1.0).astype(F32)
    tables = []
    for positions in (offsets, tile_starts):
        ang = positions.astype(F32)[:, None] * inv[None, :]
        tables += [jnp.cos(ang), jnp.sin(ang) * sign[None, :]]
    rows8 = lambda a: jnp.broadcast_to(a[:, None, :], (a.shape[0], 8, LANES))
    return tables[0], tables[1], rows8(tables[2]), rows8(tables[3])


def _stat_lane(c, k):
    return (1 - c) * HEAD_DIM + k


def _band_attention_kernel(seg_rows, tiles_per_subseq, q_ref, kp_ref, kc_ref, vp_ref, vc_ref,
                           o_ref, stat_ref, k_scr, v_scr, bias_scr, s_scr, p_scr):
    n_seg = q_ref.shape[0] // seg_rows
    bands_per_seg = seg_rows // BAND
    n_bands = n_seg * bands_per_seg
    step = pl.program_id(1)
    lane = lax.broadcasted_iota(jnp.int32, (1, LANES), 1)
    for c in range(KV_PER_GROUP):
        own = (lane // HEAD_DIM) == c
        for u in range(n_seg):
            base = u * (seg_rows + BAND)
            for dst, k_rows, v_rows in (
                    (slice(base, base + BAND), kp_ref[...], vp_ref[...]),
                    (slice(base + BAND, base + BAND + seg_rows),
                     kc_ref[u * seg_rows:(u + 1) * seg_rows, :],
                     vc_ref[u * seg_rows:(u + 1) * seg_rows, :])):
                k_scr[c, dst, :] = jnp.where(own, k_rows, jnp.zeros((), BF16))
                v_scr[c, dst, :] = jnp.where(own, v_rows, jnp.ones((), BF16))
    stat_ref[...] = jnp.ones(stat_ref.shape, F32)

    row = lax.broadcasted_iota(jnp.int32, (BAND, 2 * BAND), 0)
    col = lax.broadcasted_iota(jnp.int32, (BAND, 2 * BAND), 1)
    band_bias = jnp.where((col >= row) & (col <= row + BAND), 0.0, NEG_BIG)
    bias_scr[0] = band_bias
    bias_scr[1] = band_bias + jnp.where(col < BAND, NEG_BIG, 0.0)
    starts_subseq = True if n_seg > 1 else lax.rem(step, tiles_per_subseq) == 0

    def rows_of(g):
        return pl.ds(g * BAND if isinstance(g, int) else pl.multiple_of(g * BAND, BAND), BAND)

    def key_rows_of(g):
        staged = g + g // bands_per_seg
        start = staged * BAND if isinstance(g, int) else pl.multiple_of(staged * BAND, BAND)
        return pl.ds(start, 2 * BAND)

    def scores(g, slot):
        yield KV_PER_GROUP
        q = jnp.concatenate(
            [q_ref[rows_of(g), k * LANES:(k + 1) * LANES] for k in range(Q_PER_KV)], axis=0)
        for c in range(KV_PER_GROUP):
            s_scr[slot, c] = lax.dot_general(q, k_scr[c, key_rows_of(g), :],
                                             (((1,), (1,)), ((), ())),
                                             preferred_element_type=F32)
            yield

    def softmax(g, slot):
        yield KV_PER_GROUP * Q_PER_KV
        first = jnp.logical_and(starts_subseq, g % bands_per_seg == 0).astype(jnp.int32)
        for c in range(KV_PER_GROUP):
            for k in range(Q_PER_KV):
                part = pl.ds(k * BAND, BAND)
                s = s_scr[slot, c, part, :] + bias_scr[first]
                m = jnp.max(s, axis=1, keepdims=True)
                p_scr[slot, c, part, :] = jnp.exp(s - m).astype(BF16)
                stat_ref[rows_of(g), pl.ds(_stat_lane(c, k) + STAT_MAX_OFFSET, 1)] = m
                yield

    def weighted_values(g, slot):
        yield KV_PER_GROUP + Q_PER_KV + 1
        ov = []
        for c in range(KV_PER_GROUP):
            ov.append(jnp.dot(p_scr[slot, c], v_scr[c, key_rows_of(g), :],
                              preferred_element_type=F32))
            yield
        low = lane < HEAD_DIM
        sums = None
        for k in range(Q_PER_KV):
            part = slice(k * BAND, (k + 1) * BAND)
            o_ref[rows_of(g), k * LANES:(k + 1) * LANES] = (
                jnp.where(low, ov[0][part], ov[1][part]).astype(o_ref.dtype))
            sums_k = jnp.where(low, ov[1][part], ov[0][part])
            sums = sums_k if sums is None else jnp.where((lane % HEAD_DIM) == k, sums_k, sums)
            yield
        for c in range(KV_PER_GROUP):
            lanes = pl.ds(_stat_lane(c, 0), Q_PER_KV)
            stat_ref[rows_of(g), lanes] = sums[:, _stat_lane(c, 0):_stat_lane(c, 0) + Q_PER_KV]
        yield

    _interleave(scores(0, 0))
    _interleave(softmax(0, 0), scores(1, 1))

    def steady(t, carry):
        for g, slot in ((2 * t + 1, 1), (2 * t + 2, 0)):
            _interleave(weighted_values(g - 1, 1 - slot), softmax(g, slot),
                        scores(g + 1, 1 - slot))
        return carry

    assert n_bands % 2 == 0
    lax.fori_loop(0, (n_bands - 2) // 2, steady, 0)
    last = n_bands - 1
    _interleave(weighted_values(last - 1, (last - 1) % 2), softmax(last, last % 2))
    _interleave(weighted_values(last, last % 2))


def _band_attention(q, k, v, block_rows):
    b, dilation, sub, _ = q.shape
    seg_rows = min(sub, block_rows)
    total = dilation * sub
    assert total % block_rows == 0 and block_rows % seg_rows == 0 and sub % seg_rows == 0
    assert seg_rows % BAND == 0 and block_rows >= 2 * BAND
    n_seg = block_rows // seg_rows
    flat = lambda a: a.reshape(b, total, a.shape[-1])
    cur = lambda width: pl.BlockSpec((None, block_rows, width), lambda bi, i: (bi, i, 0))
    prev = pl.BlockSpec((None, BAND, GROUP_KV_WIDTH),
                        lambda bi, i: (bi, jnp.maximum(i * (block_rows // BAND) - 1, 0), 0))
    staged_rows = n_seg * (seg_rows + BAND)
    o, stat = pl.pallas_call(
        functools.partial(_band_attention_kernel, seg_rows, sub // seg_rows),
        grid=(b, total // block_rows),
        in_specs=[cur(GROUP_Q_WIDTH), prev, cur(GROUP_KV_WIDTH), prev, cur(GROUP_KV_WIDTH)],
        out_specs=[cur(GROUP_Q_WIDTH), cur(STAT_LANES)],
        out_shape=[jax.ShapeDtypeStruct((b, total, GROUP_Q_WIDTH), BF16),
                   jax.ShapeDtypeStruct((b, total, STAT_LANES), F32)],
        scratch_shapes=[pltpu.VMEM((KV_PER_GROUP, staged_rows, GROUP_KV_WIDTH), BF16),
                        pltpu.VMEM((KV_PER_GROUP, staged_rows, GROUP_KV_WIDTH), BF16),
                        pltpu.VMEM((2, BAND, 2 * BAND), F32),
                        pltpu.VMEM((2, KV_PER_GROUP, Q_PER_KV * BAND, 2 * BAND), F32),
                        pltpu.VMEM((2, KV_PER_GROUP, Q_PER_KV * BAND, 2 * BAND), BF16)],
        compiler_params=pltpu.CompilerParams(
            dimension_semantics=("arbitrary", "arbitrary"), vmem_limit_bytes=VMEM_LIMIT),
        name=f"band_attention_d{dilation}",
    )(flat(q), flat(k), flat(k), flat(v), flat(v))
    return (o.reshape(b, dilation, sub, GROUP_Q_WIDTH), stat.reshape(b, dilation, sub, STAT_LANES))


SAMPLE_Q_ROWS = 16
NEW_KEY_ROWS = 8
SAMPLE_SEQS_PER_STEP = 4


def _sample_attention_kernel(n_new, q_ref, kn_ref, vn_ref, kc0, kc1, kc2, vc0, vc1, vc2,
                             o_ref, lse_ref):
    n_seq = q_ref.shape[0]
    n_rows = q_ref.shape[2]
    contract_last = (((1,), (1,)), ((), ()))
    cache_refs = ((kc0, vc0), (kc1, vc1), (kc2, vc2))
    bias_c, bias_n = [], []
    for gi, (kc_ref, _) in enumerate(cache_refs):
        dmask = ATTN_DILATIONS[gi] - 1
        n_cache = kc_ref.shape[3]
        assert n_cache == ATTN_WINDOWS[gi]
        t_c = lax.broadcasted_iota(jnp.int32, (n_rows, n_cache), 0) & (n_new - 1)
        j_c = lax.broadcasted_iota(jnp.int32, (n_rows, n_cache), 1)
        valid_c = (j_c >= t_c) & (((j_c - t_c) & dmask) == 0)
        t_n = lax.broadcasted_iota(jnp.int32, (n_rows, NEW_KEY_ROWS), 0) & (n_new - 1)
        j_n = lax.broadcasted_iota(jnp.int32, (n_rows, NEW_KEY_ROWS), 1)
        valid_n = (j_n <= t_n) & (((t_n - j_n) & dmask) == 0)
        bias_c.append(jnp.where(valid_c, 0.0, NEG_BIG))
        bias_n.append(jnp.where(valid_n, 0.0, NEG_BIG))

    def one_sequence(b):
        yield len(cache_refs) * KV_PER_GROUP
        kn_all = kn_ref[b]
        vn_all = vn_ref[b]
        for gi, (kc_ref, vc_ref) in enumerate(cache_refs):
            for c in range(KV_PER_GROUP):
                kvh = gi * KV_PER_GROUP + c
                ns = slice(kvh * HEAD_DIM, (kvh + 1) * HEAD_DIM)
                q = q_ref[b, kvh]
                kt = kc_ref[b, c].astype(BF16)
                vt = vc_ref[b, c].astype(BF16)
                s_c = jnp.dot(q, kt, preferred_element_type=F32) + bias_c[gi]
                s_n = lax.dot_general(q, kn_all[:, ns].astype(BF16), contract_last,
                                      preferred_element_type=F32) + bias_n[gi]
                m = jnp.maximum(jnp.max(s_c, axis=1, keepdims=True),
                                jnp.max(s_n, axis=1, keepdims=True))
                p_c = jnp.exp(s_c - m)
                p_n = jnp.exp(s_n - m)
                l = jnp.sum(p_c, axis=1, keepdims=True) + jnp.sum(p_n, axis=1, keepdims=True)
                o = (lax.dot_general(p_c.astype(BF16), vt, contract_last,
                                     preferred_element_type=F32)
                     + jnp.dot(p_n.astype(BF16), vn_all[:, ns].astype(BF16),
                               preferred_element_type=F32))
                o_ref[b, kvh] = o / l
                lse_ref[b, kvh] = m + jnp.log(l)
                yield

    _interleave(*[one_sequence(b) for b in range(n_seq)])


def _sample_attention(q16, k_new, v_new, cache_kt, cache_vt, n_new):
    b = q16.shape[0]
    n_kv = q16.shape[1]
    assert cache_kt.shape[3] == KV_WINDOW and n_new & (n_new - 1) == 0 and n_new <= NEW_KEY_ROWS

    per_step = SAMPLE_SEQS_PER_STEP
    assert b % per_step == 0

    def cache_spec(gi):
        cols = ATTN_WINDOWS[gi]
        last = KV_WINDOW // cols - 1
        return pl.BlockSpec((per_step, KV_PER_GROUP, HEAD_DIM, cols),
                            lambda bi: (bi, gi, 0, last))

    whole = lambda a: pl.BlockSpec((per_step,) + a.shape[1:],
                                   lambda bi: (bi,) + (0,) * (a.ndim - 1))
    o_shape = jax.ShapeDtypeStruct((b, n_kv, SAMPLE_Q_ROWS, HEAD_DIM), F32)
    lse_shape = jax.ShapeDtypeStruct((b, n_kv, SAMPLE_Q_ROWS, 1), F32)
    return pl.pallas_call(
        functools.partial(_sample_attention_kernel, n_new),
        grid=(b // per_step,),
        in_specs=[whole(q16), whole(k_new), whole(v_new)]
        + [cache_spec(gi) for gi in range(N_GROUPS)] * 2,
        out_specs=[whole(o_shape), whole(lse_shape)],
        out_shape=[o_shape, lse_shape],
        compiler_params=pltpu.CompilerParams(
            dimension_semantics=("arbitrary",), vmem_limit_bytes=VMEM_LIMIT),
        name="sample_attention",
    )(q16, k_new, v_new, cache_kt, cache_kt, cache_kt, cache_vt, cache_vt, cache_vt)


def _attn_mix_steps(dilations, x_ref, o_refs, s_refs, g, wo_ref, o_scr, stat_scr, a_scr,
                    x1_ref, h_ref):
    tm = x_ref.shape[0]
    chunks_per_group = GROUP_Q_WIDTH // LANES
    yield 2 * N_GROUPS + 1 + 2 * len(_row_blocks(tm))
    for gi, (o_ref, s_ref) in enumerate(zip(o_refs, s_refs)):
        for r in range(dilations[gi]):
            rows = _subseq_rows(tm, r, dilations[gi])
            stat_scr[gi, rows, :] = s_ref[r]
            for c in range(chunks_per_group):
                o_scr[gi * chunks_per_group + c, rows, :] = (
                    o_ref[r, :, c * LANES:(c + 1) * LANES].astype(F32))
        yield
    sums = [stat_scr[gi] for gi in range(N_GROUPS)]
    maxes = [pltpu.roll(sm, STAT_LANES - STAT_MAX_OFFSET, 1) for sm in sums]
    top = jnp.maximum(jnp.maximum(maxes[0], maxes[1]), maxes[2])
    es = [jnp.exp(mx - top) for mx in maxes]
    den = sums[0] * es[0] + sums[1] * es[1] + sums[2] * es[2]
    slot_id = lax.broadcasted_iota(jnp.int32, (tm, GROUP_Q_WIDTH), 1) // HEAD_DIM
    for gi in range(N_GROUPS):
        scale = es[gi] / den
        wide = jnp.zeros((tm, GROUP_Q_WIDTH), F32)
        for slot in range(HEADS_PER_GROUP):
            stat_lane = _stat_lane(slot % KV_PER_GROUP, slot // KV_PER_GROUP)
            wide = jnp.where(slot_id == slot, scale[:, stat_lane:stat_lane + 1], wide)
        for c in range(chunks_per_group):
            ci = gi * chunks_per_group + c
            a_scr[:, ci * LANES:(ci + 1) * LANES] = (
                o_scr[ci] * wide[:, c * LANES:(c + 1) * LANES]).astype(BF16)
        yield
    mix = jnp.dot(a_scr[...], wo_ref[...], preferred_element_type=F32)
    yield
    tail = _chain(
        _residual_norm_steps(x1_ref, lambda rows: x_ref[rows, :], lambda rows: mix[rows], g[1:2]),
        _norm_cast_steps(h_ref, lambda rows: x1_ref[rows, :], g[2:3]))
    next(tail)
    yield from tail


def _layer1_scratch(tm, d, slots):
    lead = (slots,) if slots else ()
    return [pltpu.VMEM((Q_WIDTH // LANES, tm, LANES), F32),
            pltpu.VMEM((N_GROUPS, tm, STAT_LANES), F32),
            pltpu.VMEM((tm, Q_WIDTH), BF16),
            pltpu.VMEM(lead + (tm, d), F32),
            pltpu.VMEM(lead + (tm, d), BF16)]


def _layer1_single_kernel(dilations, x_ref, o0_ref, o1_ref, o2_ref, s0_ref, s1_ref, s2_ref, g_ref,
                          wo_ref, wup_ref, wdn_ref, y_ref, o_scr, stat_scr, a_scr, x1_scr, h_scr):
    g = g_ref[...]
    _interleave(_attn_mix_steps(dilations, x_ref, (o0_ref, o1_ref, o2_ref),
                                (s0_ref, s1_ref, s2_ref), g, wo_ref, o_scr, stat_scr, a_scr,
                                x1_scr, h_scr))
    y_ref[...] = _mlp_from_hidden(x1_scr[...], h_scr[...], g[3:4], wup_ref, wdn_ref)


def _layer1_first_kernel(dilations, x_ref, o0_ref, o1_ref, o2_ref, s0_ref, s1_ref, s2_ref, g_ref,
                         wo_ref, x1_ref, h_ref, o_scr, stat_scr, a_scr):
    _interleave(_attn_mix_steps(dilations, x_ref, (o0_ref, o1_ref, o2_ref),
                                (s0_ref, s1_ref, s2_ref), g_ref[...], wo_ref, o_scr, stat_scr,
                                a_scr, x1_ref, h_ref))


def _layer1_prompt_kernel(dilations, x_ref, o0_ref, o1_ref, o2_ref, s0_ref, s1_ref, s2_ref,
                          x1f_ref, hf_ref, g_ref, wo_ref, wup_ref, wdn_ref, y_ref,
                          o_scr, stat_scr, a_scr, x1_scr, h_scr, acc_scr):
    s = pl.program_id(0)
    n_tiles = pl.num_programs(0) - 1
    g = g_ref[...]

    @pl.when(s == 0)
    def _():
        x1_scr[0] = x1f_ref[...]
        h_scr[0] = hf_ref[...]
        x1_scr[2] = jnp.zeros(x1_scr.shape[1:], F32)
        acc_scr[1] = jnp.zeros(acc_scr.shape[1:], F32)

    def finish_previous():
        x1_slot, acc_slot = lax.rem(s + 2, 3), lax.rem(s + 1, 2)
        return _residual_norm_steps(y_ref, lambda rows: x1_scr[x1_slot, rows, :],
                                    lambda rows: acc_scr[acc_slot, rows, :], g[3:4])

    @pl.when(s < n_tiles)
    def _():
        cur = lax.rem(s, 2)

        def store_acc(acc):
            acc_scr[cur] = acc

        _interleave(
            finish_previous(),
            _mlp_matmul_steps(lambda: h_scr[cur], wup_ref, wdn_ref, store_acc),
            _attn_mix_steps(dilations, x_ref, (o0_ref, o1_ref, o2_ref), (s0_ref, s1_ref, s2_ref),
                            g, wo_ref, o_scr, stat_scr, a_scr,
                            x1_scr.at[lax.rem(s + 1, 3)], h_scr.at[1 - cur]))

    @pl.when(s == n_tiles)
    def _():
        _interleave(finish_previous())


def _layer1(x, os, stats, gains, w_o, w_up, w_down, layer, tm):
    b, t, d = x.shape
    dilations = tuple(o.shape[1] for o in os)
    tiles_per_seq = t // tm
    n_tiles = b * tiles_per_seq
    assert t % tm == 0 and all(tm % (16 * dil) == 0 for dil in dilations)
    x2 = x.reshape(b * t, d)
    weights = (gains, w_o, w_up, w_down)
    weight_specs = [_const_spec(gains.shape), _const_spec(w_o.shape),
                    _layer_spec(w_up, layer), _layer_spec(w_down, layer)]

    def tile_specs(tile):
        sub = lambda dil, width: pl.BlockSpec(
            (None, dil, tm // dil, width),
            lambda s: (tile(s) // tiles_per_seq, 0, tile(s) % tiles_per_seq, 0))
        return ([pl.BlockSpec((tm, d), lambda s: (tile(s), 0))]
                + [sub(dil, GROUP_Q_WIDTH) for dil in dilations]
                + [sub(dil, STAT_LANES) for dil in dilations])

    if n_tiles == 1:
        y = pl.pallas_call(
            functools.partial(_layer1_single_kernel, dilations),
            grid=(1,),
            in_specs=tile_specs(lambda s: 0) + weight_specs,
            out_specs=pl.BlockSpec((tm, d), lambda s: (0, 0)),
            out_shape=jax.ShapeDtypeStruct((b * t, d), F32),
            scratch_shapes=_layer1_scratch(tm, d, 0),
            compiler_params=pltpu.CompilerParams(vmem_limit_bytes=VMEM_LIMIT),
            name="layer1_single_tile",
        )(x2, *os, *stats, *weights)
        return y.reshape(b, t, d)

    x1_first, h_first = pl.pallas_call(
        functools.partial(_layer1_first_kernel, dilations),
        grid=(1,),
        in_specs=tile_specs(lambda s: 0) + [_const_spec(gains.shape), _const_spec(w_o.shape)],
        out_specs=[pl.BlockSpec((tm, d), lambda s: (0, 0))] * 2,
        out_shape=[jax.ShapeDtypeStruct((tm, d), F32), jax.ShapeDtypeStruct((tm, d), BF16)],
        scratch_shapes=_layer1_scratch(tm, d, 0)[:3],
        compiler_params=pltpu.CompilerParams(vmem_limit_bytes=VMEM_LIMIT),
        name="layer1_first_tile",
    )(x2, *os, *stats, gains, w_o)

    scratch = _layer1_scratch(tm, d, 0)[:3] + [
        pltpu.VMEM((3, tm, d), F32), pltpu.VMEM((2, tm, d), BF16), pltpu.VMEM((2, tm, d), F32)]
    y = pl.pallas_call(
        functools.partial(_layer1_prompt_kernel, dilations),
        grid=(n_tiles + 1,),
        in_specs=tile_specs(lambda s: jnp.minimum(s + 1, n_tiles - 1))
        + [_const_spec(x1_first.shape), _const_spec(h_first.shape)]
        + weight_specs,
        out_specs=pl.BlockSpec((tm, d), lambda s: (jnp.maximum(s - 1, 0), 0)),
        out_shape=jax.ShapeDtypeStruct((b * t, d), F32),
        scratch_shapes=scratch,
        compiler_params=pltpu.CompilerParams(
            dimension_semantics=("arbitrary",), vmem_limit_bytes=VMEM_LIMIT),
        name="layer1_prompt",
    )(x2, *os, *stats, x1_first, h_first, *weights)
    return y.reshape(b, t, d)


PROMPT_TILE = 512
LAYER1_TILE = 512
ATTN_BLOCK_ROWS = 2048


def kernel(x_prompt, x_sample, cache_pool, cache_k, cache_v, norm_gains, kv_norm_gain, w_pool,
           pool_scale, w_q, w_o, w_kv, w_up, w_down):
    depth = norm_gains.shape[0]
    assert depth == 2 and cache_pool.shape[0] == 1 and w_q.shape[0] == 1
    bp, tp, d = x_prompt.shape
    bs, ts, _ = x_sample.shape

    g0, g1 = norm_gains[0], norm_gains[1]
    g1_q = g1[0:1]
    g_kv = kv_norm_gain[None, :]
    wp = w_pool[0].astype(BF16)
    ps = pool_scale[0][None, :]
    wq = w_q[0].reshape(d, N_GROUPS, KV_PER_GROUP, Q_PER_KV, HEAD_DIM).transpose(0, 1, 3, 2, 4)
    wq = wq.reshape(d, Q_WIDTH).astype(BF16)
    wo = w_o[0].reshape(N_GROUPS, KV_PER_GROUP, Q_PER_KV, HEAD_DIM, d).transpose(0, 2, 1, 3, 4)
    wo = wo.reshape(Q_WIDTH, d).astype(BF16)
    wkv = w_kv.astype(BF16)
    wup, wdn = w_up.astype(BF16), w_down.astype(BF16)

    keep = min(KV_WINDOW, tp)
    xp1, utail = _layer0_prompt(x_prompt, g0, wp, ps, wup, wdn, 0, PROMPT_TILE)
    rope_p = _rope_tables(jnp.arange(PROMPT_TILE), jnp.arange(tp // PROMPT_TILE) * PROMPT_TILE)
    qkv_p = _qkv(xp1, g1_q, g_kv, wq, wkv, rope_p, PROMPT_TILE, keep, ATTN_DILATIONS, True)
    os_p, stats_p = [], []
    for gi, dil in enumerate(ATTN_DILATIONS):
        o, stat = _band_attention(qkv_p[gi], qkv_p[N_GROUPS + gi], qkv_p[2 * N_GROUPS + gi],
                                  ATTN_BLOCK_ROWS)
        os_p.append(o)
        stats_p.append(stat)
    y_prompt = _layer1(xp1, os_p, stats_p, g1, wo, wup, wdn, 1, LAYER1_TILE)
    pool_prompt = utail[:, POOL_HALO - POOL_BUF:][None]
    k_prompt = qkv_p[-2].reshape(bp, N_KV_HEADS, HEAD_DIM, keep).transpose(0, 3, 1, 2)
    v_prompt = qkv_p[-1].reshape(bp, N_KV_HEADS, HEAD_DIM, keep).transpose(0, 3, 1, 2)

    n_tok = ts * bs
    xs_tm = jnp.swapaxes(x_sample, 0, 1)
    buf_tm = jnp.swapaxes(cache_pool[0], 0, 1)
    xs1_tm, us_tm = _layer0_sample(xs_tm, buf_tm, PAST_LEN, g0, wp, ps, wup, wdn, 0)
    xs1 = xs1_tm.reshape(1, n_tok, d)
    rope_s = _rope_tables(jnp.arange(n_tok) // bs, jnp.full((1,), PAST_LEN))
    qkv_s = _qkv(xs1, g1_q, g_kv, wq, wkv, rope_s, n_tok, n_tok, (1,) * N_GROUPS, False)
    q_s = jnp.concatenate([q[0, 0] for q in qkv_s[:N_GROUPS]], axis=-1)
    q16 = q_s.reshape(ts, bs, N_GROUPS, Q_PER_KV, KV_PER_GROUP, HEAD_DIM)
    q16 = q16.transpose(1, 2, 4, 3, 0, 5).reshape(bs, N_KV_HEADS, Q_PER_KV, ts, HEAD_DIM)
    q16 = jnp.pad(q16, ((0, 0), (0, 0), (0, SAMPLE_Q_ROWS // ts - Q_PER_KV), (0, 0), (0, 0)))
    q16 = q16.reshape(bs, N_KV_HEADS, SAMPLE_Q_ROWS, HEAD_DIM)
    k_s = jnp.swapaxes(qkv_s[-2].reshape(ts, bs, KV_WIDTH), 0, 1)
    v_s = jnp.swapaxes(qkv_s[-1].reshape(ts, bs, KV_WIDTH), 0, 1)
    pad_new = ((0, 0), (0, NEW_KEY_ROWS - ts), (0, 0))
    o16, lse16 = _sample_attention(
        q16, jnp.pad(k_s, pad_new), jnp.pad(v_s, pad_new),
        cache_k.transpose(0, 2, 3, 1), cache_v.transpose(0, 2, 3, 1), ts)
    heads_padded = SAMPLE_Q_ROWS // ts
    o_s = o16.reshape(bs, N_GROUPS, KV_PER_GROUP, heads_padded, ts, HEAD_DIM)[:, :, :, :Q_PER_KV]
    o_s = o_s.transpose(4, 0, 1, 3, 2, 5).reshape(1, 1, n_tok, N_GROUPS, GROUP_Q_WIDTH).astype(BF16)
    lse_s = lse16.reshape(bs, N_GROUPS, KV_PER_GROUP, heads_padded, ts)[:, :, :, :Q_PER_KV]
    lse_s = lse_s.transpose(4, 0, 1, 2, 3).reshape(1, 1, n_tok, N_GROUPS, KV_PER_GROUP, Q_PER_KV)
    ones = lambda n: jnp.ones((1, 1, n_tok, N_GROUPS, n), F32)
    max_lane = [_stat_lane(c, 0) + STAT_MAX_OFFSET for c in range(KV_PER_GROUP)]
    assert max_lane[1] < max_lane[0]
    stat_s = jnp.concatenate(
        [ones(max_lane[1]), lse_s[..., 1, :],
         ones(max_lane[0] - max_lane[1] - Q_PER_KV), lse_s[..., 0, :],
         ones(STAT_LANES - max_lane[0] - Q_PER_KV)], axis=-1)
    ys_tm = _layer1(xs1, [o_s[:, :, :, gi] for gi in range(N_GROUPS)],
                    [stat_s[:, :, :, gi] for gi in range(N_GROUPS)], g1, wo, wup, wdn, 1, n_tok)
    y_sample = jnp.swapaxes(ys_tm.reshape(ts, bs, d), 0, 1)
    u_s = jnp.swapaxes(us_tm, 0, 1)
    pool_sample = jnp.concatenate([cache_pool[0], u_s], axis=1)[:, -POOL_BUF:][None]
    k_sample = k_s.reshape(bs, ts, N_KV_HEADS, HEAD_DIM)
    v_sample = v_s.reshape(bs, ts, N_KV_HEADS, HEAD_DIM)

    return (y_prompt, y_sample, pool_prompt, k_prompt, v_prompt, pool_sample, k_sample, v_sample)
```

```python
import functools

import jax
import jax.numpy as jnp
from jax import lax
from jax.experimental import pallas as pl
from jax.experimental.pallas import tpu as pltpu

F32 = jnp.float32
BF16 = jnp.bfloat16

EPS = 1e-6
ROPE_THETA = 10000.0
PAST_LEN = 16384
POOL_WINDOWS = (2, 4, 8, 16)
POOL_BUF = max(POOL_WINDOWS) - 1
POOL_HALO = 16
HEAD_DIM = 64
ATTN_WINDOWS = (128, 512, 2048)
ATTN_DILATIONS = (1, 4, 16)
N_GROUPS = len(ATTN_WINDOWS)
KV_PER_GROUP = 2
Q_PER_KV = 3
HEADS_PER_GROUP = KV_PER_GROUP * Q_PER_KV
N_KV_HEADS = N_GROUPS * KV_PER_GROUP
GROUP_Q_WIDTH = HEADS_PER_GROUP * HEAD_DIM
GROUP_KV_WIDTH = KV_PER_GROUP * HEAD_DIM
Q_WIDTH = N_GROUPS * GROUP_Q_WIDTH
KV_WIDTH = N_GROUPS * GROUP_KV_WIDTH
KV_WINDOW = max(ATTN_WINDOWS)
BAND = 128
NEG_BIG = -1e30

LANES = 128
STAT_LANES = LANES
STAT_MAX_OFFSET = 8
VMEM_LIMIT = 56 * 1024 * 1024
FF_CHUNK = 512
NORM_ROW_BLOCKS = 4
POOL_NORM_BLOCKS = 1

for _w, _d in zip(ATTN_WINDOWS, ATTN_DILATIONS):
    assert _w // _d == BAND and _w % _d == 0
assert GROUP_KV_WIDTH == LANES


def _rms(x, g):
    return x * lax.rsqrt(jnp.mean(x * x, axis=-1, keepdims=True) + EPS) * g


def _mlp_residual(x1, g_in, g_out, wup_ref, wdn_ref):
    return _mlp_from_hidden(x1, _rms(x1, g_in).astype(BF16), g_out, wup_ref, wdn_ref)


def _mlp_from_hidden(x1, h, g_out, wup_ref, wdn_ref):
    acc = []
    _interleave(_mlp_matmul_steps(lambda: h, wup_ref, wdn_ref, acc.append))
    return x1 + _rms(acc[0], g_out)


def _mlp_matmul_steps(load_h, wup_ref, wdn_ref, emit_acc):
    n_chunks = wup_ref.shape[1] // FF_CHUNK
    yield n_chunks
    h = load_h()
    acc = None
    for c in range(n_chunks):
        cs = slice(c * FF_CHUNK, (c + 1) * FF_CHUNK)
        a = jnp.dot(h, wup_ref[:, cs], preferred_element_type=F32)
        a = jnp.square(jnp.maximum(a, 0.0)).astype(BF16)
        part = jnp.dot(a, wdn_ref[cs, :], preferred_element_type=F32)
        acc = part if acc is None else acc + part
        if c + 1 == n_chunks:
            emit_acc(acc)
        yield


def _row_blocks(n_rows, n_blocks=NORM_ROW_BLOCKS):
    size = n_rows // n_blocks if n_rows % (8 * n_blocks) == 0 else n_rows
    return [slice(lo, lo + size) for lo in range(0, n_rows, size)]


def _residual_norm_steps(dst_ref, load_x, load_v, g, n_blocks=NORM_ROW_BLOCKS):
    blocks = _row_blocks(dst_ref.shape[0], n_blocks)
    yield len(blocks)
    for rows in blocks:
        dst_ref[rows, :] = load_x(rows) + _rms(load_v(rows), g)
        yield


def _norm_cast_steps(dst_ref, load_v, g, n_blocks=NORM_ROW_BLOCKS):
    blocks = _row_blocks(dst_ref.shape[0], n_blocks)
    yield len(blocks)
    for rows in blocks:
        dst_ref[rows, :] = _rms(load_v(rows), g).astype(dst_ref.dtype)
        yield


def _chain(*generators):
    counts = [next(gen) for gen in generators]
    yield sum(counts)
    for gen in generators:
        yield from gen


def _interleave(*generators):
    totals = [next(gen) for gen in generators]
    done = [0] * len(generators)
    while any(d < t for d, t in zip(done, totals)):
        i = min((i for i in range(len(generators)) if done[i] < totals[i]),
                key=lambda i: (done[i] + 1) / totals[i])
        next(generators[i])
        done[i] += 1


def _const_spec(shape):
    zeros = (0,) * len(shape)
    return pl.BlockSpec(shape, lambda *_: zeros, pipeline_mode=pl.Buffered(1))


def _layer_spec(w, layer):
    index = (layer,) + (0,) * (w.ndim - 1)
    return pl.BlockSpec((None,) + w.shape[1:], lambda *_: index, pipeline_mode=pl.Buffered(1))


def _subseq_spec(dilation, rows, width):
    return pl.BlockSpec((None, dilation, rows // dilation, width), lambda bi, i: (bi, 0, i, 0))


def _subseq_rows(n_rows, r, dilation):
    n = n_rows // dilation
    return pl.ds(r, n, stride=dilation) if dilation > 1 else pl.ds(0, n)


def _pool_mixer_steps(x_ref, xh_ref, tile_in_seq, g, wp_ref, ps_ref, ext_ref, x1_ref, h_ref,
                      utail_ref=None):
    tm = x_ref.shape[0]
    pool_ch = wp_ref.shape[1]
    blocks = _row_blocks(tm, POOL_NORM_BLOCKS)
    yield 3 * len(blocks) + len(POOL_WINDOWS)
    uh = _rms(xh_ref[0:POOL_HALO, :], g[0:1]) * jnp.where(tile_in_seq > 0, 1.0, 0.0)
    ext_ref[0:POOL_HALO, :] = uh
    for rows in blocks:
        u = _rms(x_ref[rows, :], g[0:1])
        ext_ref[POOL_HALO + rows.start:POOL_HALO + rows.stop, :] = u
        if utail_ref is not None and rows.stop == tm:
            utail_ref[0] = u[u.shape[0] - POOL_HALO:, :]
        yield
    pos = tile_in_seq * tm + lax.broadcasted_iota(jnp.int32, (tm, 1), 0)
    parts = []
    for gi, w in enumerate(POOL_WINDOWS):
        cs = slice(gi * pool_ch, (gi + 1) * pool_ch)
        s = ext_ref[POOL_HALO:, cs]
        for j in range(1, w):
            s = s + ext_ref[POOL_HALO - j:POOL_HALO - j + tm, cs]
        cnt = jnp.minimum(w, pos + 1).astype(F32)
        pooled = s / cnt - ext_ref[POOL_HALO:, cs]
        parts.append(jnp.dot(pooled.astype(BF16), wp_ref[gi], preferred_element_type=F32))
        yield
    mix = jnp.concatenate(parts, axis=1) * ps_ref[...]
    tail = _chain(
        _residual_norm_steps(x1_ref, lambda rows: x_ref[rows, :], lambda rows: mix[rows], g[1:2],
                             POOL_NORM_BLOCKS),
        _norm_cast_steps(h_ref, lambda rows: x1_ref[rows, :], g[2:3], POOL_NORM_BLOCKS))
    next(tail)
    yield from tail


def _layer0_first_kernel(x_ref, g_ref, wp_ref, ps_ref, x1_ref, h_ref, ext_scr):
    _interleave(_pool_mixer_steps(x_ref, x_ref, 0, g_ref[...], wp_ref, ps_ref, ext_scr,
                                  x1_ref, h_ref))


def _layer0_prompt_kernel(tiles_per_seq, xa_ref, xha_ref, xb_ref, xhb_ref, x1f_ref, hf_ref, g_ref,
                          wp_ref, ps_ref, wup_ref, wdn_ref, y_ref, utail_ref,
                          x1_scr, h_scr, ext_scr):
    s = pl.program_id(0)
    n_tiles = 2 * pl.num_programs(0)
    tm = xa_ref.shape[0]
    g = g_ref[...]

    @pl.when(s == 0)
    def _():
        x1_scr[0] = x1f_ref[...]
        h_scr[0] = hf_ref[...]

    def prepare(x_ref, xh_ref, tile, slot, tail_ref):
        return _pool_mixer_steps(x_ref, xh_ref, lax.rem(tile, tiles_per_seq), g, wp_ref, ps_ref,
                                 ext_scr.at[slot], x1_scr.at[slot], h_scr.at[slot], tail_ref)

    def mlp(slot):
        acc = []
        return _chain(
            _mlp_matmul_steps(lambda: h_scr[slot], wup_ref, wdn_ref, acc.append),
            _residual_norm_steps(y_ref.at[pl.ds(slot * tm, tm)],
                                 lambda rows: x1_scr[slot, rows, :],
                                 lambda rows: acc[0][rows], g[3:4], POOL_NORM_BLOCKS))

    _interleave(mlp(0), prepare(xa_ref, xha_ref, 2 * s + 1, 1, utail_ref))
    _interleave(mlp(1), prepare(xb_ref, xhb_ref, jnp.minimum(2 * s + 2, n_tiles - 1), 0, None))


def _layer0_prompt(x, gains, w_pool, pool_scale, w_up, w_down, layer, tm):
    b, t, d = x.shape
    tiles_per_seq = t // tm
    n_tiles = b * tiles_per_seq
    assert t % tm == 0 and tm % POOL_HALO == 0 and tiles_per_seq % 2 == 0
    halo_per_tile = tm // POOL_HALO
    x2 = x.reshape(b * t, d)

    x1_first, h_first = pl.pallas_call(
        _layer0_first_kernel,
        grid=(1,),
        in_specs=[pl.BlockSpec((tm, d), lambda i: (0, 0)), _const_spec(gains.shape),
                  _const_spec(w_pool.shape), _const_spec(pool_scale.shape)],
        out_specs=[pl.BlockSpec((tm, d), lambda i: (0, 0))] * 2,
        out_shape=[jax.ShapeDtypeStruct((tm, d), F32), jax.ShapeDtypeStruct((tm, d), BF16)],
        scratch_shapes=[pltpu.VMEM((tm + POOL_HALO, d), F32)],
        compiler_params=pltpu.CompilerParams(vmem_limit_bytes=VMEM_LIMIT),
        name="layer0_first_tile",
    )(x2, gains, w_pool, pool_scale)

    tile_a = lambda s: 2 * s + 1
    tile_b = lambda s: jnp.minimum(2 * s + 2, n_tiles - 1)
    tile_spec = lambda tile: pl.BlockSpec((tm, d), lambda s: (tile(s), 0))
    halo_spec = lambda tile: pl.BlockSpec((POOL_HALO, d),
                                          lambda s: (tile(s) * halo_per_tile - 1, 0))
    y, utail = pl.pallas_call(
        functools.partial(_layer0_prompt_kernel, tiles_per_seq),
        grid=(n_tiles // 2,),
        in_specs=[
            tile_spec(tile_a), halo_spec(tile_a), tile_spec(tile_b), halo_spec(tile_b),
            _const_spec(x1_first.shape), _const_spec(h_first.shape),
            _const_spec(gains.shape), _const_spec(w_pool.shape), _const_spec(pool_scale.shape),
            _layer_spec(w_up, layer), _layer_spec(w_down, layer),
        ],
        out_specs=[
            pl.BlockSpec((2 * tm, d), lambda s: (s, 0)),
            pl.BlockSpec((1, POOL_HALO, d), lambda s: (tile_a(s) // tiles_per_seq, 0, 0)),
        ],
        out_shape=[
            jax.ShapeDtypeStruct((b * t, d), F32),
            jax.ShapeDtypeStruct((b, POOL_HALO, d), F32),
        ],
        scratch_shapes=[pltpu.VMEM((2, tm, d), F32), pltpu.VMEM((2, tm, d), BF16),
                        pltpu.VMEM((2, tm + POOL_HALO, d), F32)],
        compiler_params=pltpu.CompilerParams(
            dimension_semantics=("arbitrary",), vmem_limit_bytes=VMEM_LIMIT),
        name="layer0_prompt",
    )(x2, x2, x2, x2, x1_first, h_first, gains, w_pool, pool_scale, w_up, w_down)
    return y.reshape(b, t, d), utail


def _layer0_sample_kernel(start_pos, x_ref, buf_ref, g_ref, wp_ref, ps_ref, wup_ref, wdn_ref,
                          y_ref, u_ref):
    n_t, n_b, _ = x_ref.shape
    pool_ch = wp_ref.shape[1]
    g = g_ref[...]
    xs = [x_ref[t] for t in range(n_t)]
    us = [_rms(xt, g[0:1]) for xt in xs]
    ext = [buf_ref[j] for j in range(POOL_BUF)] + us
    parts = []
    for gi, w in enumerate(POOL_WINDOWS):
        cs = slice(gi * pool_ch, (gi + 1) * pool_ch)
        rows = []
        for t in range(n_t):
            s = us[t][:, cs]
            for j in range(1, w):
                s = s + ext[POOL_BUF + t - j][:, cs]
            cnt = float(min(w, start_pos + t + 1))
            rows.append(s / cnt - us[t][:, cs])
        pooled = jnp.concatenate(rows, axis=0)
        parts.append(jnp.dot(pooled.astype(BF16), wp_ref[gi], preferred_element_type=F32))
    mix = jnp.concatenate(parts, axis=1) * ps_ref[...]
    x = jnp.concatenate(xs, axis=0)
    x1 = x + _rms(mix, g[1:2])
    y = _mlp_residual(x1, g[2:3], g[3:4], wup_ref, wdn_ref)
    for t in range(n_t):
        y_ref[t] = y[t * n_b:(t + 1) * n_b]
        u_ref[t] = us[t]


def _layer0_sample(x_tm, buf_tm, start_pos, gains, w_pool, pool_scale, w_up, w_down, layer):
    assert start_pos + 1 >= max(POOL_WINDOWS) and buf_tm.shape[0] == POOL_BUF
    return pl.pallas_call(
        functools.partial(_layer0_sample_kernel, start_pos),
        grid=(1,),
        in_specs=[_const_spec(a.shape) for a in (x_tm, buf_tm, gains, w_pool, pool_scale)]
        + [_layer_spec(w_up, layer), _layer_spec(w_down, layer)],
        out_specs=[pl.BlockSpec(x_tm.shape, lambda i: (0, 0, 0))] * 2,
        out_shape=[jax.ShapeDtypeStruct(x_tm.shape, F32), jax.ShapeDtypeStruct(x_tm.shape, F32)],
        compiler_params=pltpu.CompilerParams(vmem_limit_bytes=VMEM_LIMIT),
        name="layer0_sample",
    )(x_tm, buf_tm, gains, w_pool, pool_scale, w_up, w_down)


def _rope(x, cos, sin_signed):
    lane = lax.broadcasted_iota(jnp.int32, (x.shape[0], LANES), 1)
    first_half = (lane & (HEAD_DIM // 2)) == 0
    out = []
    for c in range(x.shape[1] // LANES):
        xc = x[:, c * LANES:(c + 1) * LANES]
        partner = jnp.where(first_half,
                            pltpu.roll(xc, LANES - HEAD_DIM // 2, 1),
                            pltpu.roll(xc, HEAD_DIM // 2, 1))
        out.append(xc * cos + partner * sin_signed)
    return out


def _qkv_kernel(dilations, first_kept_tile, transpose_kept,
                x_ref, gq_ref, gkv_ref, wq_ref, wkv_ref, cos_ref, sin_ref, cos_step_ref, sin_step_ref,
                q0_ref, q1_ref, q2_ref, k0_ref, k1_ref, k2_ref, v0_ref, v1_ref, v2_ref,
                kf_ref, vf_ref, stage_scr):
    i = pl.program_id(1)
    x = x_ref[0]
    cos_in, sin_in = cos_ref[...], sin_ref[...]
    cos_at, sin_at = cos_step_ref[0:1, :], sin_step_ref[0:1, :]
    cos = cos_in * cos_at - sin_in * sin_at
    sin_signed = sin_in * cos_at + cos_in * sin_at
    chunks_per_group = GROUP_Q_WIDTH // LANES

    def emit(out_ref, dilation, chunks):
        if dilation == 1:
            for c, chunk in enumerate(chunks):
                out_ref[0, :, c * LANES:(c + 1) * LANES] = chunk.astype(BF16)
            return
        for c, chunk in enumerate(chunks):
            stage_scr[c] = chunk
        for r in range(dilation):
            rows = _subseq_rows(stage_scr.shape[1], r, dilation)
            for c in range(len(chunks)):
                out_ref[r, :, c * LANES:(c + 1) * LANES] = stage_scr[c, rows, :].astype(BF16)

    u = _rms(x, gq_ref[...]).astype(BF16)
    q = jnp.dot(u, wq_ref[...], preferred_element_type=F32)
    scale = HEAD_DIM ** -0.5
    q_chunks = _rope(q, cos * scale, sin_signed * scale)
    for gi, q_ref in enumerate((q0_ref, q1_ref, q2_ref)):
        emit(q_ref, dilations[gi], q_chunks[gi * chunks_per_group:(gi + 1) * chunks_per_group])

    un = _rms(x, gkv_ref[...]).astype(BF16)
    kv = jnp.dot(un, wkv_ref[...], preferred_element_type=F32)
    k_chunks = _rope(kv[:, :KV_WIDTH], cos, sin_signed)
    v_chunks = [kv[:, KV_WIDTH + gi * LANES:KV_WIDTH + (gi + 1) * LANES] for gi in range(N_GROUPS)]
    for gi, (k_ref, v_ref) in enumerate(((k0_ref, v0_ref), (k1_ref, v1_ref), (k2_ref, v2_ref))):
        emit(k_ref, dilations[gi], [k_chunks[gi]])
        emit(v_ref, dilations[gi], [v_chunks[gi]])

    @pl.when(i >= first_kept_tile)
    def _():
        for gi in range(N_GROUPS):
            cs = slice(gi * LANES, (gi + 1) * LANES)
            if transpose_kept:
                kf_ref[0, cs, :] = k_chunks[gi].T
                vf_ref[0, cs, :] = v_chunks[gi].T
            else:
                kf_ref[0, :, cs] = k_chunks[gi]
                vf_ref[0, :, cs] = v_chunks[gi]


def _qkv(x, g_q, g_kv, w_q, w_kv, rope, tm, keep_rows, dilations, transpose_kept):
    b, t, d = x.shape
    cos, sin_signed, cos_step, sin_step = rope
    assert t % tm == 0 and keep_rows % tm == 0 and all(tm % (16 * dil) == 0 for dil in dilations)
    first_kept_tile = (t - keep_rows) // tm
    kept_block = lambda bi, i: jnp.maximum(i - first_kept_tile, 0)
    if transpose_kept:
        kept = pl.BlockSpec((1, KV_WIDTH, tm), lambda bi, i: (bi, 0, kept_block(bi, i)))
        kept_shape = jax.ShapeDtypeStruct((b, KV_WIDTH, keep_rows), F32)
    else:
        kept = pl.BlockSpec((1, tm, KV_WIDTH), lambda bi, i: (bi, kept_block(bi, i), 0))
        kept_shape = jax.ShapeDtypeStruct((b, keep_rows, KV_WIDTH), F32)
    sub_shape = lambda dil, width: jax.ShapeDtypeStruct((b, dil, t // dil, width), BF16)
    q_specs = [_subseq_spec(dil, tm, GROUP_Q_WIDTH) for dil in dilations]
    kv_specs = [_subseq_spec(dil, tm, GROUP_KV_WIDTH) for dil in dilations]
    q_shapes = [sub_shape(dil, GROUP_Q_WIDTH) for dil in dilations]
    kv_shapes = [sub_shape(dil, GROUP_KV_WIDTH) for dil in dilations]
    return pl.pallas_call(
        functools.partial(_qkv_kernel, tuple(dilations), first_kept_tile, transpose_kept),
        grid=(b, t // tm),
        in_specs=[
            pl.BlockSpec((1, tm, d), lambda bi, i: (bi, i, 0)),
            _const_spec(g_q.shape), _const_spec(g_kv.shape),
            _const_spec(w_q.shape), _const_spec(w_kv.shape),
            _const_spec(cos.shape), _const_spec(sin_signed.shape),
            pl.BlockSpec((None,) + cos_step.shape[1:], lambda bi, i: (i, 0, 0)),
            pl.BlockSpec((None,) + sin_step.shape[1:], lambda bi, i: (i, 0, 0)),
        ],
        out_specs=q_specs + kv_specs + kv_specs + [kept, kept],
        out_shape=q_shapes + kv_shapes + kv_shapes + [kept_shape, kept_shape],
        scratch_shapes=[pltpu.VMEM((GROUP_Q_WIDTH // LANES, tm, LANES), F32)],
        compiler_params=pltpu.CompilerParams(
            dimension_semantics=("arbitrary", "arbitrary"), vmem_limit_bytes=VMEM_LIMIT),
        name="qkv_rope",
    )(x, g_q, g_kv, w_q, w_kv, cos, sin_signed, cos_step, sin_step)


def _rope_tables(offsets, tile_starts):
    half = HEAD_DIM // 2
    inv = jnp.tile(ROPE_THETA ** (-jnp.arange(0, HEAD_DIM, 2, dtype=F32) / HEAD_DIM), LANES // half)
    sign = jnp.where((jnp.arange(LANES) // half) % 2 == 0, -1.0, 1.0).astype(F32)
    tables = []
    for positions in (offsets, tile_starts):
        ang = positions.astype(F32)[:, None] * inv[None, :]
        tables += [jnp.cos(ang), jnp.sin(ang) * sign[None, :]]
    rows8 = lambda a: jnp.broadcast_to(a[:, None, :], (a.shape[0], 8, LANES))
    return tables[0], tables[1], rows8(tables[2]), rows8(tables[3])


def _stat_lane(c, k):
    return (1 - c) * HEAD_DIM + k


def _band_attention_kernel(seg_rows, tiles_per_subseq, q_ref, kp_ref, kc_ref, vp_ref, vc_ref,
                           o_ref, stat_ref, k_scr, v_scr, bias_scr, s_scr, p_scr):
    n_seg = q_ref.shape[0] // seg_rows
    bands_per_seg = seg_rows // BAND
    n_bands = n_seg * bands_per_seg
    step = pl.program_id(1)
    lane = lax.broadcasted_iota(jnp.int32, (1, LANES), 1)
    for c in range(KV_PER_GROUP):
        own = (lane // HEAD_DIM) == c
        for u in range(n_seg):
            base = u * (seg_rows + BAND)
            for dst, k_rows, v_rows in (
                    (slice(base, base + BAND), kp_ref[...], vp_ref[...]),
                    (slice(base + BAND, base + BAND + seg_rows),
                     kc_ref[u * seg_rows:(u + 1) * seg_rows, :],
                     vc_ref[u * seg_rows:(u + 1) * seg_rows, :])):
                k_scr[c, dst, :] = jnp.where(own, k_rows, jnp.zeros((), BF16))
                v_scr[c, dst, :] = jnp.where(own, v_rows, jnp.ones((), BF16))
    stat_ref[...] = jnp.ones(stat_ref.shape, F32)

    row = lax.broadcasted_iota(jnp.int32, (BAND, 2 * BAND), 0)
    col = lax.broadcasted_iota(jnp.int32, (BAND, 2 * BAND), 1)
    band_bias = jnp.where((col >= row) & (col <= row + BAND), 0.0, NEG_BIG)
    bias_scr[0] = band_bias
    bias_scr[1] = band_bias + jnp.where(col < BAND, NEG_BIG, 0.0)
    starts_subseq = True if n_seg > 1 else lax.rem(step, tiles_per_subseq) == 0

    def rows_of(g):
        return pl.ds(g * BAND if isinstance(g, int) else pl.multiple_of(g * BAND, BAND), BAND)

    def key_rows_of(g):
        staged = g + g // bands_per_seg
        start = staged * BAND if isinstance(g, int) else pl.multiple_of(staged * BAND, BAND)
        return pl.ds(start, 2 * BAND)

    def scores(g, slot):
        yield KV_PER_GROUP
        q = jnp.concatenate(
            [q_ref[rows_of(g), k * LANES:(k + 1) * LANES] for k in range(Q_PER_KV)], axis=0)
        for c in range(KV_PER_GROUP):
            s_scr[slot, c] = lax.dot_general(q, k_scr[c, key_rows_of(g), :],
                                             (((1,), (1,)), ((), ())),
                                             preferred_element_type=F32)
            yield

    def softmax(g, slot):
        yield KV_PER_GROUP * Q_PER_KV
        first = jnp.logical_and(starts_subseq, g % bands_per_seg == 0).astype(jnp.int32)
        for c in range(KV_PER_GROUP):
            for k in range(Q_PER_KV):
                part = pl.ds(k * BAND, BAND)
                s = s_scr[slot, c, part, :] + bias_scr[first]
                m = jnp.max(s, axis=1, keepdims=True)
                p_scr[slot, c, part, :] = jnp.exp(s - m).astype(BF16)
                stat_ref[rows_of(g), pl.ds(_stat_lane(c, k) + STAT_MAX_OFFSET, 1)] = m
                yield

    def weighted_values(g, slot):
        yield KV_PER_GROUP + Q_PER_KV + 1
        ov = []
        for c in range(KV_PER_GROUP):
            ov.append(jnp.dot(p_scr[slot, c], v_scr[c, key_rows_of(g), :],
                              preferred_element_type=F32))
            yield
        low = lane < HEAD_DIM
        sums = None
        for k in range(Q_PER_KV):
            part = slice(k * BAND, (k + 1) * BAND)
            o_ref[rows_of(g), k * LANES:(k + 1) * LANES] = (
                jnp.where(low, ov[0][part], ov[1][part]).astype(o_ref.dtype))
            sums_k = jnp.where(low, ov[1][part], ov[0][part])
            sums = sums_k if sums is None else jnp.where((lane % HEAD_DIM) == k, sums_k, sums)
            yield
        for c in range(KV_PER_GROUP):
            lanes = pl.ds(_stat_lane(c, 0), Q_PER_KV)
            stat_ref[rows_of(g), lanes] = sums[:, _stat_lane(c, 0):_stat_lane(c, 0) + Q_PER_KV]
        yield

    _interleave(scores(0, 0))
    _interleave(softmax(0, 0), scores(1, 1))

    def steady(t, carry):
        for g, slot in ((2 * t + 1, 1), (2 * t + 2, 0)):
            _interleave(weighted_values(g - 1, 1 - slot), softmax(g, slot),
                        scores(g + 1, 1 - slot))
        return carry

    assert n_bands % 2 == 0
    lax.fori_loop(0, (n_bands - 2) // 2, steady, 0)
    last = n_bands - 1
    _interleave(weighted_values(last - 1, (last - 1) % 2), softmax(last, last % 2))
    _interleave(weighted_values(last, last % 2))


def _band_attention(q, k, v, block_rows):
    b, dilation, sub, _ = q.shape
    seg_rows = min(sub, block_rows)
    total = dilation * sub
    assert total % block_rows == 0 and block_rows % seg_rows == 0 and sub % seg_rows == 0
    assert seg_rows % BAND == 0 and block_rows >= 2 * BAND
    n_seg = block_rows // seg_rows
    flat = lambda a: a.reshape(b, total, a.shape[-1])
    cur = lambda width: pl.BlockSpec((None, block_rows, width), lambda bi, i: (bi, i, 0))
    prev = pl.BlockSpec((None, BAND, GROUP_KV_WIDTH),
                        lambda bi, i: (bi, jnp.maximum(i * (block_rows // BAND) - 1, 0), 0))
    staged_rows = n_seg * (seg_rows + BAND)
    o, stat = pl.pallas_call(
        functools.partial(_band_attention_kernel, seg_rows, sub // seg_rows),
        grid=(b, total // block_rows),
        in_specs=[cur(GROUP_Q_WIDTH), prev, cur(GROUP_KV_WIDTH), prev, cur(GROUP_KV_WIDTH)],
        out_specs=[cur(GROUP_Q_WIDTH), cur(STAT_LANES)],
        out_shape=[jax.ShapeDtypeStruct((b, total, GROUP_Q_WIDTH), BF16),
                   jax.ShapeDtypeStruct((b, total, STAT_LANES), F32)],
        scratch_shapes=[pltpu.VMEM((KV_PER_GROUP, staged_rows, GROUP_KV_WIDTH), BF16),
                        pltpu.VMEM((KV_PER_GROUP, staged_rows, GROUP_KV_WIDTH), BF16),
                        pltpu.VMEM((2, BAND, 2 * BAND), F32),
                        pltpu.VMEM((2, KV_PER_GROUP, Q_PER_KV * BAND, 2 * BAND), F32),
                        pltpu.VMEM((2, KV_PER_GROUP, Q_PER_KV * BAND, 2 * BAND), BF16)],
        compiler_params=pltpu.CompilerParams(
            dimension_semantics=("arbitrary", "arbitrary"), vmem_limit_bytes=VMEM_LIMIT),
        name=f"band_attention_d{dilation}",
    )(flat(q), flat(k), flat(k), flat(v), flat(v))
    return (o.reshape(b, dilation, sub, GROUP_Q_WIDTH), stat.reshape(b, dilation, sub, STAT_LANES))


SAMPLE_Q_ROWS = 16
NEW_KEY_ROWS = 8
SAMPLE_SEQS_PER_STEP = 4


def _sample_attention_kernel(n_new, q_ref, kn_ref, vn_ref, kc0, kc1, kc2, vc0, vc1, vc2,
                             o_ref, lse_ref):
    n_seq = q_ref.shape[0]
    n_rows = q_ref.shape[2]
    contract_last = (((1,), (1,)), ((), ()))
    cache_refs = ((kc0, vc0), (kc1, vc1), (kc2, vc2))
    bias_c, bias_n = [], []
    for gi, (kc_ref, _) in enumerate(cache_refs):
        dmask = ATTN_DILATIONS[gi] - 1
        n_cache = kc_ref.shape[3]
        assert n_cache == ATTN_WINDOWS[gi]
        t_c = lax.broadcasted_iota(jnp.int32, (n_rows, n_cache), 0) & (n_new - 1)
        j_c = lax.broadcasted_iota(jnp.int32, (n_rows, n_cache), 1)
        valid_c = (j_c >= t_c) & (((j_c - t_c) & dmask) == 0)
        t_n = lax.broadcasted_iota(jnp.int32, (n_rows, NEW_KEY_ROWS), 0) & (n_new - 1)
        j_n = lax.broadcasted_iota(jnp.int32, (n_rows, NEW_KEY_ROWS), 1)
        valid_n = (j_n <= t_n) & (((t_n - j_n) & dmask) == 0)
        bias_c.append(jnp.where(valid_c, 0.0, NEG_BIG))
        bias_n.append(jnp.where(valid_n, 0.0, NEG_BIG))

    def one_sequence(b):
        yield len(cache_refs) * KV_PER_GROUP
        kn_all = kn_ref[b]
        vn_all = vn_ref[b]
        for gi, (kc_ref, vc_ref) in enumerate(cache_refs):
            for c in range(KV_PER_GROUP):
                kvh = gi * KV_PER_GROUP + c
                ns = slice(kvh * HEAD_DIM, (kvh + 1) * HEAD_DIM)
                q = q_ref[b, kvh]
                kt = kc_ref[b, c].astype(BF16)
                vt = vc_ref[b, c].astype(BF16)
                s_c = jnp.dot(q, kt, preferred_element_type=F32) + bias_c[gi]
                s_n = lax.dot_general(q, kn_all[:, ns].astype(BF16), contract_last,
                                      preferred_element_type=F32) + bias_n[gi]
                m = jnp.maximum(jnp.max(s_c, axis=1, keepdims=True),
                                jnp.max(s_n, axis=1, keepdims=True))
                p_c = jnp.exp(s_c - m)
                p_n = jnp.exp(s_n - m)
                l = jnp.sum(p_c, axis=1, keepdims=True) + jnp.sum(p_n, axis=1, keepdims=True)
                o = (lax.dot_general(p_c.astype(BF16), vt, contract_last,
                                     preferred_element_type=F32)
                     + jnp.dot(p_n.astype(BF16), vn_all[:, ns].astype(BF16),
                               preferred_element_type=F32))
                o_ref[b, kvh] = o / l
                lse_ref[b, kvh] = m + jnp.log(l)
                yield

    _interleave(*[one_sequence(b) for b in range(n_seq)])


def _sample_attention(q16, k_new, v_new, cache_kt, cache_vt, n_new):
    b = q16.shape[0]
    n_kv = q16.shape[1]
    assert cache_kt.shape[3] == KV_WINDOW and n_new & (n_new - 1) == 0 and n_new <= NEW_KEY_ROWS

    per_step = SAMPLE_SEQS_PER_STEP
    assert b % per_step == 0

    def cache_spec(gi):
        cols = ATTN_WINDOWS[gi]
        last = KV_WINDOW // cols - 1
        return pl.BlockSpec((per_step, KV_PER_GROUP, HEAD_DIM, cols),
                            lambda bi: (bi, gi, 0, last))

    whole = lambda a: pl.BlockSpec((per_step,) + a.shape[1:],
                                   lambda bi: (bi,) + (0,) * (a.ndim - 1))
    o_shape = jax.ShapeDtypeStruct((b, n_kv, SAMPLE_Q_ROWS, HEAD_DIM), F32)
    lse_shape = jax.ShapeDtypeStruct((b, n_kv, SAMPLE_Q_ROWS, 1), F32)
    return pl.pallas_call(
        functools.partial(_sample_attention_kernel, n_new),
        grid=(b // per_step,),
        in_specs=[whole(q16), whole(k_new), whole(v_new)]
        + [cache_spec(gi) for gi in range(N_GROUPS)] * 2,
        out_specs=[whole(o_shape), whole(lse_shape)],
        out_shape=[o_shape, lse_shape],
        compiler_params=pltpu.CompilerParams(
            dimension_semantics=("arbitrary",), vmem_limit_bytes=VMEM_LIMIT),
        name="sample_attention",
    )(q16, k_new, v_new, cache_kt, cache_kt, cache_kt, cache_vt, cache_vt, cache_vt)


def _attn_mix_steps(dilations, x_ref, o_refs, s_refs, g, wo_ref, o_scr, stat_scr, a_scr,
                    x1_ref, h_ref):
    tm = x_ref.shape[0]
    chunks_per_group = GROUP_Q_WIDTH // LANES
    yield 2 * N_GROUPS + 1 + 2 * len(_row_blocks(tm))
    for gi, (o_ref, s_ref) in enumerate(zip(o_refs, s_refs)):
        for r in range(dilations[gi]):
            rows = _subseq_rows(tm, r, dilations[gi])
            stat_scr[gi, rows, :] = s_ref[r]
            for c in range(chunks_per_group):
                o_scr[gi * chunks_per_group + c, rows, :] = (
                    o_ref[r, :, c * LANES:(c + 1) * LANES].astype(F32))
        yield
    sums = [stat_scr[gi] for gi in range(N_GROUPS)]
    maxes = [pltpu.roll(sm, STAT_LANES - STAT_MAX_OFFSET, 1) for sm in sums]
    top = jnp.maximum(jnp.maximum(maxes[0], maxes[1]), maxes[2])
    es = [jnp.exp(mx - top) for mx in maxes]
    den = sums[0] * es[0] + sums[1] * es[1] + sums[2] * es[2]
    slot_id = lax.broadcasted_iota(jnp.int32, (tm, GROUP_Q_WIDTH), 1) // HEAD_DIM
    for gi in range(N_GROUPS):
        scale = es[gi] / den
        wide = jnp.zeros((tm, GROUP_Q_WIDTH), F32)
        for slot in range(HEADS_PER_GROUP):
            stat_lane = _stat_lane(slot % KV_PER_GROUP, slot // KV_PER_GROUP)
            wide = jnp.where(slot_id == slot, scale[:, stat_lane:stat_lane + 1], wide)
        for c in range(chunks_per_group):
            ci = gi * chunks_per_group + c
            a_scr[:, ci * LANES:(ci + 1) * LANES] = (
                o_scr[ci] * wide[:, c * LANES:(c + 1) * LANES]).astype(BF16)
        yield
    mix = jnp.dot(a_scr[...], wo_ref[...], preferred_element_type=F32)
    yield
    tail = _chain(
        _residual_norm_steps(x1_ref, lambda rows: x_ref[rows, :], lambda rows: mix[rows], g[1:2]),
        _norm_cast_steps(h_ref, lambda rows: x1_ref[rows, :], g[2:3]))
    next(tail)
    yield from tail


def _layer1_scratch(tm, d, slots):
    lead = (slots,) if slots else ()
    return [pltpu.VMEM((Q_WIDTH // LANES, tm, LANES), F32),
            pltpu.VMEM((N_GROUPS, tm, STAT_LANES), F32),
            pltpu.VMEM((tm, Q_WIDTH), BF16),
            pltpu.VMEM(lead + (tm, d), F32),
            pltpu.VMEM(lead + (tm, d), BF16)]


def _layer1_single_kernel(dilations, x_ref, o0_ref, o1_ref, o2_ref, s0_ref, s1_ref, s2_ref, g_ref,
                          wo_ref, wup_ref, wdn_ref, y_ref, o_scr, stat_scr, a_scr, x1_scr, h_scr):
    g = g_ref[...]
    _interleave(_attn_mix_steps(dilations, x_ref, (o0_ref, o1_ref, o2_ref),
                                (s0_ref, s1_ref, s2_ref), g, wo_ref, o_scr, stat_scr, a_scr,
                                x1_scr, h_scr))
    y_ref[...] = _mlp_from_hidden(x1_scr[...], h_scr[...], g[3:4], wup_ref, wdn_ref)


def _layer1_first_kernel(dilations, x_ref, o0_ref, o1_ref, o2_ref, s0_ref, s1_ref, s2_ref, g_ref,
                         wo_ref, x1_ref, h_ref, o_scr, stat_scr, a_scr):
    _interleave(_attn_mix_steps(dilations, x_ref, (o0_ref, o1_ref, o2_ref),
                                (s0_ref, s1_ref, s2_ref), g_ref[...], wo_ref, o_scr, stat_scr,
                                a_scr, x1_ref, h_ref))


def _layer1_prompt_kernel(dilations, x_ref, o0_ref, o1_ref, o2_ref, s0_ref, s1_ref, s2_ref,
                          x1f_ref, hf_ref, g_ref, wo_ref, wup_ref, wdn_ref, y_ref,
                          o_scr, stat_scr, a_scr, x1_scr, h_scr, acc_scr):
    s = pl.program_id(0)
    n_tiles = pl.num_programs(0) - 1
    g = g_ref[...]

    @pl.when(s == 0)
    def _():
        x1_scr[0] = x1f_ref[...]
        h_scr[0] = hf_ref[...]
        x1_scr[2] = jnp.zeros(x1_scr.shape[1:], F32)
        acc_scr[1] = jnp.zeros(acc_scr.shape[1:], F32)

    def finish_previous():
        x1_slot, acc_slot = lax.rem(s + 2, 3), lax.rem(s + 1, 2)
        return _residual_norm_steps(y_ref, lambda rows: x1_scr[x1_slot, rows, :],
                                    lambda rows: acc_scr[acc_slot, rows, :], g[3:4])

    @pl.when(s < n_tiles)
    def _():
        cur = lax.rem(s, 2)

        def store_acc(acc):
            acc_scr[cur] = acc

        _interleave(
            finish_previous(),
            _mlp_matmul_steps(lambda: h_scr[cur], wup_ref, wdn_ref, store_acc),
            _attn_mix_steps(dilations, x_ref, (o0_ref, o1_ref, o2_ref), (s0_ref, s1_ref, s2_ref),
                            g, wo_ref, o_scr, stat_scr, a_scr,
                            x1_scr.at[lax.rem(s + 1, 3)], h_scr.at[1 - cur]))

    @pl.when(s == n_tiles)
    def _():
        _interleave(finish_previous())


def _layer1(x, os, stats, gains, w_o, w_up, w_down, layer, tm):
    b, t, d = x.shape
    dilations = tuple(o.shape[1] for o in os)
    tiles_per_seq = t // tm
    n_tiles = b * tiles_per_seq
    assert t % tm == 0 and all(tm % (16 * dil) == 0 for dil in dilations)
    x2 = x.reshape(b * t, d)
    weights = (gains, w_o, w_up, w_down)
    weight_specs = [_const_spec(gains.shape), _const_spec(w_o.shape),
                    _layer_spec(w_up, layer), _layer_spec(w_down, layer)]

    def tile_specs(tile):
        sub = lambda dil, width: pl.BlockSpec(
            (None, dil, tm // dil, width),
            lambda s: (tile(s) // tiles_per_seq, 0, tile(s) % tiles_per_seq, 0))
        return ([pl.BlockSpec((tm, d), lambda s: (tile(s), 0))]
                + [sub(dil, GROUP_Q_WIDTH) for dil in dilations]
                + [sub(dil, STAT_LANES) for dil in dilations])

    if n_tiles == 1:
        y = pl.pallas_call(
            functools.partial(_layer1_single_kernel, dilations),
            grid=(1,),
            in_specs=tile_specs(lambda s: 0) + weight_specs,
            out_specs=pl.BlockSpec((tm, d), lambda s: (0, 0)),
            out_shape=jax.ShapeDtypeStruct((b * t, d), F32),
            scratch_shapes=_layer1_scratch(tm, d, 0),
            compiler_params=pltpu.CompilerParams(vmem_limit_bytes=VMEM_LIMIT),
            name="layer1_single_tile",
        )(x2, *os, *stats, *weights)
        return y.reshape(b, t, d)

    x1_first, h_first = pl.pallas_call(
        functools.partial(_layer1_first_kernel, dilations),
        grid=(1,),
        in_specs=tile_specs(lambda s: 0) + [_const_spec(gains.shape), _const_spec(w_o.shape)],
        out_specs=[pl.BlockSpec((tm, d), lambda s: (0, 0))] * 2,
        out_shape=[jax.ShapeDtypeStruct((tm, d), F32), jax.ShapeDtypeStruct((tm, d), BF16)],
        scratch_shapes=_layer1_scratch(tm, d, 0)[:3],
        compiler_params=pltpu.CompilerParams(vmem_limit_bytes=VMEM_LIMIT),
        name="layer1_first_tile",
    )(x2, *os, *stats, gains, w_o)

    scratch = _layer1_scratch(tm, d, 0)[:3] + [
        pltpu.VMEM((3, tm, d), F32), pltpu.VMEM((2, tm, d), BF16), pltpu.VMEM((2, tm, d), F32)]
    y = pl.pallas_call(
        functools.partial(_layer1_prompt_kernel, dilations),
        grid=(n_tiles + 1,),
        in_specs=tile_specs(lambda s: jnp.minimum(s + 1, n_tiles - 1))
        + [_const_spec(x1_first.shape), _const_spec(h_first.shape)]
        + weight_specs,
        out_specs=pl.BlockSpec((tm, d), lambda s: (jnp.maximum(s - 1, 0), 0)),
        out_shape=jax.ShapeDtypeStruct((b * t, d), F32),
        scratch_shapes=scratch,
        compiler_params=pltpu.CompilerParams(
            dimension_semantics=("arbitrary",), vmem_limit_bytes=VMEM_LIMIT),
        name="layer1_prompt",
    )(x2, *os, *stats, x1_first, h_first, *weights)
    return y.reshape(b, t, d)


PROMPT_TILE = 512
LAYER1_TILE = 512
ATTN_BLOCK_ROWS = 4096


def kernel(x_prompt, x_sample, cache_pool, cache_k, cache_v, norm_gains, kv_norm_gain, w_pool,
           pool_scale, w_q, w_o, w_kv, w_up, w_down):
    depth = norm_gains.shape[0]
    assert depth == 2 and cache_pool.shape[0] == 1 and w_q.shape[0] == 1
    bp, tp, d = x_prompt.shape
    bs, ts, _ = x_sample.shape

    g0, g1 = norm_gains[0], norm_gains[1]
    g1_q = g1[0:1]
    g_kv = kv_norm_gain[None, :]
    wp = w_pool[0].astype(BF16)
    ps = pool_scale[0][None, :]
    wq = w_q[0].reshape(d, N_GROUPS, KV_PER_GROUP, Q_PER_KV, HEAD_DIM).transpose(0, 1, 3, 2, 4)
    wq = wq.reshape(d, Q_WIDTH).astype(BF16)
    wo = w_o[0].reshape(N_GROUPS, KV_PER_GROUP, Q_PER_KV, HEAD_DIM, d).transpose(0, 2, 1, 3, 4)
    wo = wo.reshape(Q_WIDTH, d).astype(BF16)
    wkv = w_kv.astype(BF16)
    wup, wdn = w_up.astype(BF16), w_down.astype(BF16)

    keep = min(KV_WINDOW, tp)
    xp1, utail = _layer0_prompt(x_prompt, g0, wp, ps, wup, wdn, 0, PROMPT_TILE)
    rope_p = _rope_tables(jnp.arange(PROMPT_TILE), jnp.arange(tp // PROMPT_TILE) * PROMPT_TILE)
    qkv_p = _qkv(xp1, g1_q, g_kv, wq, wkv, rope_p, PROMPT_TILE, keep, ATTN_DILATIONS, True)
    os_p, stats_p = [], []
    for gi, dil in enumerate(ATTN_DILATIONS):
        o, stat = _band_attention(qkv_p[gi], qkv_p[N_GROUPS + gi], qkv_p[2 * N_GROUPS + gi],
                                  ATTN_BLOCK_ROWS)
        os_p.append(o)
        stats_p.append(stat)
    y_prompt = _layer1(xp1, os_p, stats_p, g1, wo, wup, wdn, 1, LAYER1_TILE)
    pool_prompt = utail[:, POOL_HALO - POOL_BUF:][None]
    k_prompt = qkv_p[-2].reshape(bp, N_KV_HEADS, HEAD_DIM, keep).transpose(0, 3, 1, 2)
    v_prompt = qkv_p[-1].reshape(bp, N_KV_HEADS, HEAD_DIM, keep).transpose(0, 3, 1, 2)

    n_tok = ts * bs
    xs_tm = jnp.swapaxes(x_sample, 0, 1)
    buf_tm = jnp.swapaxes(cache_pool[0], 0, 1)
    xs1_tm, us_tm = _layer0_sample(xs_tm, buf_tm, PAST_LEN, g0, wp, ps, wup, wdn, 0)
    xs1 = xs1_tm.reshape(1, n_tok, d)
    rope_s = _rope_tables(jnp.arange(n_tok) // bs, jnp.full((1,), PAST_LEN))
    qkv_s = _qkv(xs1, g1_q, g_kv, wq, wkv, rope_s, n_tok, n_tok, (1,) * N_GROUPS, False)
    q_s = jnp.concatenate([q[0, 0] for q in qkv_s[:N_GROUPS]], axis=-1)
    q16 = q_s.reshape(ts, bs, N_GROUPS, Q_PER_KV, KV_PER_GROUP, HEAD_DIM)
    q16 = q16.transpose(1, 2, 4, 3, 0, 5).reshape(bs, N_KV_HEADS, Q_PER_KV, ts, HEAD_DIM)
    q16 = jnp.pad(q16, ((0, 0), (0, 0), (0, SAMPLE_Q_ROWS // ts - Q_PER_KV), (0, 0), (0, 0)))
    q16 = q16.reshape(bs, N_KV_HEADS, SAMPLE_Q_ROWS, HEAD_DIM)
    k_s = jnp.swapaxes(qkv_s[-2].reshape(ts, bs, KV_WIDTH), 0, 1)
    v_s = jnp.swapaxes(qkv_s[-1].reshape(ts, bs, KV_WIDTH), 0, 1)
    pad_new = ((0, 0), (0, NEW_KEY_ROWS - ts), (0, 0))
    o16, lse16 = _sample_attention(
        q16, jnp.pad(k_s, pad_new), jnp.pad(v_s, pad_new),
        cache_k.transpose(0, 2, 3, 1), cache_v.transpose(0, 2, 3, 1), ts)
    heads_padded = SAMPLE_Q_ROWS // ts
    o_s = o16.reshape(bs, N_GROUPS, KV_PER_GROUP, heads_padded, ts, HEAD_DIM)[:, :, :, :Q_PER_KV]
    o_s = o_s.transpose(4, 0, 1, 3, 2, 5).reshape(1, 1, n_tok, N_GROUPS, GROUP_Q_WIDTH).astype(BF16)
    lse_s = lse16.reshape(bs, N_GROUPS, KV_PER_GROUP, heads_padded, ts)[:, :, :, :Q_PER_KV]
    lse_s = lse_s.transpose(4, 0, 1, 2, 3).reshape(1, 1, n_tok, N_GROUPS, KV_PER_GROUP, Q_PER_KV)
    ones = lambda n: jnp.ones((1, 1, n_tok, N_GROUPS, n), F32)
    max_lane = [_stat_lane(c, 0) + STAT_MAX_OFFSET for c in range(KV_PER_GROUP)]
    assert max_lane[1] < max_lane[0]
    stat_s = jnp.concatenate(
        [ones(max_lane[1]), lse_s[..., 1, :],
         ones(max_lane[0] - max_lane[1] - Q_PER_KV), lse_s[..., 0, :],
         ones(STAT_LANES - max_lane[0] - Q_PER_KV)], axis=-1)
    ys_tm = _layer1(xs1, [o_s[:, :, :, gi] for gi in range(N_GROUPS)],
                    [stat_s[:, :, :, gi] for gi in range(N_GROUPS)], g1, wo, wup, wdn, 1, n_tok)
    y_sample = jnp.swapaxes(ys_tm.reshape(ts, bs, d), 0, 1)
    u_s = jnp.swapaxes(us_tm, 0, 1)
    pool_sample = jnp.concatenate([cache_pool[0], u_s], axis=1)[:, -POOL_BUF:][None]
    k_sample = k_s.reshape(bs, ts, N_KV_HEADS, HEAD_DIM)
    v_sample = v_s.reshape(bs, ts, N_KV_HEADS, HEAD_DIM)

    return (y_prompt, y_sample, pool_prompt, k_prompt, v_prompt, pool_sample, k_sample, v_sample)
```

```python
import functools

import jax
import jax.numpy as jnp
from jax import lax
from jax.experimental import pallas as pl
from jax.experimental.pallas import tpu as pltpu

F32 = jnp.float32
BF16 = jnp.bfloat16

EPS = 1e-6
ROPE_THETA = 10000.0
PAST_LEN = 16384
POOL_WINDOWS = (2, 4, 8, 16)
POOL_BUF = max(POOL_WINDOWS) - 1
POOL_HALO = 16
HEAD_DIM = 64
ATTN_WINDOWS = (128, 512, 2048)
ATTN_DILATIONS = (1, 4, 16)
N_GROUPS = len(ATTN_WINDOWS)
KV_PER_GROUP = 2
Q_PER_KV = 3
HEADS_PER_GROUP = KV_PER_GROUP * Q_PER_KV
N_KV_HEADS = N_GROUPS * KV_PER_GROUP
GROUP_Q_WIDTH = HEADS_PER_GROUP * HEAD_DIM
GROUP_KV_WIDTH = KV_PER_GROUP * HEAD_DIM
Q_WIDTH = N_GROUPS * GROUP_Q_WIDTH
KV_WIDTH = N_GROUPS * GROUP_KV_WIDTH
KV_WINDOW = max(ATTN_WINDOWS)
BAND = 128
NEG_BIG = -1e30

LANES = 128
STAT_LANES = LANES
STAT_MAX_OFFSET = 8
VMEM_LIMIT = 56 * 1024 * 1024
FF_CHUNK = 512
NORM_ROW_BLOCKS = 4
POOL_NORM_BLOCKS = 1
POOL_FF_CHUNK = 1024

for _w, _d in zip(ATTN_WINDOWS, ATTN_DILATIONS):
    assert _w // _d == BAND and _w % _d == 0
assert GROUP_KV_WIDTH == LANES


def _rms(x, g):
    return x * lax.rsqrt(jnp.mean(x * x, axis=-1, keepdims=True) + EPS) * g


def _mlp_residual(x1, g_in, g_out, wup_ref, wdn_ref):
    return _mlp_from_hidden(x1, _rms(x1, g_in).astype(BF16), g_out, wup_ref, wdn_ref)


def _mlp_from_hidden(x1, h, g_out, wup_ref, wdn_ref):
    acc = []
    _interleave(_mlp_matmul_steps(lambda: h, wup_ref, wdn_ref, acc.append))
    return x1 + _rms(acc[0], g_out)


def _mlp_matmul_steps(load_h, wup_ref, wdn_ref, emit_acc, chunk=FF_CHUNK):
    n_chunks = wup_ref.shape[1] // chunk
    yield n_chunks
    h = load_h()
    acc = None
    for c in range(n_chunks):
        cs = slice(c * chunk, (c + 1) * chunk)
        a = jnp.dot(h, wup_ref[:, cs], preferred_element_type=F32)
        a = jnp.square(jnp.maximum(a, 0.0)).astype(BF16)
        part = jnp.dot(a, wdn_ref[cs, :], preferred_element_type=F32)
        acc = part if acc is None else acc + part
        if c + 1 == n_chunks:
            emit_acc(acc)
        yield


def _row_blocks(n_rows, n_blocks=NORM_ROW_BLOCKS):
    size = n_rows // n_blocks if n_rows % (8 * n_blocks) == 0 else n_rows
    return [slice(lo, lo + size) for lo in range(0, n_rows, size)]


def _residual_norm_steps(dst_ref, load_x, load_v, g, n_blocks=NORM_ROW_BLOCKS):
    blocks = _row_blocks(dst_ref.shape[0], n_blocks)
    yield len(blocks)
    for rows in blocks:
        dst_ref[rows, :] = load_x(rows) + _rms(load_v(rows), g)
        yield


def _norm_cast_steps(dst_ref, load_v, g, n_blocks=NORM_ROW_BLOCKS):
    blocks = _row_blocks(dst_ref.shape[0], n_blocks)
    yield len(blocks)
    for rows in blocks:
        dst_ref[rows, :] = _rms(load_v(rows), g).astype(dst_ref.dtype)
        yield


def _chain(*generators):
    counts = [next(gen) for gen in generators]
    yield sum(counts)
    for gen in generators:
        yield from gen


def _interleave(*generators):
    totals = [next(gen) for gen in generators]
    done = [0] * len(generators)
    while any(d < t for d, t in zip(done, totals)):
        i = min((i for i in range(len(generators)) if done[i] < totals[i]),
                key=lambda i: (done[i] + 1) / totals[i])
        next(generators[i])
        done[i] += 1


def _const_spec(shape):
    zeros = (0,) * len(shape)
    return pl.BlockSpec(shape, lambda *_: zeros, pipeline_mode=pl.Buffered(1))


def _layer_spec(w, layer):
    index = (layer,) + (0,) * (w.ndim - 1)
    return pl.BlockSpec((None,) + w.shape[1:], lambda *_: index, pipeline_mode=pl.Buffered(1))


def _subseq_spec(dilation, rows, width):
    return pl.BlockSpec((None, dilation, rows // dilation, width), lambda bi, i: (bi, 0, i, 0))


def _subseq_rows(n_rows, r, dilation):
    n = n_rows // dilation
    return pl.ds(r, n, stride=dilation) if dilation > 1 else pl.ds(0, n)


def _pool_mixer_steps(x_ref, xh_ref, tile_in_seq, g, wp_ref, ps_ref, ext_ref, x1_ref, h_ref,
                      utail_ref=None):
    tm = x_ref.shape[0]
    pool_ch = wp_ref.shape[1]
    blocks = _row_blocks(tm, POOL_NORM_BLOCKS)
    yield 3 * len(blocks) + len(POOL_WINDOWS)
    uh = _rms(xh_ref[0:POOL_HALO, :], g[0:1]) * jnp.where(tile_in_seq > 0, 1.0, 0.0)
    ext_ref[0:POOL_HALO, :] = uh
    for rows in blocks:
        u = _rms(x_ref[rows, :], g[0:1])
        ext_ref[POOL_HALO + rows.start:POOL_HALO + rows.stop, :] = u
        if utail_ref is not None and rows.stop == tm:
            utail_ref[0] = u[u.shape[0] - POOL_HALO:, :]
        yield
    pos = tile_in_seq * tm + lax.broadcasted_iota(jnp.int32, (tm, 1), 0)
    parts = []
    for gi, w in enumerate(POOL_WINDOWS):
        cs = slice(gi * pool_ch, (gi + 1) * pool_ch)
        s = ext_ref[POOL_HALO:, cs]
        for j in range(1, w):
            s = s + ext_ref[POOL_HALO - j:POOL_HALO - j + tm, cs]
        cnt = jnp.minimum(w, pos + 1).astype(F32)
        pooled = s / cnt - ext_ref[POOL_HALO:, cs]
        parts.append(jnp.dot(pooled.astype(BF16), wp_ref[gi], preferred_element_type=F32))
        yield
    mix = jnp.concatenate(parts, axis=1) * ps_ref[...]
    tail = _chain(
        _residual_norm_steps(x1_ref, lambda rows: x_ref[rows, :], lambda rows: mix[rows], g[1:2],
                             POOL_NORM_BLOCKS),
        _norm_cast_steps(h_ref, lambda rows: x1_ref[rows, :], g[2:3], POOL_NORM_BLOCKS))
    next(tail)
    yield from tail


def _layer0_first_kernel(x_ref, g_ref, wp_ref, ps_ref, x1_ref, h_ref, ext_scr):
    _interleave(_pool_mixer_steps(x_ref, x_ref, 0, g_ref[...], wp_ref, ps_ref, ext_scr,
                                  x1_ref, h_ref))


def _layer0_prompt_kernel(tiles_per_seq, xa_ref, xha_ref, xb_ref, xhb_ref, x1f_ref, hf_ref, g_ref,
                          wp_ref, ps_ref, wup_ref, wdn_ref, y_ref, utail_ref,
                          x1_scr, h_scr, ext_scr):
    s = pl.program_id(0)
    n_tiles = 2 * pl.num_programs(0)
    tm = xa_ref.shape[0]
    g = g_ref[...]

    @pl.when(s == 0)
    def _():
        x1_scr[0] = x1f_ref[...]
        h_scr[0] = hf_ref[...]

    def prepare(x_ref, xh_ref, tile, slot, tail_ref):
        return _pool_mixer_steps(x_ref, xh_ref, lax.rem(tile, tiles_per_seq), g, wp_ref, ps_ref,
                                 ext_scr.at[slot], x1_scr.at[slot], h_scr.at[slot], tail_ref)

    def mlp(slot):
        acc = []
        return _chain(
            _mlp_matmul_steps(lambda: h_scr[slot], wup_ref, wdn_ref, acc.append, POOL_FF_CHUNK),
            _residual_norm_steps(y_ref.at[pl.ds(slot * tm, tm)],
                                 lambda rows: x1_scr[slot, rows, :],
                                 lambda rows: acc[0][rows], g[3:4], POOL_NORM_BLOCKS))

    _interleave(mlp(0), prepare(xa_ref, xha_ref, 2 * s + 1, 1, utail_ref))
    _interleave(mlp(1), prepare(xb_ref, xhb_ref, jnp.minimum(2 * s + 2, n_tiles - 1), 0, None))


def _layer0_prompt(x, gains, w_pool, pool_scale, w_up, w_down, layer, tm):
    b, t, d = x.shape
    tiles_per_seq = t // tm
    n_tiles = b * tiles_per_seq
    assert t % tm == 0 and tm % POOL_HALO == 0 and tiles_per_seq % 2 == 0
    halo_per_tile = tm // POOL_HALO
    x2 = x.reshape(b * t, d)

    x1_first, h_first = pl.pallas_call(
        _layer0_first_kernel,
        grid=(1,),
        in_specs=[pl.BlockSpec((tm, d), lambda i: (0, 0)), _const_spec(gains.shape),
                  _const_spec(w_pool.shape), _const_spec(pool_scale.shape)],
        out_specs=[pl.BlockSpec((tm, d), lambda i: (0, 0))] * 2,
        out_shape=[jax.ShapeDtypeStruct((tm, d), F32), jax.ShapeDtypeStruct((tm, d), BF16)],
        scratch_shapes=[pltpu.VMEM((tm + POOL_HALO, d), F32)],
        compiler_params=pltpu.CompilerParams(vmem_limit_bytes=VMEM_LIMIT),
        name="layer0_first_tile",
    )(x2, gains, w_pool, pool_scale)

    tile_a = lambda s: 2 * s + 1
    tile_b = lambda s: jnp.minimum(2 * s + 2, n_tiles - 1)
    tile_spec = lambda tile: pl.BlockSpec((tm, d), lambda s: (tile(s), 0))
    halo_spec = lambda tile: pl.BlockSpec((POOL_HALO, d),
                                          lambda s: (tile(s) * halo_per_tile - 1, 0))
    y, utail = pl.pallas_call(
        functools.partial(_layer0_prompt_kernel, tiles_per_seq),
        grid=(n_tiles // 2,),
        in_specs=[
            tile_spec(tile_a), halo_spec(tile_a), tile_spec(tile_b), halo_spec(tile_b),
            _const_spec(x1_first.shape), _const_spec(h_first.shape),
            _const_spec(gains.shape), _const_spec(w_pool.shape), _const_spec(pool_scale.shape),
            _layer_spec(w_up, layer), _layer_spec(w_down, layer),
        ],
        out_specs=[
            pl.BlockSpec((2 * tm, d), lambda s: (s, 0)),
            pl.BlockSpec((1, POOL_HALO, d), lambda s: (tile_a(s) // tiles_per_seq, 0, 0)),
        ],
        out_shape=[
            jax.ShapeDtypeStruct((b * t, d), F32),
            jax.ShapeDtypeStruct((b, POOL_HALO, d), F32),
        ],
        scratch_shapes=[pltpu.VMEM((2, tm, d), F32), pltpu.VMEM((2, tm, d), BF16),
                        pltpu.VMEM((2, tm + POOL_HALO, d), F32)],
        compiler_params=pltpu.CompilerParams(
            dimension_semantics=("arbitrary",), vmem_limit_bytes=VMEM_LIMIT),
        name="layer0_prompt",
    )(x2, x2, x2, x2, x1_first, h_first, gains, w_pool, pool_scale, w_up, w_down)
    return y.reshape(b, t, d), utail


def _layer0_sample_kernel(start_pos, x_ref, buf_ref, g_ref, wp_ref, ps_ref, wup_ref, wdn_ref,
                          y_ref, u_ref):
    n_t, n_b, _ = x_ref.shape
    pool_ch = wp_ref.shape[1]
    g = g_ref[...]
    xs = [x_ref[t] for t in range(n_t)]
    us = [_rms(xt, g[0:1]) for xt in xs]
    ext = [buf_ref[j] for j in range(POOL_BUF)] + us
    parts = []
    for gi, w in enumerate(POOL_WINDOWS):
        cs = slice(gi * pool_ch, (gi + 1) * pool_ch)
        rows = []
        for t in range(n_t):
            s = us[t][:, cs]
            for j in range(1, w):
                s = s + ext[POOL_BUF + t - j][:, cs]
            cnt = float(min(w, start_pos + t + 1))
            rows.append(s / cnt - us[t][:, cs])
        pooled = jnp.concatenate(rows, axis=0)
        parts.append(jnp.dot(pooled.astype(BF16), wp_ref[gi], preferred_element_type=F32))
    mix = jnp.concatenate(parts, axis=1) * ps_ref[...]
    x = jnp.concatenate(xs, axis=0)
    x1 = x + _rms(mix, g[1:2])
    y = _mlp_residual(x1, g[2:3], g[3:4], wup_ref, wdn_ref)
    for t in range(n_t):
        y_ref[t] = y[t * n_b:(t + 1) * n_b]
        u_ref[t] = us[t]


def _layer0_sample(x_tm, buf_tm, start_pos, gains, w_pool, pool_scale, w_up, w_down, layer):
    assert start_pos + 1 >= max(POOL_WINDOWS) and buf_tm.shape[0] == POOL_BUF
    return pl.pallas_call(
        functools.partial(_layer0_sample_kernel, start_pos),
        grid=(1,),
        in_specs=[_const_spec(a.shape) for a in (x_tm, buf_tm, gains, w_pool, pool_scale)]
        + [_layer_spec(w_up, layer), _layer_spec(w_down, layer)],
        out_specs=[pl.BlockSpec(x_tm.shape, lambda i: (0, 0, 0))] * 2,
        out_shape=[jax.ShapeDtypeStruct(x_tm.shape, F32), jax.ShapeDtypeStruct(x_tm.shape, F32)],
        compiler_params=pltpu.CompilerParams(vmem_limit_bytes=VMEM_LIMIT),
        name="layer0_sample",
    )(x_tm, buf_tm, gains, w_pool, pool_scale, w_up, w_down)


def _rope(x, cos, sin_signed):
    lane = lax.broadcasted_iota(jnp.int32, (x.shape[0], LANES), 1)
    first_half = (lane & (HEAD_DIM // 2)) == 0
    out = []
    for c in range(x.shape[1] // LANES):
        xc = x[:, c * LANES:(c + 1) * LANES]
        partner = jnp.where(first_half,
                            pltpu.roll(xc, LANES - HEAD_DIM // 2, 1),
                            pltpu.roll(xc, HEAD_DIM // 2, 1))
        out.append(xc * cos + partner * sin_signed)
    return out


def _qkv_kernel(dilations, first_kept_tile, transpose_kept,
                x_ref, gq_ref, gkv_ref, wq_ref, wkv_ref, cos_ref, sin_ref, cos_step_ref, sin_step_ref,
                q0_ref, q1_ref, q2_ref, k0_ref, k1_ref, k2_ref, v0_ref, v1_ref, v2_ref,
                kf_ref, vf_ref, stage_scr):
    i = pl.program_id(1)
    x = x_ref[0]
    cos_in, sin_in = cos_ref[...], sin_ref[...]
    cos_at, sin_at = cos_step_ref[0:1, :], sin_step_ref[0:1, :]
    cos = cos_in * cos_at - sin_in * sin_at
    sin_signed = sin_in * cos_at + cos_in * sin_at
    chunks_per_group = GROUP_Q_WIDTH // LANES

    def emit(out_ref, dilation, chunks):
        if dilation == 1:
            for c, chunk in enumerate(chunks):
                out_ref[0, :, c * LANES:(c + 1) * LANES] = chunk.astype(BF16)
            return
        for c, chunk in enumerate(chunks):
            stage_scr[c] = chunk
        for r in range(dilation):
            rows = _subseq_rows(stage_scr.shape[1], r, dilation)
            for c in range(len(chunks)):
                out_ref[r, :, c * LANES:(c + 1) * LANES] = stage_scr[c, rows, :].astype(BF16)

    u = _rms(x, gq_ref[...]).astype(BF16)
    q = jnp.dot(u, wq_ref[...], preferred_element_type=F32)
    scale = HEAD_DIM ** -0.5
    q_chunks = _rope(q, cos * scale, sin_signed * scale)
    for gi, q_ref in enumerate((q0_ref, q1_ref, q2_ref)):
        emit(q_ref, dilations[gi], q_chunks[gi * chunks_per_group:(gi + 1) * chunks_per_group])

    un = _rms(x, gkv_ref[...]).astype(BF16)
    kv = jnp.dot(un, wkv_ref[...], preferred_element_type=F32)
    k_chunks = _rope(kv[:, :KV_WIDTH], cos, sin_signed)
    v_chunks = [kv[:, KV_WIDTH + gi * LANES:KV_WIDTH + (gi + 1) * LANES] for gi in range(N_GROUPS)]
    for gi, (k_ref, v_ref) in enumerate(((k0_ref, v0_ref), (k1_ref, v1_ref), (k2_ref, v2_ref))):
        emit(k_ref, dilations[gi], [k_chunks[gi]])
        emit(v_ref, dilations[gi], [v_chunks[gi]])

    @pl.when(i >= first_kept_tile)
    def _():
        for gi in range(N_GROUPS):
            cs = slice(gi * LANES, (gi + 1) * LANES)
            if transpose_kept:
                kf_ref[0, cs, :] = k_chunks[gi].T
                vf_ref[0, cs, :] = v_chunks[gi].T
            else:
                kf_ref[0, :, cs] = k_chunks[gi]
                vf_ref[0, :, cs] = v_chunks[gi]


def _qkv(x, g_q, g_kv, w_q, w_kv, rope, tm, keep_rows, dilations, transpose_kept):
    b, t, d = x.shape
    cos, sin_signed, cos_step, sin_step = rope
    assert t % tm == 0 and keep_rows % tm == 0 and all(tm % (16 * dil) == 0 for dil in dilations)
    first_kept_tile = (t - keep_rows) // tm
    kept_block = lambda bi, i: jnp.maximum(i - first_kept_tile, 0)
    if transpose_kept:
        kept = pl.BlockSpec((1, KV_WIDTH, tm), lambda bi, i: (bi, 0, kept_block(bi, i)))
        kept_shape = jax.ShapeDtypeStruct((b, KV_WIDTH, keep_rows), F32)
    else:
        kept = pl.BlockSpec((1, tm, KV_WIDTH), lambda bi, i: (bi, kept_block(bi, i), 0))
        kept_shape = jax.ShapeDtypeStruct((b, keep_rows, KV_WIDTH), F32)
    sub_shape = lambda dil, width: jax.ShapeDtypeStruct((b, dil, t // dil, width), BF16)
    q_specs = [_subseq_spec(dil, tm, GROUP_Q_WIDTH) for dil in dilations]
    kv_specs = [_subseq_spec(dil, tm, GROUP_KV_WIDTH) for dil in dilations]
    q_shapes = [sub_shape(dil, GROUP_Q_WIDTH) for dil in dilations]
    kv_shapes = [sub_shape(dil, GROUP_KV_WIDTH) for dil in dilations]
    return pl.pallas_call(
        functools.partial(_qkv_kernel, tuple(dilations), first_kept_tile, transpose_kept),
        grid=(b, t // tm),
        in_specs=[
            pl.BlockSpec((1, tm, d), lambda bi, i: (bi, i, 0)),
            _const_spec(g_q.shape), _const_spec(g_kv.shape),
            _const_spec(w_q.shape), _const_spec(w_kv.shape),
            _const_spec(cos.shape), _const_spec(sin_signed.shape),
            pl.BlockSpec((None,) + cos_step.shape[1:], lambda bi, i: (i, 0, 0)),
            pl.BlockSpec((None,) + sin_step.shape[1:], lambda bi, i: (i, 0, 0)),
        ],
        out_specs=q_specs + kv_specs + kv_specs + [kept, kept],
        out_shape=q_shapes + kv_shapes + kv_shapes + [kept_shape, kept_shape],
        scratch_shapes=[pltpu.VMEM((GROUP_Q_WIDTH // LANES, tm, LANES), F32)],
        compiler_params=pltpu.CompilerParams(
            dimension_semantics=("arbitrary", "arbitrary"), vmem_limit_bytes=VMEM_LIMIT),
        name="qkv_rope",
    )(x, g_q, g_kv, w_q, w_kv, cos, sin_signed, cos_step, sin_step)


def _rope_tables(offsets, tile_starts):
    half = HEAD_DIM // 2
    inv = jnp.tile(ROPE_THETA ** (-jnp.arange(0, HEAD_DIM, 2, dtype=F32) / HEAD_DIM), LANES // half)
    sign = jnp.where((jnp.arange(LANES) // half) % 2 == 0, -1.0, 1.0).astype(F32)
    tables = []
    for positions in (offsets, tile_starts):
        ang = positions.astype(F32)[:, None] * inv[None, :]
        tables += [jnp.cos(ang), jnp.sin(ang) * sign[None, :]]
    rows8 = lambda a: jnp.broadcast_to(a[:, None, :], (a.shape[0], 8, LANES))
    return tables[0], tables[1], rows8(tables[2]), rows8(tables[3])


def _stat_lane(c, k):
    return (1 - c) * HEAD_DIM + k


def _band_attention_kernel(seg_rows, tiles_per_subseq, q_ref, kp_ref, kc_ref, vp_ref, vc_ref,
                           o_ref, stat_ref, k_scr, v_scr, bias_scr, s_scr, p_scr):
    n_seg = q_ref.shape[0] // seg_rows
    bands_per_seg = seg_rows // BAND
    n_bands = n_seg * bands_per_seg
    step = pl.program_id(1)
    lane = lax.broadcasted_iota(jnp.int32, (1, LANES), 1)
    for c in range(KV_PER_GROUP):
        own = (lane // HEAD_DIM) == c
        for u in range(n_seg):
            base = u * (seg_rows + BAND)
            for dst, k_rows, v_rows in (
                    (slice(base, base + BAND), kp_ref[...], vp_ref[...]),
                    (slice(base + BAND, base + BAND + seg_rows),
                     kc_ref[u * seg_rows:(u + 1) * seg_rows, :],
                     vc_ref[u * seg_rows:(u + 1) * seg_rows, :])):
                k_scr[c, dst, :] = jnp.where(own, k_rows, jnp.zeros((), BF16))
                v_scr[c, dst, :] = jnp.where(own, v_rows, jnp.ones((), BF16))
    stat_ref[...] = jnp.ones(stat_ref.shape, F32)

    row = lax.broadcasted_iota(jnp.int32, (BAND, 2 * BAND), 0)
    col = lax.broadcasted_iota(jnp.int32, (BAND, 2 * BAND), 1)
    band_bias = jnp.where((col >= row) & (col <= row + BAND), 0.0, NEG_BIG)
    bias_scr[0] = band_bias
    bias_scr[1] = band_bias + jnp.where(col < BAND, NEG_BIG, 0.0)
    starts_subseq = True if n_seg > 1 else lax.rem(step, tiles_per_subseq) == 0

    def rows_of(g):
        return pl.ds(g * BAND if isinstance(g, int) else pl.multiple_of(g * BAND, BAND), BAND)

    def key_rows_of(g):
        staged = g + g // bands_per_seg
        start = staged * BAND if isinstance(g, int) else pl.multiple_of(staged * BAND, BAND)
        return pl.ds(start, 2 * BAND)

    def scores(g, slot):
        yield KV_PER_GROUP
        q = jnp.concatenate(
            [q_ref[rows_of(g), k * LANES:(k + 1) * LANES] for k in range(Q_PER_KV)], axis=0)
        for c in range(KV_PER_GROUP):
            s_scr[slot, c] = lax.dot_general(q, k_scr[c, key_rows_of(g), :],
                                             (((1,), (1,)), ((), ())),
                                             preferred_element_type=F32)
            yield

    def softmax(g, slot):
        yield KV_PER_GROUP * Q_PER_KV
        first = jnp.logical_and(starts_subseq, g % bands_per_seg == 0).astype(jnp.int32)
        for c in range(KV_PER_GROUP):
            for k in range(Q_PER_KV):
                part = pl.ds(k * BAND, BAND)
                s = s_scr[slot, c, part, :] + bias_scr[first]
                m = jnp.max(s, axis=1, keepdims=True)
                p_scr[slot, c, part, :] = jnp.exp(s - m).astype(BF16)
                stat_ref[rows_of(g), pl.ds(_stat_lane(c, k) + STAT_MAX_OFFSET, 1)] = m
                yield

    def weighted_values(g, slot):
        yield KV_PER_GROUP + Q_PER_KV + 1
        ov = []
        for c in range(KV_PER_GROUP):
            ov.append(jnp.dot(p_scr[slot, c], v_scr[c, key_rows_of(g), :],
                              preferred_element_type=F32))
            yield
        low = lane < HEAD_DIM
        sums = None
        for k in range(Q_PER_KV):
            part = slice(k * BAND, (k + 1) * BAND)
            o_ref[rows_of(g), k * LANES:(k + 1) * LANES] = (
                jnp.where(low, ov[0][part], ov[1][part]).astype(o_ref.dtype))
            sums_k = jnp.where(low, ov[1][part], ov[0][part])
            sums = sums_k if sums is None else jnp.where((lane % HEAD_DIM) == k, sums_k, sums)
            yield
        for c in range(KV_PER_GROUP):
            lanes = pl.ds(_stat_lane(c, 0), Q_PER_KV)
            stat_ref[rows_of(g), lanes] = sums[:, _stat_lane(c, 0):_stat_lane(c, 0) + Q_PER_KV]
        yield

    _interleave(scores(0, 0))
    _interleave(softmax(0, 0), scores(1, 1))

    def steady(t, carry):
        for g, slot in ((2 * t + 1, 1), (2 * t + 2, 0)):
            _interleave(weighted_values(g - 1, 1 - slot), softmax(g, slot),
                        scores(g + 1, 1 - slot))
        return carry

    assert n_bands % 2 == 0
    lax.fori_loop(0, (n_bands - 2) // 2, steady, 0)
    last = n_bands - 1
    _interleave(weighted_values(last - 1, (last - 1) % 2), softmax(last, last % 2))
    _interleave(weighted_values(last, last % 2))


def _band_attention(q, k, v, block_rows):
    b, dilation, sub, _ = q.shape
    seg_rows = min(sub, block_rows)
    total = dilation * sub
    assert total % block_rows == 0 and block_rows % seg_rows == 0 and sub % seg_rows == 0
    assert seg_rows % BAND == 0 and block_rows >= 2 * BAND
    n_seg = block_rows // seg_rows
    flat = lambda a: a.reshape(b, total, a.shape[-1])
    cur = lambda width: pl.BlockSpec((None, block_rows, width), lambda bi, i: (bi, i, 0))
    prev = pl.BlockSpec((None, BAND, GROUP_KV_WIDTH),
                        lambda bi, i: (bi, jnp.maximum(i * (block_rows // BAND) - 1, 0), 0))
    staged_rows = n_seg * (seg_rows + BAND)
    o, stat = pl.pallas_call(
        functools.partial(_band_attention_kernel, seg_rows, sub // seg_rows),
        grid=(b, total // block_rows),
        in_specs=[cur(GROUP_Q_WIDTH), prev, cur(GROUP_KV_WIDTH), prev, cur(GROUP_KV_WIDTH)],
        out_specs=[cur(GROUP_Q_WIDTH), cur(STAT_LANES)],
        out_shape=[jax.ShapeDtypeStruct((b, total, GROUP_Q_WIDTH), BF16),
                   jax.ShapeDtypeStruct((b, total, STAT_LANES), F32)],
        scratch_shapes=[pltpu.VMEM((KV_PER_GROUP, staged_rows, GROUP_KV_WIDTH), BF16),
                        pltpu.VMEM((KV_PER_GROUP, staged_rows, GROUP_KV_WIDTH), BF16),
                        pltpu.VMEM((2, BAND, 2 * BAND), F32),
                        pltpu.VMEM((2, KV_PER_GROUP, Q_PER_KV * BAND, 2 * BAND), F32),
                        pltpu.VMEM((2, KV_PER_GROUP, Q_PER_KV * BAND, 2 * BAND), BF16)],
        compiler_params=pltpu.CompilerParams(
            dimension_semantics=("arbitrary", "arbitrary"), vmem_limit_bytes=VMEM_LIMIT),
        name=f"band_attention_d{dilation}",
    )(flat(q), flat(k), flat(k), flat(v), flat(v))
    return (o.reshape(b, dilation, sub, GROUP_Q_WIDTH), stat.reshape(b, dilation, sub, STAT_LANES))


SAMPLE_Q_ROWS = 16
NEW_KEY_ROWS = 8
SAMPLE_SEQS_PER_STEP = 8


def _sample_attention_kernel(n_new, q_ref, kn_ref, vn_ref, kc0, kc1, kc2, vc0, vc1, vc2,
                             o_ref, lse_ref):
    n_seq = q_ref.shape[0]
    n_rows = q_ref.shape[2]
    contract_last = (((1,), (1,)), ((), ()))
    cache_refs = ((kc0, vc0), (kc1, vc1), (kc2, vc2))
    bias_c, bias_n = [], []
    for gi, (kc_ref, _) in enumerate(cache_refs):
        dmask = ATTN_DILATIONS[gi] - 1
        n_cache = kc_ref.shape[3]
        assert n_cache == ATTN_WINDOWS[gi]
        t_c = lax.broadcasted_iota(jnp.int32, (n_rows, n_cache), 0) & (n_new - 1)
        j_c = lax.broadcasted_iota(jnp.int32, (n_rows, n_cache), 1)
        valid_c = (j_c >= t_c) & (((j_c - t_c) & dmask) == 0)
        t_n = lax.broadcasted_iota(jnp.int32, (n_rows, NEW_KEY_ROWS), 0) & (n_new - 1)
        j_n = lax.broadcasted_iota(jnp.int32, (n_rows, NEW_KEY_ROWS), 1)
        valid_n = (j_n <= t_n) & (((t_n - j_n) & dmask) == 0)
        bias_c.append(jnp.where(valid_c, 0.0, NEG_BIG))
        bias_n.append(jnp.where(valid_n, 0.0, NEG_BIG))

    def one_sequence(b):
        yield len(cache_refs) * KV_PER_GROUP
        kn_all = kn_ref[b]
        vn_all = vn_ref[b]
        for gi, (kc_ref, vc_ref) in enumerate(cache_refs):
            for c in range(KV_PER_GROUP):
                kvh = gi * KV_PER_GROUP + c
                ns = slice(kvh * HEAD_DIM, (kvh + 1) * HEAD_DIM)
                q = q_ref[b, kvh]
                kt = kc_ref[b, c].astype(BF16)
                vt = vc_ref[b, c].astype(BF16)
                s_c = jnp.dot(q, kt, preferred_element_type=F32) + bias_c[gi]
                s_n = lax.dot_general(q, kn_all[:, ns].astype(BF16), contract_last,
                                      preferred_element_type=F32) + bias_n[gi]
                m = jnp.maximum(jnp.max(s_c, axis=1, keepdims=True),
                                jnp.max(s_n, axis=1, keepdims=True))
                p_c = jnp.exp(s_c - m)
                p_n = jnp.exp(s_n - m)
                l = jnp.sum(p_c, axis=1, keepdims=True) + jnp.sum(p_n, axis=1, keepdims=True)
                o = (lax.dot_general(p_c.astype(BF16), vt, contract_last,
                                     preferred_element_type=F32)
                     + jnp.dot(p_n.astype(BF16), vn_all[:, ns].astype(BF16),
                               preferred_element_type=F32))
                o_ref[b, kvh] = o / l
                lse_ref[b, kvh] = m + jnp.log(l)
                yield

    _interleave(*[one_sequence(b) for b in range(n_seq)])


def _sample_attention(q16, k_new, v_new, cache_kt, cache_vt, n_new):
    b = q16.shape[0]
    n_kv = q16.shape[1]
    assert cache_kt.shape[3] == KV_WINDOW and n_new & (n_new - 1) == 0 and n_new <= NEW_KEY_ROWS

    per_step = SAMPLE_SEQS_PER_STEP
    assert b % per_step == 0

    def cache_spec(gi):
        cols = ATTN_WINDOWS[gi]
        last = KV_WINDOW // cols - 1
        return pl.BlockSpec((per_step, KV_PER_GROUP, HEAD_DIM, cols),
                            lambda bi: (bi, gi, 0, last))

    whole = lambda a: pl.BlockSpec((per_step,) + a.shape[1:],
                                   lambda bi: (bi,) + (0,) * (a.ndim - 1))
    o_shape = jax.ShapeDtypeStruct((b, n_kv, SAMPLE_Q_ROWS, HEAD_DIM), F32)
    lse_shape = jax.ShapeDtypeStruct((b, n_kv, SAMPLE_Q_ROWS, 1), F32)
    return pl.pallas_call(
        functools.partial(_sample_attention_kernel, n_new),
        grid=(b // per_step,),
        in_specs=[whole(q16), whole(k_new), whole(v_new)]
        + [cache_spec(gi) for gi in range(N_GROUPS)] * 2,
        out_specs=[whole(o_shape), whole(lse_shape)],
        out_shape=[o_shape, lse_shape],
        compiler_params=pltpu.CompilerParams(
            dimension_semantics=("arbitrary",), vmem_limit_bytes=VMEM_LIMIT),
        name="sample_attention",
    )(q16, k_new, v_new, cache_kt, cache_kt, cache_kt, cache_vt, cache_vt, cache_vt)


def _attn_mix_steps(dilations, x_ref, o_refs, s_refs, g, wo_ref, o_scr, stat_scr, a_scr,
                    x1_ref, h_ref):
    tm = x_ref.shape[0]
    chunks_per_group = GROUP_Q_WIDTH // LANES
    yield 2 * N_GROUPS + 1 + 2 * len(_row_blocks(tm))
    for gi, (o_ref, s_ref) in enumerate(zip(o_refs, s_refs)):
        for r in range(dilations[gi]):
            rows = _subseq_rows(tm, r, dilations[gi])
            stat_scr[gi, rows, :] = s_ref[r]
            for c in range(chunks_per_group):
                o_scr[gi * chunks_per_group + c, rows, :] = (
                    o_ref[r, :, c * LANES:(c + 1) * LANES].astype(F32))
        yield
    sums = [stat_scr[gi] for gi in range(N_GROUPS)]
    maxes = [pltpu.roll(sm, STAT_LANES - STAT_MAX_OFFSET, 1) for sm in sums]
    top = jnp.maximum(jnp.maximum(maxes[0], maxes[1]), maxes[2])
    es = [jnp.exp(mx - top) for mx in maxes]
    den = sums[0] * es[0] + sums[1] * es[1] + sums[2] * es[2]
    slot_id = lax.broadcasted_iota(jnp.int32, (tm, GROUP_Q_WIDTH), 1) // HEAD_DIM
    for gi in range(N_GROUPS):
        scale = es[gi] / den
        wide = jnp.zeros((tm, GROUP_Q_WIDTH), F32)
        for slot in range(HEADS_PER_GROUP):
            stat_lane = _stat_lane(slot % KV_PER_GROUP, slot // KV_PER_GROUP)
            wide = jnp.where(slot_id == slot, scale[:, stat_lane:stat_lane + 1], wide)
        for c in range(chunks_per_group):
            ci = gi * chunks_per_group + c
            a_scr[:, ci * LANES:(ci + 1) * LANES] = (
                o_scr[ci] * wide[:, c * LANES:(c + 1) * LANES]).astype(BF16)
        yield
    mix = jnp.dot(a_scr[...], wo_ref[...], preferred_element_type=F32)
    yield
    tail = _chain(
        _residual_norm_steps(x1_ref, lambda rows: x_ref[rows, :], lambda rows: mix[rows], g[1:2]),
        _norm_cast_steps(h_ref, lambda rows: x1_ref[rows, :], g[2:3]))
    next(tail)
    yield from tail


def _layer1_scratch(tm, d, slots):
    lead = (slots,) if slots else ()
    return [pltpu.VMEM((Q_WIDTH // LANES, tm, LANES), F32),
            pltpu.VMEM((N_GROUPS, tm, STAT_LANES), F32),
            pltpu.VMEM((tm, Q_WIDTH), BF16),
            pltpu.VMEM(lead + (tm, d), F32),
            pltpu.VMEM(lead + (tm, d), BF16)]


def _layer1_single_kernel(dilations, x_ref, o0_ref, o1_ref, o2_ref, s0_ref, s1_ref, s2_ref, g_ref,
                          wo_ref, wup_ref, wdn_ref, y_ref, o_scr, stat_scr, a_scr, x1_scr, h_scr):
    g = g_ref[...]
    _interleave(_attn_mix_steps(dilations, x_ref, (o0_ref, o1_ref, o2_ref),
                                (s0_ref, s1_ref, s2_ref), g, wo_ref, o_scr, stat_scr, a_scr,
                                x1_scr, h_scr))
    y_ref[...] = _mlp_from_hidden(x1_scr[...], h_scr[...], g[3:4], wup_ref, wdn_ref)


def _layer1_first_kernel(dilations, x_ref, o0_ref, o1_ref, o2_ref, s0_ref, s1_ref, s2_ref, g_ref,
                         wo_ref, x1_ref, h_ref, o_scr, stat_scr, a_scr):
    _interleave(_attn_mix_steps(dilations, x_ref, (o0_ref, o1_ref, o2_ref),
                                (s0_ref, s1_ref, s2_ref), g_ref[...], wo_ref, o_scr, stat_scr,
                                a_scr, x1_ref, h_ref))


def _layer1_prompt_kernel(dilations, x_ref, o0_ref, o1_ref, o2_ref, s0_ref, s1_ref, s2_ref,
                          x1f_ref, hf_ref, g_ref, wo_ref, wup_ref, wdn_ref, y_ref,
                          o_scr, stat_scr, a_scr, x1_scr, h_scr, acc_scr):
    s = pl.program_id(0)
    n_tiles = pl.num_programs(0) - 1
    g = g_ref[...]

    @pl.when(s == 0)
    def _():
        x1_scr[0] = x1f_ref[...]
        h_scr[0] = hf_ref[...]
        x1_scr[2] = jnp.zeros(x1_scr.shape[1:], F32)
        acc_scr[1] = jnp.zeros(acc_scr.shape[1:], F32)

    def finish_previous():
        x1_slot, acc_slot = lax.rem(s + 2, 3), lax.rem(s + 1, 2)
        return _residual_norm_steps(y_ref, lambda rows: x1_scr[x1_slot, rows, :],
                                    lambda rows: acc_scr[acc_slot, rows, :], g[3:4])

    @pl.when(s < n_tiles)
    def _():
        cur = lax.rem(s, 2)

        def store_acc(acc):
            acc_scr[cur] = acc

        _interleave(
            finish_previous(),
            _mlp_matmul_steps(lambda: h_scr[cur], wup_ref, wdn_ref, store_acc),
            _attn_mix_steps(dilations, x_ref, (o0_ref, o1_ref, o2_ref), (s0_ref, s1_ref, s2_ref),
                            g, wo_ref, o_scr, stat_scr, a_scr,
                            x1_scr.at[lax.rem(s + 1, 3)], h_scr.at[1 - cur]))

    @pl.when(s == n_tiles)
    def _():
        _interleave(finish_previous())


def _layer1(x, os, stats, gains, w_o, w_up, w_down, layer, tm):
    b, t, d = x.shape
    dilations = tuple(o.shape[1] for o in os)
    tiles_per_seq = t // tm
    n_tiles = b * tiles_per_seq
    assert t % tm == 0 and all(tm % (16 * dil) == 0 for dil in dilations)
    x2 = x.reshape(b * t, d)
    weights = (gains, w_o, w_up, w_down)
    weight_specs = [_const_spec(gains.shape), _const_spec(w_o.shape),
                    _layer_spec(w_up, layer), _layer_spec(w_down, layer)]

    def tile_specs(tile):
        sub = lambda dil, width: pl.BlockSpec(
            (None, dil, tm // dil, width),
            lambda s: (tile(s) // tiles_per_seq, 0, tile(s) % tiles_per_seq, 0))
        return ([pl.BlockSpec((tm, d), lambda s: (tile(s), 0))]
                + [sub(dil, GROUP_Q_WIDTH) for dil in dilations]
                + [sub(dil, STAT_LANES) for dil in dilations])

    if n_tiles == 1:
        y = pl.pallas_call(
            functools.partial(_layer1_single_kernel, dilations),
            grid=(1,),
            in_specs=tile_specs(lambda s: 0) + weight_specs,
            out_specs=pl.BlockSpec((tm, d), lambda s: (0, 0)),
            out_shape=jax.ShapeDtypeStruct((b * t, d), F32),
            scratch_shapes=_layer1_scratch(tm, d, 0),
            compiler_params=pltpu.CompilerParams(vmem_limit_bytes=VMEM_LIMIT),
            name="layer1_single_tile",
        )(x2, *os, *stats, *weights)
        return y.reshape(b, t, d)

    x1_first, h_first = pl.pallas_call(
        functools.partial(_layer1_first_kernel, dilations),
        grid=(1,),
        in_specs=tile_specs(lambda s: 0) + [_const_spec(gains.shape), _const_spec(w_o.shape)],
        out_specs=[pl.BlockSpec((tm, d), lambda s: (0, 0))] * 2,
        out_shape=[jax.ShapeDtypeStruct((tm, d), F32), jax.ShapeDtypeStruct((tm, d), BF16)],
        scratch_shapes=_layer1_scratch(tm, d, 0)[:3],
        compiler_params=pltpu.CompilerParams(vmem_limit_bytes=VMEM_LIMIT),
        name="layer1_first_tile",
    )(x2, *os, *stats, gains, w_o)

    scratch = _layer1_scratch(tm, d, 0)[:3] + [
        pltpu.VMEM((3, tm, d), F32), pltpu.VMEM((2, tm, d), BF16), pltpu.VMEM((2, tm, d), F32)]
    y = pl.pallas_call(
        functools.partial(_layer1_prompt_kernel, dilations),
        grid=(n_tiles + 1,),
        in_specs=tile_specs(lambda s: jnp.minimum(s + 1, n_tiles - 1))
        + [_const_spec(x1_first.shape), _const_spec(h_first.shape)]
        + weight_specs,
        out_specs=pl.BlockSpec((tm, d), lambda s: (jnp.maximum(s - 1, 0), 0)),
        out_shape=jax.ShapeDtypeStruct((b * t, d), F32),
        scratch_shapes=scratch,
        compiler_params=pltpu.CompilerParams(
            dimension_semantics=("arbitrary",), vmem_limit_bytes=VMEM_LIMIT),
        name="layer1_prompt",
    )(x2, *os, *stats, x1_first, h_first, *weights)
    return y.reshape(b, t, d)


PROMPT_TILE = 512
LAYER1_TILE = 512
ATTN_BLOCK_ROWS = 2048


def kernel(x_prompt, x_sample, cache_pool, cache_k, cache_v, norm_gains, kv_norm_gain, w_pool,
           pool_scale, w_q, w_o, w_kv, w_up, w_down):
    depth = norm_gains.shape[0]
    assert depth == 2 and cache_pool.shape[0] == 1 and w_q.shape[0] == 1
    bp, tp, d = x_prompt.shape
    bs, ts, _ = x_sample.shape

    g0, g1 = norm_gains[0], norm_gains[1]
    g1_q = g1[0:1]
    g_kv = kv_norm_gain[None, :]
    wp = w_pool[0].astype(BF16)
    ps = pool_scale[0][None, :]
    wq = w_q[0].reshape(d, N_GROUPS, KV_PER_GROUP, Q_PER_KV, HEAD_DIM).transpose(0, 1, 3, 2, 4)
    wq = wq.reshape(d, Q_WIDTH).astype(BF16)
    wo = w_o[0].reshape(N_GROUPS, KV_PER_GROUP, Q_PER_KV, HEAD_DIM, d).transpose(0, 2, 1, 3, 4)
    wo = wo.reshape(Q_WIDTH, d).astype(BF16)
    wkv = w_kv.astype(BF16)
    wup, wdn = w_up.astype(BF16), w_down.astype(BF16)

    keep = min(KV_WINDOW, tp)
    xp1, utail = _layer0_prompt(x_prompt, g0, wp, ps, wup, wdn, 0, PROMPT_TILE)
    rope_p = _rope_tables(jnp.arange(PROMPT_TILE), jnp.arange(tp // PROMPT_TILE) * PROMPT_TILE)
    qkv_p = _qkv(xp1, g1_q, g_kv, wq, wkv, rope_p, PROMPT_TILE, keep, ATTN_DILATIONS, True)
    os_p, stats_p = [], []
    for gi, dil in enumerate(ATTN_DILATIONS):
        o, stat = _band_attention(qkv_p[gi], qkv_p[N_GROUPS + gi], qkv_p[2 * N_GROUPS + gi],
                                  ATTN_BLOCK_ROWS)
        os_p.append(o)
        stats_p.append(stat)
    y_prompt = _layer1(xp1, os_p, stats_p, g1, wo, wup, wdn, 1, LAYER1_TILE)
    pool_prompt = utail[:, POOL_HALO - POOL_BUF:][None]
    k_prompt = qkv_p[-2].reshape(bp, N_KV_HEADS, HEAD_DIM, keep).transpose(0, 3, 1, 2)
    v_prompt = qkv_p[-1].reshape(bp, N_KV_HEADS, HEAD_DIM, keep).transpose(0, 3, 1, 2)

    n_tok = ts * bs
    xs_tm = jnp.swapaxes(x_sample, 0, 1)
    buf_tm = jnp.swapaxes(cache_pool[0], 0, 1)
    xs1_tm, us_tm = _layer0_sample(xs_tm, buf_tm, PAST_LEN, g0, wp, ps, wup, wdn, 0)
    xs1 = xs1_tm.reshape(1, n_tok, d)
    rope_s = _rope_tables(jnp.arange(n_tok) // bs, jnp.full((1,), PAST_LEN))
    qkv_s = _qkv(xs1, g1_q, g_kv, wq, wkv, rope_s, n_tok, n_tok, (1,) * N_GROUPS, False)
    q_s = jnp.concatenate([q[0, 0] for q in qkv_s[:N_GROUPS]], axis=-1)
    q16 = q_s.reshape(ts, bs, N_GROUPS, Q_PER_KV, KV_PER_GROUP, HEAD_DIM)
    q16 = q16.transpose(1, 2, 4, 3, 0, 5).reshape(bs, N_KV_HEADS, Q_PER_KV, ts, HEAD_DIM)
    q16 = jnp.pad(q16, ((0, 0), (0, 0), (0, SAMPLE_Q_ROWS // ts - Q_PER_KV), (0, 0), (0, 0)))
    q16 = q16.reshape(bs, N_KV_HEADS, SAMPLE_Q_ROWS, HEAD_DIM)
    k_s = jnp.swapaxes(qkv_s[-2].reshape(ts, bs, KV_WIDTH), 0, 1)
    v_s = jnp.swapaxes(qkv_s[-1].reshape(ts, bs, KV_WIDTH), 0, 1)
    pad_new = ((0, 0), (0, NEW_KEY_ROWS - ts), (0, 0))
    o16, lse16 = _sample_attention(
        q16, jnp.pad(k_s, pad_new), jnp.pad(v_s, pad_new),
        cache_k.transpose(0, 2, 3, 1), cache_v.transpose(0, 2, 3, 1), ts)
    heads_padded = SAMPLE_Q_ROWS // ts
    o_s = o16.reshape(bs, N_GROUPS, KV_PER_GROUP, heads_padded, ts, HEAD_DIM)[:, :, :, :Q_PER_KV]
    o_s = o_s.transpose(4, 0, 1, 3, 2, 5).reshape(1, 1, n_tok, N_GROUPS, GROUP_Q_WIDTH).astype(BF16)
    lse_s = lse16.reshape(bs, N_GROUPS, KV_PER_GROUP, heads_padded, ts)[:, :, :, :Q_PER_KV]
    lse_s = lse_s.transpose(4, 0, 1, 2, 3).reshape(1, 1, n_tok, N_GROUPS, KV_PER_GROUP, Q_PER_KV)
    ones = lambda n: jnp.ones((1, 1, n_tok, N_GROUPS, n), F32)
    max_lane = [_stat_lane(c, 0) + STAT_MAX_OFFSET for c in range(KV_PER_GROUP)]
    assert max_lane[1] < max_lane[0]
    stat_s = jnp.concatenate(
        [ones(max_lane[1]), lse_s[..., 1, :],
         ones(max_lane[0] - max_lane[1] - Q_PER_KV), lse_s[..., 0, :],
         ones(STAT_LANES - max_lane[0] - Q_PER_KV)], axis=-1)
    ys_tm = _layer1(xs1, [o_s[:, :, :, gi] for gi in range(N_GROUPS)],
                    [stat_s[:, :, :, gi] for gi in range(N_GROUPS)], g1, wo, wup, wdn, 1, n_tok)
    y_sample = jnp.swapaxes(ys_tm.reshape(ts, bs, d), 0, 1)
    u_s = jnp.swapaxes(us_tm, 0, 1)
    pool_sample = jnp.concatenate([cache_pool[0], u_s], axis=1)[:, -POOL_BUF:][None]
    k_sample = k_s.reshape(bs, ts, N_KV_HEADS, HEAD_DIM)
    v_sample = v_s.reshape(bs, ts, N_KV_HEADS, HEAD_DIM)

    return (y_prompt, y_sample, pool_prompt, k_prompt, v_prompt, pool_sample, k_sample, v_sample)
```

```python
import functools

import jax
import jax.numpy as jnp
from jax import lax
from jax.experimental import pallas as pl
from jax.experimental.pallas import tpu as pltpu

F32 = jnp.float32
BF16 = jnp.bfloat16

EPS = 1e-6
ROPE_THETA = 10000.0
PAST_LEN = 16384
POOL_WINDOWS = (2, 4, 8, 16)
POOL_BUF = max(POOL_WINDOWS) - 1
POOL_HALO = 16
HEAD_DIM = 64
ATTN_WINDOWS = (128, 512, 2048)
ATTN_DILATIONS = (1, 4, 16)
N_GROUPS = len(ATTN_WINDOWS)
KV_PER_GROUP = 2
Q_PER_KV = 3
HEADS_PER_GROUP = KV_PER_GROUP * Q_PER_KV
N_KV_HEADS = N_GROUPS * KV_PER_GROUP
GROUP_Q_WIDTH = HEADS_PER_GROUP * HEAD_DIM
GROUP_KV_WIDTH = KV_PER_GROUP * HEAD_DIM
Q_WIDTH = N_GROUPS * GROUP_Q_WIDTH
KV_WIDTH = N_GROUPS * GROUP_KV_WIDTH
KV_WINDOW = max(ATTN_WINDOWS)
BAND = 128
NEG_BIG = -1e30

LANES = 128
STAT_LANES = LANES
STAT_MAX_OFFSET = 8
VMEM_LIMIT = 56 * 1024 * 1024
FF_CHUNK = 512
NORM_ROW_BLOCKS = 4
POOL_NORM_BLOCKS = 1
POOL_FF_CHUNK = 1024

for _w, _d in zip(ATTN_WINDOWS, ATTN_DILATIONS):
    assert _w // _d == BAND and _w % _d == 0
assert GROUP_KV_WIDTH == LANES


def _rms(x, g):
    return x * lax.rsqrt(jnp.mean(x * x, axis=-1, keepdims=True) + EPS) * g


def _mlp_residual(x1, g_in, g_out, wup_ref, wdn_ref):
    return _mlp_from_hidden(x1, _rms(x1, g_in).astype(BF16), g_out, wup_ref, wdn_ref)


def _mlp_from_hidden(x1, h, g_out, wup_ref, wdn_ref):
    acc = []
    _interleave(_mlp_matmul_steps(lambda: h, wup_ref, wdn_ref, acc.append))
    return x1 + _rms(acc[0], g_out)


def _mlp_matmul_steps(load_h, wup_ref, wdn_ref, emit_acc, chunk=FF_CHUNK):
    n_chunks = wup_ref.shape[1] // chunk
    yield n_chunks
    h = load_h()
    acc = None
    for c in range(n_chunks):
        cs = slice(c * chunk, (c + 1) * chunk)
        a = jnp.dot(h, wup_ref[:, cs], preferred_element_type=F32)
        a = jnp.square(jnp.maximum(a, 0.0)).astype(BF16)
        part = jnp.dot(a, wdn_ref[cs, :], preferred_element_type=F32)
        acc = part if acc is None else acc + part
        if c + 1 == n_chunks:
            emit_acc(acc)
        yield


def _row_blocks(n_rows, n_blocks=NORM_ROW_BLOCKS):
    size = n_rows // n_blocks if n_rows % (8 * n_blocks) == 0 else n_rows
    return [slice(lo, lo + size) for lo in range(0, n_rows, size)]


def _residual_norm_steps(dst_ref, load_x, load_v, g, n_blocks=NORM_ROW_BLOCKS):
    blocks = _row_blocks(dst_ref.shape[0], n_blocks)
    yield len(blocks)
    for rows in blocks:
        dst_ref[rows, :] = load_x(rows) + _rms(load_v(rows), g)
        yield


def _norm_cast_steps(dst_ref, load_v, g, n_blocks=NORM_ROW_BLOCKS):
    blocks = _row_blocks(dst_ref.shape[0], n_blocks)
    yield len(blocks)
    for rows in blocks:
        dst_ref[rows, :] = _rms(load_v(rows), g).astype(dst_ref.dtype)
        yield


def _chain(*generators):
    counts = [next(gen) for gen in generators]
    yield sum(counts)
    for gen in generators:
        yield from gen


def _interleave(*generators):
    totals = [next(gen) for gen in generators]
    done = [0] * len(generators)
    while any(d < t for d, t in zip(done, totals)):
        i = min((i for i in range(len(generators)) if done[i] < totals[i]),
                key=lambda i: (done[i] + 1) / totals[i])
        next(generators[i])
        done[i] += 1


def _const_spec(shape):
    zeros = (0,) * len(shape)
    return pl.BlockSpec(shape, lambda *_: zeros, pipeline_mode=pl.Buffered(1))


def _layer_spec(w, layer):
    index = (layer,) + (0,) * (w.ndim - 1)
    return pl.BlockSpec((None,) + w.shape[1:], lambda *_: index, pipeline_mode=pl.Buffered(1))


def _subseq_spec(dilation, rows, width):
    return pl.BlockSpec((None, dilation, rows // dilation, width), lambda bi, i: (bi, 0, i, 0))


def _subseq_rows(n_rows, r, dilation):
    n = n_rows // dilation
    return pl.ds(r, n, stride=dilation) if dilation > 1 else pl.ds(0, n)


def _pool_mixer_steps(x_ref, xh_ref, tile_in_seq, g, wp_ref, ps_ref, ext_ref, x1_ref, h_ref,
                      utail_ref=None):
    tm = x_ref.shape[0]
    pool_ch = wp_ref.shape[1]
    blocks = _row_blocks(tm, POOL_NORM_BLOCKS)
    yield 3 * len(blocks) + len(POOL_WINDOWS)
    uh = _rms(xh_ref[0:POOL_HALO, :], g[0:1]) * jnp.where(tile_in_seq > 0, 1.0, 0.0)
    ext_ref[0:POOL_HALO, :] = uh
    for rows in blocks:
        u = _rms(x_ref[rows, :], g[0:1])
        ext_ref[POOL_HALO + rows.start:POOL_HALO + rows.stop, :] = u
        if utail_ref is not None and rows.stop == tm:
            utail_ref[0] = u[u.shape[0] - POOL_HALO:, :]
        yield
    pos = tile_in_seq * tm + lax.broadcasted_iota(jnp.int32, (tm, 1), 0)
    parts = []
    for gi, w in enumerate(POOL_WINDOWS):
        cs = slice(gi * pool_ch, (gi + 1) * pool_ch)
        s = ext_ref[POOL_HALO:, cs]
        for j in range(1, w):
            s = s + ext_ref[POOL_HALO - j:POOL_HALO - j + tm, cs]
        cnt = jnp.minimum(w, pos + 1).astype(F32)
        pooled = s / cnt - ext_ref[POOL_HALO:, cs]
        parts.append(jnp.dot(pooled.astype(BF16), wp_ref[gi], preferred_element_type=F32))
        yield
    mix = jnp.concatenate(parts, axis=1) * ps_ref[...]
    tail = _chain(
        _residual_norm_steps(x1_ref, lambda rows: x_ref[rows, :], lambda rows: mix[rows], g[1:2],
                             POOL_NORM_BLOCKS),
        _norm_cast_steps(h_ref, lambda rows: x1_ref[rows, :], g[2:3], POOL_NORM_BLOCKS))
    next(tail)
    yield from tail


def _layer0_first_kernel(x_ref, g_ref, wp_ref, ps_ref, x1_ref, h_ref, ext_scr):
    _interleave(_pool_mixer_steps(x_ref, x_ref, 0, g_ref[...], wp_ref, ps_ref, ext_scr,
                                  x1_ref, h_ref))


def _layer0_prompt_kernel(tiles_per_seq, xa_ref, xha_ref, xb_ref, xhb_ref, x1f_ref, hf_ref, g_ref,
                          wp_ref, ps_ref, wup_ref, wdn_ref, y_ref, utail_ref,
                          x1_scr, h_scr, ext_scr):
    s = pl.program_id(0)
    n_tiles = 2 * pl.num_programs(0)
    tm = xa_ref.shape[0]
    g = g_ref[...]

    @pl.when(s == 0)
    def _():
        x1_scr[0] = x1f_ref[...]
        h_scr[0] = hf_ref[...]

    def prepare(x_ref, xh_ref, tile, slot, tail_ref):
        return _pool_mixer_steps(x_ref, xh_ref, lax.rem(tile, tiles_per_seq), g, wp_ref, ps_ref,
                                 ext_scr.at[slot], x1_scr.at[slot], h_scr.at[slot], tail_ref)

    def mlp(slot):
        acc = []
        return _chain(
            _mlp_matmul_steps(lambda: h_scr[slot], wup_ref, wdn_ref, acc.append, POOL_FF_CHUNK),
            _residual_norm_steps(y_ref.at[pl.ds(slot * tm, tm)],
                                 lambda rows: x1_scr[slot, rows, :],
                                 lambda rows: acc[0][rows], g[3:4], POOL_NORM_BLOCKS))

    _interleave(mlp(0), prepare(xa_ref, xha_ref, 2 * s + 1, 1, utail_ref))
    _interleave(mlp(1), prepare(xb_ref, xhb_ref, jnp.minimum(2 * s + 2, n_tiles - 1), 0, None))


def _layer0_prompt(x, gains, w_pool, pool_scale, w_up, w_down, layer, tm):
    b, t, d = x.shape
    tiles_per_seq = t // tm
    n_tiles = b * tiles_per_seq
    assert t % tm == 0 and tm % POOL_HALO == 0 and tiles_per_seq % 2 == 0
    halo_per_tile = tm // POOL_HALO
    x2 = x.reshape(b * t, d)

    x1_first, h_first = pl.pallas_call(
        _layer0_first_kernel,
        grid=(1,),
        in_specs=[pl.BlockSpec((tm, d), lambda i: (0, 0)), _const_spec(gains.shape),
                  _const_spec(w_pool.shape), _const_spec(pool_scale.shape)],
        out_specs=[pl.BlockSpec((tm, d), lambda i: (0, 0))] * 2,
        out_shape=[jax.ShapeDtypeStruct((tm, d), F32), jax.ShapeDtypeStruct((tm, d), BF16)],
        scratch_shapes=[pltpu.VMEM((tm + POOL_HALO, d), F32)],
        compiler_params=pltpu.CompilerParams(vmem_limit_bytes=VMEM_LIMIT),
        name="layer0_first_tile",
    )(x2, gains, w_pool, pool_scale)

    tile_a = lambda s: 2 * s + 1
    tile_b = lambda s: jnp.minimum(2 * s + 2, n_tiles - 1)
    tile_spec = lambda tile: pl.BlockSpec((tm, d), lambda s: (tile(s), 0))
    halo_spec = lambda tile: pl.BlockSpec((POOL_HALO, d),
                                          lambda s: (tile(s) * halo_per_tile - 1, 0))
    y, utail = pl.pallas_call(
        functools.partial(_layer0_prompt_kernel, tiles_per_seq),
        grid=(n_tiles // 2,),
        in_specs=[
            tile_spec(tile_a), halo_spec(tile_a), tile_spec(tile_b), halo_spec(tile_b),
            _const_spec(x1_first.shape), _const_spec(h_first.shape),
            _const_spec(gains.shape), _const_spec(w_pool.shape), _const_spec(pool_scale.shape),
            _layer_spec(w_up, layer), _layer_spec(w_down, layer),
        ],
        out_specs=[
            pl.BlockSpec((2 * tm, d), lambda s: (s, 0)),
            pl.BlockSpec((1, POOL_HALO, d), lambda s: (tile_a(s) // tiles_per_seq, 0, 0)),
        ],
        out_shape=[
            jax.ShapeDtypeStruct((b * t, d), F32),
            jax.ShapeDtypeStruct((b, POOL_HALO, d), F32),
        ],
        scratch_shapes=[pltpu.VMEM((2, tm, d), F32), pltpu.VMEM((2, tm, d), BF16),
                        pltpu.VMEM((2, tm + POOL_HALO, d), F32)],
        compiler_params=pltpu.CompilerParams(
            dimension_semantics=("arbitrary",), vmem_limit_bytes=VMEM_LIMIT),
        name="layer0_prompt",
    )(x2, x2, x2, x2, x1_first, h_first, gains, w_pool, pool_scale, w_up, w_down)
    return y.reshape(b, t, d), utail


def _layer0_sample_kernel(start_pos, x_ref, buf_ref, g_ref, wp_ref, ps_ref, wup_ref, wdn_ref,
                          y_ref, u_ref):
    n_t, n_b, _ = x_ref.shape
    pool_ch = wp_ref.shape[1]
    g = g_ref[...]
    xs = [x_ref[t] for t in range(n_t)]
    us = [_rms(xt, g[0:1]) for xt in xs]
    ext = [buf_ref[j] for j in range(POOL_BUF)] + us
    parts = []
    for gi, w in enumerate(POOL_WINDOWS):
        cs = slice(gi * pool_ch, (gi + 1) * pool_ch)
        rows = []
        for t in range(n_t):
            s = us[t][:, cs]
            for j in range(1, w):
                s = s + ext[POOL_BUF + t - j][:, cs]
            cnt = float(min(w, start_pos + t + 1))
            rows.append(s / cnt - us[t][:, cs])
        pooled = jnp.concatenate(rows, axis=0)
        parts.append(jnp.dot(pooled.astype(BF16), wp_ref[gi], preferred_element_type=F32))
    mix = jnp.concatenate(parts, axis=1) * ps_ref[...]
    x = jnp.concatenate(xs, axis=0)
    x1 = x + _rms(mix, g[1:2])
    y = _mlp_residual(x1, g[2:3], g[3:4], wup_ref, wdn_ref)
    for t in range(n_t):
        y_ref[t] = y[t * n_b:(t + 1) * n_b]
        u_ref[t] = us[t]


def _layer0_sample(x_tm, buf_tm, start_pos, gains, w_pool, pool_scale, w_up, w_down, layer):
    assert start_pos + 1 >= max(POOL_WINDOWS) and buf_tm.shape[0] == POOL_BUF
    return pl.pallas_call(
        functools.partial(_layer0_sample_kernel, start_pos),
        grid=(1,),
        in_specs=[_const_spec(a.shape) for a in (x_tm, buf_tm, gains, w_pool, pool_scale)]
        + [_layer_spec(w_up, layer), _layer_spec(w_down, layer)],
        out_specs=[pl.BlockSpec(x_tm.shape, lambda i: (0, 0, 0))] * 2,
        out_shape=[jax.ShapeDtypeStruct(x_tm.shape, F32), jax.ShapeDtypeStruct(x_tm.shape, F32)],
        compiler_params=pltpu.CompilerParams(vmem_limit_bytes=VMEM_LIMIT),
        name="layer0_sample",
    )(x_tm, buf_tm, gains, w_pool, pool_scale, w_up, w_down)


def _rope(x, cos, sin_signed):
    lane = lax.broadcasted_iota(jnp.int32, (x.shape[0], LANES), 1)
    first_half = (lane & (HEAD_DIM // 2)) == 0
    out = []
    for c in range(x.shape[1] // LANES):
        xc = x[:, c * LANES:(c + 1) * LANES]
        partner = jnp.where(first_half,
                            pltpu.roll(xc, LANES - HEAD_DIM // 2, 1),
                            pltpu.roll(xc, HEAD_DIM // 2, 1))
        out.append(xc * cos + partner * sin_signed)
    return out


def _qkv_kernel(dilations, first_kept_tile, transpose_kept,
                x_ref, gq_ref, gkv_ref, wq_ref, wkv_ref, cos_ref, sin_ref, cos_step_ref, sin_step_ref,
                q0_ref, q1_ref, q2_ref, k0_ref, k1_ref, k2_ref, v0_ref, v1_ref, v2_ref,
                kf_ref, vf_ref, stage_scr):
    i = pl.program_id(1)
    x = x_ref[0]
    cos_in, sin_in = cos_ref[...], sin_ref[...]
    cos_at, sin_at = cos_step_ref[0:1, :], sin_step_ref[0:1, :]
    cos = cos_in * cos_at - sin_in * sin_at
    sin_signed = sin_in * cos_at + cos_in * sin_at
    chunks_per_group = GROUP_Q_WIDTH // LANES

    def emit(out_ref, dilation, chunks):
        if dilation == 1:
            for c, chunk in enumerate(chunks):
                out_ref[0, :, c * LANES:(c + 1) * LANES] = chunk.astype(BF16)
            return
        for c, chunk in enumerate(chunks):
            stage_scr[c] = chunk
        for r in range(dilation):
            rows = _subseq_rows(stage_scr.shape[1], r, dilation)
            for c in range(len(chunks)):
                out_ref[r, :, c * LANES:(c + 1) * LANES] = stage_scr[c, rows, :].astype(BF16)

    u = _rms(x, gq_ref[...]).astype(BF16)
    q = jnp.dot(u, wq_ref[...], preferred_element_type=F32)
    scale = HEAD_DIM ** -0.5
    q_chunks = _rope(q, cos * scale, sin_signed * scale)
    for gi, q_ref in enumerate((q0_ref, q1_ref, q2_ref)):
        emit(q_ref, dilations[gi], q_chunks[gi * chunks_per_group:(gi + 1) * chunks_per_group])

    un = _rms(x, gkv_ref[...]).astype(BF16)
    kv = jnp.dot(un, wkv_ref[...], preferred_element_type=F32)
    k_chunks = _rope(kv[:, :KV_WIDTH], cos, sin_signed)
    v_chunks = [kv[:, KV_WIDTH + gi * LANES:KV_WIDTH + (gi + 1) * LANES] for gi in range(N_GROUPS)]
    for gi, (k_ref, v_ref) in enumerate(((k0_ref, v0_ref), (k1_ref, v1_ref), (k2_ref, v2_ref))):
        emit(k_ref, dilations[gi], [k_chunks[gi]])
        emit(v_ref, dilations[gi], [v_chunks[gi]])

    @pl.when(i >= first_kept_tile)
    def _():
        for gi in range(N_GROUPS):
            cs = slice(gi * LANES, (gi + 1) * LANES)
            if transpose_kept:
                kf_ref[0, cs, :] = k_chunks[gi].T
                vf_ref[0, cs, :] = v_chunks[gi].T
            else:
                kf_ref[0, :, cs] = k_chunks[gi]
                vf_ref[0, :, cs] = v_chunks[gi]


def _qkv(x, g_q, g_kv, w_q, w_kv, rope, tm, keep_rows, dilations, transpose_kept):
    b, t, d = x.shape
    cos, sin_signed, cos_step, sin_step = rope
    assert t % tm == 0 and keep_rows % tm == 0 and all(tm % (16 * dil) == 0 for dil in dilations)
    first_kept_tile = (t - keep_rows) // tm
    kept_block = lambda bi, i: jnp.maximum(i - first_kept_tile, 0)
    if transpose_kept:
        kept = pl.BlockSpec((1, KV_WIDTH, tm), lambda bi, i: (bi, 0, kept_block(bi, i)))
        kept_shape = jax.ShapeDtypeStruct((b, KV_WIDTH, keep_rows), F32)
    else:
        kept = pl.BlockSpec((1, tm, KV_WIDTH), lambda bi, i: (bi, kept_block(bi, i), 0))
        kept_shape = jax.ShapeDtypeStruct((b, keep_rows, KV_WIDTH), F32)
    sub_shape = lambda dil, width: jax.ShapeDtypeStruct((b, dil, t // dil, width), BF16)
    q_specs = [_subseq_spec(dil, tm, GROUP_Q_WIDTH) for dil in dilations]
    kv_specs = [_subseq_spec(dil, tm, GROUP_KV_WIDTH) for dil in dilations]
    q_shapes = [sub_shape(dil, GROUP_Q_WIDTH) for dil in dilations]
    kv_shapes = [sub_shape(dil, GROUP_KV_WIDTH) for dil in dilations]
    return pl.pallas_call(
        functools.partial(_qkv_kernel, tuple(dilations), first_kept_tile, transpose_kept),
        grid=(b, t // tm),
        in_specs=[
            pl.BlockSpec((1, tm, d), lambda bi, i: (bi, i, 0)),
            _const_spec(g_q.shape), _const_spec(g_kv.shape),
            _const_spec(w_q.shape), _const_spec(w_kv.shape),
            _const_spec(cos.shape), _const_spec(sin_signed.shape),
            pl.BlockSpec((None,) + cos_step.shape[1:], lambda bi, i: (i, 0, 0)),
            pl.BlockSpec((None,) + sin_step.shape[1:], lambda bi, i: (i, 0, 0)),
        ],
        out_specs=q_specs + kv_specs + kv_specs + [kept, kept],
        out_shape=q_shapes + kv_shapes + kv_shapes + [kept_shape, kept_shape],
        scratch_shapes=[pltpu.VMEM((GROUP_Q_WIDTH // LANES, tm, LANES), F32)],
        compiler_params=pltpu.CompilerParams(
            dimension_semantics=("arbitrary", "arbitrary"), vmem_limit_bytes=VMEM_LIMIT),
        name="qkv_rope",
    )(x, g_q, g_kv, w_q, w_kv, cos, sin_signed, cos_step, sin_step)


def _rope_tables(offsets, tile_starts):
    half = HEAD_DIM // 2
    inv = jnp.tile(ROPE_THETA ** (-jnp.arange(0, HEAD_DIM, 2, dtype=F32) / HEAD_DIM), LANES // half)
    sign = jnp.where((jnp.arange(LANES) // half) % 2 == 0, -1.0, 1.0).astype(F32)
    tables = []
    for positions in (offsets, tile_starts):
        ang = positions.astype(F32)[:, None] * inv[None, :]
        tables += [jnp.cos(ang), jnp.sin(ang) * sign[None, :]]
    rows8 = lambda a: jnp.broadcast_to(a[:, None, :], (a.shape[0], 8, LANES))
    return tables[0], tables[1], rows8(tables[2]), rows8(tables[3])


def _stat_lane(c, k):
    return (1 - c) * HEAD_DIM + k


def _band_attention_kernel(seg_rows, tiles_per_subseq, q_ref, kp_ref, kc_ref, vp_ref, vc_ref,
                           o_ref, stat_ref, k_scr, v_scr, bias_scr, s_scr, p_scr):
    n_seg = q_ref.shape[0] // seg_rows
    bands_per_seg = seg_rows // BAND
    n_bands = n_seg * bands_per_seg
    step = pl.program_id(1)
    lane = lax.broadcasted_iota(jnp.int32, (1, LANES), 1)
    for c in range(KV_PER_GROUP):
        own = (lane // HEAD_DIM) == c
        for u in range(n_seg):
            base = u * (seg_rows + BAND)
            for dst, k_rows, v_rows in (
                    (slice(base, base + BAND), kp_ref[...], vp_ref[...]),
                    (slice(base + BAND, base + BAND + seg_rows),
                     kc_ref[u * seg_rows:(u + 1) * seg_rows, :],
                     vc_ref[u * seg_rows:(u + 1) * seg_rows, :])):
                k_scr[c, dst, :] = jnp.where(own, k_rows, jnp.zeros((), BF16))
                v_scr[c, dst, :] = jnp.where(own, v_rows, jnp.ones((), BF16))
    stat_ref[...] = jnp.ones(stat_ref.shape, F32)

    row = lax.broadcasted_iota(jnp.int32, (BAND, 2 * BAND), 0)
    col = lax.broadcasted_iota(jnp.int32, (BAND, 2 * BAND), 1)
    band_bias = jnp.where((col >= row) & (col <= row + BAND), 0.0, NEG_BIG)
    bias_scr[0] = band_bias
    bias_scr[1] = band_bias + jnp.where(col < BAND, NEG_BIG, 0.0)
    starts_subseq = True if n_seg > 1 else lax.rem(step, tiles_per_subseq) == 0

    def rows_of(g):
        return pl.ds(g * BAND if isinstance(g, int) else pl.multiple_of(g * BAND, BAND), BAND)

    def key_rows_of(g):
        staged = g + g // bands_per_seg
        start = staged * BAND if isinstance(g, int) else pl.multiple_of(staged * BAND, BAND)
        return pl.ds(start, 2 * BAND)

    def scores(g, slot):
        yield KV_PER_GROUP
        q = jnp.concatenate(
            [q_ref[rows_of(g), k * LANES:(k + 1) * LANES] for k in range(Q_PER_KV)], axis=0)
        for c in range(KV_PER_GROUP):
            s_scr[slot, c] = lax.dot_general(q, k_scr[c, key_rows_of(g), :],
                                             (((1,), (1,)), ((), ())),
                                             preferred_element_type=F32)
            yield

    def softmax(g, slot):
        yield KV_PER_GROUP * Q_PER_KV
        first = jnp.logical_and(starts_subseq, g % bands_per_seg == 0).astype(jnp.int32)
        for c in range(KV_PER_GROUP):
            for k in range(Q_PER_KV):
                part = pl.ds(k * BAND, BAND)
                s = s_scr[slot, c, part, :] + bias_scr[first]
                m = jnp.max(s, axis=1, keepdims=True)
                p_scr[slot, c, part, :] = jnp.exp(s - m).astype(BF16)
                stat_ref[rows_of(g), pl.ds(_stat_lane(c, k) + STAT_MAX_OFFSET, 1)] = m
                yield

    def weighted_values(g, slot):
        yield KV_PER_GROUP + Q_PER_KV + 1
        ov = []
        for c in range(KV_PER_GROUP):
            ov.append(jnp.dot(p_scr[slot, c], v_scr[c, key_rows_of(g), :],
                              preferred_element_type=F32))
            yield
        low = lane < HEAD_DIM
        sums = None
        for k in range(Q_PER_KV):
            part = slice(k * BAND, (k + 1) * BAND)
            o_ref[rows_of(g), k * LANES:(k + 1) * LANES] = (
                jnp.where(low, ov[0][part], ov[1][part]).astype(o_ref.dtype))
            sums_k = jnp.where(low, ov[1][part], ov[0][part])
            sums = sums_k if sums is None else jnp.where((lane % HEAD_DIM) == k, sums_k, sums)
            yield
        for c in range(KV_PER_GROUP):
            lanes = pl.ds(_stat_lane(c, 0), Q_PER_KV)
            stat_ref[rows_of(g), lanes] = sums[:, _stat_lane(c, 0):_stat_lane(c, 0) + Q_PER_KV]
        yield

    _interleave(scores(0, 0))
    _interleave(softmax(0, 0), scores(1, 1))

    def steady(t, carry):
        for g, slot in ((2 * t + 1, 1), (2 * t + 2, 0)):
            _interleave(weighted_values(g - 1, 1 - slot), softmax(g, slot),
                        scores(g + 1, 1 - slot))
        return carry

    assert n_bands % 2 == 0
    lax.fori_loop(0, (n_bands - 2) // 2, steady, 0)
    last = n_bands - 1
    _interleave(weighted_values(last - 1, (last - 1) % 2), softmax(last, last % 2))
    _interleave(weighted_values(last, last % 2))


def _band_attention(q, k, v, block_rows):
    b, dilation, sub, _ = q.shape
    seg_rows = min(sub, block_rows)
    total = dilation * sub
    assert total % block_rows == 0 and block_rows % seg_rows == 0 and sub % seg_rows == 0
    assert seg_rows % BAND == 0 and block_rows >= 2 * BAND
    n_seg = block_rows // seg_rows
    flat = lambda a: a.reshape(b, total, a.shape[-1])
    cur = lambda width: pl.BlockSpec((None, block_rows, width), lambda bi, i: (bi, i, 0))
    prev = pl.BlockSpec((None, BAND, GROUP_KV_WIDTH),
                        lambda bi, i: (bi, jnp.maximum(i * (block_rows // BAND) - 1, 0), 0))
    staged_rows = n_seg * (seg_rows + BAND)
    o, stat = pl.pallas_call(
        functools.partial(_band_attention_kernel, seg_rows, sub // seg_rows),
        grid=(b, total // block_rows),
        in_specs=[cur(GROUP_Q_WIDTH), prev, cur(GROUP_KV_WIDTH), prev, cur(GROUP_KV_WIDTH)],
        out_specs=[cur(GROUP_Q_WIDTH), cur(STAT_LANES)],
        out_shape=[jax.ShapeDtypeStruct((b, total, GROUP_Q_WIDTH), BF16),
                   jax.ShapeDtypeStruct((b, total, STAT_LANES), F32)],
        scratch_shapes=[pltpu.VMEM((KV_PER_GROUP, staged_rows, GROUP_KV_WIDTH), BF16),
                        pltpu.VMEM((KV_PER_GROUP, staged_rows, GROUP_KV_WIDTH), BF16),
                        pltpu.VMEM((2, BAND, 2 * BAND), F32),
                        pltpu.VMEM((2, KV_PER_GROUP, Q_PER_KV * BAND, 2 * BAND), F32),
                        pltpu.VMEM((2, KV_PER_GROUP, Q_PER_KV * BAND, 2 * BAND), BF16)],
        compiler_params=pltpu.CompilerParams(
            dimension_semantics=("arbitrary", "arbitrary"), vmem_limit_bytes=VMEM_LIMIT),
        name=f"band_attention_d{dilation}",
    )(flat(q), flat(k), flat(k), flat(v), flat(v))
    return (o.reshape(b, dilation, sub, GROUP_Q_WIDTH), stat.reshape(b, dilation, sub, STAT_LANES))


SAMPLE_Q_ROWS = 16
NEW_KEY_ROWS = 8
SAMPLE_SEQS_PER_STEP = 4


def _sample_attention_kernel(n_new, q_ref, kn_ref, vn_ref, kc0, kc1, kc2, vc0, vc1, vc2,
                             o_ref, lse_ref):
    n_seq = q_ref.shape[0]
    n_rows = q_ref.shape[2]
    contract_last = (((1,), (1,)), ((), ()))
    cache_refs = ((kc0, vc0), (kc1, vc1), (kc2, vc2))
    bias_c, bias_n = [], []
    for gi, (kc_ref, _) in enumerate(cache_refs):
        dmask = ATTN_DILATIONS[gi] - 1
        n_cache = kc_ref.shape[3]
        assert n_cache == ATTN_WINDOWS[gi]
        t_c = lax.broadcasted_iota(jnp.int32, (n_rows, n_cache), 0) & (n_new - 1)
        j_c = lax.broadcasted_iota(jnp.int32, (n_rows, n_cache), 1)
        valid_c = (j_c >= t_c) & (((j_c - t_c) & dmask) == 0)
        t_n = lax.broadcasted_iota(jnp.int32, (n_rows, NEW_KEY_ROWS), 0) & (n_new - 1)
        j_n = lax.broadcasted_iota(jnp.int32, (n_rows, NEW_KEY_ROWS), 1)
        valid_n = (j_n <= t_n) & (((t_n - j_n) & dmask) == 0)
        bias_c.append(jnp.where(valid_c, 0.0, NEG_BIG))
        bias_n.append(jnp.where(valid_n, 0.0, NEG_BIG))

    def one_sequence(b):
        yield len(cache_refs) * KV_PER_GROUP
        kn_all = kn_ref[b]
        vn_all = vn_ref[b]
        for gi, (kc_ref, vc_ref) in enumerate(cache_refs):
            for c in range(KV_PER_GROUP):
                kvh = gi * KV_PER_GROUP + c
                ns = slice(kvh * HEAD_DIM, (kvh + 1) * HEAD_DIM)
                q = q_ref[b, kvh]
                kt = kc_ref[b, c].astype(BF16)
                vt = vc_ref[b, c].astype(BF16)
                s_c = jnp.dot(q, kt, preferred_element_type=F32) + bias_c[gi]
                s_n = lax.dot_general(q, kn_all[:, ns].astype(BF16), contract_last,
                                      preferred_element_type=F32) + bias_n[gi]
                m = jnp.maximum(jnp.max(s_c, axis=1, keepdims=True),
                                jnp.max(s_n, axis=1, keepdims=True))
                p_c = jnp.exp(s_c - m)
                p_n = jnp.exp(s_n - m)
                l = jnp.sum(p_c, axis=1, keepdims=True) + jnp.sum(p_n, axis=1, keepdims=True)
                o = (lax.dot_general(p_c.astype(BF16), vt, contract_last,
                                     preferred_element_type=F32)
                     + jnp.dot(p_n.astype(BF16), vn_all[:, ns].astype(BF16),
                               preferred_element_type=F32))
                o_ref[b, kvh] = o / l
                lse_ref[b, kvh] = m + jnp.log(l)
                yield

    _interleave(*[one_sequence(b) for b in range(n_seq)])


def _sample_attention(q16, k_new, v_new, cache_kt, cache_vt, n_new):
    b = q16.shape[0]
    n_kv = q16.shape[1]
    assert cache_kt.shape[3] == KV_WINDOW and n_new & (n_new - 1) == 0 and n_new <= NEW_KEY_ROWS

    per_step = SAMPLE_SEQS_PER_STEP
    assert b % per_step == 0

    def cache_spec(gi):
        cols = ATTN_WINDOWS[gi]
        last = KV_WINDOW // cols - 1
        return pl.BlockSpec((per_step, KV_PER_GROUP, HEAD_DIM, cols),
                            lambda bi: (bi, gi, 0, last))

    whole = lambda a: pl.BlockSpec((per_step,) + a.shape[1:],
                                   lambda bi: (bi,) + (0,) * (a.ndim - 1))
    o_shape = jax.ShapeDtypeStruct((b, n_kv, SAMPLE_Q_ROWS, HEAD_DIM), F32)
    lse_shape = jax.ShapeDtypeStruct((b, n_kv, SAMPLE_Q_ROWS, 1), F32)
    return pl.pallas_call(
        functools.partial(_sample_attention_kernel, n_new),
        grid=(b // per_step,),
        in_specs=[whole(q16), whole(k_new), whole(v_new)]
        + [cache_spec(gi) for gi in range(N_GROUPS)] * 2,
        out_specs=[whole(o_shape), whole(lse_shape)],
        out_shape=[o_shape, lse_shape],
        compiler_params=pltpu.CompilerParams(
            dimension_semantics=("arbitrary",), vmem_limit_bytes=VMEM_LIMIT),
        name="sample_attention",
    )(q16, k_new, v_new, cache_kt, cache_kt, cache_kt, cache_vt, cache_vt, cache_vt)


def _attn_mix_steps(dilations, x_ref, o_refs, s_refs, g, wo_ref, o_scr, stat_scr, a_scr,
                    x1_ref, h_ref):
    tm = x_ref.shape[0]
    chunks_per_group = GROUP_Q_WIDTH // LANES
    yield 2 * N_GROUPS + 1 + 2 * len(_row_blocks(tm))
    for gi, (o_ref, s_ref) in enumerate(zip(o_refs, s_refs)):
        for r in range(dilations[gi]):
            rows = _subseq_rows(tm, r, dilations[gi])
            stat_scr[gi, rows, :] = s_ref[r]
            for c in range(chunks_per_group):
                o_scr[gi * chunks_per_group + c, rows, :] = (
                    o_ref[r, :, c * LANES:(c + 1) * LANES].astype(F32))
        yield
    sums = [stat_scr[gi] for gi in range(N_GROUPS)]
    maxes = [pltpu.roll(sm, STAT_LANES - STAT_MAX_OFFSET, 1) for sm in sums]
    top = jnp.maximum(jnp.maximum(maxes[0], maxes[1]), maxes[2])
    es = [jnp.exp(mx - top) for mx in maxes]
    den = sums[0] * es[0] + sums[1] * es[1] + sums[2] * es[2]
    slot_id = lax.broadcasted_iota(jnp.int32, (tm, GROUP_Q_WIDTH), 1) // HEAD_DIM
    for gi in range(N_GROUPS):
        scale = es[gi] / den
        wide = jnp.zeros((tm, GROUP_Q_WIDTH), F32)
        for slot in range(HEADS_PER_GROUP):
            stat_lane = _stat_lane(slot % KV_PER_GROUP, slot // KV_PER_GROUP)
            wide = jnp.where(slot_id == slot, scale[:, stat_lane:stat_lane + 1], wide)
        for c in range(chunks_per_group):
            ci = gi * chunks_per_group + c
            a_scr[:, ci * LANES:(ci + 1) * LANES] = (
                o_scr[ci] * wide[:, c * LANES:(c + 1) * LANES]).astype(BF16)
        yield
    mix = jnp.dot(a_scr[...], wo_ref[...], preferred_element_type=F32)
    yield
    tail = _chain(
        _residual_norm_steps(x1_ref, lambda rows: x_ref[rows, :], lambda rows: mix[rows], g[1:2]),
        _norm_cast_steps(h_ref, lambda rows: x1_ref[rows, :], g[2:3]))
    next(tail)
    yield from tail


def _layer1_scratch(tm, d, slots):
    lead = (slots,) if slots else ()
    return [pltpu.VMEM((Q_WIDTH // LANES, tm, LANES), F32),
            pltpu.VMEM((N_GROUPS, tm, STAT_LANES), F32),
            pltpu.VMEM((tm, Q_WIDTH), BF16),
            pltpu.VMEM(lead + (tm, d), F32),
            pltpu.VMEM(lead + (tm, d), BF16)]


def _layer1_single_kernel(dilations, x_ref, o0_ref, o1_ref, o2_ref, s0_ref, s1_ref, s2_ref, g_ref,
                          wo_ref, wup_ref, wdn_ref, y_ref, o_scr, stat_scr, a_scr, x1_scr, h_scr):
    g = g_ref[...]
    _interleave(_attn_mix_steps(dilations, x_ref, (o0_ref, o1_ref, o2_ref),
                                (s0_ref, s1_ref, s2_ref), g, wo_ref, o_scr, stat_scr, a_scr,
                                x1_scr, h_scr))
    y_ref[...] = _mlp_from_hidden(x1_scr[...], h_scr[...], g[3:4], wup_ref, wdn_ref)


def _layer1_first_kernel(dilations, x_ref, o0_ref, o1_ref, o2_ref, s0_ref, s1_ref, s2_ref, g_ref,
                         wo_ref, x1_ref, h_ref, o_scr, stat_scr, a_scr):
    _interleave(_attn_mix_steps(dilations, x_ref, (o0_ref, o1_ref, o2_ref),
                                (s0_ref, s1_ref, s2_ref), g_ref[...], wo_ref, o_scr, stat_scr,
                                a_scr, x1_ref, h_ref))


def _layer1_prompt_kernel(dilations, x_ref, o0_ref, o1_ref, o2_ref, s0_ref, s1_ref, s2_ref,
                          x1f_ref, hf_ref, g_ref, wo_ref, wup_ref, wdn_ref, y_ref,
                          o_scr, stat_scr, a_scr, x1_scr, h_scr, acc_scr):
    s = pl.program_id(0)
    n_tiles = pl.num_programs(0) - 1
    g = g_ref[...]

    @pl.when(s == 0)
    def _():
        x1_scr[0] = x1f_ref[...]
        h_scr[0] = hf_ref[...]
        x1_scr[2] = jnp.zeros(x1_scr.shape[1:], F32)
        acc_scr[1] = jnp.zeros(acc_scr.shape[1:], F32)

    def finish_previous():
        x1_slot, acc_slot = lax.rem(s + 2, 3), lax.rem(s + 1, 2)
        return _residual_norm_steps(y_ref, lambda rows: x1_scr[x1_slot, rows, :],
                                    lambda rows: acc_scr[acc_slot, rows, :], g[3:4])

    @pl.when(s < n_tiles)
    def _():
        cur = lax.rem(s, 2)

        def store_acc(acc):
            acc_scr[cur] = acc

        _interleave(
            finish_previous(),
            _mlp_matmul_steps(lambda: h_scr[cur], wup_ref, wdn_ref, store_acc),
            _attn_mix_steps(dilations, x_ref, (o0_ref, o1_ref, o2_ref), (s0_ref, s1_ref, s2_ref),
                            g, wo_ref, o_scr, stat_scr, a_scr,
                            x1_scr.at[lax.rem(s + 1, 3)], h_scr.at[1 - cur]))

    @pl.when(s == n_tiles)
    def _():
        _interleave(finish_previous())


def _layer1(x, os, stats, gains, w_o, w_up, w_down, layer, tm):
    b, t, d = x.shape
    dilations = tuple(o.shape[1] for o in os)
    tiles_per_seq = t // tm
    n_tiles = b * tiles_per_seq
    assert t % tm == 0 and all(tm % (16 * dil) == 0 for dil in dilations)
    x2 = x.reshape(b * t, d)
    weights = (gains, w_o, w_up, w_down)
    weight_specs = [_const_spec(gains.shape), _const_spec(w_o.shape),
                    _layer_spec(w_up, layer), _layer_spec(w_down, layer)]

    def tile_specs(tile):
        sub = lambda dil, width: pl.BlockSpec(
            (None, dil, tm // dil, width),
            lambda s: (tile(s) // tiles_per_seq, 0, tile(s) % tiles_per_seq, 0))
        return ([pl.BlockSpec((tm, d), lambda s: (tile(s), 0))]
                + [sub(dil, GROUP_Q_WIDTH) for dil in dilations]
                + [sub(dil, STAT_LANES) for dil in dilations])

    if n_tiles == 1:
        y = pl.pallas_call(
            functools.partial(_layer1_single_kernel, dilations),
            grid=(1,),
            in_specs=tile_specs(lambda s: 0) + weight_specs,
            out_specs=pl.BlockSpec((tm, d), lambda s: (0, 0)),
            out_shape=jax.ShapeDtypeStruct((b * t, d), F32),
            scratch_shapes=_layer1_scratch(tm, d, 0),
            compiler_params=pltpu.CompilerParams(vmem_limit_bytes=VMEM_LIMIT),
            name="layer1_single_tile",
        )(x2, *os, *stats, *weights)
        return y.reshape(b, t, d)

    x1_first, h_first = pl.pallas_call(
        functools.partial(_layer1_first_kernel, dilations),
        grid=(1,),
        in_specs=tile_specs(lambda s: 0) + [_const_spec(gains.shape), _const_spec(w_o.shape)],
        out_specs=[pl.BlockSpec((tm, d), lambda s: (0, 0))] * 2,
        out_shape=[jax.ShapeDtypeStruct((tm, d), F32), jax.ShapeDtypeStruct((tm, d), BF16)],
        scratch_shapes=_layer1_scratch(tm, d, 0)[:3],
        compiler_params=pltpu.CompilerParams(vmem_limit_bytes=VMEM_LIMIT),
        name="layer1_first_tile",
    )(x2, *os, *stats, gains, w_o)

    scratch = _layer1_scratch(tm, d, 0)[:3] + [
        pltpu.VMEM((3, tm, d), F32), pltpu.VMEM((2, tm, d), BF16), pltpu.VMEM((2, tm, d), F32)]
    y = pl.pallas_call(
        functools.partial(_layer1_prompt_kernel, dilations),
        grid=(n_tiles + 1,),
        in_specs=tile_specs(lambda s: jnp.minimum(s + 1, n_tiles - 1))
        + [_const_spec(x1_first.shape), _const_spec(h_first.shape)]
        + weight_specs,
        out_specs=pl.BlockSpec((tm, d), lambda s: (jnp.maximum(s - 1, 0), 0)),
        out_shape=jax.ShapeDtypeStruct((b * t, d), F32),
        scratch_shapes=scratch,
        compiler_params=pltpu.CompilerParams(
            dimension_semantics=("arbitrary",), vmem_limit_bytes=VMEM_LIMIT),
        name="layer1_prompt",
    )(x2, *os, *stats, x1_first, h_first, *weights)
    return y.reshape(b, t, d)


PROMPT_TILE = 512
LAYER1_TILE = 512
QKV_TILE = 1024
ATTN_BLOCK_ROWS = 2048


def kernel(x_prompt, x_sample, cache_pool, cache_k, cache_v, norm_gains, kv_norm_gain, w_pool,
           pool_scale, w_q, w_o, w_kv, w_up, w_down):
    depth = norm_gains.shape[0]
    assert depth == 2 and cache_pool.shape[0] == 1 and w_q.shape[0] == 1
    bp, tp, d = x_prompt.shape
    bs, ts, _ = x_sample.shape

    g0, g1 = norm_gains[0], norm_gains[1]
    g1_q = g1[0:1]
    g_kv = kv_norm_gain[None, :]
    wp = w_pool[0].astype(BF16)
    ps = pool_scale[0][None, :]
    wq = w_q[0].reshape(d, N_GROUPS, KV_PER_GROUP, Q_PER_KV, HEAD_DIM).transpose(0, 1, 3, 2, 4)
    wq = wq.reshape(d, Q_WIDTH).astype(BF16)
    wo = w_o[0].reshape(N_GROUPS, KV_PER_GROUP, Q_PER_KV, HEAD_DIM, d).transpose(0, 2, 1, 3, 4)
    wo = wo.reshape(Q_WIDTH, d).astype(BF16)
    wkv = w_kv.astype(BF16)
    wup, wdn = w_up.astype(BF16), w_down.astype(BF16)

    keep = min(KV_WINDOW, tp)
    xp1, utail = _layer0_prompt(x_prompt, g0, wp, ps, wup, wdn, 0, PROMPT_TILE)
    rope_p = _rope_tables(jnp.arange(QKV_TILE), jnp.arange(tp // QKV_TILE) * QKV_TILE)
    qkv_p = _qkv(xp1, g1_q, g_kv, wq, wkv, rope_p, QKV_TILE, keep, ATTN_DILATIONS, True)
    os_p, stats_p = [], []
    for gi, dil in enumerate(ATTN_DILATIONS):
        o, stat = _band_attention(qkv_p[gi], qkv_p[N_GROUPS + gi], qkv_p[2 * N_GROUPS + gi],
                                  ATTN_BLOCK_ROWS)
        os_p.append(o)
        stats_p.append(stat)
    y_prompt = _layer1(xp1, os_p, stats_p, g1, wo, wup, wdn, 1, LAYER1_TILE)
    pool_prompt = utail[:, POOL_HALO - POOL_BUF:][None]
    k_prompt = qkv_p[-2].reshape(bp, N_KV_HEADS, HEAD_DIM, keep).transpose(0, 3, 1, 2)
    v_prompt = qkv_p[-1].reshape(bp, N_KV_HEADS, HEAD_DIM, keep).transpose(0, 3, 1, 2)

    n_tok = ts * bs
    xs_tm = jnp.swapaxes(x_sample, 0, 1)
    buf_tm = jnp.swapaxes(cache_pool[0], 0, 1)
    xs1_tm, us_tm = _layer0_sample(xs_tm, buf_tm, PAST_LEN, g0, wp, ps, wup, wdn, 0)
    xs1 = xs1_tm.reshape(1, n_tok, d)
    rope_s = _rope_tables(jnp.arange(n_tok) // bs, jnp.full((1,), PAST_LEN))
    qkv_s = _qkv(xs1, g1_q, g_kv, wq, wkv, rope_s, n_tok, n_tok, (1,) * N_GROUPS, False)
    q_s = jnp.concatenate([q[0, 0] for q in qkv_s[:N_GROUPS]], axis=-1)
    q16 = q_s.reshape(ts, bs, N_GROUPS, Q_PER_KV, KV_PER_GROUP, HEAD_DIM)
    q16 = q16.transpose(1, 2, 4, 3, 0, 5).reshape(bs, N_KV_HEADS, Q_PER_KV, ts, HEAD_DIM)
    q16 = jnp.pad(q16, ((0, 0), (0, 0), (0, SAMPLE_Q_ROWS // ts - Q_PER_KV), (0, 0), (0, 0)))
    q16 = q16.reshape(bs, N_KV_HEADS, SAMPLE_Q_ROWS, HEAD_DIM)
    k_s = jnp.swapaxes(qkv_s[-2].reshape(ts, bs, KV_WIDTH), 0, 1)
    v_s = jnp.swapaxes(qkv_s[-1].reshape(ts, bs, KV_WIDTH), 0, 1)
    pad_new = ((0, 0), (0, NEW_KEY_ROWS - ts), (0, 0))
    o16, lse16 = _sample_attention(
        q16, jnp.pad(k_s, pad_new), jnp.pad(v_s, pad_new),
        cache_k.transpose(0, 2, 3, 1), cache_v.transpose(0, 2, 3, 1), ts)
    heads_padded = SAMPLE_Q_ROWS // ts
    o_s = o16.reshape(bs, N_GROUPS, KV_PER_GROUP, heads_padded, ts, HEAD_DIM)[:, :, :, :Q_PER_KV]
    o_s = o_s.transpose(4, 0, 1, 3, 2, 5).reshape(1, 1, n_tok, N_GROUPS, GROUP_Q_WIDTH).astype(BF16)
    lse_s = lse16.reshape(bs, N_GROUPS, KV_PER_GROUP, heads_padded, ts)[:, :, :, :Q_PER_KV]
    lse_s = lse_s.transpose(4, 0, 1, 2, 3).reshape(1, 1, n_tok, N_GROUPS, KV_PER_GROUP, Q_PER_KV)
    ones = lambda n: jnp.ones((1, 1, n_tok, N_GROUPS, n), F32)
    max_lane = [_stat_lane(c, 0) + STAT_MAX_OFFSET for c in range(KV_PER_GROUP)]
    assert max_lane[1] < max_lane[0]
    stat_s = jnp.concatenate(
        [ones(max_lane[1]), lse_s[..., 1, :],
         ones(max_lane[0] - max_lane[1] - Q_PER_KV), lse_s[..., 0, :],
         ones(STAT_LANES - max_lane[0] - Q_PER_KV)], axis=-1)
    ys_tm = _layer1(xs1, [o_s[:, :, :, gi] for gi in range(N_GROUPS)],
                    [stat_s[:, :, :, gi] for gi in range(N_GROUPS)], g1, wo, wup, wdn, 1, n_tok)
    y_sample = jnp.swapaxes(ys_tm.reshape(ts, bs, d), 0, 1)
    u_s = jnp.swapaxes(us_tm, 0, 1)
    pool_sample = jnp.concatenate([cache_pool[0], u_s], axis=1)[:, -POOL_BUF:][None]
    k_sample = k_s.reshape(bs, ts, N_KV_HEADS, HEAD_DIM)
    v_sample = v_s.reshape(bs, ts, N_KV_HEADS, HEAD_DIM)

    return (y_prompt, y_sample, pool_prompt, k_prompt, v_prompt, pool_sample, k_sample, v_sample)
```

```python
import functools

import jax
import jax.numpy as jnp
from jax import lax
from jax.experimental import pallas as pl
from jax.experimental.pallas import tpu as pltpu

F32 = jnp.float32
BF16 = jnp.bfloat16

EPS = 1e-6
ROPE_THETA = 10000.0
PAST_LEN = 16384
POOL_WINDOWS = (2, 4, 8, 16)
POOL_BUF = max(POOL_WINDOWS) - 1
POOL_HALO = 16
HEAD_DIM = 64
ATTN_WINDOWS = (128, 512, 2048)
ATTN_DILATIONS = (1, 4, 16)
N_GROUPS = len(ATTN_WINDOWS)
KV_PER_GROUP = 2
Q_PER_KV = 3
HEADS_PER_GROUP = KV_PER_GROUP * Q_PER_KV
N_KV_HEADS = N_GROUPS * KV_PER_GROUP
GROUP_Q_WIDTH = HEADS_PER_GROUP * HEAD_DIM
GROUP_KV_WIDTH = KV_PER_GROUP * HEAD_DIM
Q_WIDTH = N_GROUPS * GROUP_Q_WIDTH
KV_WIDTH = N_GROUPS * GROUP_KV_WIDTH
KV_WINDOW = max(ATTN_WINDOWS)
BAND = 128
NEG_BIG = -1e30

LANES = 128
STAT_LANES = LANES
STAT_MAX_OFFSET = 8
VMEM_LIMIT = 56 * 1024 * 1024
FF_CHUNK = 512
NORM_ROW_BLOCKS = 4
POOL_NORM_BLOCKS = 1
POOL_FF_CHUNK = 1024
SAMPLE_FF_CHUNK = 1024

for _w, _d in zip(ATTN_WINDOWS, ATTN_DILATIONS):
    assert _w // _d == BAND and _w % _d == 0
assert GROUP_KV_WIDTH == LANES


def _rms(x, g):
    return x * lax.rsqrt(jnp.mean(x * x, axis=-1, keepdims=True) + EPS) * g


def _mlp_matmul_steps(load_h, wup_ref, wdn_ref, emit_acc, chunk=FF_CHUNK):
    n_chunks = wup_ref.shape[1] // chunk
    yield n_chunks
    h = load_h()
    acc = None
    for c in range(n_chunks):
        cs = slice(c * chunk, (c + 1) * chunk)
        a = jnp.dot(h, wup_ref[:, cs], preferred_element_type=F32)
        a = jnp.square(jnp.maximum(a, 0.0)).astype(BF16)
        part = jnp.dot(a, wdn_ref[cs, :], preferred_element_type=F32)
        acc = part if acc is None else acc + part
        if c + 1 == n_chunks:
            emit_acc(acc)
        yield


def _row_blocks(n_rows, n_blocks=NORM_ROW_BLOCKS):
    size = n_rows // n_blocks if n_rows % (8 * n_blocks) == 0 else n_rows
    return [slice(lo, lo + size) for lo in range(0, n_rows, size)]


def _residual_norm_steps(dst_ref, load_x, load_v, g, n_blocks=NORM_ROW_BLOCKS):
    blocks = _row_blocks(dst_ref.shape[0], n_blocks)
    yield len(blocks)
    for rows in blocks:
        dst_ref[rows, :] = load_x(rows) + _rms(load_v(rows), g)
        yield


def _norm_cast_steps(dst_ref, load_v, g, n_blocks=NORM_ROW_BLOCKS):
    blocks = _row_blocks(dst_ref.shape[0], n_blocks)
    yield len(blocks)
    for rows in blocks:
        dst_ref[rows, :] = _rms(load_v(rows), g).astype(dst_ref.dtype)
        yield


def _chain(*generators):
    counts = [next(gen) for gen in generators]
    yield sum(counts)
    for gen in generators:
        yield from gen


def _interleave(*generators):
    totals = [next(gen) for gen in generators]
    done = [0] * len(generators)
    while any(d < t for d, t in zip(done, totals)):
        i = min((i for i in range(len(generators)) if done[i] < totals[i]),
                key=lambda i: (done[i] + 1) / totals[i])
        next(generators[i])
        done[i] += 1


def _const_spec(shape):
    zeros = (0,) * len(shape)
    return pl.BlockSpec(shape, lambda *_: zeros, pipeline_mode=pl.Buffered(1))


def _layer_spec(w, layer):
    index = (layer,) + (0,) * (w.ndim - 1)
    return pl.BlockSpec((None,) + w.shape[1:], lambda *_: index, pipeline_mode=pl.Buffered(1))


def _subseq_spec(dilation, rows, width):
    return pl.BlockSpec((None, dilation, rows // dilation, width), lambda bi, i: (bi, 0, i, 0))


def _subseq_rows(n_rows, r, dilation):
    n = n_rows // dilation
    return pl.ds(r, n, stride=dilation) if dilation > 1 else pl.ds(0, n)


def _pool_mixer_steps(x_ref, xh_ref, tile_in_seq, g, wp_ref, ps_ref, ext_ref, x1_ref, h_ref,
                      utail_ref=None):
    tm = x_ref.shape[0]
    pool_ch = wp_ref.shape[1]
    blocks = _row_blocks(tm, POOL_NORM_BLOCKS)
    yield 3 * len(blocks) + len(POOL_WINDOWS)
    uh = _rms(xh_ref[0:POOL_HALO, :], g[0:1]) * jnp.where(tile_in_seq > 0, 1.0, 0.0)
    ext_ref[0:POOL_HALO, :] = uh
    for rows in blocks:
        u = _rms(x_ref[rows, :], g[0:1])
        ext_ref[POOL_HALO + rows.start:POOL_HALO + rows.stop, :] = u
        if utail_ref is not None and rows.stop == tm:
            utail_ref[0] = u[u.shape[0] - POOL_HALO:, :]
        yield
    pos = tile_in_seq * tm + lax.broadcasted_iota(jnp.int32, (tm, 1), 0)
    parts = []
    for gi, w in enumerate(POOL_WINDOWS):
        cs = slice(gi * pool_ch, (gi + 1) * pool_ch)
        s = ext_ref[POOL_HALO:, cs]
        for j in range(1, w):
            s = s + ext_ref[POOL_HALO - j:POOL_HALO - j + tm, cs]
        cnt = jnp.minimum(w, pos + 1).astype(F32)
        pooled = s / cnt - ext_ref[POOL_HALO:, cs]
        parts.append(jnp.dot(pooled.astype(BF16), wp_ref[gi], preferred_element_type=F32))
        yield
    mix = jnp.concatenate(parts, axis=1) * ps_ref[...]
    tail = _chain(
        _residual_norm_steps(x1_ref, lambda rows: x_ref[rows, :], lambda rows: mix[rows], g[1:2],
                             POOL_NORM_BLOCKS),
        _norm_cast_steps(h_ref, lambda rows: x1_ref[rows, :], g[2:3], POOL_NORM_BLOCKS))
    next(tail)
    yield from tail


def _layer0_first_kernel(x_ref, g_ref, wp_ref, ps_ref, x1_ref, h_ref, ext_scr):
    _interleave(_pool_mixer_steps(x_ref, x_ref, 0, g_ref[...], wp_ref, ps_ref, ext_scr,
                                  x1_ref, h_ref))


def _layer0_prompt_kernel(tiles_per_seq, xa_ref, xha_ref, xb_ref, xhb_ref, x1f_ref, hf_ref, g_ref,
                          wp_ref, ps_ref, wup_ref, wdn_ref, y_ref, utail_ref,
                          x1_scr, h_scr, ext_scr):
    s = pl.program_id(0)
    n_tiles = 2 * pl.num_programs(0)
    tm = xa_ref.shape[0]
    g = g_ref[...]

    @pl.when(s == 0)
    def _():
        x1_scr[0] = x1f_ref[...]
        h_scr[0] = hf_ref[...]

    def prepare(x_ref, xh_ref, tile, slot, tail_ref):
        return _pool_mixer_steps(x_ref, xh_ref, lax.rem(tile, tiles_per_seq), g, wp_ref, ps_ref,
                                 ext_scr.at[slot], x1_scr.at[slot], h_scr.at[slot], tail_ref)

    def mlp(slot):
        acc = []
        return _chain(
            _mlp_matmul_steps(lambda: h_scr[slot], wup_ref, wdn_ref, acc.append, POOL_FF_CHUNK),
            _residual_norm_steps(y_ref.at[pl.ds(slot * tm, tm)],
                                 lambda rows: x1_scr[slot, rows, :],
                                 lambda rows: acc[0][rows], g[3:4], POOL_NORM_BLOCKS))

    _interleave(mlp(0), prepare(xa_ref, xha_ref, 2 * s + 1, 1, utail_ref))
    _interleave(mlp(1), prepare(xb_ref, xhb_ref, jnp.minimum(2 * s + 2, n_tiles - 1), 0, None))


def _layer0_prompt(x, gains, w_pool, pool_scale, w_up, w_down, layer, tm):
    b, t, d = x.shape
    tiles_per_seq = t // tm
    n_tiles = b * tiles_per_seq
    assert t % tm == 0 and tm % POOL_HALO == 0 and tiles_per_seq % 2 == 0
    halo_per_tile = tm // POOL_HALO
    x2 = x.reshape(b * t, d)

    x1_first, h_first = pl.pallas_call(
        _layer0_first_kernel,
        grid=(1,),
        in_specs=[pl.BlockSpec((tm, d), lambda i: (0, 0)), _const_spec(gains.shape),
                  _const_spec(w_pool.shape), _const_spec(pool_scale.shape)],
        out_specs=[pl.BlockSpec((tm, d), lambda i: (0, 0))] * 2,
        out_shape=[jax.ShapeDtypeStruct((tm, d), F32), jax.ShapeDtypeStruct((tm, d), BF16)],
        scratch_shapes=[pltpu.VMEM((tm + POOL_HALO, d), F32)],
        compiler_params=pltpu.CompilerParams(vmem_limit_bytes=VMEM_LIMIT),
        name="layer0_first_tile",
    )(x2, gains, w_pool, pool_scale)

    tile_a = lambda s: 2 * s + 1
    tile_b = lambda s: jnp.minimum(2 * s + 2, n_tiles - 1)
    tile_spec = lambda tile: pl.BlockSpec((tm, d), lambda s: (tile(s), 0))
    halo_spec = lambda tile: pl.BlockSpec((POOL_HALO, d),
                                          lambda s: (tile(s) * halo_per_tile - 1, 0))
    y, utail = pl.pallas_call(
        functools.partial(_layer0_prompt_kernel, tiles_per_seq),
        grid=(n_tiles // 2,),
        in_specs=[
            tile_spec(tile_a), halo_spec(tile_a), tile_spec(tile_b), halo_spec(tile_b),
            _const_spec(x1_first.shape), _const_spec(h_first.shape),
            _const_spec(gains.shape), _const_spec(w_pool.shape), _const_spec(pool_scale.shape),
            _layer_spec(w_up, layer), _layer_spec(w_down, layer),
        ],
        out_specs=[
            pl.BlockSpec((2 * tm, d), lambda s: (s, 0)),
            pl.BlockSpec((1, POOL_HALO, d), lambda s: (tile_a(s) // tiles_per_seq, 0, 0)),
        ],
        out_shape=[
            jax.ShapeDtypeStruct((b * t, d), F32),
            jax.ShapeDtypeStruct((b, POOL_HALO, d), F32),
        ],
        scratch_shapes=[pltpu.VMEM((2, tm, d), F32), pltpu.VMEM((2, tm, d), BF16),
                        pltpu.VMEM((2, tm + POOL_HALO, d), F32)],
        compiler_params=pltpu.CompilerParams(
            dimension_semantics=("arbitrary",), vmem_limit_bytes=VMEM_LIMIT),
        name="layer0_prompt",
    )(x2, x2, x2, x2, x1_first, h_first, gains, w_pool, pool_scale, w_up, w_down)
    return y.reshape(b, t, d), utail


def _streamed_mlp_step(g, wup_ref, wdn_ref, x1_scr, h_scr, acc_scr, emit):
    a = jnp.dot(h_scr[...], wup_ref[...], preferred_element_type=F32)
    a = jnp.square(jnp.maximum(a, 0.0)).astype(BF16)
    acc_scr[...] += jnp.dot(a, wdn_ref[...], preferred_element_type=F32)

    @pl.when(pl.program_id(0) == pl.num_programs(0) - 1)
    def _():
        emit(x1_scr[...] + _rms(acc_scr[...], g[3:4]))


def _streamed_mlp_specs(w_up, w_down, layer):
    assert w_up.shape[2] % SAMPLE_FF_CHUNK == 0
    return (w_up.shape[2] // SAMPLE_FF_CHUNK,
            [pl.BlockSpec((None, w_up.shape[1], SAMPLE_FF_CHUNK), lambda c: (layer, 0, c)),
             pl.BlockSpec((None, SAMPLE_FF_CHUNK, w_down.shape[2]), lambda c: (layer, c, 0))])


def _layer0_sample_kernel(start_pos, x_ref, buf_ref, g_ref, wp_ref, ps_ref, wup_ref, wdn_ref,
                          y_ref, u_ref, x1_scr, h_scr, acc_scr):
    n_t, n_b, _ = x_ref.shape
    g = g_ref[...]

    @pl.when(pl.program_id(0) == 0)
    def _():
        _pool_mixer_time_major(start_pos, x_ref, buf_ref, g, wp_ref, ps_ref, u_ref, x1_scr, h_scr)
        acc_scr[...] = jnp.zeros(acc_scr.shape, F32)

    def emit(y):
        for t in range(n_t):
            y_ref[t] = y[t * n_b:(t + 1) * n_b]

    _streamed_mlp_step(g, wup_ref, wdn_ref, x1_scr, h_scr, acc_scr, emit)


def _pool_mixer_time_major(start_pos, x_ref, buf_ref, g, wp_ref, ps_ref, u_ref, x1_ref, h_ref):
    n_t = x_ref.shape[0]
    pool_ch = wp_ref.shape[1]
    xs = [x_ref[t] for t in range(n_t)]
    us = [_rms(xt, g[0:1]) for xt in xs]
    ext = [buf_ref[j] for j in range(POOL_BUF)] + us
    parts = []
    for gi, w in enumerate(POOL_WINDOWS):
        cs = slice(gi * pool_ch, (gi + 1) * pool_ch)
        rows = []
        for t in range(n_t):
            s = us[t][:, cs]
            for j in range(1, w):
                s = s + ext[POOL_BUF + t - j][:, cs]
            cnt = float(min(w, start_pos + t + 1))
            rows.append(s / cnt - us[t][:, cs])
        pooled = jnp.concatenate(rows, axis=0)
        parts.append(jnp.dot(pooled.astype(BF16), wp_ref[gi], preferred_element_type=F32))
    mix = jnp.concatenate(parts, axis=1) * ps_ref[...]
    x = jnp.concatenate(xs, axis=0)
    x1 = x + _rms(mix, g[1:2])
    x1_ref[...] = x1
    h_ref[...] = _rms(x1, g[2:3]).astype(BF16)
    for t in range(n_t):
        u_ref[t] = us[t]


def _layer0_sample(x_tm, buf_tm, start_pos, gains, w_pool, pool_scale, w_up, w_down, layer):
    assert start_pos + 1 >= max(POOL_WINDOWS) and buf_tm.shape[0] == POOL_BUF
    n_tok, d = x_tm.shape[0] * x_tm.shape[1], x_tm.shape[2]
    n_chunks, weight_specs = _streamed_mlp_specs(w_up, w_down, layer)
    return pl.pallas_call(
        functools.partial(_layer0_sample_kernel, start_pos),
        grid=(n_chunks,),
        in_specs=[_const_spec(a.shape) for a in (x_tm, buf_tm, gains, w_pool, pool_scale)]
        + weight_specs,
        out_specs=[pl.BlockSpec(x_tm.shape, lambda c: (0, 0, 0))] * 2,
        out_shape=[jax.ShapeDtypeStruct(x_tm.shape, F32), jax.ShapeDtypeStruct(x_tm.shape, F32)],
        scratch_shapes=[pltpu.VMEM((n_tok, d), F32), pltpu.VMEM((n_tok, d), BF16),
                        pltpu.VMEM((n_tok, d), F32)],
        compiler_params=pltpu.CompilerParams(
            dimension_semantics=("arbitrary",), vmem_limit_bytes=VMEM_LIMIT),
        name="layer0_sample",
    )(x_tm, buf_tm, gains, w_pool, pool_scale, w_up, w_down)


def _rope(x, cos, sin_signed):
    lane = lax.broadcasted_iota(jnp.int32, (x.shape[0], LANES), 1)
    first_half = (lane & (HEAD_DIM // 2)) == 0
    out = []
    for c in range(x.shape[1] // LANES):
        xc = x[:, c * LANES:(c + 1) * LANES]
        partner = jnp.where(first_half,
                            pltpu.roll(xc, LANES - HEAD_DIM // 2, 1),
                            pltpu.roll(xc, HEAD_DIM // 2, 1))
        out.append(xc * cos + partner * sin_signed)
    return out


def _qkv_kernel(dilations, first_kept_tile, transpose_kept,
                x_ref, gq_ref, gkv_ref, wq_ref, wkv_ref, cos_ref, sin_ref, cos_step_ref, sin_step_ref,
                q0_ref, q1_ref, q2_ref, k0_ref, k1_ref, k2_ref, v0_ref, v1_ref, v2_ref,
                kf_ref, vf_ref, stage_scr):
    i = pl.program_id(1)
    x = x_ref[0]
    cos_in, sin_in = cos_ref[...], sin_ref[...]
    cos_at, sin_at = cos_step_ref[0:1, :], sin_step_ref[0:1, :]
    cos = cos_in * cos_at - sin_in * sin_at
    sin_signed = sin_in * cos_at + cos_in * sin_at
    chunks_per_group = GROUP_Q_WIDTH // LANES

    def emit(out_ref, dilation, chunks):
        if dilation == 1:
            for c, chunk in enumerate(chunks):
                out_ref[0, :, c * LANES:(c + 1) * LANES] = chunk.astype(BF16)
            return
        for c, chunk in enumerate(chunks):
            stage_scr[c] = chunk
        for r in range(dilation):
            rows = _subseq_rows(stage_scr.shape[1], r, dilation)
            for c in range(len(chunks)):
                out_ref[r, :, c * LANES:(c + 1) * LANES] = stage_scr[c, rows, :].astype(BF16)

    u = _rms(x, gq_ref[...]).astype(BF16)
    q = jnp.dot(u, wq_ref[...], preferred_element_type=F32)
    scale = HEAD_DIM ** -0.5
    q_chunks = _rope(q, cos * scale, sin_signed * scale)
    for gi, q_ref in enumerate((q0_ref, q1_ref, q2_ref)):
        emit(q_ref, dilations[gi], q_chunks[gi * chunks_per_group:(gi + 1) * chunks_per_group])

    un = _rms(x, gkv_ref[...]).astype(BF16)
    kv = jnp.dot(un, wkv_ref[...], preferred_element_type=F32)
    k_chunks = _rope(kv[:, :KV_WIDTH], cos, sin_signed)
    v_chunks = [kv[:, KV_WIDTH + gi * LANES:KV_WIDTH + (gi + 1) * LANES] for gi in range(N_GROUPS)]
    for gi, (k_ref, v_ref) in enumerate(((k0_ref, v0_ref), (k1_ref, v1_ref), (k2_ref, v2_ref))):
        emit(k_ref, dilations[gi], [k_chunks[gi]])
        emit(v_ref, dilations[gi], [v_chunks[gi]])

    @pl.when(i >= first_kept_tile)
    def _():
        for gi in range(N_GROUPS):
            cs = slice(gi * LANES, (gi + 1) * LANES)
            if transpose_kept:
                kf_ref[0, cs, :] = k_chunks[gi].T
                vf_ref[0, cs, :] = v_chunks[gi].T
            else:
                kf_ref[0, :, cs] = k_chunks[gi]
                vf_ref[0, :, cs] = v_chunks[gi]


def _qkv(x, g_q, g_kv, w_q, w_kv, rope, tm, keep_rows, dilations, transpose_kept):
    b, t, d = x.shape
    cos, sin_signed, cos_step, sin_step = rope
    assert t % tm == 0 and keep_rows % tm == 0 and all(tm % (16 * dil) == 0 for dil in dilations)
    first_kept_tile = (t - keep_rows) // tm
    kept_block = lambda bi, i: jnp.maximum(i - first_kept_tile, 0)
    if transpose_kept:
        kept = pl.BlockSpec((1, KV_WIDTH, tm), lambda bi, i: (bi, 0, kept_block(bi, i)))
        kept_shape = jax.ShapeDtypeStruct((b, KV_WIDTH, keep_rows), F32)
    else:
        kept = pl.BlockSpec((1, tm, KV_WIDTH), lambda bi, i: (bi, kept_block(bi, i), 0))
        kept_shape = jax.ShapeDtypeStruct((b, keep_rows, KV_WIDTH), F32)
    sub_shape = lambda dil, width: jax.ShapeDtypeStruct((b, dil, t // dil, width), BF16)
    q_specs = [_subseq_spec(dil, tm, GROUP_Q_WIDTH) for dil in dilations]
    kv_specs = [_subseq_spec(dil, tm, GROUP_KV_WIDTH) for dil in dilations]
    q_shapes = [sub_shape(dil, GROUP_Q_WIDTH) for dil in dilations]
    kv_shapes = [sub_shape(dil, GROUP_KV_WIDTH) for dil in dilations]
    return pl.pallas_call(
        functools.partial(_qkv_kernel, tuple(dilations), first_kept_tile, transpose_kept),
        grid=(b, t // tm),
        in_specs=[
            pl.BlockSpec((1, tm, d), lambda bi, i: (bi, i, 0)),
            _const_spec(g_q.shape), _const_spec(g_kv.shape),
            _const_spec(w_q.shape), _const_spec(w_kv.shape),
            _const_spec(cos.shape), _const_spec(sin_signed.shape),
            pl.BlockSpec((None,) + cos_step.shape[1:], lambda bi, i: (i, 0, 0)),
            pl.BlockSpec((None,) + sin_step.shape[1:], lambda bi, i: (i, 0, 0)),
        ],
        out_specs=q_specs + kv_specs + kv_specs + [kept, kept],
        out_shape=q_shapes + kv_shapes + kv_shapes + [kept_shape, kept_shape],
        scratch_shapes=[pltpu.VMEM((GROUP_Q_WIDTH // LANES, tm, LANES), F32)],
        compiler_params=pltpu.CompilerParams(
            dimension_semantics=("arbitrary", "arbitrary"), vmem_limit_bytes=VMEM_LIMIT),
        name="qkv_rope",
    )(x, g_q, g_kv, w_q, w_kv, cos, sin_signed, cos_step, sin_step)


def _rope_tables(offsets, tile_starts):
    half = HEAD_DIM // 2
    inv = jnp.tile(ROPE_THETA ** (-jnp.arange(0, HEAD_DIM, 2, dtype=F32) / HEAD_DIM), LANES // half)
    sign = jnp.where((jnp.arange(LANES) // half) % 2 == 0, -1.0, 1.0).astype(F32)
    tables = []
    for positions in (offsets, tile_starts):
        ang = positions.astype(F32)[:, None] * inv[None, :]
        tables += [jnp.cos(ang), jnp.sin(ang) * sign[None, :]]
    rows8 = lambda a: jnp.broadcast_to(a[:, None, :], (a.shape[0], 8, LANES))
    return tables[0], tables[1], rows8(tables[2]), rows8(tables[3])


def _stat_lane(c, k):
    return (1 - c) * HEAD_DIM + k


def _band_attention_kernel(seg_rows, tiles_per_subseq, q_ref, kp_ref, kc_ref, vp_ref, vc_ref,
                           o_ref, stat_ref, k_scr, v_scr, bias_scr, s_scr, p_scr):
    n_seg = q_ref.shape[0] // seg_rows
    bands_per_seg = seg_rows // BAND
    n_bands = n_seg * bands_per_seg
    step = pl.program_id(1)
    lane = lax.broadcasted_iota(jnp.int32, (1, LANES), 1)
    for c in range(KV_PER_GROUP):
        own = (lane // HEAD_DIM) == c
        for u in range(n_seg):
            base = u * (seg_rows + BAND)
            for dst, k_rows, v_rows in (
                    (slice(base, base + BAND), kp_ref[...], vp_ref[...]),
                    (slice(base + BAND, base + BAND + seg_rows),
                     kc_ref[u * seg_rows:(u + 1) * seg_rows, :],
                     vc_ref[u * seg_rows:(u + 1) * seg_rows, :])):
                k_scr[c, dst, :] = jnp.where(own, k_rows, jnp.zeros((), BF16))
                v_scr[c, dst, :] = jnp.where(own, v_rows, jnp.ones((), BF16))
    stat_ref[...] = jnp.ones(stat_ref.shape, F32)

    row = lax.broadcasted_iota(jnp.int32, (BAND, 2 * BAND), 0)
    col = lax.broadcasted_iota(jnp.int32, (BAND, 2 * BAND), 1)
    band_bias = jnp.where((col >= row) & (col <= row + BAND), 0.0, NEG_BIG)
    bias_scr[0] = band_bias
    bias_scr[1] = band_bias + jnp.where(col < BAND, NEG_BIG, 0.0)
    starts_subseq = True if n_seg > 1 else lax.rem(step, tiles_per_subseq) == 0

    def rows_of(g):
        return pl.ds(g * BAND if isinstance(g, int) else pl.multiple_of(g * BAND, BAND), BAND)

    def key_rows_of(g):
        staged = g + g // bands_per_seg
        start = staged * BAND if isinstance(g, int) else pl.multiple_of(staged * BAND, BAND)
        return pl.ds(start, 2 * BAND)

    def scores(g, slot):
        yield KV_PER_GROUP
        q = jnp.concatenate(
            [q_ref[rows_of(g), k * LANES:(k + 1) * LANES] for k in range(Q_PER_KV)], axis=0)
        for c in range(KV_PER_GROUP):
            s_scr[slot, c] = lax.dot_general(q, k_scr[c, key_rows_of(g), :],
                                             (((1,), (1,)), ((), ())),
                                             preferred_element_type=F32)
            yield

    def softmax(g, slot):
        yield KV_PER_GROUP * Q_PER_KV
        first = jnp.logical_and(starts_subseq, g % bands_per_seg == 0).astype(jnp.int32)
        for c in range(KV_PER_GROUP):
            for k in range(Q_PER_KV):
                part = pl.ds(k * BAND, BAND)
                s = s_scr[slot, c, part, :] + bias_scr[first]
                m = jnp.max(s, axis=1, keepdims=True)
                p_scr[slot, c, part, :] = jnp.exp(s - m).astype(BF16)
                stat_ref[rows_of(g), pl.ds(_stat_lane(c, k) + STAT_MAX_OFFSET, 1)] = m
                yield

    def weighted_values(g, slot):
        yield KV_PER_GROUP + Q_PER_KV + 1
        ov = []
        for c in range(KV_PER_GROUP):
            ov.append(jnp.dot(p_scr[slot, c], v_scr[c, key_rows_of(g), :],
                              preferred_element_type=F32))
            yield
        low = lane < HEAD_DIM
        sums = None
        for k in range(Q_PER_KV):
            part = slice(k * BAND, (k + 1) * BAND)
            o_ref[rows_of(g), k * LANES:(k + 1) * LANES] = (
                jnp.where(low, ov[0][part], ov[1][part]).astype(o_ref.dtype))
            sums_k = jnp.where(low, ov[1][part], ov[0][part])
            sums = sums_k if sums is None else jnp.where((lane % HEAD_DIM) == k, sums_k, sums)
            yield
        for c in range(KV_PER_GROUP):
            lanes = pl.ds(_stat_lane(c, 0), Q_PER_KV)
            stat_ref[rows_of(g), lanes] = sums[:, _stat_lane(c, 0):_stat_lane(c, 0) + Q_PER_KV]
        yield

    _interleave(scores(0, 0))
    _interleave(softmax(0, 0), scores(1, 1))

    def steady(t, carry):
        for g, slot in ((2 * t + 1, 1), (2 * t + 2, 0)):
            _interleave(weighted_values(g - 1, 1 - slot), softmax(g, slot),
                        scores(g + 1, 1 - slot))
        return carry

    assert n_bands % 2 == 0
    lax.fori_loop(0, (n_bands - 2) // 2, steady, 0)
    last = n_bands - 1
    _interleave(weighted_values(last - 1, (last - 1) % 2), softmax(last, last % 2))
    _interleave(weighted_values(last, last % 2))


def _band_attention(q, k, v, block_rows):
    b, dilation, sub, _ = q.shape
    seg_rows = min(sub, block_rows)
    total = dilation * sub
    assert total % block_rows == 0 and block_rows % seg_rows == 0 and sub % seg_rows == 0
    assert seg_rows % BAND == 0 and block_rows >= 2 * BAND
    n_seg = block_rows // seg_rows
    flat = lambda a: a.reshape(b, total, a.shape[-1])
    cur = lambda width: pl.BlockSpec((None, block_rows, width), lambda bi, i: (bi, i, 0))
    prev = pl.BlockSpec((None, BAND, GROUP_KV_WIDTH),
                        lambda bi, i: (bi, jnp.maximum(i * (block_rows // BAND) - 1, 0), 0))
    staged_rows = n_seg * (seg_rows + BAND)
    o, stat = pl.pallas_call(
        functools.partial(_band_attention_kernel, seg_rows, sub // seg_rows),
        grid=(b, total // block_rows),
        in_specs=[cur(GROUP_Q_WIDTH), prev, cur(GROUP_KV_WIDTH), prev, cur(GROUP_KV_WIDTH)],
        out_specs=[cur(GROUP_Q_WIDTH), cur(STAT_LANES)],
        out_shape=[jax.ShapeDtypeStruct((b, total, GROUP_Q_WIDTH), BF16),
                   jax.ShapeDtypeStruct((b, total, STAT_LANES), F32)],
        scratch_shapes=[pltpu.VMEM((KV_PER_GROUP, staged_rows, GROUP_KV_WIDTH), BF16),
                        pltpu.VMEM((KV_PER_GROUP, staged_rows, GROUP_KV_WIDTH), BF16),
                        pltpu.VMEM((2, BAND, 2 * BAND), F32),
                        pltpu.VMEM((2, KV_PER_GROUP, Q_PER_KV * BAND, 2 * BAND), F32),
                        pltpu.VMEM((2, KV_PER_GROUP, Q_PER_KV * BAND, 2 * BAND), BF16)],
        compiler_params=pltpu.CompilerParams(
            dimension_semantics=("arbitrary", "arbitrary"), vmem_limit_bytes=VMEM_LIMIT),
        name=f"band_attention_d{dilation}",
    )(flat(q), flat(k), flat(k), flat(v), flat(v))
    return (o.reshape(b, dilation, sub, GROUP_Q_WIDTH), stat.reshape(b, dilation, sub, STAT_LANES))


SAMPLE_Q_ROWS = 16
NEW_KEY_ROWS = 8
SAMPLE_SEQS_PER_STEP = 4


def _sample_attention_kernel(n_new, q_ref, kn_ref, vn_ref, kc0, kc1, kc2, vc0, vc1, vc2,
                             o_ref, lse_ref):
    n_seq = q_ref.shape[0]
    n_rows = q_ref.shape[2]
    contract_last = (((1,), (1,)), ((), ()))
    cache_refs = ((kc0, vc0), (kc1, vc1), (kc2, vc2))
    bias_c, bias_n = [], []
    for gi, (kc_ref, _) in enumerate(cache_refs):
        dmask = ATTN_DILATIONS[gi] - 1
        n_cache = kc_ref.shape[3]
        assert n_cache == ATTN_WINDOWS[gi]
        t_c = lax.broadcasted_iota(jnp.int32, (n_rows, n_cache), 0) & (n_new - 1)
        j_c = lax.broadcasted_iota(jnp.int32, (n_rows, n_cache), 1)
        valid_c = (j_c >= t_c) & (((j_c - t_c) & dmask) == 0)
        t_n = lax.broadcasted_iota(jnp.int32, (n_rows, NEW_KEY_ROWS), 0) & (n_new - 1)
        j_n = lax.broadcasted_iota(jnp.int32, (n_rows, NEW_KEY_ROWS), 1)
        valid_n = (j_n <= t_n) & (((t_n - j_n) & dmask) == 0)
        bias_c.append(jnp.where(valid_c, 0.0, NEG_BIG))
        bias_n.append(jnp.where(valid_n, 0.0, NEG_BIG))

    def one_sequence(b):
        yield len(cache_refs) * KV_PER_GROUP
        kn_all = kn_ref[b]
        vn_all = vn_ref[b]
        for gi, (kc_ref, vc_ref) in enumerate(cache_refs):
            for c in range(KV_PER_GROUP):
                kvh = gi * KV_PER_GROUP + c
                ns = slice(kvh * HEAD_DIM, (kvh + 1) * HEAD_DIM)
                q = q_ref[b, kvh]
                kt = kc_ref[b, c].astype(BF16)
                vt = vc_ref[b, c].astype(BF16)
                s_c = jnp.dot(q, kt, preferred_element_type=F32) + bias_c[gi]
                s_n = lax.dot_general(q, kn_all[:, ns].astype(BF16), contract_last,
                                      preferred_element_type=F32) + bias_n[gi]
                m = jnp.maximum(jnp.max(s_c, axis=1, keepdims=True),
                                jnp.max(s_n, axis=1, keepdims=True))
                p_c = jnp.exp(s_c - m)
                p_n = jnp.exp(s_n - m)
                l = jnp.sum(p_c, axis=1, keepdims=True) + jnp.sum(p_n, axis=1, keepdims=True)
                o = (lax.dot_general(p_c.astype(BF16), vt, contract_last,
                                     preferred_element_type=F32)
                     + jnp.dot(p_n.astype(BF16), vn_all[:, ns].astype(BF16),
                               preferred_element_type=F32))
                o_ref[b, kvh] = o / l
                lse_ref[b, kvh] = m + jnp.log(l)
                yield

    _interleave(*[one_sequence(b) for b in range(n_seq)])


def _sample_attention(q16, k_new, v_new, cache_kt, cache_vt, n_new):
    b = q16.shape[0]
    n_kv = q16.shape[1]
    assert cache_kt.shape[3] == KV_WINDOW and n_new & (n_new - 1) == 0 and n_new <= NEW_KEY_ROWS

    per_step = SAMPLE_SEQS_PER_STEP
    assert b % per_step == 0

    def cache_spec(gi):
        cols = ATTN_WINDOWS[gi]
        last = KV_WINDOW // cols - 1
        return pl.BlockSpec((per_step, KV_PER_GROUP, HEAD_DIM, cols),
                            lambda bi: (bi, gi, 0, last))

    whole = lambda a: pl.BlockSpec((per_step,) + a.shape[1:],
                                   lambda bi: (bi,) + (0,) * (a.ndim - 1))
    o_shape = jax.ShapeDtypeStruct((b, n_kv, SAMPLE_Q_ROWS, HEAD_DIM), F32)
    lse_shape = jax.ShapeDtypeStruct((b, n_kv, SAMPLE_Q_ROWS, 1), F32)
    return pl.pallas_call(
        functools.partial(_sample_attention_kernel, n_new),
        grid=(b // per_step,),
        in_specs=[whole(q16), whole(k_new), whole(v_new)]
        + [cache_spec(gi) for gi in range(N_GROUPS)] * 2,
        out_specs=[whole(o_shape), whole(lse_shape)],
        out_shape=[o_shape, lse_shape],
        compiler_params=pltpu.CompilerParams(
            dimension_semantics=("arbitrary",), vmem_limit_bytes=VMEM_LIMIT),
        name="sample_attention",
    )(q16, k_new, v_new, cache_kt, cache_kt, cache_kt, cache_vt, cache_vt, cache_vt)


def _attn_mix_steps(dilations, x_ref, o_refs, s_refs, g, wo_ref, o_scr, stat_scr, a_scr,
                    x1_ref, h_ref):
    tm = x_ref.shape[0]
    chunks_per_group = GROUP_Q_WIDTH // LANES
    yield 2 * N_GROUPS + 1 + 2 * len(_row_blocks(tm))
    for gi, (o_ref, s_ref) in enumerate(zip(o_refs, s_refs)):
        for r in range(dilations[gi]):
            rows = _subseq_rows(tm, r, dilations[gi])
            stat_scr[gi, rows, :] = s_ref[r]
            for c in range(chunks_per_group):
                o_scr[gi * chunks_per_group + c, rows, :] = (
                    o_ref[r, :, c * LANES:(c + 1) * LANES].astype(F32))
        yield
    sums = [stat_scr[gi] for gi in range(N_GROUPS)]
    maxes = [pltpu.roll(sm, STAT_LANES - STAT_MAX_OFFSET, 1) for sm in sums]
    top = jnp.maximum(jnp.maximum(maxes[0], maxes[1]), maxes[2])
    es = [jnp.exp(mx - top) for mx in maxes]
    den = sums[0] * es[0] + sums[1] * es[1] + sums[2] * es[2]
    slot_id = lax.broadcasted_iota(jnp.int32, (tm, GROUP_Q_WIDTH), 1) // HEAD_DIM
    for gi in range(N_GROUPS):
        scale = es[gi] / den
        wide = jnp.zeros((tm, GROUP_Q_WIDTH), F32)
        for slot in range(HEADS_PER_GROUP):
            stat_lane = _stat_lane(slot % KV_PER_GROUP, slot // KV_PER_GROUP)
            wide = jnp.where(slot_id == slot, scale[:, stat_lane:stat_lane + 1], wide)
        for c in range(chunks_per_group):
            ci = gi * chunks_per_group + c
            a_scr[:, ci * LANES:(ci + 1) * LANES] = (
                o_scr[ci] * wide[:, c * LANES:(c + 1) * LANES]).astype(BF16)
        yield
    mix = jnp.dot(a_scr[...], wo_ref[...], preferred_element_type=F32)
    yield
    tail = _chain(
        _residual_norm_steps(x1_ref, lambda rows: x_ref[rows, :], lambda rows: mix[rows], g[1:2]),
        _norm_cast_steps(h_ref, lambda rows: x1_ref[rows, :], g[2:3]))
    next(tail)
    yield from tail


def _attn_mix_scratch(tm):
    return [pltpu.VMEM((Q_WIDTH // LANES, tm, LANES), F32),
            pltpu.VMEM((N_GROUPS, tm, STAT_LANES), F32),
            pltpu.VMEM((tm, Q_WIDTH), BF16)]


def _layer1_single_kernel(dilations, x_ref, o0_ref, o1_ref, o2_ref, s0_ref, s1_ref, s2_ref, g_ref,
                          wo_ref, wup_ref, wdn_ref, y_ref, o_scr, stat_scr, a_scr,
                          x1_scr, h_scr, acc_scr):
    g = g_ref[...]

    @pl.when(pl.program_id(0) == 0)
    def _():
        _interleave(_attn_mix_steps(dilations, x_ref, (o0_ref, o1_ref, o2_ref),
                                    (s0_ref, s1_ref, s2_ref), g, wo_ref, o_scr, stat_scr, a_scr,
                                    x1_scr, h_scr))
        acc_scr[...] = jnp.zeros(acc_scr.shape, F32)

    def emit(y):
        y_ref[...] = y

    _streamed_mlp_step(g, wup_ref, wdn_ref, x1_scr, h_scr, acc_scr, emit)


def _layer1_first_kernel(dilations, x_ref, o0_ref, o1_ref, o2_ref, s0_ref, s1_ref, s2_ref, g_ref,
                         wo_ref, x1_ref, h_ref, o_scr, stat_scr, a_scr):
    _interleave(_attn_mix_steps(dilations, x_ref, (o0_ref, o1_ref, o2_ref),
                                (s0_ref, s1_ref, s2_ref), g_ref[...], wo_ref, o_scr, stat_scr,
                                a_scr, x1_ref, h_ref))


def _layer1_prompt_kernel(dilations, x_ref, o0_ref, o1_ref, o2_ref, s0_ref, s1_ref, s2_ref,
                          x1f_ref, hf_ref, g_ref, wo_ref, wup_ref, wdn_ref, y_ref,
                          o_scr, stat_scr, a_scr, x1_scr, h_scr, acc_scr):
    s = pl.program_id(0)
    n_tiles = pl.num_programs(0) - 1
    g = g_ref[...]

    @pl.when(s == 0)
    def _():
        x1_scr[0] = x1f_ref[...]
        h_scr[0] = hf_ref[...]
        x1_scr[2] = jnp.zeros(x1_scr.shape[1:], F32)
        acc_scr[1] = jnp.zeros(acc_scr.shape[1:], F32)

    def finish_previous():
        x1_slot, acc_slot = lax.rem(s + 2, 3), lax.rem(s + 1, 2)
        return _residual_norm_steps(y_ref, lambda rows: x1_scr[x1_slot, rows, :],
                                    lambda rows: acc_scr[acc_slot, rows, :], g[3:4])

    @pl.when(s < n_tiles)
    def _():
        cur = lax.rem(s, 2)

        def store_acc(acc):
            acc_scr[cur] = acc

        _interleave(
            finish_previous(),
            _mlp_matmul_steps(lambda: h_scr[cur], wup_ref, wdn_ref, store_acc),
            _attn_mix_steps(dilations, x_ref, (o0_ref, o1_ref, o2_ref), (s0_ref, s1_ref, s2_ref),
                            g, wo_ref, o_scr, stat_scr, a_scr,
                            x1_scr.at[lax.rem(s + 1, 3)], h_scr.at[1 - cur]))

    @pl.when(s == n_tiles)
    def _():
        _interleave(finish_previous())


def _layer1(x, os, stats, gains, w_o, w_up, w_down, layer, tm):
    b, t, d = x.shape
    dilations = tuple(o.shape[1] for o in os)
    tiles_per_seq = t // tm
    n_tiles = b * tiles_per_seq
    assert t % tm == 0 and all(tm % (16 * dil) == 0 for dil in dilations)
    x2 = x.reshape(b * t, d)
    weights = (gains, w_o, w_up, w_down)
    weight_specs = [_const_spec(gains.shape), _const_spec(w_o.shape),
                    _layer_spec(w_up, layer), _layer_spec(w_down, layer)]

    def tile_specs(tile):
        sub = lambda dil, width: pl.BlockSpec(
            (None, dil, tm // dil, width),
            lambda s: (tile(s) // tiles_per_seq, 0, tile(s) % tiles_per_seq, 0))
        return ([pl.BlockSpec((tm, d), lambda s: (tile(s), 0))]
                + [sub(dil, GROUP_Q_WIDTH) for dil in dilations]
                + [sub(dil, STAT_LANES) for dil in dilations])

    if n_tiles == 1:
        n_chunks, mlp_specs = _streamed_mlp_specs(w_up, w_down, layer)
        y = pl.pallas_call(
            functools.partial(_layer1_single_kernel, dilations),
            grid=(n_chunks,),
            in_specs=tile_specs(lambda s: 0) + weight_specs[:2] + mlp_specs,
            out_specs=pl.BlockSpec((tm, d), lambda s: (0, 0)),
            out_shape=jax.ShapeDtypeStruct((b * t, d), F32),
            scratch_shapes=_attn_mix_scratch(tm) + [
                pltpu.VMEM((tm, d), F32), pltpu.VMEM((tm, d), BF16), pltpu.VMEM((tm, d), F32)],
            compiler_params=pltpu.CompilerParams(
                dimension_semantics=("arbitrary",), vmem_limit_bytes=VMEM_LIMIT),
            name="layer1_single_tile",
        )(x2, *os, *stats, *weights)
        return y.reshape(b, t, d)

    x1_first, h_first = pl.pallas_call(
        functools.partial(_layer1_first_kernel, dilations),
        grid=(1,),
        in_specs=tile_specs(lambda s: 0) + [_const_spec(gains.shape), _const_spec(w_o.shape)],
        out_specs=[pl.BlockSpec((tm, d), lambda s: (0, 0))] * 2,
        out_shape=[jax.ShapeDtypeStruct((tm, d), F32), jax.ShapeDtypeStruct((tm, d), BF16)],
        scratch_shapes=_attn_mix_scratch(tm),
        compiler_params=pltpu.CompilerParams(vmem_limit_bytes=VMEM_LIMIT),
        name="layer1_first_tile",
    )(x2, *os, *stats, gains, w_o)

    scratch = _attn_mix_scratch(tm) + [
        pltpu.VMEM((3, tm, d), F32), pltpu.VMEM((2, tm, d), BF16), pltpu.VMEM((2, tm, d), F32)]
    y = pl.pallas_call(
        functools.partial(_layer1_prompt_kernel, dilations),
        grid=(n_tiles + 1,),
        in_specs=tile_specs(lambda s: jnp.minimum(s + 1, n_tiles - 1))
        + [_const_spec(x1_first.shape), _const_spec(h_first.shape)]
        + weight_specs,
        out_specs=pl.BlockSpec((tm, d), lambda s: (jnp.maximum(s - 1, 0), 0)),
        out_shape=jax.ShapeDtypeStruct((b * t, d), F32),
        scratch_shapes=scratch,
        compiler_params=pltpu.CompilerParams(
            dimension_semantics=("arbitrary",), vmem_limit_bytes=VMEM_LIMIT),
        name="layer1_prompt",
    )(x2, *os, *stats, x1_first, h_first, *weights)
    return y.reshape(b, t, d)


PROMPT_TILE = 512
LAYER1_TILE = 512
QKV_TILE = 1024
ATTN_BLOCK_ROWS = 2048


def kernel(x_prompt, x_sample, cache_pool, cache_k, cache_v, norm_gains, kv_norm_gain, w_pool,
           pool_scale, w_q, w_o, w_kv, w_up, w_down):
    depth = norm_gains.shape[0]
    assert depth == 2 and cache_pool.shape[0] == 1 and w_q.shape[0] == 1
    bp, tp, d = x_prompt.shape
    bs, ts, _ = x_sample.shape

    g0, g1 = norm_gains[0], norm_gains[1]
    g1_q = g1[0:1]
    g_kv = kv_norm_gain[None, :]
    wp = w_pool[0].astype(BF16)
    ps = pool_scale[0][None, :]
    wq = w_q[0].reshape(d, N_GROUPS, KV_PER_GROUP, Q_PER_KV, HEAD_DIM).transpose(0, 1, 3, 2, 4)
    wq = wq.reshape(d, Q_WIDTH).astype(BF16)
    wo = w_o[0].reshape(N_GROUPS, KV_PER_GROUP, Q_PER_KV, HEAD_DIM, d).transpose(0, 2, 1, 3, 4)
    wo = wo.reshape(Q_WIDTH, d).astype(BF16)
    wkv = w_kv.astype(BF16)
    wup, wdn = w_up.astype(BF16), w_down.astype(BF16)

    keep = min(KV_WINDOW, tp)
    xp1, utail = _layer0_prompt(x_prompt, g0, wp, ps, wup, wdn, 0, PROMPT_TILE)
    rope_p = _rope_tables(jnp.arange(QKV_TILE), jnp.arange(tp // QKV_TILE) * QKV_TILE)
    qkv_p = _qkv(xp1, g1_q, g_kv, wq, wkv, rope_p, QKV_TILE, keep, ATTN_DILATIONS, True)
    os_p, stats_p = [], []
    for gi, dil in enumerate(ATTN_DILATIONS):
        o, stat = _band_attention(qkv_p[gi], qkv_p[N_GROUPS + gi], qkv_p[2 * N_GROUPS + gi],
                                  ATTN_BLOCK_ROWS)
        os_p.append(o)
        stats_p.append(stat)
    y_prompt = _layer1(xp1, os_p, stats_p, g1, wo, wup, wdn, 1, LAYER1_TILE)
    pool_prompt = utail[:, POOL_HALO - POOL_BUF:][None]
    k_prompt = qkv_p[-2].reshape(bp, N_KV_HEADS, HEAD_DIM, keep).transpose(0, 3, 1, 2)
    v_prompt = qkv_p[-1].reshape(bp, N_KV_HEADS, HEAD_DIM, keep).transpose(0, 3, 1, 2)

    n_tok = ts * bs
    xs_tm = jnp.swapaxes(x_sample, 0, 1)
    buf_tm = jnp.swapaxes(cache_pool[0], 0, 1)
    xs1_tm, us_tm = _layer0_sample(xs_tm, buf_tm, PAST_LEN, g0, wp, ps, wup, wdn, 0)
    xs1 = xs1_tm.reshape(1, n_tok, d)
    rope_s = _rope_tables(jnp.arange(n_tok) // bs, jnp.full((1,), PAST_LEN))
    qkv_s = _qkv(xs1, g1_q, g_kv, wq, wkv, rope_s, n_tok, n_tok, (1,) * N_GROUPS, False)
    q_s = jnp.concatenate([q[0, 0] for q in qkv_s[:N_GROUPS]], axis=-1)
    q16 = q_s.reshape(ts, bs, N_GROUPS, Q_PER_KV, KV_PER_GROUP, HEAD_DIM)
    q16 = q16.transpose(1, 2, 4, 3, 0, 5).reshape(bs, N_KV_HEADS, Q_PER_KV, ts, HEAD_DIM)
    q16 = jnp.pad(q16, ((0, 0), (0, 0), (0, SAMPLE_Q_ROWS // ts - Q_PER_KV), (0, 0), (0, 0)))
    q16 = q16.reshape(bs, N_KV_HEADS, SAMPLE_Q_ROWS, HEAD_DIM)
    k_s = jnp.swapaxes(qkv_s[-2].reshape(ts, bs, KV_WIDTH), 0, 1)
    v_s = jnp.swapaxes(qkv_s[-1].reshape(ts, bs, KV_WIDTH), 0, 1)
    pad_new = ((0, 0), (0, NEW_KEY_ROWS - ts), (0, 0))
    o16, lse16 = _sample_attention(
        q16, jnp.pad(k_s, pad_new), jnp.pad(v_s, pad_new),
        cache_k.transpose(0, 2, 3, 1), cache_v.transpose(0, 2, 3, 1), ts)
    heads_padded = SAMPLE_Q_ROWS // ts
    o_s = o16.reshape(bs, N_GROUPS, KV_PER_GROUP, heads_padded, ts, HEAD_DIM)[:, :, :, :Q_PER_KV]
    o_s = o_s.transpose(4, 0, 1, 3, 2, 5).reshape(1, 1, n_tok, N_GROUPS, GROUP_Q_WIDTH).astype(BF16)
    lse_s = lse16.reshape(bs, N_GROUPS, KV_PER_GROUP, heads_padded, ts)[:, :, :, :Q_PER_KV]
    lse_s = lse_s.transpose(4, 0, 1, 2, 3).reshape(1, 1, n_tok, N_GROUPS, KV_PER_GROUP, Q_PER_KV)
    ones = lambda n: jnp.ones((1, 1, n_tok, N_GROUPS, n), F32)
    max_lane = [_stat_lane(c, 0) + STAT_MAX_OFFSET for c in range(KV_PER_GROUP)]
    assert max_lane[1] < max_lane[0]
    stat_s = jnp.concatenate(
        [ones(max_lane[1]), lse_s[..., 1, :],
         ones(max_lane[0] - max_lane[1] - Q_PER_KV), lse_s[..., 0, :],
         ones(STAT_LANES - max_lane[0] - Q_PER_KV)], axis=-1)
    ys_tm = _layer1(xs1, [o_s[:, :, :, gi] for gi in range(N_GROUPS)],
                    [stat_s[:, :, :, gi] for gi in range(N_GROUPS)], g1, wo, wup, wdn, 1, n_tok)
    y_sample = jnp.swapaxes(ys_tm.reshape(ts, bs, d), 0, 1)
    u_s = jnp.swapaxes(us_tm, 0, 1)
    pool_sample = jnp.concatenate([cache_pool[0], u_s], axis=1)[:, -POOL_BUF:][None]
    k_sample = k_s.reshape(bs, ts, N_KV_HEADS, HEAD_DIM)
    v_sample = v_s.reshape(bs, ts, N_KV_HEADS, HEAD_DIM)

    return (y_prompt, y_sample, pool_prompt, k_prompt, v_prompt, pool_sample, k_sample, v_sample)
```

```python
import functools

import jax
import jax.numpy as jnp
from jax import lax
from jax.experimental import pallas as pl
from jax.experimental.pallas import tpu as pltpu

F32 = jnp.float32
BF16 = jnp.bfloat16

EPS = 1e-6
ROPE_THETA = 10000.0
PAST_LEN = 16384
POOL_WINDOWS = (2, 4, 8, 16)
POOL_BUF = max(POOL_WINDOWS) - 1
POOL_HALO = 16
HEAD_DIM = 64
ATTN_WINDOWS = (128, 512, 2048)
ATTN_DILATIONS = (1, 4, 16)
N_GROUPS = len(ATTN_WINDOWS)
KV_PER_GROUP = 2
Q_PER_KV = 3
HEADS_PER_GROUP = KV_PER_GROUP * Q_PER_KV
N_KV_HEADS = N_GROUPS * KV_PER_GROUP
GROUP_Q_WIDTH = HEADS_PER_GROUP * HEAD_DIM
GROUP_KV_WIDTH = KV_PER_GROUP * HEAD_DIM
Q_WIDTH = N_GROUPS * GROUP_Q_WIDTH
KV_WIDTH = N_GROUPS * GROUP_KV_WIDTH
KV_WINDOW = max(ATTN_WINDOWS)
BAND = 128
NEG_BIG = -1e30

LANES = 128
STAT_LANES = LANES
STAT_MAX_OFFSET = 8
VMEM_LIMIT = 56 * 1024 * 1024
FF_CHUNK = 512
NORM_ROW_BLOCKS = 4
POOL_NORM_BLOCKS = 1
POOL_FF_CHUNK = 1024
SAMPLE_FF_CHUNK = 1024

for _w, _d in zip(ATTN_WINDOWS, ATTN_DILATIONS):
    assert _w // _d == BAND and _w % _d == 0
assert GROUP_KV_WIDTH == LANES


def _rms(x, g):
    return x * lax.rsqrt(jnp.mean(x * x, axis=-1, keepdims=True) + EPS) * g


def _mlp_matmul_steps(load_h, wup_ref, wdn_ref, emit_acc, chunk=FF_CHUNK):
    n_chunks = wup_ref.shape[1] // chunk
    yield n_chunks
    h = load_h()
    acc = None
    for c in range(n_chunks):
        cs = slice(c * chunk, (c + 1) * chunk)
        a = jnp.dot(h, wup_ref[:, cs], preferred_element_type=F32)
        a = jnp.square(jnp.maximum(a, 0.0)).astype(BF16)
        part = jnp.dot(a, wdn_ref[cs, :], preferred_element_type=F32)
        acc = part if acc is None else acc + part
        if c + 1 == n_chunks:
            emit_acc(acc)
        yield


def _row_blocks(n_rows, n_blocks=NORM_ROW_BLOCKS):
    size = n_rows // n_blocks if n_rows % (8 * n_blocks) == 0 else n_rows
    return [slice(lo, lo + size) for lo in range(0, n_rows, size)]


def _residual_norm_steps(dst_ref, load_x, load_v, g, n_blocks=NORM_ROW_BLOCKS):
    blocks = _row_blocks(dst_ref.shape[0], n_blocks)
    yield len(blocks)
    for rows in blocks:
        dst_ref[rows, :] = load_x(rows) + _rms(load_v(rows), g)
        yield


def _norm_cast_steps(dst_ref, load_v, g, n_blocks=NORM_ROW_BLOCKS):
    blocks = _row_blocks(dst_ref.shape[0], n_blocks)
    yield len(blocks)
    for rows in blocks:
        dst_ref[rows, :] = _rms(load_v(rows), g).astype(dst_ref.dtype)
        yield


def _chain(*generators):
    counts = [next(gen) for gen in generators]
    yield sum(counts)
    for gen in generators:
        yield from gen


def _interleave(*generators):
    totals = [next(gen) for gen in generators]
    done = [0] * len(generators)
    while any(d < t for d, t in zip(done, totals)):
        i = min((i for i in range(len(generators)) if done[i] < totals[i]),
                key=lambda i: (done[i] + 1) / totals[i])
        next(generators[i])
        done[i] += 1


def _const_spec(shape):
    zeros = (0,) * len(shape)
    return pl.BlockSpec(shape, lambda *_: zeros, pipeline_mode=pl.Buffered(1))


def _layer_spec(w, layer):
    index = (layer,) + (0,) * (w.ndim - 1)
    return pl.BlockSpec((None,) + w.shape[1:], lambda *_: index, pipeline_mode=pl.Buffered(1))


def _subseq_spec(dilation, rows, width):
    return pl.BlockSpec((None, dilation, rows // dilation, width), lambda bi, i: (bi, 0, i, 0))


def _subseq_rows(n_rows, r, dilation):
    n = n_rows // dilation
    return pl.ds(r, n, stride=dilation) if dilation > 1 else pl.ds(0, n)


def _pool_mixer_steps(x_ref, xh_ref, tile_in_seq, g, wp_ref, ps_ref, ext_ref, x1_ref, h_ref,
                      utail_ref=None):
    tm = x_ref.shape[0]
    pool_ch = wp_ref.shape[1]
    blocks = _row_blocks(tm, POOL_NORM_BLOCKS)
    yield 3 * len(blocks) + len(POOL_WINDOWS)
    uh = _rms(xh_ref[0:POOL_HALO, :], g[0:1]) * jnp.where(tile_in_seq > 0, 1.0, 0.0)
    ext_ref[0:POOL_HALO, :] = uh
    for rows in blocks:
        u = _rms(x_ref[rows, :], g[0:1])
        ext_ref[POOL_HALO + rows.start:POOL_HALO + rows.stop, :] = u
        if utail_ref is not None and rows.stop == tm:
            utail_ref[0] = u[u.shape[0] - POOL_HALO:, :]
        yield
    pos = tile_in_seq * tm + lax.broadcasted_iota(jnp.int32, (tm, 1), 0)
    parts = []
    for gi, w in enumerate(POOL_WINDOWS):
        cs = slice(gi * pool_ch, (gi + 1) * pool_ch)
        s = ext_ref[POOL_HALO:, cs]
        for j in range(1, w):
            s = s + ext_ref[POOL_HALO - j:POOL_HALO - j + tm, cs]
        cnt = jnp.minimum(w, pos + 1).astype(F32)
        pooled = s / cnt - ext_ref[POOL_HALO:, cs]
        parts.append(jnp.dot(pooled.astype(BF16), wp_ref[gi], preferred_element_type=F32))
        yield
    mix = jnp.concatenate(parts, axis=1) * ps_ref[...]
    tail = _chain(
        _residual_norm_steps(x1_ref, lambda rows: x_ref[rows, :], lambda rows: mix[rows], g[1:2],
                             POOL_NORM_BLOCKS),
        _norm_cast_steps(h_ref, lambda rows: x1_ref[rows, :], g[2:3], POOL_NORM_BLOCKS))
    next(tail)
    yield from tail


def _layer0_prompt_kernel(tiles_per_seq, x0_ref, xa_ref, xha_ref, xb_ref, xhb_ref, g_ref,
                          wp_ref, ps_ref, wup_ref, wdn_ref, y_ref, utail_ref,
                          x1_scr, h_scr, ext_scr):
    s = pl.program_id(0)
    n_tiles = 2 * pl.num_programs(0)
    tm = xa_ref.shape[0]
    g = g_ref[...]

    def prepare(x_ref, xh_ref, tile, slot, tail_ref):
        return _pool_mixer_steps(x_ref, xh_ref, lax.rem(tile, tiles_per_seq), g, wp_ref, ps_ref,
                                 ext_scr.at[slot], x1_scr.at[slot], h_scr.at[slot], tail_ref)

    @pl.when(s == 0)
    def _():
        _interleave(prepare(x0_ref, x0_ref, 0, 0, None))

    def mlp(slot):
        acc = []
        return _chain(
            _mlp_matmul_steps(lambda: h_scr[slot], wup_ref, wdn_ref, acc.append, POOL_FF_CHUNK),
            _residual_norm_steps(y_ref.at[pl.ds(slot * tm, tm)],
                                 lambda rows: x1_scr[slot, rows, :],
                                 lambda rows: acc[0][rows], g[3:4], POOL_NORM_BLOCKS))

    _interleave(mlp(0), prepare(xa_ref, xha_ref, 2 * s + 1, 1, utail_ref))
    _interleave(mlp(1), prepare(xb_ref, xhb_ref, jnp.minimum(2 * s + 2, n_tiles - 1), 0, None))


def _layer0_prompt(x, gains, w_pool, pool_scale, w_up, w_down, layer, tm):
    b, t, d = x.shape
    tiles_per_seq = t // tm
    n_tiles = b * tiles_per_seq
    assert t % tm == 0 and tm % POOL_HALO == 0 and tiles_per_seq % 2 == 0
    halo_per_tile = tm // POOL_HALO
    x2 = x.reshape(b * t, d)
    tile_a = lambda s: 2 * s + 1
    tile_b = lambda s: jnp.minimum(2 * s + 2, n_tiles - 1)
    tile_spec = lambda tile: pl.BlockSpec((tm, d), lambda s: (tile(s), 0))
    halo_spec = lambda tile: pl.BlockSpec((POOL_HALO, d),
                                          lambda s: (tile(s) * halo_per_tile - 1, 0))
    y, utail = pl.pallas_call(
        functools.partial(_layer0_prompt_kernel, tiles_per_seq),
        grid=(n_tiles // 2,),
        in_specs=[
            pl.BlockSpec((tm, d), lambda s: (0, 0), pipeline_mode=pl.Buffered(1)),
            tile_spec(tile_a), halo_spec(tile_a), tile_spec(tile_b), halo_spec(tile_b),
            _const_spec(gains.shape), _const_spec(w_pool.shape), _const_spec(pool_scale.shape),
            _layer_spec(w_up, layer), _layer_spec(w_down, layer),
        ],
        out_specs=[
            pl.BlockSpec((2 * tm, d), lambda s: (s, 0)),
            pl.BlockSpec((1, POOL_HALO, d), lambda s: (tile_a(s) // tiles_per_seq, 0, 0)),
        ],
        out_shape=[
            jax.ShapeDtypeStruct((b * t, d), F32),
            jax.ShapeDtypeStruct((b, POOL_HALO, d), F32),
        ],
        scratch_shapes=[pltpu.VMEM((2, tm, d), F32), pltpu.VMEM((2, tm, d), BF16),
                        pltpu.VMEM((2, tm + POOL_HALO, d), F32)],
        compiler_params=pltpu.CompilerParams(
            dimension_semantics=("arbitrary",), vmem_limit_bytes=VMEM_LIMIT),
        name="layer0_prompt",
    )(x2, x2, x2, x2, x2, gains, w_pool, pool_scale, w_up, w_down)
    return y.reshape(b, t, d), utail


def _streamed_mlp_step(g, wup_ref, wdn_ref, x1_scr, h_scr, acc_scr, emit):
    a = jnp.dot(h_scr[...], wup_ref[...], preferred_element_type=F32)
    a = jnp.square(jnp.maximum(a, 0.0)).astype(BF16)
    acc_scr[...] += jnp.dot(a, wdn_ref[...], preferred_element_type=F32)

    @pl.when(pl.program_id(0) == pl.num_programs(0) - 1)
    def _():
        emit(x1_scr[...] + _rms(acc_scr[...], g[3:4]))


def _streamed_mlp_specs(w_up, w_down, layer):
    assert w_up.shape[2] % SAMPLE_FF_CHUNK == 0
    return (w_up.shape[2] // SAMPLE_FF_CHUNK,
            [pl.BlockSpec((None, w_up.shape[1], SAMPLE_FF_CHUNK), lambda c: (layer, 0, c)),
             pl.BlockSpec((None, SAMPLE_FF_CHUNK, w_down.shape[2]), lambda c: (layer, c, 0))])


def _layer0_sample_kernel(start_pos, x_ref, buf_ref, g_ref, wp_ref, ps_ref, wup_ref, wdn_ref,
                          y_ref, u_ref, x1_scr, h_scr, acc_scr):
    n_t, n_b, _ = x_ref.shape
    g = g_ref[...]

    @pl.when(pl.program_id(0) == 0)
    def _():
        _pool_mixer_time_major(start_pos, x_ref, buf_ref, g, wp_ref, ps_ref, u_ref, x1_scr, h_scr)
        acc_scr[...] = jnp.zeros(acc_scr.shape, F32)

    def emit(y):
        for t in range(n_t):
            y_ref[t] = y[t * n_b:(t + 1) * n_b]

    _streamed_mlp_step(g, wup_ref, wdn_ref, x1_scr, h_scr, acc_scr, emit)


def _pool_mixer_time_major(start_pos, x_ref, buf_ref, g, wp_ref, ps_ref, u_ref, x1_ref, h_ref):
    n_t = x_ref.shape[0]
    pool_ch = wp_ref.shape[1]
    xs = [x_ref[t] for t in range(n_t)]
    us = [_rms(xt, g[0:1]) for xt in xs]
    ext = [buf_ref[j] for j in range(POOL_BUF)] + us
    parts = []
    for gi, w in enumerate(POOL_WINDOWS):
        cs = slice(gi * pool_ch, (gi + 1) * pool_ch)
        rows = []
        for t in range(n_t):
            s = us[t][:, cs]
            for j in range(1, w):
                s = s + ext[POOL_BUF + t - j][:, cs]
            cnt = float(min(w, start_pos + t + 1))
            rows.append(s / cnt - us[t][:, cs])
        pooled = jnp.concatenate(rows, axis=0)
        parts.append(jnp.dot(pooled.astype(BF16), wp_ref[gi], preferred_element_type=F32))
    mix = jnp.concatenate(parts, axis=1) * ps_ref[...]
    x = jnp.concatenate(xs, axis=0)
    x1 = x + _rms(mix, g[1:2])
    x1_ref[...] = x1
    h_ref[...] = _rms(x1, g[2:3]).astype(BF16)
    for t in range(n_t):
        u_ref[t] = us[t]


def _layer0_sample(x_tm, buf_tm, start_pos, gains, w_pool, pool_scale, w_up, w_down, layer):
    assert start_pos + 1 >= max(POOL_WINDOWS) and buf_tm.shape[0] == POOL_BUF
    n_tok, d = x_tm.shape[0] * x_tm.shape[1], x_tm.shape[2]
    n_chunks, weight_specs = _streamed_mlp_specs(w_up, w_down, layer)
    return pl.pallas_call(
        functools.partial(_layer0_sample_kernel, start_pos),
        grid=(n_chunks,),
        in_specs=[_const_spec(a.shape) for a in (x_tm, buf_tm, gains, w_pool, pool_scale)]
        + weight_specs,
        out_specs=[pl.BlockSpec(x_tm.shape, lambda c: (0, 0, 0))] * 2,
        out_shape=[jax.ShapeDtypeStruct(x_tm.shape, F32), jax.ShapeDtypeStruct(x_tm.shape, F32)],
        scratch_shapes=[pltpu.VMEM((n_tok, d), F32), pltpu.VMEM((n_tok, d), BF16),
                        pltpu.VMEM((n_tok, d), F32)],
        compiler_params=pltpu.CompilerParams(
            dimension_semantics=("arbitrary",), vmem_limit_bytes=VMEM_LIMIT),
        name="layer0_sample",
    )(x_tm, buf_tm, gains, w_pool, pool_scale, w_up, w_down)


def _rope(x, cos, sin_signed):
    lane = lax.broadcasted_iota(jnp.int32, (x.shape[0], LANES), 1)
    first_half = (lane & (HEAD_DIM // 2)) == 0
    out = []
    for c in range(x.shape[1] // LANES):
        xc = x[:, c * LANES:(c + 1) * LANES]
        partner = jnp.where(first_half,
                            pltpu.roll(xc, LANES - HEAD_DIM // 2, 1),
                            pltpu.roll(xc, HEAD_DIM // 2, 1))
        out.append(xc * cos + partner * sin_signed)
    return out


def _qkv_kernel(dilations, first_kept_tile, transpose_kept,
                x_ref, gq_ref, gkv_ref, wq_ref, wkv_ref, cos_ref, sin_ref, cos_step_ref, sin_step_ref,
                q0_ref, q1_ref, q2_ref, k0_ref, k1_ref, k2_ref, v0_ref, v1_ref, v2_ref,
                kf_ref, vf_ref, stage_scr):
    i = pl.program_id(1)
    x = x_ref[0]
    cos_in, sin_in = cos_ref[...], sin_ref[...]
    cos_at, sin_at = cos_step_ref[0:1, :], sin_step_ref[0:1, :]
    cos = cos_in * cos_at - sin_in * sin_at
    sin_signed = sin_in * cos_at + cos_in * sin_at
    chunks_per_group = GROUP_Q_WIDTH // LANES

    def emit(out_ref, dilation, chunks):
        if dilation == 1:
            for c, chunk in enumerate(chunks):
                out_ref[0, :, c * LANES:(c + 1) * LANES] = chunk.astype(BF16)
            return
        for c, chunk in enumerate(chunks):
            stage_scr[c] = chunk
        for r in range(dilation):
            rows = _subseq_rows(stage_scr.shape[1], r, dilation)
            for c in range(len(chunks)):
                out_ref[r, :, c * LANES:(c + 1) * LANES] = stage_scr[c, rows, :].astype(BF16)

    u = _rms(x, gq_ref[...]).astype(BF16)
    q = jnp.dot(u, wq_ref[...], preferred_element_type=F32)
    scale = HEAD_DIM ** -0.5
    q_chunks = _rope(q, cos * scale, sin_signed * scale)
    for gi, q_ref in enumerate((q0_ref, q1_ref, q2_ref)):
        emit(q_ref, dilations[gi], q_chunks[gi * chunks_per_group:(gi + 1) * chunks_per_group])

    un = _rms(x, gkv_ref[...]).astype(BF16)
    kv = jnp.dot(un, wkv_ref[...], preferred_element_type=F32)
    k_chunks = _rope(kv[:, :KV_WIDTH], cos, sin_signed)
    v_chunks = [kv[:, KV_WIDTH + gi * LANES:KV_WIDTH + (gi + 1) * LANES] for gi in range(N_GROUPS)]
    for gi, (k_ref, v_ref) in enumerate(((k0_ref, v0_ref), (k1_ref, v1_ref), (k2_ref, v2_ref))):
        emit(k_ref, dilations[gi], [k_chunks[gi]])
        emit(v_ref, dilations[gi], [v_chunks[gi]])

    @pl.when(i >= first_kept_tile)
    def _():
        for gi in range(N_GROUPS):
            cs = slice(gi * LANES, (gi + 1) * LANES)
            if transpose_kept:
                kf_ref[0, cs, :] = k_chunks[gi].T
                vf_ref[0, cs, :] = v_chunks[gi].T
            else:
                kf_ref[0, :, cs] = k_chunks[gi]
                vf_ref[0, :, cs] = v_chunks[gi]


def _qkv(x, g_q, g_kv, w_q, w_kv, rope, tm, keep_rows, dilations, transpose_kept):
    b, t, d = x.shape
    cos, sin_signed, cos_step, sin_step = rope
    assert t % tm == 0 and keep_rows % tm == 0 and all(tm % (16 * dil) == 0 for dil in dilations)
    first_kept_tile = (t - keep_rows) // tm
    kept_block = lambda bi, i: jnp.maximum(i - first_kept_tile, 0)
    if transpose_kept:
        kept = pl.BlockSpec((1, KV_WIDTH, tm), lambda bi, i: (bi, 0, kept_block(bi, i)))
        kept_shape = jax.ShapeDtypeStruct((b, KV_WIDTH, keep_rows), F32)
    else:
        kept = pl.BlockSpec((1, tm, KV_WIDTH), lambda bi, i: (bi, kept_block(bi, i), 0))
        kept_shape = jax.ShapeDtypeStruct((b, keep_rows, KV_WIDTH), F32)
    sub_shape = lambda dil, width: jax.ShapeDtypeStruct((b, dil, t // dil, width), BF16)
    q_specs = [_subseq_spec(dil, tm, GROUP_Q_WIDTH) for dil in dilations]
    kv_specs = [_subseq_spec(dil, tm, GROUP_KV_WIDTH) for dil in dilations]
    q_shapes = [sub_shape(dil, GROUP_Q_WIDTH) for dil in dilations]
    kv_shapes = [sub_shape(dil, GROUP_KV_WIDTH) for dil in dilations]
    return pl.pallas_call(
        functools.partial(_qkv_kernel, tuple(dilations), first_kept_tile, transpose_kept),
        grid=(b, t // tm),
        in_specs=[
            pl.BlockSpec((1, tm, d), lambda bi, i: (bi, i, 0)),
            _const_spec(g_q.shape), _const_spec(g_kv.shape),
            _const_spec(w_q.shape), _const_spec(w_kv.shape),
            _const_spec(cos.shape), _const_spec(sin_signed.shape),
            pl.BlockSpec((None,) + cos_step.shape[1:], lambda bi, i: (i, 0, 0)),
            pl.BlockSpec((None,) + sin_step.shape[1:], lambda bi, i: (i, 0, 0)),
        ],
        out_specs=q_specs + kv_specs + kv_specs + [kept, kept],
        out_shape=q_shapes + kv_shapes + kv_shapes + [kept_shape, kept_shape],
        scratch_shapes=[pltpu.VMEM((GROUP_Q_WIDTH // LANES, tm, LANES), F32)],
        compiler_params=pltpu.CompilerParams(
            dimension_semantics=("arbitrary", "arbitrary"), vmem_limit_bytes=VMEM_LIMIT),
        name="qkv_rope",
    )(x, g_q, g_kv, w_q, w_kv, cos, sin_signed, cos_step, sin_step)


def _rope_tables(offsets, tile_starts):
    half = HEAD_DIM // 2
    inv = jnp.tile(ROPE_THETA ** (-jnp.arange(0, HEAD_DIM, 2, dtype=F32) / HEAD_DIM), LANES // half)
    sign = jnp.where((jnp.arange(LANES) // half) % 2 == 0, -1.0, 1.0).astype(F32)
    tables = []
    for positions in (offsets, tile_starts):
        ang = positions.astype(F32)[:, None] * inv[None, :]
        tables += [jnp.cos(ang), jnp.sin(ang) * sign[None, :]]
    rows8 = lambda a: jnp.broadcast_to(a[:, None, :], (a.shape[0], 8, LANES))
    return tables[0], tables[1], rows8(tables[2]), rows8(tables[3])


def _stat_lane(c, k):
    return (1 - c) * HEAD_DIM + k


def _band_attention_kernel(seg_rows, tiles_per_subseq, q_ref, kp_ref, kc_ref, vp_ref, vc_ref,
                           o_ref, stat_ref, k_scr, v_scr, bias_scr, s_scr, p_scr):
    n_seg = q_ref.shape[0] // seg_rows
    bands_per_seg = seg_rows // BAND
    n_bands = n_seg * bands_per_seg
    step = pl.program_id(1)
    lane = lax.broadcasted_iota(jnp.int32, (1, LANES), 1)
    for c in range(KV_PER_GROUP):
        own = (lane // HEAD_DIM) == c
        for u in range(n_seg):
            base = u * (seg_rows + BAND)
            for dst, k_rows, v_rows in (
                    (slice(base, base + BAND), kp_ref[...], vp_ref[...]),
                    (slice(base + BAND, base + BAND + seg_rows),
                     kc_ref[u * seg_rows:(u + 1) * seg_rows, :],
                     vc_ref[u * seg_rows:(u + 1) * seg_rows, :])):
                k_scr[c, dst, :] = jnp.where(own, k_rows, jnp.zeros((), BF16))
                v_scr[c, dst, :] = jnp.where(own, v_rows, jnp.ones((), BF16))
    stat_ref[...] = jnp.ones(stat_ref.shape, F32)

    row = lax.broadcasted_iota(jnp.int32, (BAND, 2 * BAND), 0)
    col = lax.broadcasted_iota(jnp.int32, (BAND, 2 * BAND), 1)
    band_bias = jnp.where((col >= row) & (col <= row + BAND), 0.0, NEG_BIG)
    bias_scr[0] = band_bias
    bias_scr[1] = band_bias + jnp.where(col < BAND, NEG_BIG, 0.0)
    starts_subseq = True if n_seg > 1 else lax.rem(step, tiles_per_subseq) == 0

    def rows_of(g):
        return pl.ds(g * BAND if isinstance(g, int) else pl.multiple_of(g * BAND, BAND), BAND)

    def key_rows_of(g):
        staged = g + g // bands_per_seg
        start = staged * BAND if isinstance(g, int) else pl.multiple_of(staged * BAND, BAND)
        return pl.ds(start, 2 * BAND)

    def scores(g, slot):
        yield KV_PER_GROUP
        q = jnp.concatenate(
            [q_ref[rows_of(g), k * LANES:(k + 1) * LANES] for k in range(Q_PER_KV)], axis=0)
        for c in range(KV_PER_GROUP):
            s_scr[slot, c] = lax.dot_general(q, k_scr[c, key_rows_of(g), :],
                                             (((1,), (1,)), ((), ())),
                                             preferred_element_type=F32)
            yield

    def softmax(g, slot):
        yield KV_PER_GROUP * Q_PER_KV
        first = jnp.logical_and(starts_subseq, g % bands_per_seg == 0).astype(jnp.int32)
        for c in range(KV_PER_GROUP):
            for k in range(Q_PER_KV):
                part = pl.ds(k * BAND, BAND)
                s = s_scr[slot, c, part, :] + bias_scr[first]
                m = jnp.max(s, axis=1, keepdims=True)
                p_scr[slot, c, part, :] = jnp.exp(s - m).astype(BF16)
                stat_ref[rows_of(g), pl.ds(_stat_lane(c, k) + STAT_MAX_OFFSET, 1)] = m
                yield

    def weighted_values(g, slot):
        yield KV_PER_GROUP + Q_PER_KV + 1
        ov = []
        for c in range(KV_PER_GROUP):
            ov.append(jnp.dot(p_scr[slot, c], v_scr[c, key_rows_of(g), :],
                              preferred_element_type=F32))
            yield
        low = lane < HEAD_DIM
        sums = None
        for k in range(Q_PER_KV):
            part = slice(k * BAND, (k + 1) * BAND)
            o_ref[rows_of(g), k * LANES:(k + 1) * LANES] = (
                jnp.where(low, ov[0][part], ov[1][part]).astype(o_ref.dtype))
            sums_k = jnp.where(low, ov[1][part], ov[0][part])
            sums = sums_k if sums is None else jnp.where((lane % HEAD_DIM) == k, sums_k, sums)
            yield
        for c in range(KV_PER_GROUP):
            lanes = pl.ds(_stat_lane(c, 0), Q_PER_KV)
            stat_ref[rows_of(g), lanes] = sums[:, _stat_lane(c, 0):_stat_lane(c, 0) + Q_PER_KV]
        yield

    _interleave(scores(0, 0))
    _interleave(softmax(0, 0), scores(1, 1))

    def steady(t, carry):
        for g, slot in ((2 * t + 1, 1), (2 * t + 2, 0)):
            _interleave(weighted_values(g - 1, 1 - slot), softmax(g, slot),
                        scores(g + 1, 1 - slot))
        return carry

    assert n_bands % 2 == 0
    lax.fori_loop(0, (n_bands - 2) // 2, steady, 0)
    last = n_bands - 1
    _interleave(weighted_values(last - 1, (last - 1) % 2), softmax(last, last % 2))
    _interleave(weighted_values(last, last % 2))


def _band_attention(q, k, v, block_rows):
    b, dilation, sub, _ = q.shape
    seg_rows = min(sub, block_rows)
    total = dilation * sub
    assert total % block_rows == 0 and block_rows % seg_rows == 0 and sub % seg_rows == 0
    assert seg_rows % BAND == 0 and block_rows >= 2 * BAND
    n_seg = block_rows // seg_rows
    flat = lambda a: a.reshape(b, total, a.shape[-1])
    cur = lambda width: pl.BlockSpec((None, block_rows, width), lambda bi, i: (bi, i, 0))
    prev = pl.BlockSpec((None, BAND, GROUP_KV_WIDTH),
                        lambda bi, i: (bi, jnp.maximum(i * (block_rows // BAND) - 1, 0), 0))
    staged_rows = n_seg * (seg_rows + BAND)
    o, stat = pl.pallas_call(
        functools.partial(_band_attention_kernel, seg_rows, sub // seg_rows),
        grid=(b, total // block_rows),
        in_specs=[cur(GROUP_Q_WIDTH), prev, cur(GROUP_KV_WIDTH), prev, cur(GROUP_KV_WIDTH)],
        out_specs=[cur(GROUP_Q_WIDTH), cur(STAT_LANES)],
        out_shape=[jax.ShapeDtypeStruct((b, total, GROUP_Q_WIDTH), BF16),
                   jax.ShapeDtypeStruct((b, total, STAT_LANES), F32)],
        scratch_shapes=[pltpu.VMEM((KV_PER_GROUP, staged_rows, GROUP_KV_WIDTH), BF16),
                        pltpu.VMEM((KV_PER_GROUP, staged_rows, GROUP_KV_WIDTH), BF16),
                        pltpu.VMEM((2, BAND, 2 * BAND), F32),
                        pltpu.VMEM((2, KV_PER_GROUP, Q_PER_KV * BAND, 2 * BAND), F32),
                        pltpu.VMEM((2, KV_PER_GROUP, Q_PER_KV * BAND, 2 * BAND), BF16)],
        compiler_params=pltpu.CompilerParams(
            dimension_semantics=("arbitrary", "arbitrary"), vmem_limit_bytes=VMEM_LIMIT),
        name=f"band_attention_d{dilation}",
    )(flat(q), flat(k), flat(k), flat(v), flat(v))
    return (o.reshape(b, dilation, sub, GROUP_Q_WIDTH), stat.reshape(b, dilation, sub, STAT_LANES))


SAMPLE_Q_ROWS = 16
NEW_KEY_ROWS = 8
SAMPLE_SEQS_PER_STEP = 4


def _sample_attention_kernel(n_new, q_ref, kn_ref, vn_ref, kc0, kc1, kc2, vc0, vc1, vc2,
                             o_ref, lse_ref):
    n_seq = q_ref.shape[0]
    n_rows = q_ref.shape[2]
    contract_last = (((1,), (1,)), ((), ()))
    cache_refs = ((kc0, vc0), (kc1, vc1), (kc2, vc2))
    bias_c, bias_n = [], []
    for gi, (kc_ref, _) in enumerate(cache_refs):
        dmask = ATTN_DILATIONS[gi] - 1
        n_cache = kc_ref.shape[3]
        assert n_cache == ATTN_WINDOWS[gi]
        t_c = lax.broadcasted_iota(jnp.int32, (n_rows, n_cache), 0) & (n_new - 1)
        j_c = lax.broadcasted_iota(jnp.int32, (n_rows, n_cache), 1)
        valid_c = (j_c >= t_c) & (((j_c - t_c) & dmask) == 0)
        t_n = lax.broadcasted_iota(jnp.int32, (n_rows, NEW_KEY_ROWS), 0) & (n_new - 1)
        j_n = lax.broadcasted_iota(jnp.int32, (n_rows, NEW_KEY_ROWS), 1)
        valid_n = (j_n <= t_n) & (((t_n - j_n) & dmask) == 0)
        bias_c.append(jnp.where(valid_c, 0.0, NEG_BIG))
        bias_n.append(jnp.where(valid_n, 0.0, NEG_BIG))

    def one_sequence(b):
        yield len(cache_refs) * KV_PER_GROUP
        kn_all = kn_ref[b]
        vn_all = vn_ref[b]
        for gi, (kc_ref, vc_ref) in enumerate(cache_refs):
            for c in range(KV_PER_GROUP):
                kvh = gi * KV_PER_GROUP + c
                ns = slice(kvh * HEAD_DIM, (kvh + 1) * HEAD_DIM)
                q = q_ref[b, kvh]
                kt = kc_ref[b, c].astype(BF16)
                vt = vc_ref[b, c].astype(BF16)
                s_c = jnp.dot(q, kt, preferred_element_type=F32) + bias_c[gi]
                s_n = lax.dot_general(q, kn_all[:, ns].astype(BF16), contract_last,
                                      preferred_element_type=F32) + bias_n[gi]
                m = jnp.maximum(jnp.max(s_c, axis=1, keepdims=True),
                                jnp.max(s_n, axis=1, keepdims=True))
                p_c = jnp.exp(s_c - m)
                p_n = jnp.exp(s_n - m)
                l = jnp.sum(p_c, axis=1, keepdims=True) + jnp.sum(p_n, axis=1, keepdims=True)
                o = (lax.dot_general(p_c.astype(BF16), vt, contract_last,
                                     preferred_element_type=F32)
                     + jnp.dot(p_n.astype(BF16), vn_all[:, ns].astype(BF16),
                               preferred_element_type=F32))
                o_ref[b, kvh] = o / l
                lse_ref[b, kvh] = m + jnp.log(l)
                yield

    _interleave(*[one_sequence(b) for b in range(n_seq)])


def _sample_attention(q16, k_new, v_new, cache_kt, cache_vt, n_new):
    b = q16.shape[0]
    n_kv = q16.shape[1]
    assert cache_kt.shape[3] == KV_WINDOW and n_new & (n_new - 1) == 0 and n_new <= NEW_KEY_ROWS

    per_step = SAMPLE_SEQS_PER_STEP
    assert b % per_step == 0

    def cache_spec(gi):
        cols = ATTN_WINDOWS[gi]
        last = KV_WINDOW // cols - 1
        return pl.BlockSpec((per_step, KV_PER_GROUP, HEAD_DIM, cols),
                            lambda bi: (bi, gi, 0, last))

    whole = lambda a: pl.BlockSpec((per_step,) + a.shape[1:],
                                   lambda bi: (bi,) + (0,) * (a.ndim - 1))
    o_shape = jax.ShapeDtypeStruct((b, n_kv, SAMPLE_Q_ROWS, HEAD_DIM), F32)
    lse_shape = jax.ShapeDtypeStruct((b, n_kv, SAMPLE_Q_ROWS, 1), F32)
    return pl.pallas_call(
        functools.partial(_sample_attention_kernel, n_new),
        grid=(b // per_step,),
        in_specs=[whole(q16), whole(k_new), whole(v_new)]
        + [cache_spec(gi) for gi in range(N_GROUPS)] * 2,
        out_specs=[whole(o_shape), whole(lse_shape)],
        out_shape=[o_shape, lse_shape],
        compiler_params=pltpu.CompilerParams(
            dimension_semantics=("arbitrary",), vmem_limit_bytes=VMEM_LIMIT),
        name="sample_attention",
    )(q16, k_new, v_new, cache_kt, cache_kt, cache_kt, cache_vt, cache_vt, cache_vt)


def _attn_mix_steps(dilations, x_ref, o_refs, s_refs, g, wo_ref, o_scr, stat_scr, a_scr,
                    x1_ref, h_ref):
    tm = x_ref.shape[0]
    chunks_per_group = GROUP_Q_WIDTH // LANES
    yield 2 * N_GROUPS + 1 + 2 * len(_row_blocks(tm))
    for gi, (o_ref, s_ref) in enumerate(zip(o_refs, s_refs)):
        for r in range(dilations[gi]):
            rows = _subseq_rows(tm, r, dilations[gi])
            stat_scr[gi, rows, :] = s_ref[r]
            for c in range(chunks_per_group):
                o_scr[gi * chunks_per_group + c, rows, :] = (
                    o_ref[r, :, c * LANES:(c + 1) * LANES].astype(F32))
        yield
    sums = [stat_scr[gi] for gi in range(N_GROUPS)]
    maxes = [pltpu.roll(sm, STAT_LANES - STAT_MAX_OFFSET, 1) for sm in sums]
    top = jnp.maximum(jnp.maximum(maxes[0], maxes[1]), maxes[2])
    es = [jnp.exp(mx - top) for mx in maxes]
    den = sums[0] * es[0] + sums[1] * es[1] + sums[2] * es[2]
    slot_id = lax.broadcasted_iota(jnp.int32, (tm, GROUP_Q_WIDTH), 1) // HEAD_DIM
    for gi in range(N_GROUPS):
        scale = es[gi] / den
        wide = jnp.zeros((tm, GROUP_Q_WIDTH), F32)
        for slot in range(HEADS_PER_GROUP):
            stat_lane = _stat_lane(slot % KV_PER_GROUP, slot // KV_PER_GROUP)
            wide = jnp.where(slot_id == slot, scale[:, stat_lane:stat_lane + 1], wide)
        for c in range(chunks_per_group):
            ci = gi * chunks_per_group + c
            a_scr[:, ci * LANES:(ci + 1) * LANES] = (
                o_scr[ci] * wide[:, c * LANES:(c + 1) * LANES]).astype(BF16)
        yield
    mix = jnp.dot(a_scr[...], wo_ref[...], preferred_element_type=F32)
    yield
    tail = _chain(
        _residual_norm_steps(x1_ref, lambda rows: x_ref[rows, :], lambda rows: mix[rows], g[1:2]),
        _norm_cast_steps(h_ref, lambda rows: x1_ref[rows, :], g[2:3]))
    next(tail)
    yield from tail


def _attn_mix_scratch(tm):
    return [pltpu.VMEM((Q_WIDTH // LANES, tm, LANES), F32),
            pltpu.VMEM((N_GROUPS, tm, STAT_LANES), F32),
            pltpu.VMEM((tm, Q_WIDTH), BF16)]


def _layer1_single_kernel(dilations, x_ref, o0_ref, o1_ref, o2_ref, s0_ref, s1_ref, s2_ref, g_ref,
                          wo_ref, wup_ref, wdn_ref, y_ref, o_scr, stat_scr, a_scr,
                          x1_scr, h_scr, acc_scr):
    g = g_ref[...]

    @pl.when(pl.program_id(0) == 0)
    def _():
        _interleave(_attn_mix_steps(dilations, x_ref, (o0_ref, o1_ref, o2_ref),
                                    (s0_ref, s1_ref, s2_ref), g, wo_ref, o_scr, stat_scr, a_scr,
                                    x1_scr, h_scr))
        acc_scr[...] = jnp.zeros(acc_scr.shape, F32)

    def emit(y):
        y_ref[...] = y

    _streamed_mlp_step(g, wup_ref, wdn_ref, x1_scr, h_scr, acc_scr, emit)


def _layer1_prompt_kernel(dilations, x_ref, o0_ref, o1_ref, o2_ref, s0_ref, s1_ref, s2_ref,
                          g_ref, wo_ref, wup_ref, wdn_ref, y_ref,
                          o_scr, stat_scr, a_scr, x1_scr, h_scr, acc_scr):
    s = pl.program_id(0)
    n_tiles = pl.num_programs(0) - 2
    g = g_ref[...]

    def prepare():
        return _attn_mix_steps(dilations, x_ref, (o0_ref, o1_ref, o2_ref),
                               (s0_ref, s1_ref, s2_ref), g, wo_ref, o_scr, stat_scr, a_scr,
                               x1_scr.at[lax.rem(s, 3)], h_scr.at[lax.rem(s, 2)])

    def finish():
        x1_slot, acc_slot = lax.rem(s + 1, 3), lax.rem(s, 2)
        return _residual_norm_steps(y_ref, lambda rows: x1_scr[x1_slot, rows, :],
                                    lambda rows: acc_scr[acc_slot, rows, :], g[3:4])

    @pl.when(s == 0)
    def _():
        _interleave(prepare())
        x1_scr[2] = jnp.zeros(x1_scr.shape[1:], F32)
        acc_scr[1] = jnp.zeros(acc_scr.shape[1:], F32)

    @pl.when(jnp.logical_and(s > 0, s <= n_tiles))
    def _():
        mlp_slot = lax.rem(s + 1, 2)

        def store_acc(acc):
            acc_scr[mlp_slot] = acc

        _interleave(
            finish(),
            _mlp_matmul_steps(lambda: h_scr[mlp_slot], wup_ref, wdn_ref, store_acc),
            prepare())

    @pl.when(s == n_tiles + 1)
    def _():
        _interleave(finish())


def _layer1(x, os, stats, gains, w_o, w_up, w_down, layer, tm):
    b, t, d = x.shape
    dilations = tuple(o.shape[1] for o in os)
    tiles_per_seq = t // tm
    n_tiles = b * tiles_per_seq
    assert t % tm == 0 and all(tm % (16 * dil) == 0 for dil in dilations)
    x2 = x.reshape(b * t, d)
    weights = (gains, w_o, w_up, w_down)
    weight_specs = [_const_spec(gains.shape), _const_spec(w_o.shape),
                    _layer_spec(w_up, layer), _layer_spec(w_down, layer)]

    def tile_specs(tile):
        sub = lambda dil, width: pl.BlockSpec(
            (None, dil, tm // dil, width),
            lambda s: (tile(s) // tiles_per_seq, 0, tile(s) % tiles_per_seq, 0))
        return ([pl.BlockSpec((tm, d), lambda s: (tile(s), 0))]
                + [sub(dil, GROUP_Q_WIDTH) for dil in dilations]
                + [sub(dil, STAT_LANES) for dil in dilations])

    if n_tiles == 1:
        n_chunks, mlp_specs = _streamed_mlp_specs(w_up, w_down, layer)
        y = pl.pallas_call(
            functools.partial(_layer1_single_kernel, dilations),
            grid=(n_chunks,),
            in_specs=tile_specs(lambda s: 0) + weight_specs[:2] + mlp_specs,
            out_specs=pl.BlockSpec((tm, d), lambda s: (0, 0)),
            out_shape=jax.ShapeDtypeStruct((b * t, d), F32),
            scratch_shapes=_attn_mix_scratch(tm) + [
                pltpu.VMEM((tm, d), F32), pltpu.VMEM((tm, d), BF16), pltpu.VMEM((tm, d), F32)],
            compiler_params=pltpu.CompilerParams(
                dimension_semantics=("arbitrary",), vmem_limit_bytes=VMEM_LIMIT),
            name="layer1_single_tile",
        )(x2, *os, *stats, *weights)
        return y.reshape(b, t, d)

    scratch = _attn_mix_scratch(tm) + [
        pltpu.VMEM((3, tm, d), F32), pltpu.VMEM((2, tm, d), BF16), pltpu.VMEM((2, tm, d), F32)]
    y = pl.pallas_call(
        functools.partial(_layer1_prompt_kernel, dilations),
        grid=(n_tiles + 2,),
        in_specs=tile_specs(lambda s: jnp.minimum(s, n_tiles - 1)) + weight_specs,
        out_specs=pl.BlockSpec((tm, d), lambda s: (jnp.maximum(s - 2, 0), 0)),
        out_shape=jax.ShapeDtypeStruct((b * t, d), F32),
        scratch_shapes=scratch,
        compiler_params=pltpu.CompilerParams(
            dimension_semantics=("arbitrary",), vmem_limit_bytes=VMEM_LIMIT),
        name="layer1_prompt",
    )(x2, *os, *stats, *weights)
    return y.reshape(b, t, d)


PROMPT_TILE = 512
LAYER1_TILE = 512
QKV_TILE = 1024
ATTN_BLOCK_ROWS = 2048


def kernel(x_prompt, x_sample, cache_pool, cache_k, cache_v, norm_gains, kv_norm_gain, w_pool,
           pool_scale, w_q, w_o, w_kv, w_up, w_down):
    depth = norm_gains.shape[0]
    assert depth == 2 and cache_pool.shape[0] == 1 and w_q.shape[0] == 1
    bp, tp, d = x_prompt.shape
    bs, ts, _ = x_sample.shape

    g0, g1 = norm_gains[0], norm_gains[1]
    g1_q = g1[0:1]
    g_kv = kv_norm_gain[None, :]
    wp = w_pool[0].astype(BF16)
    ps = pool_scale[0][None, :]
    wq = w_q[0].reshape(d, N_GROUPS, KV_PER_GROUP, Q_PER_KV, HEAD_DIM).transpose(0, 1, 3, 2, 4)
    wq = wq.reshape(d, Q_WIDTH).astype(BF16)
    wo = w_o[0].reshape(N_GROUPS, KV_PER_GROUP, Q_PER_KV, HEAD_DIM, d).transpose(0, 2, 1, 3, 4)
    wo = wo.reshape(Q_WIDTH, d).astype(BF16)
    wkv = w_kv.astype(BF16)
    wup, wdn = w_up.astype(BF16), w_down.astype(BF16)

    keep = min(KV_WINDOW, tp)
    xp1, utail = _layer0_prompt(x_prompt, g0, wp, ps, wup, wdn, 0, PROMPT_TILE)
    rope_p = _rope_tables(jnp.arange(QKV_TILE), jnp.arange(tp // QKV_TILE) * QKV_TILE)
    qkv_p = _qkv(xp1, g1_q, g_kv, wq, wkv, rope_p, QKV_TILE, keep, ATTN_DILATIONS, True)
    os_p, stats_p = [], []
    for gi, dil in enumerate(ATTN_DILATIONS):
        o, stat = _band_attention(qkv_p[gi], qkv_p[N_GROUPS + gi], qkv_p[2 * N_GROUPS + gi],
                                  ATTN_BLOCK_ROWS)
        os_p.append(o)
        stats_p.append(stat)
    y_prompt = _layer1(xp1, os_p, stats_p, g1, wo, wup, wdn, 1, LAYER1_TILE)
    pool_prompt = utail[:, POOL_HALO - POOL_BUF:][None]
    k_prompt = qkv_p[-2].reshape(bp, N_KV_HEADS, HEAD_DIM, keep).transpose(0, 3, 1, 2)
    v_prompt = qkv_p[-1].reshape(bp, N_KV_HEADS, HEAD_DIM, keep).transpose(0, 3, 1, 2)

    n_tok = ts * bs
    xs_tm = jnp.swapaxes(x_sample, 0, 1)
    buf_tm = jnp.swapaxes(cache_pool[0], 0, 1)
    xs1_tm, us_tm = _layer0_sample(xs_tm, buf_tm, PAST_LEN, g0, wp, ps, wup, wdn, 0)
    xs1 = xs1_tm.reshape(1, n_tok, d)
    rope_s = _rope_tables(jnp.arange(n_tok) // bs, jnp.full((1,), PAST_LEN))
    qkv_s = _qkv(xs1, g1_q, g_kv, wq, wkv, rope_s, n_tok, n_tok, (1,) * N_GROUPS, False)
    q_s = jnp.concatenate([q[0, 0] for q in qkv_s[:N_GROUPS]], axis=-1)
    q16 = q_s.reshape(ts, bs, N_GROUPS, Q_PER_KV, KV_PER_GROUP, HEAD_DIM)
    q16 = q16.transpose(1, 2, 4, 3, 0, 5).reshape(bs, N_KV_HEADS, Q_PER_KV, ts, HEAD_DIM)
    q16 = jnp.pad(q16, ((0, 0), (0, 0), (0, SAMPLE_Q_ROWS // ts - Q_PER_KV), (0, 0), (0, 0)))
    q16 = q16.reshape(bs, N_KV_HEADS, SAMPLE_Q_ROWS, HEAD_DIM)
    k_s = jnp.swapaxes(qkv_s[-2].reshape(ts, bs, KV_WIDTH), 0, 1)
    v_s = jnp.swapaxes(qkv_s[-1].reshape(ts, bs, KV_WIDTH), 0, 1)
    pad_new = ((0, 0), (0, NEW_KEY_ROWS - ts), (0, 0))
    o16, lse16 = _sample_attention(
        q16, jnp.pad(k_s, pad_new), jnp.pad(v_s, pad_new),
        cache_k.transpose(0, 2, 3, 1), cache_v.transpose(0, 2, 3, 1), ts)
    heads_padded = SAMPLE_Q_ROWS // ts
    o_s = o16.reshape(bs, N_GROUPS, KV_PER_GROUP, heads_padded, ts, HEAD_DIM)[:, :, :, :Q_PER_KV]
    o_s = o_s.transpose(4, 0, 1, 3, 2, 5).reshape(1, 1, n_tok, N_GROUPS, GROUP_Q_WIDTH).astype(BF16)
    lse_s = lse16.reshape(bs, N_GROUPS, KV_PER_GROUP, heads_padded, ts)[:, :, :, :Q_PER_KV]
    lse_s = lse_s.transpose(4, 0, 1, 2, 3).reshape(1, 1, n_tok, N_GROUPS, KV_PER_GROUP, Q_PER_KV)
    ones = lambda n: jnp.ones((1, 1, n_tok, N_GROUPS, n), F32)
    max_lane = [_stat_lane(c, 0) + STAT_MAX_OFFSET for c in range(KV_PER_GROUP)]
    assert max_lane[1] < max_lane[0]
    stat_s = jnp.concatenate(
        [ones(max_lane[1]), lse_s[..., 1, :],
         ones(max_lane[0] - max_lane[1] - Q_PER_KV), lse_s[..., 0, :],
         ones(STAT_LANES - max_lane[0] - Q_PER_KV)], axis=-1)
    ys_tm = _layer1(xs1, [o_s[:, :, :, gi] for gi in range(N_GROUPS)],
                    [stat_s[:, :, :, gi] for gi in range(N_GROUPS)], g1, wo, wup, wdn, 1, n_tok)
    y_sample = jnp.swapaxes(ys_tm.reshape(ts, bs, d), 0, 1)
    u_s = jnp.swapaxes(us_tm, 0, 1)
    pool_sample = jnp.concatenate([cache_pool[0], u_s], axis=1)[:, -POOL_BUF:][None]
    k_sample = k_s.reshape(bs, ts, N_KV_HEADS, HEAD_DIM)
    v_sample = v_s.reshape(bs, ts, N_KV_HEADS, HEAD_DIM)

    return (y_prompt, y_sample, pool_prompt, k_prompt, v_prompt, pool_sample, k_sample, v_sample)
```

```python
import functools

import jax
import jax.numpy as jnp
from jax import lax
from jax.experimental import pallas as pl
from jax.experimental.pallas import tpu as pltpu

F32 = jnp.float32
BF16 = jnp.bfloat16

EPS = 1e-6
ROPE_THETA = 10000.0
PAST_LEN = 16384
POOL_WINDOWS = (2, 4, 8, 16)
POOL_BUF = max(POOL_WINDOWS) - 1
POOL_HALO = 16
HEAD_DIM = 64
ATTN_WINDOWS = (128, 512, 2048)
ATTN_DILATIONS = (1, 4, 16)
N_GROUPS = len(ATTN_WINDOWS)
KV_PER_GROUP = 2
Q_PER_KV = 3
HEADS_PER_GROUP = KV_PER_GROUP * Q_PER_KV
N_KV_HEADS = N_GROUPS * KV_PER_GROUP
GROUP_Q_WIDTH = HEADS_PER_GROUP * HEAD_DIM
GROUP_KV_WIDTH = KV_PER_GROUP * HEAD_DIM
Q_WIDTH = N_GROUPS * GROUP_Q_WIDTH
KV_WIDTH = N_GROUPS * GROUP_KV_WIDTH
KV_WINDOW = max(ATTN_WINDOWS)
BAND = 128
NEG_BIG = -1e30

LANES = 128
STAT_LANES = LANES
STAT_MAX_OFFSET = 8
VMEM_LIMIT = 56 * 1024 * 1024
FF_CHUNK = 512
NORM_ROW_BLOCKS = 4
POOL_NORM_BLOCKS = 1
POOL_FF_CHUNK = 1024
SAMPLE_FF_CHUNK = 1024
CAST_BLOCK_BYTES = 8 * 1024 * 1024

for _w, _d in zip(ATTN_WINDOWS, ATTN_DILATIONS):
    assert _w // _d == BAND and _w % _d == 0
assert GROUP_KV_WIDTH == LANES


def _rms(x, g):
    return x * lax.rsqrt(jnp.mean(x * x, axis=-1, keepdims=True) + EPS) * g


def _mlp_matmul_steps(load_h, wup_ref, wdn_ref, emit_acc, chunk=FF_CHUNK):
    n_chunks = wup_ref.shape[1] // chunk
    yield n_chunks
    h = load_h()
    acc = None
    for c in range(n_chunks):
        cs = slice(c * chunk, (c + 1) * chunk)
        a = jnp.dot(h, wup_ref[:, cs], preferred_element_type=F32)
        a = jnp.square(jnp.maximum(a, 0.0)).astype(BF16)
        part = jnp.dot(a, wdn_ref[cs, :], preferred_element_type=F32)
        acc = part if acc is None else acc + part
        if c + 1 == n_chunks:
            emit_acc(acc)
        yield


def _row_blocks(n_rows, n_blocks=NORM_ROW_BLOCKS):
    size = n_rows // n_blocks if n_rows % (8 * n_blocks) == 0 else n_rows
    return [slice(lo, lo + size) for lo in range(0, n_rows, size)]


def _residual_norm_steps(dst_ref, load_x, load_v, g, n_blocks=NORM_ROW_BLOCKS):
    blocks = _row_blocks(dst_ref.shape[0], n_blocks)
    yield len(blocks)
    for rows in blocks:
        dst_ref[rows, :] = load_x(rows) + _rms(load_v(rows), g)
        yield


def _norm_cast_steps(dst_ref, load_v, g, n_blocks=NORM_ROW_BLOCKS):
    blocks = _row_blocks(dst_ref.shape[0], n_blocks)
    yield len(blocks)
    for rows in blocks:
        dst_ref[rows, :] = _rms(load_v(rows), g).astype(dst_ref.dtype)
        yield


def _chain(*generators):
    counts = [next(gen) for gen in generators]
    yield sum(counts)
    for gen in generators:
        yield from gen


def _interleave(*generators):
    totals = [next(gen) for gen in generators]
    done = [0] * len(generators)
    while any(d < t for d, t in zip(done, totals)):
        i = min((i for i in range(len(generators)) if done[i] < totals[i]),
                key=lambda i: (done[i] + 1) / totals[i])
        next(generators[i])
        done[i] += 1


def _cast_kernel(x_ref, o_ref):
    o_ref[...] = x_ref[...].astype(o_ref.dtype)


def _to_bf16(w):
    layers, rows, cols = w.shape
    block_rows = min(rows, CAST_BLOCK_BYTES // (4 * cols))
    assert rows % block_rows == 0 and block_rows % 16 == 0
    spec = pl.BlockSpec((None, block_rows, cols), lambda l, i: (l, i, 0))
    return pl.pallas_call(
        _cast_kernel,
        grid=(layers, rows // block_rows),
        in_specs=[spec],
        out_specs=spec,
        out_shape=jax.ShapeDtypeStruct(w.shape, BF16),
        compiler_params=pltpu.CompilerParams(
            dimension_semantics=("arbitrary", "arbitrary"), vmem_limit_bytes=VMEM_LIMIT),
        name="cast_to_bf16",
    )(w)


def _const_spec(shape):
    zeros = (0,) * len(shape)
    return pl.BlockSpec(shape, lambda *_: zeros, pipeline_mode=pl.Buffered(1))


def _layer_spec(w, layer):
    index = (layer,) + (0,) * (w.ndim - 1)
    return pl.BlockSpec((None,) + w.shape[1:], lambda *_: index, pipeline_mode=pl.Buffered(1))


def _subseq_spec(dilation, rows, width):
    return pl.BlockSpec((None, dilation, rows // dilation, width), lambda bi, i: (bi, 0, i, 0))


def _subseq_rows(n_rows, r, dilation):
    n = n_rows // dilation
    return pl.ds(r, n, stride=dilation) if dilation > 1 else pl.ds(0, n)


def _pool_mixer_steps(x_ref, xh_ref, tile_in_seq, g, wp_ref, ps_ref, ext_ref, x1_ref, h_ref,
                      utail_ref=None):
    tm = x_ref.shape[0]
    pool_ch = wp_ref.shape[1]
    blocks = _row_blocks(tm, POOL_NORM_BLOCKS)
    yield 3 * len(blocks) + len(POOL_WINDOWS)
    uh = _rms(xh_ref[0:POOL_HALO, :], g[0:1]) * jnp.where(tile_in_seq > 0, 1.0, 0.0)
    ext_ref[0:POOL_HALO, :] = uh
    for rows in blocks:
        u = _rms(x_ref[rows, :], g[0:1])
        ext_ref[POOL_HALO + rows.start:POOL_HALO + rows.stop, :] = u
        if utail_ref is not None and rows.stop == tm:
            utail_ref[0] = u[u.shape[0] - POOL_HALO:, :]
        yield
    pos = tile_in_seq * tm + lax.broadcasted_iota(jnp.int32, (tm, 1), 0)
    parts = []
    for gi, w in enumerate(POOL_WINDOWS):
        cs = slice(gi * pool_ch, (gi + 1) * pool_ch)
        s = ext_ref[POOL_HALO:, cs]
        for j in range(1, w):
            s = s + ext_ref[POOL_HALO - j:POOL_HALO - j + tm, cs]
        cnt = jnp.minimum(w, pos + 1).astype(F32)
        pooled = s / cnt - ext_ref[POOL_HALO:, cs]
        parts.append(jnp.dot(pooled.astype(BF16), wp_ref[gi], preferred_element_type=F32))
        yield
    mix = jnp.concatenate(parts, axis=1) * ps_ref[...]
    tail = _chain(
        _residual_norm_steps(x1_ref, lambda rows: x_ref[rows, :], lambda rows: mix[rows], g[1:2],
                             POOL_NORM_BLOCKS),
        _norm_cast_steps(h_ref, lambda rows: x1_ref[rows, :], g[2:3], POOL_NORM_BLOCKS))
    next(tail)
    yield from tail


def _layer0_prompt_kernel(tiles_per_seq, x0_ref, xa_ref, xha_ref, xb_ref, xhb_ref, g_ref,
                          wp_ref, ps_ref, wup_ref, wdn_ref, y_ref, utail_ref,
                          x1_scr, h_scr, ext_scr):
    s = pl.program_id(0)
    n_tiles = 2 * pl.num_programs(0)
    tm = xa_ref.shape[0]
    g = g_ref[...]

    def prepare(x_ref, xh_ref, tile, slot, tail_ref):
        return _pool_mixer_steps(x_ref, xh_ref, lax.rem(tile, tiles_per_seq), g, wp_ref, ps_ref,
                                 ext_scr.at[slot], x1_scr.at[slot], h_scr.at[slot], tail_ref)

    @pl.when(s == 0)
    def _():
        _interleave(prepare(x0_ref, x0_ref, 0, 0, None))

    def mlp(slot):
        acc = []
        return _chain(
            _mlp_matmul_steps(lambda: h_scr[slot], wup_ref, wdn_ref, acc.append, POOL_FF_CHUNK),
            _residual_norm_steps(y_ref.at[pl.ds(slot * tm, tm)],
                                 lambda rows: x1_scr[slot, rows, :],
                                 lambda rows: acc[0][rows], g[3:4], POOL_NORM_BLOCKS))

    _interleave(mlp(0), prepare(xa_ref, xha_ref, 2 * s + 1, 1, utail_ref))
    _interleave(mlp(1), prepare(xb_ref, xhb_ref, jnp.minimum(2 * s + 2, n_tiles - 1), 0, None))


def _layer0_prompt(x, gains, w_pool, pool_scale, w_up, w_down, layer, tm):
    b, t, d = x.shape
    tiles_per_seq = t // tm
    n_tiles = b * tiles_per_seq
    assert t % tm == 0 and tm % POOL_HALO == 0 and tiles_per_seq % 2 == 0
    halo_per_tile = tm // POOL_HALO
    x2 = x.reshape(b * t, d)
    tile_a = lambda s: 2 * s + 1
    tile_b = lambda s: jnp.minimum(2 * s + 2, n_tiles - 1)
    tile_spec = lambda tile: pl.BlockSpec((tm, d), lambda s: (tile(s), 0))
    halo_spec = lambda tile: pl.BlockSpec((POOL_HALO, d),
                                          lambda s: (tile(s) * halo_per_tile - 1, 0))
    y, utail = pl.pallas_call(
        functools.partial(_layer0_prompt_kernel, tiles_per_seq),
        grid=(n_tiles // 2,),
        in_specs=[
            pl.BlockSpec((tm, d), lambda s: (0, 0), pipeline_mode=pl.Buffered(1)),
            tile_spec(tile_a), halo_spec(tile_a), tile_spec(tile_b), halo_spec(tile_b),
            _const_spec(gains.shape), _const_spec(w_pool.shape), _const_spec(pool_scale.shape),
            _layer_spec(w_up, layer), _layer_spec(w_down, layer),
        ],
        out_specs=[
            pl.BlockSpec((2 * tm, d), lambda s: (s, 0)),
            pl.BlockSpec((1, POOL_HALO, d), lambda s: (tile_a(s) // tiles_per_seq, 0, 0)),
        ],
        out_shape=[
            jax.ShapeDtypeStruct((b * t, d), F32),
            jax.ShapeDtypeStruct((b, POOL_HALO, d), F32),
        ],
        scratch_shapes=[pltpu.VMEM((2, tm, d), F32), pltpu.VMEM((2, tm, d), BF16),
                        pltpu.VMEM((2, tm + POOL_HALO, d), F32)],
        compiler_params=pltpu.CompilerParams(
            dimension_semantics=("arbitrary",), vmem_limit_bytes=VMEM_LIMIT),
        name="layer0_prompt",
    )(x2, x2, x2, x2, x2, gains, w_pool, pool_scale, w_up, w_down)
    return y.reshape(b, t, d), utail


def _streamed_mlp_step(g, wup_ref, wdn_ref, x1_scr, h_scr, acc_scr, emit):
    a = jnp.dot(h_scr[...], wup_ref[...], preferred_element_type=F32)
    a = jnp.square(jnp.maximum(a, 0.0)).astype(BF16)
    acc_scr[...] += jnp.dot(a, wdn_ref[...], preferred_element_type=F32)

    @pl.when(pl.program_id(0) == pl.num_programs(0) - 1)
    def _():
        emit(x1_scr[...] + _rms(acc_scr[...], g[3:4]))


def _streamed_mlp_specs(w_up, w_down, layer):
    assert w_up.shape[2] % SAMPLE_FF_CHUNK == 0
    return (w_up.shape[2] // SAMPLE_FF_CHUNK,
            [pl.BlockSpec((None, w_up.shape[1], SAMPLE_FF_CHUNK), lambda c: (layer, 0, c)),
             pl.BlockSpec((None, SAMPLE_FF_CHUNK, w_down.shape[2]), lambda c: (layer, c, 0))])


def _layer0_sample_kernel(start_pos, x_ref, buf_ref, g_ref, wp_ref, ps_ref, wup_ref, wdn_ref,
                          y_ref, u_ref, x1_scr, h_scr, acc_scr):
    n_t, n_b, _ = x_ref.shape
    g = g_ref[...]

    @pl.when(pl.program_id(0) == 0)
    def _():
        _pool_mixer_time_major(start_pos, x_ref, buf_ref, g, wp_ref, ps_ref, u_ref, x1_scr, h_scr)
        acc_scr[...] = jnp.zeros(acc_scr.shape, F32)

    def emit(y):
        for t in range(n_t):
            y_ref[t] = y[t * n_b:(t + 1) * n_b]

    _streamed_mlp_step(g, wup_ref, wdn_ref, x1_scr, h_scr, acc_scr, emit)


def _pool_mixer_time_major(start_pos, x_ref, buf_ref, g, wp_ref, ps_ref, u_ref, x1_ref, h_ref):
    n_t = x_ref.shape[0]
    pool_ch = wp_ref.shape[1]
    xs = [x_ref[t] for t in range(n_t)]
    us = [_rms(xt, g[0:1]) for xt in xs]
    ext = [buf_ref[j] for j in range(POOL_BUF)] + us
    parts = []
    for gi, w in enumerate(POOL_WINDOWS):
        cs = slice(gi * pool_ch, (gi + 1) * pool_ch)
        rows = []
        for t in range(n_t):
            s = us[t][:, cs]
            for j in range(1, w):
                s = s + ext[POOL_BUF + t - j][:, cs]
            cnt = float(min(w, start_pos + t + 1))
            rows.append(s / cnt - us[t][:, cs])
        pooled = jnp.concatenate(rows, axis=0)
        parts.append(jnp.dot(pooled.astype(BF16), wp_ref[gi], preferred_element_type=F32))
    mix = jnp.concatenate(parts, axis=1) * ps_ref[...]
    x = jnp.concatenate(xs, axis=0)
    x1 = x + _rms(mix, g[1:2])
    x1_ref[...] = x1
    h_ref[...] = _rms(x1, g[2:3]).astype(BF16)
    for t in range(n_t):
        u_ref[t] = us[t]


def _layer0_sample(x_tm, buf_tm, start_pos, gains, w_pool, pool_scale, w_up, w_down, layer):
    assert start_pos + 1 >= max(POOL_WINDOWS) and buf_tm.shape[0] == POOL_BUF
    n_tok, d = x_tm.shape[0] * x_tm.shape[1], x_tm.shape[2]
    n_chunks, weight_specs = _streamed_mlp_specs(w_up, w_down, layer)
    return pl.pallas_call(
        functools.partial(_layer0_sample_kernel, start_pos),
        grid=(n_chunks,),
        in_specs=[_const_spec(a.shape) for a in (x_tm, buf_tm, gains, w_pool, pool_scale)]
        + weight_specs,
        out_specs=[pl.BlockSpec(x_tm.shape, lambda c: (0, 0, 0))] * 2,
        out_shape=[jax.ShapeDtypeStruct(x_tm.shape, F32), jax.ShapeDtypeStruct(x_tm.shape, F32)],
        scratch_shapes=[pltpu.VMEM((n_tok, d), F32), pltpu.VMEM((n_tok, d), BF16),
                        pltpu.VMEM((n_tok, d), F32)],
        compiler_params=pltpu.CompilerParams(
            dimension_semantics=("arbitrary",), vmem_limit_bytes=VMEM_LIMIT),
        name="layer0_sample",
    )(x_tm, buf_tm, gains, w_pool, pool_scale, w_up, w_down)


def _rope(x, cos, sin_signed):
    lane = lax.broadcasted_iota(jnp.int32, (x.shape[0], LANES), 1)
    first_half = (lane & (HEAD_DIM // 2)) == 0
    out = []
    for c in range(x.shape[1] // LANES):
        xc = x[:, c * LANES:(c + 1) * LANES]
        partner = jnp.where(first_half,
                            pltpu.roll(xc, LANES - HEAD_DIM // 2, 1),
                            pltpu.roll(xc, HEAD_DIM // 2, 1))
        out.append(xc * cos + partner * sin_signed)
    return out


def _qkv_kernel(dilations, first_kept_tile, transpose_kept,
                x_ref, gq_ref, gkv_ref, wq_ref, wkv_ref, cos_ref, sin_ref, cos_step_ref, sin_step_ref,
                q0_ref, q1_ref, q2_ref, k0_ref, k1_ref, k2_ref, v0_ref, v1_ref, v2_ref,
                kf_ref, vf_ref, stage_scr):
    i = pl.program_id(1)
    x = x_ref[0]
    cos_in, sin_in = cos_ref[...], sin_ref[...]
    cos_at, sin_at = cos_step_ref[0:1, :], sin_step_ref[0:1, :]
    cos = cos_in * cos_at - sin_in * sin_at
    sin_signed = sin_in * cos_at + cos_in * sin_at
    chunks_per_group = GROUP_Q_WIDTH // LANES

    def emit(out_ref, dilation, chunks):
        if dilation == 1:
            for c, chunk in enumerate(chunks):
                out_ref[0, :, c * LANES:(c + 1) * LANES] = chunk.astype(BF16)
            return
        for c, chunk in enumerate(chunks):
            stage_scr[c] = chunk
        for r in range(dilation):
            rows = _subseq_rows(stage_scr.shape[1], r, dilation)
            for c in range(len(chunks)):
                out_ref[r, :, c * LANES:(c + 1) * LANES] = stage_scr[c, rows, :].astype(BF16)

    u = _rms(x, gq_ref[...]).astype(BF16)
    q = jnp.dot(u, wq_ref[...], preferred_element_type=F32)
    scale = HEAD_DIM ** -0.5
    q_chunks = _rope(q, cos * scale, sin_signed * scale)
    for gi, q_ref in enumerate((q0_ref, q1_ref, q2_ref)):
        emit(q_ref, dilations[gi], q_chunks[gi * chunks_per_group:(gi + 1) * chunks_per_group])

    un = _rms(x, gkv_ref[...]).astype(BF16)
    kv = jnp.dot(un, wkv_ref[...], preferred_element_type=F32)
    k_chunks = _rope(kv[:, :KV_WIDTH], cos, sin_signed)
    v_chunks = [kv[:, KV_WIDTH + gi * LANES:KV_WIDTH + (gi + 1) * LANES] for gi in range(N_GROUPS)]
    for gi, (k_ref, v_ref) in enumerate(((k0_ref, v0_ref), (k1_ref, v1_ref), (k2_ref, v2_ref))):
        emit(k_ref, dilations[gi], [k_chunks[gi]])
        emit(v_ref, dilations[gi], [v_chunks[gi]])

    @pl.when(i >= first_kept_tile)
    def _():
        for gi in range(N_GROUPS):
            cs = slice(gi * LANES, (gi + 1) * LANES)
            if transpose_kept:
                kf_ref[0, cs, :] = k_chunks[gi].T
                vf_ref[0, cs, :] = v_chunks[gi].T
            else:
                kf_ref[0, :, cs] = k_chunks[gi]
                vf_ref[0, :, cs] = v_chunks[gi]


def _qkv(x, g_q, g_kv, w_q, w_kv, rope, tm, keep_rows, dilations, transpose_kept):
    b, t, d = x.shape
    cos, sin_signed, cos_step, sin_step = rope
    assert t % tm == 0 and keep_rows % tm == 0 and all(tm % (16 * dil) == 0 for dil in dilations)
    first_kept_tile = (t - keep_rows) // tm
    kept_block = lambda bi, i: jnp.maximum(i - first_kept_tile, 0)
    if transpose_kept:
        kept = pl.BlockSpec((1, KV_WIDTH, tm), lambda bi, i: (bi, 0, kept_block(bi, i)))
        kept_shape = jax.ShapeDtypeStruct((b, KV_WIDTH, keep_rows), F32)
    else:
        kept = pl.BlockSpec((1, tm, KV_WIDTH), lambda bi, i: (bi, kept_block(bi, i), 0))
        kept_shape = jax.ShapeDtypeStruct((b, keep_rows, KV_WIDTH), F32)
    sub_shape = lambda dil, width: jax.ShapeDtypeStruct((b, dil, t // dil, width), BF16)
    q_specs = [_subseq_spec(dil, tm, GROUP_Q_WIDTH) for dil in dilations]
    kv_specs = [_subseq_spec(dil, tm, GROUP_KV_WIDTH) for dil in dilations]
    q_shapes = [sub_shape(dil, GROUP_Q_WIDTH) for dil in dilations]
    kv_shapes = [sub_shape(dil, GROUP_KV_WIDTH) for dil in dilations]
    return pl.pallas_call(
        functools.partial(_qkv_kernel, tuple(dilations), first_kept_tile, transpose_kept),
        grid=(b, t // tm),
        in_specs=[
            pl.BlockSpec((1, tm, d), lambda bi, i: (bi, i, 0)),
            _const_spec(g_q.shape), _const_spec(g_kv.shape),
            _const_spec(w_q.shape), _const_spec(w_kv.shape),
            _const_spec(cos.shape), _const_spec(sin_signed.shape),
            pl.BlockSpec((None,) + cos_step.shape[1:], lambda bi, i: (i, 0, 0)),
            pl.BlockSpec((None,) + sin_step.shape[1:], lambda bi, i: (i, 0, 0)),
        ],
        out_specs=q_specs + kv_specs + kv_specs + [kept, kept],
        out_shape=q_shapes + kv_shapes + kv_shapes + [kept_shape, kept_shape],
        scratch_shapes=[pltpu.VMEM((GROUP_Q_WIDTH // LANES, tm, LANES), F32)],
        compiler_params=pltpu.CompilerParams(
            dimension_semantics=("arbitrary", "arbitrary"), vmem_limit_bytes=VMEM_LIMIT),
        name="qkv_rope",
    )(x, g_q, g_kv, w_q, w_kv, cos, sin_signed, cos_step, sin_step)


def _rope_tables(offsets, tile_starts):
    half = HEAD_DIM // 2
    inv = jnp.tile(ROPE_THETA ** (-jnp.arange(0, HEAD_DIM, 2, dtype=F32) / HEAD_DIM), LANES // half)
    sign = jnp.where((jnp.arange(LANES) // half) % 2 == 0, -1.0, 1.0).astype(F32)
    tables = []
    for positions in (offsets, tile_starts):
        ang = positions.astype(F32)[:, None] * inv[None, :]
        tables += [jnp.cos(ang), jnp.sin(ang) * sign[None, :]]
    rows8 = lambda a: jnp.broadcast_to(a[:, None, :], (a.shape[0], 8, LANES))
    return tables[0], tables[1], rows8(tables[2]), rows8(tables[3])


def _stat_lane(c, k):
    return (1 - c) * HEAD_DIM + k


def _band_attention_kernel(seg_rows, tiles_per_subseq, q_ref, kp_ref, kc_ref, vp_ref, vc_ref,
                           o_ref, stat_ref, k_scr, v_scr, bias_scr, s_scr, p_scr):
    n_seg = q_ref.shape[0] // seg_rows
    bands_per_seg = seg_rows // BAND
    n_bands = n_seg * bands_per_seg
    step = pl.program_id(1)
    lane = lax.broadcasted_iota(jnp.int32, (1, LANES), 1)
    for c in range(KV_PER_GROUP):
        own = (lane // HEAD_DIM) == c
        for u in range(n_seg):
            base = u * (seg_rows + BAND)
            for dst, k_rows, v_rows in (
                    (slice(base, base + BAND), kp_ref[...], vp_ref[...]),
                    (slice(base + BAND, base + BAND + seg_rows),
                     kc_ref[u * seg_rows:(u + 1) * seg_rows, :],
                     vc_ref[u * seg_rows:(u + 1) * seg_rows, :])):
                k_scr[c, dst, :] = jnp.where(own, k_rows, jnp.zeros((), BF16))
                v_scr[c, dst, :] = jnp.where(own, v_rows, jnp.ones((), BF16))
    stat_ref[...] = jnp.ones(stat_ref.shape, F32)

    row = lax.broadcasted_iota(jnp.int32, (BAND, 2 * BAND), 0)
    col = lax.broadcasted_iota(jnp.int32, (BAND, 2 * BAND), 1)
    band_bias = jnp.where((col >= row) & (col <= row + BAND), 0.0, NEG_BIG)
    bias_scr[0] = band_bias
    bias_scr[1] = band_bias + jnp.where(col < BAND, NEG_BIG, 0.0)
    starts_subseq = True if n_seg > 1 else lax.rem(step, tiles_per_subseq) == 0

    def rows_of(g):
        return pl.ds(g * BAND if isinstance(g, int) else pl.multiple_of(g * BAND, BAND), BAND)

    def key_rows_of(g):
        staged = g + g // bands_per_seg
        start = staged * BAND if isinstance(g, int) else pl.multiple_of(staged * BAND, BAND)
        return pl.ds(start, 2 * BAND)

    def scores(g, slot):
        yield KV_PER_GROUP
        q = jnp.concatenate(
            [q_ref[rows_of(g), k * LANES:(k + 1) * LANES] for k in range(Q_PER_KV)], axis=0)
        for c in range(KV_PER_GROUP):
            s_scr[slot, c] = lax.dot_general(q, k_scr[c, key_rows_of(g), :],
                                             (((1,), (1,)), ((), ())),
                                             preferred_element_type=F32)
            yield

    def softmax(g, slot):
        yield KV_PER_GROUP * Q_PER_KV
        first = jnp.logical_and(starts_subseq, g % bands_per_seg == 0).astype(jnp.int32)
        for c in range(KV_PER_GROUP):
            for k in range(Q_PER_KV):
                part = pl.ds(k * BAND, BAND)
                s = s_scr[slot, c, part, :] + bias_scr[first]
                m = jnp.max(s, axis=1, keepdims=True)
                p_scr[slot, c, part, :] = jnp.exp(s - m).astype(BF16)
                stat_ref[rows_of(g), pl.ds(_stat_lane(c, k) + STAT_MAX_OFFSET, 1)] = m
                yield

    def weighted_values(g, slot):
        yield KV_PER_GROUP + Q_PER_KV + 1
        ov = []
        for c in range(KV_PER_GROUP):
            ov.append(jnp.dot(p_scr[slot, c], v_scr[c, key_rows_of(g), :],
                              preferred_element_type=F32))
            yield
        low = lane < HEAD_DIM
        sums = None
        for k in range(Q_PER_KV):
            part = slice(k * BAND, (k + 1) * BAND)
            o_ref[rows_of(g), k * LANES:(k + 1) * LANES] = (
                jnp.where(low, ov[0][part], ov[1][part]).astype(o_ref.dtype))
            sums_k = jnp.where(low, ov[1][part], ov[0][part])
            sums = sums_k if sums is None else jnp.where((lane % HEAD_DIM) == k, sums_k, sums)
            yield
        for c in range(KV_PER_GROUP):
            lanes = pl.ds(_stat_lane(c, 0), Q_PER_KV)
            stat_ref[rows_of(g), lanes] = sums[:, _stat_lane(c, 0):_stat_lane(c, 0) + Q_PER_KV]
        yield

    _interleave(scores(0, 0))
    _interleave(softmax(0, 0), scores(1, 1))

    def steady(t, carry):
        for g, slot in ((2 * t + 1, 1), (2 * t + 2, 0)):
            _interleave(weighted_values(g - 1, 1 - slot), softmax(g, slot),
                        scores(g + 1, 1 - slot))
        return carry

    assert n_bands % 2 == 0
    lax.fori_loop(0, (n_bands - 2) // 2, steady, 0)
    last = n_bands - 1
    _interleave(weighted_values(last - 1, (last - 1) % 2), softmax(last, last % 2))
    _interleave(weighted_values(last, last % 2))


def _band_attention(q, k, v, block_rows):
    b, dilation, sub, _ = q.shape
    seg_rows = min(sub, block_rows)
    total = dilation * sub
    assert total % block_rows == 0 and block_rows % seg_rows == 0 and sub % seg_rows == 0
    assert seg_rows % BAND == 0 and block_rows >= 2 * BAND
    n_seg = block_rows // seg_rows
    flat = lambda a: a.reshape(b, total, a.shape[-1])
    cur = lambda width: pl.BlockSpec((None, block_rows, width), lambda bi, i: (bi, i, 0))
    prev = pl.BlockSpec((None, BAND, GROUP_KV_WIDTH),
                        lambda bi, i: (bi, jnp.maximum(i * (block_rows // BAND) - 1, 0), 0))
    staged_rows = n_seg * (seg_rows + BAND)
    o, stat = pl.pallas_call(
        functools.partial(_band_attention_kernel, seg_rows, sub // seg_rows),
        grid=(b, total // block_rows),
        in_specs=[cur(GROUP_Q_WIDTH), prev, cur(GROUP_KV_WIDTH), prev, cur(GROUP_KV_WIDTH)],
        out_specs=[cur(GROUP_Q_WIDTH), cur(STAT_LANES)],
        out_shape=[jax.ShapeDtypeStruct((b, total, GROUP_Q_WIDTH), BF16),
                   jax.ShapeDtypeStruct((b, total, STAT_LANES), F32)],
        scratch_shapes=[pltpu.VMEM((KV_PER_GROUP, staged_rows, GROUP_KV_WIDTH), BF16),
                        pltpu.VMEM((KV_PER_GROUP, staged_rows, GROUP_KV_WIDTH), BF16),
                        pltpu.VMEM((2, BAND, 2 * BAND), F32),
                        pltpu.VMEM((2, KV_PER_GROUP, Q_PER_KV * BAND, 2 * BAND), F32),
                        pltpu.VMEM((2, KV_PER_GROUP, Q_PER_KV * BAND, 2 * BAND), BF16)],
        compiler_params=pltpu.CompilerParams(
            dimension_semantics=("arbitrary", "arbitrary"), vmem_limit_bytes=VMEM_LIMIT),
        name=f"band_attention_d{dilation}",
    )(flat(q), flat(k), flat(k), flat(v), flat(v))
    return (o.reshape(b, dilation, sub, GROUP_Q_WIDTH), stat.reshape(b, dilation, sub, STAT_LANES))


SAMPLE_Q_ROWS = 16
NEW_KEY_ROWS = 8
SAMPLE_SEQS_PER_STEP = 4


def _sample_attention_kernel(n_new, q_ref, kn_ref, vn_ref, kc0, kc1, kc2, vc0, vc1, vc2,
                             o_ref, lse_ref):
    n_seq = q_ref.shape[0]
    n_rows = q_ref.shape[2]
    contract_last = (((1,), (1,)), ((), ()))
    cache_refs = ((kc0, vc0), (kc1, vc1), (kc2, vc2))
    bias_c, bias_n = [], []
    for gi, (kc_ref, _) in enumerate(cache_refs):
        dmask = ATTN_DILATIONS[gi] - 1
        n_cache = kc_ref.shape[3]
        assert n_cache == ATTN_WINDOWS[gi]
        t_c = lax.broadcasted_iota(jnp.int32, (n_rows, n_cache), 0) & (n_new - 1)
        j_c = lax.broadcasted_iota(jnp.int32, (n_rows, n_cache), 1)
        valid_c = (j_c >= t_c) & (((j_c - t_c) & dmask) == 0)
        t_n = lax.broadcasted_iota(jnp.int32, (n_rows, NEW_KEY_ROWS), 0) & (n_new - 1)
        j_n = lax.broadcasted_iota(jnp.int32, (n_rows, NEW_KEY_ROWS), 1)
        valid_n = (j_n <= t_n) & (((t_n - j_n) & dmask) == 0)
        bias_c.append(jnp.where(valid_c, 0.0, NEG_BIG))
        bias_n.append(jnp.where(valid_n, 0.0, NEG_BIG))

    def one_sequence(b):
        yield len(cache_refs) * KV_PER_GROUP
        kn_all = kn_ref[b]
        vn_all = vn_ref[b]
        for gi, (kc_ref, vc_ref) in enumerate(cache_refs):
            for c in range(KV_PER_GROUP):
                kvh = gi * KV_PER_GROUP + c
                ns = slice(kvh * HEAD_DIM, (kvh + 1) * HEAD_DIM)
                q = q_ref[b, kvh]
                kt = kc_ref[b, c].astype(BF16)
                vt = vc_ref[b, c].astype(BF16)
                s_c = jnp.dot(q, kt, preferred_element_type=F32) + bias_c[gi]
                s_n = lax.dot_general(q, kn_all[:, ns].astype(BF16), contract_last,
                                      preferred_element_type=F32) + bias_n[gi]
                m = jnp.maximum(jnp.max(s_c, axis=1, keepdims=True),
                                jnp.max(s_n, axis=1, keepdims=True))
                p_c = jnp.exp(s_c - m)
                p_n = jnp.exp(s_n - m)
                l = jnp.sum(p_c, axis=1, keepdims=True) + jnp.sum(p_n, axis=1, keepdims=True)
                o = (lax.dot_general(p_c.astype(BF16), vt, contract_last,
                                     preferred_element_type=F32)
                     + jnp.dot(p_n.astype(BF16), vn_all[:, ns].astype(BF16),
                               preferred_element_type=F32))
                o_ref[b, kvh] = o / l
                lse_ref[b, kvh] = m + jnp.log(l)
                yield

    _interleave(*[one_sequence(b) for b in range(n_seq)])


def _sample_attention(q16, k_new, v_new, cache_kt, cache_vt, n_new):
    b = q16.shape[0]
    n_kv = q16.shape[1]
    assert cache_kt.shape[3] == KV_WINDOW and n_new & (n_new - 1) == 0 and n_new <= NEW_KEY_ROWS

    per_step = SAMPLE_SEQS_PER_STEP
    assert b % per_step == 0

    def cache_spec(gi):
        cols = ATTN_WINDOWS[gi]
        last = KV_WINDOW // cols - 1
        return pl.BlockSpec((per_step, KV_PER_GROUP, HEAD_DIM, cols),
                            lambda bi: (bi, gi, 0, last))

    whole = lambda a: pl.BlockSpec((per_step,) + a.shape[1:],
                                   lambda bi: (bi,) + (0,) * (a.ndim - 1))
    o_shape = jax.ShapeDtypeStruct((b, n_kv, SAMPLE_Q_ROWS, HEAD_DIM), F32)
    lse_shape = jax.ShapeDtypeStruct((b, n_kv, SAMPLE_Q_ROWS, 1), F32)
    return pl.pallas_call(
        functools.partial(_sample_attention_kernel, n_new),
        grid=(b // per_step,),
        in_specs=[whole(q16), whole(k_new), whole(v_new)]
        + [cache_spec(gi) for gi in range(N_GROUPS)] * 2,
        out_specs=[whole(o_shape), whole(lse_shape)],
        out_shape=[o_shape, lse_shape],
        compiler_params=pltpu.CompilerParams(
            dimension_semantics=("arbitrary",), vmem_limit_bytes=VMEM_LIMIT),
        name="sample_attention",
    )(q16, k_new, v_new, cache_kt, cache_kt, cache_kt, cache_vt, cache_vt, cache_vt)


def _attn_mix_steps(dilations, x_ref, o_refs, s_refs, g, wo_ref, o_scr, stat_scr, a_scr,
                    x1_ref, h_ref):
    tm = x_ref.shape[0]
    chunks_per_group = GROUP_Q_WIDTH // LANES
    yield 2 * N_GROUPS + 1 + 2 * len(_row_blocks(tm))
    for gi, (o_ref, s_ref) in enumerate(zip(o_refs, s_refs)):
        for r in range(dilations[gi]):
            rows = _subseq_rows(tm, r, dilations[gi])
            stat_scr[gi, rows, :] = s_ref[r]
            for c in range(chunks_per_group):
                o_scr[gi * chunks_per_group + c, rows, :] = (
                    o_ref[r, :, c * LANES:(c + 1) * LANES].astype(F32))
        yield
    sums = [stat_scr[gi] for gi in range(N_GROUPS)]
    maxes = [pltpu.roll(sm, STAT_LANES - STAT_MAX_OFFSET, 1) for sm in sums]
    top = jnp.maximum(jnp.maximum(maxes[0], maxes[1]), maxes[2])
    es = [jnp.exp(mx - top) for mx in maxes]
    den = sums[0] * es[0] + sums[1] * es[1] + sums[2] * es[2]
    slot_id = lax.broadcasted_iota(jnp.int32, (tm, GROUP_Q_WIDTH), 1) // HEAD_DIM
    for gi in range(N_GROUPS):
        scale = es[gi] / den
        wide = jnp.zeros((tm, GROUP_Q_WIDTH), F32)
        for slot in range(HEADS_PER_GROUP):
            stat_lane = _stat_lane(slot % KV_PER_GROUP, slot // KV_PER_GROUP)
            wide = jnp.where(slot_id == slot, scale[:, stat_lane:stat_lane + 1], wide)
        for c in range(chunks_per_group):
            ci = gi * chunks_per_group + c
            a_scr[:, ci * LANES:(ci + 1) * LANES] = (
                o_scr[ci] * wide[:, c * LANES:(c + 1) * LANES]).astype(BF16)
        yield
    mix = jnp.dot(a_scr[...], wo_ref[...], preferred_element_type=F32)
    yield
    tail = _chain(
        _residual_norm_steps(x1_ref, lambda rows: x_ref[rows, :], lambda rows: mix[rows], g[1:2]),
        _norm_cast_steps(h_ref, lambda rows: x1_ref[rows, :], g[2:3]))
    next(tail)
    yield from tail


def _attn_mix_scratch(tm):
    return [pltpu.VMEM((Q_WIDTH // LANES, tm, LANES), F32),
            pltpu.VMEM((N_GROUPS, tm, STAT_LANES), F32),
            pltpu.VMEM((tm, Q_WIDTH), BF16)]


def _layer1_single_kernel(dilations, x_ref, o0_ref, o1_ref, o2_ref, s0_ref, s1_ref, s2_ref, g_ref,
                          wo_ref, wup_ref, wdn_ref, y_ref, o_scr, stat_scr, a_scr,
                          x1_scr, h_scr, acc_scr):
    g = g_ref[...]

    @pl.when(pl.program_id(0) == 0)
    def _():
        _interleave(_attn_mix_steps(dilations, x_ref, (o0_ref, o1_ref, o2_ref),
                                    (s0_ref, s1_ref, s2_ref), g, wo_ref, o_scr, stat_scr, a_scr,
                                    x1_scr, h_scr))
        acc_scr[...] = jnp.zeros(acc_scr.shape, F32)

    def emit(y):
        y_ref[...] = y

    _streamed_mlp_step(g, wup_ref, wdn_ref, x1_scr, h_scr, acc_scr, emit)


def _layer1_prompt_kernel(dilations, x_ref, o0_ref, o1_ref, o2_ref, s0_ref, s1_ref, s2_ref,
                          g_ref, wo_ref, wup_ref, wdn_ref, y_ref,
                          o_scr, stat_scr, a_scr, x1_scr, h_scr, acc_scr):
    s = pl.program_id(0)
    n_tiles = pl.num_programs(0) - 2
    g = g_ref[...]

    def prepare():
        return _attn_mix_steps(dilations, x_ref, (o0_ref, o1_ref, o2_ref),
                               (s0_ref, s1_ref, s2_ref), g, wo_ref, o_scr, stat_scr, a_scr,
                               x1_scr.at[lax.rem(s, 3)], h_scr.at[lax.rem(s, 2)])

    def finish():
        x1_slot, acc_slot = lax.rem(s + 1, 3), lax.rem(s, 2)
        return _residual_norm_steps(y_ref, lambda rows: x1_scr[x1_slot, rows, :],
                                    lambda rows: acc_scr[acc_slot, rows, :], g[3:4])

    @pl.when(s == 0)
    def _():
        _interleave(prepare())
        x1_scr[2] = jnp.zeros(x1_scr.shape[1:], F32)
        acc_scr[1] = jnp.zeros(acc_scr.shape[1:], F32)

    @pl.when(jnp.logical_and(s > 0, s <= n_tiles))
    def _():
        mlp_slot = lax.rem(s + 1, 2)

        def store_acc(acc):
            acc_scr[mlp_slot] = acc

        _interleave(
            finish(),
            _mlp_matmul_steps(lambda: h_scr[mlp_slot], wup_ref, wdn_ref, store_acc),
            prepare())

    @pl.when(s == n_tiles + 1)
    def _():
        _interleave(finish())


def _layer1(x, os, stats, gains, w_o, w_up, w_down, layer, tm):
    b, t, d = x.shape
    dilations = tuple(o.shape[1] for o in os)
    tiles_per_seq = t // tm
    n_tiles = b * tiles_per_seq
    assert t % tm == 0 and all(tm % (16 * dil) == 0 for dil in dilations)
    x2 = x.reshape(b * t, d)
    weights = (gains, w_o, w_up, w_down)
    weight_specs = [_const_spec(gains.shape), _const_spec(w_o.shape),
                    _layer_spec(w_up, layer), _layer_spec(w_down, layer)]

    def tile_specs(tile):
        sub = lambda dil, width: pl.BlockSpec(
            (None, dil, tm // dil, width),
            lambda s: (tile(s) // tiles_per_seq, 0, tile(s) % tiles_per_seq, 0))
        return ([pl.BlockSpec((tm, d), lambda s: (tile(s), 0))]
                + [sub(dil, GROUP_Q_WIDTH) for dil in dilations]
                + [sub(dil, STAT_LANES) for dil in dilations])

    if n_tiles == 1:
        n_chunks, mlp_specs = _streamed_mlp_specs(w_up, w_down, layer)
        y = pl.pallas_call(
            functools.partial(_layer1_single_kernel, dilations),
            grid=(n_chunks,),
            in_specs=tile_specs(lambda s: 0) + weight_specs[:2] + mlp_specs,
            out_specs=pl.BlockSpec((tm, d), lambda s: (0, 0)),
            out_shape=jax.ShapeDtypeStruct((b * t, d), F32),
            scratch_shapes=_attn_mix_scratch(tm) + [
                pltpu.VMEM((tm, d), F32), pltpu.VMEM((tm, d), BF16), pltpu.VMEM((tm, d), F32)],
            compiler_params=pltpu.CompilerParams(
                dimension_semantics=("arbitrary",), vmem_limit_bytes=VMEM_LIMIT),
            name="layer1_single_tile",
        )(x2, *os, *stats, *weights)
        return y.reshape(b, t, d)

    scratch = _attn_mix_scratch(tm) + [
        pltpu.VMEM((3, tm, d), F32), pltpu.VMEM((2, tm, d), BF16), pltpu.VMEM((2, tm, d), F32)]
    y = pl.pallas_call(
        functools.partial(_layer1_prompt_kernel, dilations),
        grid=(n_tiles + 2,),
        in_specs=tile_specs(lambda s: jnp.minimum(s, n_tiles - 1)) + weight_specs,
        out_specs=pl.BlockSpec((tm, d), lambda s: (jnp.maximum(s - 2, 0), 0)),
        out_shape=jax.ShapeDtypeStruct((b * t, d), F32),
        scratch_shapes=scratch,
        compiler_params=pltpu.CompilerParams(
            dimension_semantics=("arbitrary",), vmem_limit_bytes=VMEM_LIMIT),
        name="layer1_prompt",
    )(x2, *os, *stats, *weights)
    return y.reshape(b, t, d)


PROMPT_TILE = 512
LAYER1_TILE = 512
QKV_TILE = 1024
ATTN_BLOCK_ROWS = 2048


def kernel(x_prompt, x_sample, cache_pool, cache_k, cache_v, norm_gains, kv_norm_gain, w_pool,
           pool_scale, w_q, w_o, w_kv, w_up, w_down):
    depth = norm_gains.shape[0]
    assert depth == 2 and cache_pool.shape[0] == 1 and w_q.shape[0] == 1
    bp, tp, d = x_prompt.shape
    bs, ts, _ = x_sample.shape

    g0, g1 = norm_gains[0], norm_gains[1]
    g1_q = g1[0:1]
    g_kv = kv_norm_gain[None, :]
    wp = w_pool[0].astype(BF16)
    ps = pool_scale[0][None, :]
    wq = w_q[0].reshape(d, N_GROUPS, KV_PER_GROUP, Q_PER_KV, HEAD_DIM).transpose(0, 1, 3, 2, 4)
    wq = wq.reshape(d, Q_WIDTH).astype(BF16)
    wo = w_o[0].reshape(N_GROUPS, KV_PER_GROUP, Q_PER_KV, HEAD_DIM, d).transpose(0, 2, 1, 3, 4)
    wo = wo.reshape(Q_WIDTH, d).astype(BF16)
    wkv = w_kv.astype(BF16)
    wup, wdn = _to_bf16(w_up), _to_bf16(w_down)

    keep = min(KV_WINDOW, tp)
    xp1, utail = _layer0_prompt(x_prompt, g0, wp, ps, wup, wdn, 0, PROMPT_TILE)
    rope_p = _rope_tables(jnp.arange(QKV_TILE), jnp.arange(tp // QKV_TILE) * QKV_TILE)
    qkv_p = _qkv(xp1, g1_q, g_kv, wq, wkv, rope_p, QKV_TILE, keep, ATTN_DILATIONS, True)
    os_p, stats_p = [], []
    for gi, dil in enumerate(ATTN_DILATIONS):
        o, stat = _band_attention(qkv_p[gi], qkv_p[N_GROUPS + gi], qkv_p[2 * N_GROUPS + gi],
                                  ATTN_BLOCK_ROWS)
        os_p.append(o)
        stats_p.append(stat)
    y_prompt = _layer1(xp1, os_p, stats_p, g1, wo, wup, wdn, 1, LAYER1_TILE)
    pool_prompt = utail[:, POOL_HALO - POOL_BUF:][None]
    k_prompt = qkv_p[-2].reshape(bp, N_KV_HEADS, HEAD_DIM, keep).transpose(0, 3, 1, 2)
    v_prompt = qkv_p[-1].reshape(bp, N_KV_HEADS, HEAD_DIM, keep).transpose(0, 3, 1, 2)

    n_tok = ts * bs
    xs_tm = jnp.swapaxes(x_sample, 0, 1)
    buf_tm = jnp.swapaxes(cache_pool[0], 0, 1)
    xs1_tm, us_tm = _layer0_sample(xs_tm, buf_tm, PAST_LEN, g0, wp, ps, wup, wdn, 0)
    xs1 = xs1_tm.reshape(1, n_tok, d)
    rope_s = _rope_tables(jnp.arange(n_tok) // bs, jnp.full((1,), PAST_LEN))
    qkv_s = _qkv(xs1, g1_q, g_kv, wq, wkv, rope_s, n_tok, n_tok, (1,) * N_GROUPS, False)
    q_s = jnp.concatenate([q[0, 0] for q in qkv_s[:N_GROUPS]], axis=-1)
    q16 = q_s.reshape(ts, bs, N_GROUPS, Q_PER_KV, KV_PER_GROUP, HEAD_DIM)
    q16 = q16.transpose(1, 2, 4, 3, 0, 5).reshape(bs, N_KV_HEADS, Q_PER_KV, ts, HEAD_DIM)
    q16 = jnp.pad(q16, ((0, 0), (0, 0), (0, SAMPLE_Q_ROWS // ts - Q_PER_KV), (0, 0), (0, 0)))
    q16 = q16.reshape(bs, N_KV_HEADS, SAMPLE_Q_ROWS, HEAD_DIM)
    k_s = jnp.swapaxes(qkv_s[-2].reshape(ts, bs, KV_WIDTH), 0, 1)
    v_s = jnp.swapaxes(qkv_s[-1].reshape(ts, bs, KV_WIDTH), 0, 1)
    pad_new = ((0, 0), (0, NEW_KEY_ROWS - ts), (0, 0))
    o16, lse16 = _sample_attention(
        q16, jnp.pad(k_s, pad_new), jnp.pad(v_s, pad_new),
        cache_k.transpose(0, 2, 3, 1), cache_v.transpose(0, 2, 3, 1), ts)
    heads_padded = SAMPLE_Q_ROWS // ts
    o_s = o16.reshape(bs, N_GROUPS, KV_PER_GROUP, heads_padded, ts, HEAD_DIM)[:, :, :, :Q_PER_KV]
    o_s = o_s.transpose(4, 0, 1, 3, 2, 5).reshape(1, 1, n_tok, N_GROUPS, GROUP_Q_WIDTH).astype(BF16)
    lse_s = lse16.reshape(bs, N_GROUPS, KV_PER_GROUP, heads_padded, ts)[:, :, :, :Q_PER_KV]
    lse_s = lse_s.transpose(4, 0, 1, 2, 3).reshape(1, 1, n_tok, N_GROUPS, KV_PER_GROUP, Q_PER_KV)
    ones = lambda n: jnp.ones((1, 1, n_tok, N_GROUPS, n), F32)
    max_lane = [_stat_lane(c, 0) + STAT_MAX_OFFSET for c in range(KV_PER_GROUP)]
    assert max_lane[1] < max_lane[0]
    stat_s = jnp.concatenate(
        [ones(max_lane[1]), lse_s[..., 1, :],
         ones(max_lane[0] - max_lane[1] - Q_PER_KV), lse_s[..., 0, :],
         ones(STAT_LANES - max_lane[0] - Q_PER_KV)], axis=-1)
    ys_tm = _layer1(xs1, [o_s[:, :, :, gi] for gi in range(N_GROUPS)],
                    [stat_s[:, :, :, gi] for gi in range(N_GROUPS)], g1, wo, wup, wdn, 1, n_tok)
    y_sample = jnp.swapaxes(ys_tm.reshape(ts, bs, d), 0, 1)
    u_s = jnp.swapaxes(us_tm, 0, 1)
    pool_sample = jnp.concatenate([cache_pool[0], u_s], axis=1)[:, -POOL_BUF:][None]
    k_sample = k_s.reshape(bs, ts, N_KV_HEADS, HEAD_DIM)
    v_sample = v_s.reshape(bs, ts, N_KV_HEADS, HEAD_DIM)

    return (y_prompt, y_sample, pool_prompt, k_prompt, v_prompt, pool_sample, k_sample, v_sample)
```

```python
import functools

import jax
import jax.numpy as jnp
from jax import lax
from jax.experimental import pallas as pl
from jax.experimental.pallas import tpu as pltpu

F32 = jnp.float32
BF16 = jnp.bfloat16

EPS = 1e-6
ROPE_THETA = 10000.0
PAST_LEN = 16384
POOL_WINDOWS = (2, 4, 8, 16)
POOL_BUF = max(POOL_WINDOWS) - 1
POOL_HALO = 16
HEAD_DIM = 64
ATTN_WINDOWS = (128, 512, 2048)
ATTN_DILATIONS = (1, 4, 16)
N_GROUPS = len(ATTN_WINDOWS)
KV_PER_GROUP = 2
Q_PER_KV = 3
HEADS_PER_GROUP = KV_PER_GROUP * Q_PER_KV
N_KV_HEADS = N_GROUPS * KV_PER_GROUP
GROUP_Q_WIDTH = HEADS_PER_GROUP * HEAD_DIM
GROUP_KV_WIDTH = KV_PER_GROUP * HEAD_DIM
Q_WIDTH = N_GROUPS * GROUP_Q_WIDTH
KV_WIDTH = N_GROUPS * GROUP_KV_WIDTH
KV_WINDOW = max(ATTN_WINDOWS)
BAND = 128
NEG_BIG = -1e30

LANES = 128
STAT_LANES = LANES
STAT_MAX_OFFSET = 8
VMEM_LIMIT = 56 * 1024 * 1024
FF_CHUNK = 512
NORM_ROW_BLOCKS = 4
POOL_NORM_BLOCKS = 1
POOL_FF_CHUNK = 1024
SAMPLE_FF_CHUNK = 1024

for _w, _d in zip(ATTN_WINDOWS, ATTN_DILATIONS):
    assert _w // _d == BAND and _w % _d == 0
assert GROUP_KV_WIDTH == LANES


def _rms(x, g):
    return x * lax.rsqrt(jnp.mean(x * x, axis=-1, keepdims=True) + EPS) * g


def _mlp_matmul_steps(load_h, wup_ref, wdn_ref, emit_acc, chunk=FF_CHUNK):
    n_chunks = wup_ref.shape[1] // chunk
    yield n_chunks
    h = load_h()
    acc = None
    for c in range(n_chunks):
        cs = slice(c * chunk, (c + 1) * chunk)
        a = jnp.dot(h, wup_ref[:, cs], preferred_element_type=F32)
        a = jnp.square(jnp.maximum(a, 0.0)).astype(BF16)
        part = jnp.dot(a, wdn_ref[cs, :], preferred_element_type=F32)
        acc = part if acc is None else acc + part
        if c + 1 == n_chunks:
            emit_acc(acc)
        yield


def _row_blocks(n_rows, n_blocks=NORM_ROW_BLOCKS):
    size = n_rows // n_blocks if n_rows % (8 * n_blocks) == 0 else n_rows
    return [slice(lo, lo + size) for lo in range(0, n_rows, size)]


def _residual_norm_steps(dst_ref, load_x, load_v, g, n_blocks=NORM_ROW_BLOCKS):
    blocks = _row_blocks(dst_ref.shape[0], n_blocks)
    yield len(blocks)
    for rows in blocks:
        dst_ref[rows, :] = load_x(rows) + _rms(load_v(rows), g)
        yield


def _norm_cast_steps(dst_ref, load_v, g, n_blocks=NORM_ROW_BLOCKS):
    blocks = _row_blocks(dst_ref.shape[0], n_blocks)
    yield len(blocks)
    for rows in blocks:
        dst_ref[rows, :] = _rms(load_v(rows), g).astype(dst_ref.dtype)
        yield


def _chain(*generators):
    counts = [next(gen) for gen in generators]
    yield sum(counts)
    for gen in generators:
        yield from gen


def _interleave(*generators):
    totals = [next(gen) for gen in generators]
    done = [0] * len(generators)
    while any(d < t for d, t in zip(done, totals)):
        i = min((i for i in range(len(generators)) if done[i] < totals[i]),
                key=lambda i: (done[i] + 1) / totals[i])
        next(generators[i])
        done[i] += 1


def _const_spec(shape):
    zeros = (0,) * len(shape)
    return pl.BlockSpec(shape, lambda *_: zeros, pipeline_mode=pl.Buffered(1))


def _layer_spec(w, layer):
    index = (layer,) + (0,) * (w.ndim - 1)
    return pl.BlockSpec((None,) + w.shape[1:], lambda *_: index, pipeline_mode=pl.Buffered(1))


def _subseq_spec(dilation, rows, width):
    return pl.BlockSpec((None, dilation, rows // dilation, width), lambda bi, i: (bi, 0, i, 0))


def _subseq_rows(n_rows, r, dilation):
    n = n_rows // dilation
    return pl.ds(r, n, stride=dilation) if dilation > 1 else pl.ds(0, n)


def _pool_mixer_steps(x_ref, xh_ref, tile_in_seq, g, wp_ref, ps_ref, ext_ref, x1_ref, h_ref,
                      utail_ref=None):
    tm = x_ref.shape[0]
    pool_ch = wp_ref.shape[1]
    blocks = _row_blocks(tm, POOL_NORM_BLOCKS)
    yield 3 * len(blocks) + len(POOL_WINDOWS)
    uh = _rms(xh_ref[0:POOL_HALO, :], g[0:1]) * jnp.where(tile_in_seq > 0, 1.0, 0.0)
    ext_ref[0:POOL_HALO, :] = uh
    for rows in blocks:
        u = _rms(x_ref[rows, :], g[0:1])
        ext_ref[POOL_HALO + rows.start:POOL_HALO + rows.stop, :] = u
        if utail_ref is not None and rows.stop == tm:
            utail_ref[0] = u[u.shape[0] - POOL_HALO:, :]
        yield
    pos = tile_in_seq * tm + lax.broadcasted_iota(jnp.int32, (tm, 1), 0)
    parts = []
    for gi, w in enumerate(POOL_WINDOWS):
        cs = slice(gi * pool_ch, (gi + 1) * pool_ch)
        s = ext_ref[POOL_HALO:, cs]
        for j in range(1, w):
            s = s + ext_ref[POOL_HALO - j:POOL_HALO - j + tm, cs]
        cnt = jnp.minimum(w, pos + 1).astype(F32)
        pooled = s / cnt - ext_ref[POOL_HALO:, cs]
        parts.append(jnp.dot(pooled.astype(BF16), wp_ref[gi], preferred_element_type=F32))
        yield
    mix = jnp.concatenate(parts, axis=1) * ps_ref[...]
    tail = _chain(
        _residual_norm_steps(x1_ref, lambda rows: x_ref[rows, :], lambda rows: mix[rows], g[1:2],
                             POOL_NORM_BLOCKS),
        _norm_cast_steps(h_ref, lambda rows: x1_ref[rows, :], g[2:3], POOL_NORM_BLOCKS))
    next(tail)
    yield from tail


def _layer0_prompt_kernel(tiles_per_seq, x0_ref, xa_ref, xha_ref, xb_ref, xhb_ref, g_ref,
                          wp_ref, ps_ref, wup_ref, wdn_ref, y_ref, utail_ref,
                          x1_scr, h_scr, ext_scr):
    s = pl.program_id(0)
    n_tiles = 2 * pl.num_programs(0)
    tm = xa_ref.shape[0]
    g = g_ref[...]

    def prepare(x_ref, xh_ref, tile, slot, tail_ref):
        return _pool_mixer_steps(x_ref, xh_ref, lax.rem(tile, tiles_per_seq), g, wp_ref, ps_ref,
                                 ext_scr.at[slot], x1_scr.at[slot], h_scr.at[slot], tail_ref)

    @pl.when(s == 0)
    def _():
        _interleave(prepare(x0_ref, x0_ref, 0, 0, None))

    def mlp(slot):
        acc = []
        return _chain(
            _mlp_matmul_steps(lambda: h_scr[slot], wup_ref, wdn_ref, acc.append, POOL_FF_CHUNK),
            _residual_norm_steps(y_ref.at[pl.ds(slot * tm, tm)],
                                 lambda rows: x1_scr[slot, rows, :],
                                 lambda rows: acc[0][rows], g[3:4], POOL_NORM_BLOCKS))

    _interleave(mlp(0), prepare(xa_ref, xha_ref, 2 * s + 1, 1, utail_ref))
    _interleave(mlp(1), prepare(xb_ref, xhb_ref, jnp.minimum(2 * s + 2, n_tiles - 1), 0, None))


def _layer0_prompt(x, gains, w_pool, pool_scale, w_up, w_down, layer, tm):
    b, t, d = x.shape
    tiles_per_seq = t // tm
    n_tiles = b * tiles_per_seq
    assert t % tm == 0 and tm % POOL_HALO == 0 and tiles_per_seq % 2 == 0
    halo_per_tile = tm // POOL_HALO
    x2 = x.reshape(b * t, d)
    tile_a = lambda s: 2 * s + 1
    tile_b = lambda s: jnp.minimum(2 * s + 2, n_tiles - 1)
    tile_spec = lambda tile: pl.BlockSpec((tm, d), lambda s: (tile(s), 0))
    halo_spec = lambda tile: pl.BlockSpec((POOL_HALO, d),
                                          lambda s: (tile(s) * halo_per_tile - 1, 0))
    y, utail = pl.pallas_call(
        functools.partial(_layer0_prompt_kernel, tiles_per_seq),
        grid=(n_tiles // 2,),
        in_specs=[
            pl.BlockSpec((tm, d), lambda s: (0, 0), pipeline_mode=pl.Buffered(1)),
            tile_spec(tile_a), halo_spec(tile_a), tile_spec(tile_b), halo_spec(tile_b),
            _const_spec(gains.shape), _const_spec(w_pool.shape), _const_spec(pool_scale.shape),
            _layer_spec(w_up, layer), _layer_spec(w_down, layer),
        ],
        out_specs=[
            pl.BlockSpec((2 * tm, d), lambda s: (s, 0)),
            pl.BlockSpec((1, POOL_HALO, d), lambda s: (tile_a(s) // tiles_per_seq, 0, 0)),
        ],
        out_shape=[
            jax.ShapeDtypeStruct((b * t, d), F32),
            jax.ShapeDtypeStruct((b, POOL_HALO, d), F32),
        ],
        scratch_shapes=[pltpu.VMEM((2, tm, d), F32), pltpu.VMEM((2, tm, d), BF16),
                        pltpu.VMEM((2, tm + POOL_HALO, d), F32)],
        compiler_params=pltpu.CompilerParams(
            dimension_semantics=("arbitrary",), vmem_limit_bytes=VMEM_LIMIT),
        name="layer0_prompt",
    )(x2, x2, x2, x2, x2, gains, w_pool, pool_scale, w_up, w_down)
    return y.reshape(b, t, d), utail


def _streamed_mlp_step(g, wup_ref, wdn_ref, x1_scr, h_scr, acc_scr, emit):
    a = jnp.dot(h_scr[...], wup_ref[...], preferred_element_type=F32)
    a = jnp.square(jnp.maximum(a, 0.0)).astype(BF16)
    acc_scr[...] += jnp.dot(a, wdn_ref[...], preferred_element_type=F32)

    @pl.when(pl.program_id(0) == pl.num_programs(0) - 1)
    def _():
        emit(x1_scr[...] + _rms(acc_scr[...], g[3:4]))


def _streamed_mlp_specs(w_up, w_down, layer):
    assert w_up.shape[2] % SAMPLE_FF_CHUNK == 0
    return (w_up.shape[2] // SAMPLE_FF_CHUNK,
            [pl.BlockSpec((None, w_up.shape[1], SAMPLE_FF_CHUNK), lambda c: (layer, 0, c)),
             pl.BlockSpec((None, SAMPLE_FF_CHUNK, w_down.shape[2]), lambda c: (layer, c, 0))])


def _layer0_sample_kernel(start_pos, x_ref, buf_ref, g_ref, wp_ref, ps_ref, wup_ref, wdn_ref,
                          y_ref, u_ref, x1_scr, h_scr, acc_scr):
    n_t, n_b, _ = x_ref.shape
    g = g_ref[...]

    @pl.when(pl.program_id(0) == 0)
    def _():
        _pool_mixer_time_major(start_pos, x_ref, buf_ref, g, wp_ref, ps_ref, u_ref, x1_scr, h_scr)
        acc_scr[...] = jnp.zeros(acc_scr.shape, F32)

    def emit(y):
        for t in range(n_t):
            y_ref[t] = y[t * n_b:(t + 1) * n_b]

    _streamed_mlp_step(g, wup_ref, wdn_ref, x1_scr, h_scr, acc_scr, emit)


def _pool_mixer_time_major(start_pos, x_ref, buf_ref, g, wp_ref, ps_ref, u_ref, x1_ref, h_ref):
    n_t = x_ref.shape[0]
    pool_ch = wp_ref.shape[1]
    xs = [x_ref[t] for t in range(n_t)]
    us = [_rms(xt, g[0:1]) for xt in xs]
    ext = [buf_ref[j] for j in range(POOL_BUF)] + us
    parts = []
    for gi, w in enumerate(POOL_WINDOWS):
        cs = slice(gi * pool_ch, (gi + 1) * pool_ch)
        rows = []
        for t in range(n_t):
            s = us[t][:, cs]
            for j in range(1, w):
                s = s + ext[POOL_BUF + t - j][:, cs]
            cnt = float(min(w, start_pos + t + 1))
            rows.append(s / cnt - us[t][:, cs])
        pooled = jnp.concatenate(rows, axis=0)
        parts.append(jnp.dot(pooled.astype(BF16), wp_ref[gi], preferred_element_type=F32))
    mix = jnp.concatenate(parts, axis=1) * ps_ref[...]
    x = jnp.concatenate(xs, axis=0)
    x1 = x + _rms(mix, g[1:2])
    x1_ref[...] = x1
    h_ref[...] = _rms(x1, g[2:3]).astype(BF16)
    for t in range(n_t):
        u_ref[t] = us[t]


def _layer0_sample(x_tm, buf_tm, start_pos, gains, w_pool, pool_scale, w_up, w_down, layer):
    assert start_pos + 1 >= max(POOL_WINDOWS) and buf_tm.shape[0] == POOL_BUF
    n_tok, d = x_tm.shape[0] * x_tm.shape[1], x_tm.shape[2]
    n_chunks, weight_specs = _streamed_mlp_specs(w_up, w_down, layer)
    return pl.pallas_call(
        functools.partial(_layer0_sample_kernel, start_pos),
        grid=(n_chunks,),
        in_specs=[_const_spec(a.shape) for a in (x_tm, buf_tm, gains, w_pool, pool_scale)]
        + weight_specs,
        out_specs=[pl.BlockSpec(x_tm.shape, lambda c: (0, 0, 0))] * 2,
        out_shape=[jax.ShapeDtypeStruct(x_tm.shape, F32), jax.ShapeDtypeStruct(x_tm.shape, F32)],
        scratch_shapes=[pltpu.VMEM((n_tok, d), F32), pltpu.VMEM((n_tok, d), BF16),
                        pltpu.VMEM((n_tok, d), F32)],
        compiler_params=pltpu.CompilerParams(
            dimension_semantics=("arbitrary",), vmem_limit_bytes=VMEM_LIMIT),
        name="layer0_sample",
    )(x_tm, buf_tm, gains, w_pool, pool_scale, w_up, w_down)


def _rope(x, cos, sin_signed):
    lane = lax.broadcasted_iota(jnp.int32, (x.shape[0], LANES), 1)
    first_half = (lane & (HEAD_DIM // 2)) == 0
    out = []
    for c in range(x.shape[1] // LANES):
        xc = x[:, c * LANES:(c + 1) * LANES]
        partner = jnp.where(first_half,
                            pltpu.roll(xc, LANES - HEAD_DIM // 2, 1),
                            pltpu.roll(xc, HEAD_DIM // 2, 1))
        out.append(xc * cos + partner * sin_signed)
    return out


def _qkv_kernel(dilations, first_kept_tile, transpose_kept,
                x_ref, gq_ref, gkv_ref, wq_ref, wkv_ref, cos_ref, sin_ref, cos_step_ref, sin_step_ref,
                q0_ref, q1_ref, q2_ref, k0_ref, k1_ref, k2_ref, v0_ref, v1_ref, v2_ref,
                kf_ref, vf_ref, stage_scr):
    i = pl.program_id(1)
    x = x_ref[0]
    cos_in, sin_in = cos_ref[...], sin_ref[...]
    cos_at, sin_at = cos_step_ref[0:1, :], sin_step_ref[0:1, :]
    cos = cos_in * cos_at - sin_in * sin_at
    sin_signed = sin_in * cos_at + cos_in * sin_at
    chunks_per_group = GROUP_Q_WIDTH // LANES

    def emit(out_ref, dilation, chunks):
        if dilation == 1:
            for c, chunk in enumerate(chunks):
                out_ref[0, :, c * LANES:(c + 1) * LANES] = chunk.astype(BF16)
            return
        for c, chunk in enumerate(chunks):
            stage_scr[c] = chunk
        for r in range(dilation):
            rows = _subseq_rows(stage_scr.shape[1], r, dilation)
            for c in range(len(chunks)):
                out_ref[r, :, c * LANES:(c + 1) * LANES] = stage_scr[c, rows, :].astype(BF16)

    u = _rms(x, gq_ref[...]).astype(BF16)
    q = jnp.dot(u, wq_ref[...], preferred_element_type=F32)
    scale = HEAD_DIM ** -0.5
    q_chunks = _rope(q, cos * scale, sin_signed * scale)
    for gi, q_ref in enumerate((q0_ref, q1_ref, q2_ref)):
        emit(q_ref, dilations[gi], q_chunks[gi * chunks_per_group:(gi + 1) * chunks_per_group])

    un = _rms(x, gkv_ref[...]).astype(BF16)
    kv = jnp.dot(un, wkv_ref[...], preferred_element_type=F32)
    k_chunks = _rope(kv[:, :KV_WIDTH], cos, sin_signed)
    v_chunks = [kv[:, KV_WIDTH + gi * LANES:KV_WIDTH + (gi + 1) * LANES] for gi in range(N_GROUPS)]
    for gi, (k_ref, v_ref) in enumerate(((k0_ref, v0_ref), (k1_ref, v1_ref), (k2_ref, v2_ref))):
        emit(k_ref, dilations[gi], [k_chunks[gi]])
        emit(v_ref, dilations[gi], [v_chunks[gi]])

    @pl.when(i >= first_kept_tile)
    def _():
        for gi in range(N_GROUPS):
            cs = slice(gi * LANES, (gi + 1) * LANES)
            if transpose_kept:
                kf_ref[0, cs, :] = k_chunks[gi].T
                vf_ref[0, cs, :] = v_chunks[gi].T
            else:
                kf_ref[0, :, cs] = k_chunks[gi]
                vf_ref[0, :, cs] = v_chunks[gi]


def _qkv(x, g_q, g_kv, w_q, w_kv, rope, tm, keep_rows, dilations, transpose_kept):
    b, t, d = x.shape
    cos, sin_signed, cos_step, sin_step = rope
    assert t % tm == 0 and keep_rows % tm == 0 and all(tm % (16 * dil) == 0 for dil in dilations)
    first_kept_tile = (t - keep_rows) // tm
    kept_block = lambda bi, i: jnp.maximum(i - first_kept_tile, 0)
    if transpose_kept:
        kept = pl.BlockSpec((1, KV_WIDTH, tm), lambda bi, i: (bi, 0, kept_block(bi, i)))
        kept_shape = jax.ShapeDtypeStruct((b, KV_WIDTH, keep_rows), F32)
    else:
        kept = pl.BlockSpec((1, tm, KV_WIDTH), lambda bi, i: (bi, kept_block(bi, i), 0))
        kept_shape = jax.ShapeDtypeStruct((b, keep_rows, KV_WIDTH), F32)
    sub_shape = lambda dil, width: jax.ShapeDtypeStruct((b, dil, t // dil, width), BF16)
    q_specs = [_subseq_spec(dil, tm, GROUP_Q_WIDTH) for dil in dilations]
    kv_specs = [_subseq_spec(dil, tm, GROUP_KV_WIDTH) for dil in dilations]
    q_shapes = [sub_shape(dil, GROUP_Q_WIDTH) for dil in dilations]
    kv_shapes = [sub_shape(dil, GROUP_KV_WIDTH) for dil in dilations]
    return pl.pallas_call(
        functools.partial(_qkv_kernel, tuple(dilations), first_kept_tile, transpose_kept),
        grid=(b, t // tm),
        in_specs=[
            pl.BlockSpec((1, tm, d), lambda bi, i: (bi, i, 0)),
            _const_spec(g_q.shape), _const_spec(g_kv.shape),
            _const_spec(w_q.shape), _const_spec(w_kv.shape),
            _const_spec(cos.shape), _const_spec(sin_signed.shape),
            pl.BlockSpec((None,) + cos_step.shape[1:], lambda bi, i: (i, 0, 0)),
            pl.BlockSpec((None,) + sin_step.shape[1:], lambda bi, i: (i, 0, 0)),
        ],
        out_specs=q_specs + kv_specs + kv_specs + [kept, kept],
        out_shape=q_shapes + kv_shapes + kv_shapes + [kept_shape, kept_shape],
        scratch_shapes=[pltpu.VMEM((GROUP_Q_WIDTH // LANES, tm, LANES), F32)],
        compiler_params=pltpu.CompilerParams(
            dimension_semantics=("arbitrary", "arbitrary"), vmem_limit_bytes=VMEM_LIMIT),
        name="qkv_rope",
    )(x, g_q, g_kv, w_q, w_kv, cos, sin_signed, cos_step, sin_step)


def _rope_tables(offsets, tile_starts):
    half = HEAD_DIM // 2
    inv = jnp.tile(ROPE_THETA ** (-jnp.arange(0, HEAD_DIM, 2, dtype=F32) / HEAD_DIM), LANES // half)
    sign = jnp.where((jnp.arange(LANES) // half) % 2 == 0, -1.0, 1.0).astype(F32)
    tables = []
    for positions in (offsets, tile_starts):
        ang = positions.astype(F32)[:, None] * inv[None, :]
        tables += [jnp.cos(ang), jnp.sin(ang) * sign[None, :]]
    rows8 = lambda a: jnp.broadcast_to(a[:, None, :], (a.shape[0], 8, LANES))
    return tables[0], tables[1], rows8(tables[2]), rows8(tables[3])


def _stat_lane(c, k):
    return (1 - c) * HEAD_DIM + k


def _band_attention_kernel(seg_rows, tiles_per_subseq, q_ref, kp_ref, kc_ref, vp_ref, vc_ref,
                           o_ref, stat_ref, k_scr, v_scr, bias_scr, s_scr, p_scr):
    n_seg = q_ref.shape[0] // seg_rows
    bands_per_seg = seg_rows // BAND
    n_bands = n_seg * bands_per_seg
    step = pl.program_id(1)
    lane = lax.broadcasted_iota(jnp.int32, (1, LANES), 1)
    for c in range(KV_PER_GROUP):
        own = (lane // HEAD_DIM) == c
        for u in range(n_seg):
            base = u * (seg_rows + BAND)
            for dst, k_rows, v_rows in (
                    (slice(base, base + BAND), kp_ref[...], vp_ref[...]),
                    (slice(base + BAND, base + BAND + seg_rows),
                     kc_ref[u * seg_rows:(u + 1) * seg_rows, :],
                     vc_ref[u * seg_rows:(u + 1) * seg_rows, :])):
                k_scr[c, dst, :] = jnp.where(own, k_rows, jnp.zeros((), BF16))
                v_scr[c, dst, :] = jnp.where(own, v_rows, jnp.ones((), BF16))
    stat_ref[...] = jnp.ones(stat_ref.shape, F32)

    row = lax.broadcasted_iota(jnp.int32, (BAND, 2 * BAND), 0)
    col = lax.broadcasted_iota(jnp.int32, (BAND, 2 * BAND), 1)
    band_bias = jnp.where((col >= row) & (col <= row + BAND), 0.0, NEG_BIG)
    bias_scr[0] = band_bias
    bias_scr[1] = band_bias + jnp.where(col < BAND, NEG_BIG, 0.0)
    starts_subseq = True if n_seg > 1 else lax.rem(step, tiles_per_subseq) == 0

    def rows_of(g):
        return pl.ds(g * BAND if isinstance(g, int) else pl.multiple_of(g * BAND, BAND), BAND)

    def key_rows_of(g):
        staged = g + g // bands_per_seg
        start = staged * BAND if isinstance(g, int) else pl.multiple_of(staged * BAND, BAND)
        return pl.ds(start, 2 * BAND)

    def scores(g, slot):
        yield KV_PER_GROUP
        q = jnp.concatenate(
            [q_ref[rows_of(g), k * LANES:(k + 1) * LANES] for k in range(Q_PER_KV)], axis=0)
        for c in range(KV_PER_GROUP):
            s_scr[slot, c] = lax.dot_general(q, k_scr[c, key_rows_of(g), :],
                                             (((1,), (1,)), ((), ())),
                                             preferred_element_type=F32)
            yield

    def softmax(g, slot):
        yield KV_PER_GROUP * Q_PER_KV
        first = jnp.logical_and(starts_subseq, g % bands_per_seg == 0).astype(jnp.int32)
        for c in range(KV_PER_GROUP):
            for k in range(Q_PER_KV):
                part = pl.ds(k * BAND, BAND)
                s = s_scr[slot, c, part, :] + bias_scr[first]
                m = jnp.max(s, axis=1, keepdims=True)
                p_scr[slot, c, part, :] = jnp.exp(s - m).astype(BF16)
                stat_ref[rows_of(g), pl.ds(_stat_lane(c, k) + STAT_MAX_OFFSET, 1)] = m
                yield

    def weighted_values(g, slot):
        yield KV_PER_GROUP + Q_PER_KV + 1
        ov = []
        for c in range(KV_PER_GROUP):
            ov.append(jnp.dot(p_scr[slot, c], v_scr[c, key_rows_of(g), :],
                              preferred_element_type=F32))
            yield
        low = lane < HEAD_DIM
        sums = None
        for k in range(Q_PER_KV):
            part = slice(k * BAND, (k + 1) * BAND)
            o_ref[rows_of(g), k * LANES:(k + 1) * LANES] = (
                jnp.where(low, ov[0][part], ov[1][part]).astype(o_ref.dtype))
            sums_k = jnp.where(low, ov[1][part], ov[0][part])
            sums = sums_k if sums is None else jnp.where((lane % HEAD_DIM) == k, sums_k, sums)
            yield
        for c in range(KV_PER_GROUP):
            lanes = pl.ds(_stat_lane(c, 0), Q_PER_KV)
            stat_ref[rows_of(g), lanes] = sums[:, _stat_lane(c, 0):_stat_lane(c, 0) + Q_PER_KV]
        yield

    def probs(g, slot):
        yield KV_PER_GROUP * Q_PER_KV
        first = jnp.logical_and(starts_subseq, g % bands_per_seg == 0).astype(jnp.int32)
        q = jnp.concatenate(
            [q_ref[rows_of(g), k * LANES:(k + 1) * LANES] for k in range(Q_PER_KV)], axis=0)
        for c in range(KV_PER_GROUP):
            s_all = lax.dot_general(q, k_scr[c, key_rows_of(g), :], (((1,), (1,)), ((), ())),
                                    preferred_element_type=F32)
            for k in range(Q_PER_KV):
                s = s_all[k * BAND:(k + 1) * BAND] + bias_scr[first]
                m = jnp.max(s, axis=1, keepdims=True)
                p_scr[slot, c, pl.ds(k * BAND, BAND), :] = jnp.exp(s - m).astype(BF16)
                stat_ref[rows_of(g), pl.ds(_stat_lane(c, k) + STAT_MAX_OFFSET, 1)] = m
                yield

    _interleave(probs(0, 0))

    def steady(t, carry):
        for g, slot in ((2 * t + 1, 1), (2 * t + 2, 0)):
            _interleave(weighted_values(g - 1, 1 - slot), probs(g, slot))
        return carry

    assert n_bands % 2 == 0
    lax.fori_loop(0, (n_bands - 2) // 2, steady, 0)
    last = n_bands - 1
    _interleave(weighted_values(last - 1, (last - 1) % 2), probs(last, last % 2))
    _interleave(weighted_values(last, last % 2))


def _band_attention(q, k, v, block_rows):
    b, dilation, sub, _ = q.shape
    seg_rows = min(sub, block_rows)
    total = dilation * sub
    assert total % block_rows == 0 and block_rows % seg_rows == 0 and sub % seg_rows == 0
    assert seg_rows % BAND == 0 and block_rows >= 2 * BAND
    n_seg = block_rows // seg_rows
    flat = lambda a: a.reshape(b, total, a.shape[-1])
    cur = lambda width: pl.BlockSpec((None, block_rows, width), lambda bi, i: (bi, i, 0))
    prev = pl.BlockSpec((None, BAND, GROUP_KV_WIDTH),
                        lambda bi, i: (bi, jnp.maximum(i * (block_rows // BAND) - 1, 0), 0))
    staged_rows = n_seg * (seg_rows + BAND)
    o, stat = pl.pallas_call(
        functools.partial(_band_attention_kernel, seg_rows, sub // seg_rows),
        grid=(b, total // block_rows),
        in_specs=[cur(GROUP_Q_WIDTH), prev, cur(GROUP_KV_WIDTH), prev, cur(GROUP_KV_WIDTH)],
        out_specs=[cur(GROUP_Q_WIDTH), cur(STAT_LANES)],
        out_shape=[jax.ShapeDtypeStruct((b, total, GROUP_Q_WIDTH), BF16),
                   jax.ShapeDtypeStruct((b, total, STAT_LANES), F32)],
        scratch_shapes=[pltpu.VMEM((KV_PER_GROUP, staged_rows, GROUP_KV_WIDTH), BF16),
                        pltpu.VMEM((KV_PER_GROUP, staged_rows, GROUP_KV_WIDTH), BF16),
                        pltpu.VMEM((2, BAND, 2 * BAND), F32),
                        pltpu.VMEM((2, KV_PER_GROUP, Q_PER_KV * BAND, 2 * BAND), F32),
                        pltpu.VMEM((2, KV_PER_GROUP, Q_PER_KV * BAND, 2 * BAND), BF16)],
        compiler_params=pltpu.CompilerParams(
            dimension_semantics=("arbitrary", "arbitrary"), vmem_limit_bytes=VMEM_LIMIT),
        name=f"band_attention_d{dilation}",
    )(flat(q), flat(k), flat(k), flat(v), flat(v))
    return (o.reshape(b, dilation, sub, GROUP_Q_WIDTH), stat.reshape(b, dilation, sub, STAT_LANES))


SAMPLE_Q_ROWS = 16
NEW_KEY_ROWS = 8
SAMPLE_SEQS_PER_STEP = 4


def _sample_attention_kernel(n_new, q_ref, kn_ref, vn_ref, kc0, kc1, kc2, vc0, vc1, vc2,
                             o_ref, lse_ref):
    n_seq = q_ref.shape[0]
    n_rows = q_ref.shape[2]
    contract_last = (((1,), (1,)), ((), ()))
    cache_refs = ((kc0, vc0), (kc1, vc1), (kc2, vc2))
    bias_c, bias_n = [], []
    for gi, (kc_ref, _) in enumerate(cache_refs):
        dmask = ATTN_DILATIONS[gi] - 1
        n_cache = kc_ref.shape[3]
        assert n_cache == ATTN_WINDOWS[gi]
        t_c = lax.broadcasted_iota(jnp.int32, (n_rows, n_cache), 0) & (n_new - 1)
        j_c = lax.broadcasted_iota(jnp.int32, (n_rows, n_cache), 1)
        valid_c = (j_c >= t_c) & (((j_c - t_c) & dmask) == 0)
        t_n = lax.broadcasted_iota(jnp.int32, (n_rows, NEW_KEY_ROWS), 0) & (n_new - 1)
        j_n = lax.broadcasted_iota(jnp.int32, (n_rows, NEW_KEY_ROWS), 1)
        valid_n = (j_n <= t_n) & (((t_n - j_n) & dmask) == 0)
        bias_c.append(jnp.where(valid_c, 0.0, NEG_BIG))
        bias_n.append(jnp.where(valid_n, 0.0, NEG_BIG))

    def one_sequence(b):
        yield len(cache_refs) * KV_PER_GROUP
        kn_all = kn_ref[b]
        vn_all = vn_ref[b]
        for gi, (kc_ref, vc_ref) in enumerate(cache_refs):
            for c in range(KV_PER_GROUP):
                kvh = gi * KV_PER_GROUP + c
                ns = slice(kvh * HEAD_DIM, (kvh + 1) * HEAD_DIM)
                q = q_ref[b, kvh]
                kt = kc_ref[b, c].astype(BF16)
                vt = vc_ref[b, c].astype(BF16)
                s_c = jnp.dot(q, kt, preferred_element_type=F32) + bias_c[gi]
                s_n = lax.dot_general(q, kn_all[:, ns].astype(BF16), contract_last,
                                      preferred_element_type=F32) + bias_n[gi]
                m = jnp.maximum(jnp.max(s_c, axis=1, keepdims=True),
                                jnp.max(s_n, axis=1, keepdims=True))
                p_c = jnp.exp(s_c - m)
                p_n = jnp.exp(s_n - m)
                l = jnp.sum(p_c, axis=1, keepdims=True) + jnp.sum(p_n, axis=1, keepdims=True)
                o = (lax.dot_general(p_c.astype(BF16), vt, contract_last,
                                     preferred_element_type=F32)
                     + jnp.dot(p_n.astype(BF16), vn_all[:, ns].astype(BF16),
                               preferred_element_type=F32))
                o_ref[b, kvh] = o / l
                lse_ref[b, kvh] = m + jnp.log(l)
                yield

    _interleave(*[one_sequence(b) for b in range(n_seq)])


def _sample_attention(q16, k_new, v_new, cache_kt, cache_vt, n_new):
    b = q16.shape[0]
    n_kv = q16.shape[1]
    assert cache_kt.shape[3] == KV_WINDOW and n_new & (n_new - 1) == 0 and n_new <= NEW_KEY_ROWS

    per_step = SAMPLE_SEQS_PER_STEP
    assert b % per_step == 0

    def cache_spec(gi):
        cols = ATTN_WINDOWS[gi]
        last = KV_WINDOW // cols - 1
        return pl.BlockSpec((per_step, KV_PER_GROUP, HEAD_DIM, cols),
                            lambda bi: (bi, gi, 0, last))

    whole = lambda a: pl.BlockSpec((per_step,) + a.shape[1:],
                                   lambda bi: (bi,) + (0,) * (a.ndim - 1))
    o_shape = jax.ShapeDtypeStruct((b, n_kv, SAMPLE_Q_ROWS, HEAD_DIM), F32)
    lse_shape = jax.ShapeDtypeStruct((b, n_kv, SAMPLE_Q_ROWS, 1), F32)
    return pl.pallas_call(
        functools.partial(_sample_attention_kernel, n_new),
        grid=(b // per_step,),
        in_specs=[whole(q16), whole(k_new), whole(v_new)]
        + [cache_spec(gi) for gi in range(N_GROUPS)] * 2,
        out_specs=[whole(o_shape), whole(lse_shape)],
        out_shape=[o_shape, lse_shape],
        compiler_params=pltpu.CompilerParams(
            dimension_semantics=("arbitrary",), vmem_limit_bytes=VMEM_LIMIT),
        name="sample_attention",
    )(q16, k_new, v_new, cache_kt, cache_kt, cache_kt, cache_vt, cache_vt, cache_vt)


def _attn_mix_steps(dilations, x_ref, o_refs, s_refs, g, wo_ref, o_scr, stat_scr, a_scr,
                    x1_ref, h_ref):
    tm = x_ref.shape[0]
    chunks_per_group = GROUP_Q_WIDTH // LANES
    yield 2 * N_GROUPS + 1 + 2 * len(_row_blocks(tm))
    for gi, (o_ref, s_ref) in enumerate(zip(o_refs, s_refs)):
        for r in range(dilations[gi]):
            rows = _subseq_rows(tm, r, dilations[gi])
            stat_scr[gi, rows, :] = s_ref[r]
            for c in range(chunks_per_group):
                o_scr[gi * chunks_per_group + c, rows, :] = (
                    o_ref[r, :, c * LANES:(c + 1) * LANES].astype(F32))
        yield
    sums = [stat_scr[gi] for gi in range(N_GROUPS)]
    maxes = [pltpu.roll(sm, STAT_LANES - STAT_MAX_OFFSET, 1) for sm in sums]
    top = jnp.maximum(jnp.maximum(maxes[0], maxes[1]), maxes[2])
    es = [jnp.exp(mx - top) for mx in maxes]
    den = sums[0] * es[0] + sums[1] * es[1] + sums[2] * es[2]
    slot_id = lax.broadcasted_iota(jnp.int32, (tm, GROUP_Q_WIDTH), 1) // HEAD_DIM
    for gi in range(N_GROUPS):
        scale = es[gi] / den
        wide = jnp.zeros((tm, GROUP_Q_WIDTH), F32)
        for slot in range(HEADS_PER_GROUP):
            stat_lane = _stat_lane(slot % KV_PER_GROUP, slot // KV_PER_GROUP)
            wide = jnp.where(slot_id == slot, scale[:, stat_lane:stat_lane + 1], wide)
        for c in range(chunks_per_group):
            ci = gi * chunks_per_group + c
            a_scr[:, ci * LANES:(ci + 1) * LANES] = (
                o_scr[ci] * wide[:, c * LANES:(c + 1) * LANES]).astype(BF16)
        yield
    mix = jnp.dot(a_scr[...], wo_ref[...], preferred_element_type=F32)
    yield
    tail = _chain(
        _residual_norm_steps(x1_ref, lambda rows: x_ref[rows, :], lambda rows: mix[rows], g[1:2]),
        _norm_cast_steps(h_ref, lambda rows: x1_ref[rows, :], g[2:3]))
    next(tail)
    yield from tail


def _attn_mix_scratch(tm):
    return [pltpu.VMEM((Q_WIDTH // LANES, tm, LANES), F32),
            pltpu.VMEM((N_GROUPS, tm, STAT_LANES), F32),
            pltpu.VMEM((tm, Q_WIDTH), BF16)]


def _layer1_single_kernel(dilations, x_ref, o0_ref, o1_ref, o2_ref, s0_ref, s1_ref, s2_ref, g_ref,
                          wo_ref, wup_ref, wdn_ref, y_ref, o_scr, stat_scr, a_scr,
                          x1_scr, h_scr, acc_scr):
    g = g_ref[...]

    @pl.when(pl.program_id(0) == 0)
    def _():
        _interleave(_attn_mix_steps(dilations, x_ref, (o0_ref, o1_ref, o2_ref),
                                    (s0_ref, s1_ref, s2_ref), g, wo_ref, o_scr, stat_scr, a_scr,
                                    x1_scr, h_scr))
        acc_scr[...] = jnp.zeros(acc_scr.shape, F32)

    def emit(y):
        y_ref[...] = y

    _streamed_mlp_step(g, wup_ref, wdn_ref, x1_scr, h_scr, acc_scr, emit)


def _layer1_prompt_kernel(dilations, x_ref, o0_ref, o1_ref, o2_ref, s0_ref, s1_ref, s2_ref,
                          g_ref, wo_ref, wup_ref, wdn_ref, y_ref,
                          o_scr, stat_scr, a_scr, x1_scr, h_scr, acc_scr):
    s = pl.program_id(0)
    n_tiles = pl.num_programs(0) - 2
    g = g_ref[...]

    def prepare():
        return _attn_mix_steps(dilations, x_ref, (o0_ref, o1_ref, o2_ref),
                               (s0_ref, s1_ref, s2_ref), g, wo_ref, o_scr, stat_scr, a_scr,
                               x1_scr.at[lax.rem(s, 3)], h_scr.at[lax.rem(s, 2)])

    def finish():
        x1_slot, acc_slot = lax.rem(s + 1, 3), lax.rem(s, 2)
        return _residual_norm_steps(y_ref, lambda rows: x1_scr[x1_slot, rows, :],
                                    lambda rows: acc_scr[acc_slot, rows, :], g[3:4])

    @pl.when(s == 0)
    def _():
        _interleave(prepare())
        x1_scr[2] = jnp.zeros(x1_scr.shape[1:], F32)
        acc_scr[1] = jnp.zeros(acc_scr.shape[1:], F32)

    @pl.when(jnp.logical_and(s > 0, s <= n_tiles))
    def _():
        mlp_slot = lax.rem(s + 1, 2)

        def store_acc(acc):
            acc_scr[mlp_slot] = acc

        _interleave(
            finish(),
            _mlp_matmul_steps(lambda: h_scr[mlp_slot], wup_ref, wdn_ref, store_acc),
            prepare())

    @pl.when(s == n_tiles + 1)
    def _():
        _interleave(finish())


def _layer1(x, os, stats, gains, w_o, w_up, w_down, layer, tm):
    b, t, d = x.shape
    dilations = tuple(o.shape[1] for o in os)
    tiles_per_seq = t // tm
    n_tiles = b * tiles_per_seq
    assert t % tm == 0 and all(tm % (16 * dil) == 0 for dil in dilations)
    x2 = x.reshape(b * t, d)
    weights = (gains, w_o, w_up, w_down)
    weight_specs = [_const_spec(gains.shape), _const_spec(w_o.shape),
                    _layer_spec(w_up, layer), _layer_spec(w_down, layer)]

    def tile_specs(tile):
        sub = lambda dil, width: pl.BlockSpec(
            (None, dil, tm // dil, width),
            lambda s: (tile(s) // tiles_per_seq, 0, tile(s) % tiles_per_seq, 0))
        return ([pl.BlockSpec((tm, d), lambda s: (tile(s), 0))]
                + [sub(dil, GROUP_Q_WIDTH) for dil in dilations]
                + [sub(dil, STAT_LANES) for dil in dilations])

    if n_tiles == 1:
        n_chunks, mlp_specs = _streamed_mlp_specs(w_up, w_down, layer)
        y = pl.pallas_call(
            functools.partial(_layer1_single_kernel, dilations),
            grid=(n_chunks,),
            in_specs=tile_specs(lambda s: 0) + weight_specs[:2] + mlp_specs,
            out_specs=pl.BlockSpec((tm, d), lambda s: (0, 0)),
            out_shape=jax.ShapeDtypeStruct((b * t, d), F32),
            scratch_shapes=_attn_mix_scratch(tm) + [
                pltpu.VMEM((tm, d), F32), pltpu.VMEM((tm, d), BF16), pltpu.VMEM((tm, d), F32)],
            compiler_params=pltpu.CompilerParams(
                dimension_semantics=("arbitrary",), vmem_limit_bytes=VMEM_LIMIT),
            name="layer1_single_tile",
        )(x2, *os, *stats, *weights)
        return y.reshape(b, t, d)

    scratch = _attn_mix_scratch(tm) + [
        pltpu.VMEM((3, tm, d), F32), pltpu.VMEM((2, tm, d), BF16), pltpu.VMEM((2, tm, d), F32)]
    y = pl.pallas_call(
        functools.partial(_layer1_prompt_kernel, dilations),
        grid=(n_tiles + 2,),
        in_specs=tile_specs(lambda s: jnp.minimum(s, n_tiles - 1)) + weight_specs,
        out_specs=pl.BlockSpec((tm, d), lambda s: (jnp.maximum(s - 2, 0), 0)),
        out_shape=jax.ShapeDtypeStruct((b * t, d), F32),
        scratch_shapes=scratch,
        compiler_params=pltpu.CompilerParams(
            dimension_semantics=("arbitrary",), vmem_limit_bytes=VMEM_LIMIT),
        name="layer1_prompt",
    )(x2, *os, *stats, *weights)
    return y.reshape(b, t, d)


PROMPT_TILE = 512
LAYER1_TILE = 512
QKV_TILE = 1024
ATTN_BLOCK_ROWS = 2048


def kernel(x_prompt, x_sample, cache_pool, cache_k, cache_v, norm_gains, kv_norm_gain, w_pool,
           pool_scale, w_q, w_o, w_kv, w_up, w_down):
    depth = norm_gains.shape[0]
    assert depth == 2 and cache_pool.shape[0] == 1 and w_q.shape[0] == 1
    bp, tp, d = x_prompt.shape
    bs, ts, _ = x_sample.shape

    g0, g1 = norm_gains[0], norm_gains[1]
    g1_q = g1[0:1]
    g_kv = kv_norm_gain[None, :]
    wp = w_pool[0].astype(BF16)
    ps = pool_scale[0][None, :]
    wq = w_q[0].reshape(d, N_GROUPS, KV_PER_GROUP, Q_PER_KV, HEAD_DIM).transpose(0, 1, 3, 2, 4)
    wq = wq.reshape(d, Q_WIDTH).astype(BF16)
    wo = w_o[0].reshape(N_GROUPS, KV_PER_GROUP, Q_PER_KV, HEAD_DIM, d).transpose(0, 2, 1, 3, 4)
    wo = wo.reshape(Q_WIDTH, d).astype(BF16)
    wkv = w_kv.astype(BF16)
    wup, wdn = w_up.astype(BF16), w_down.astype(BF16)

    keep = min(KV_WINDOW, tp)
    xp1, utail = _layer0_prompt(x_prompt, g0, wp, ps, wup, wdn, 0, PROMPT_TILE)
    rope_p = _rope_tables(jnp.arange(QKV_TILE), jnp.arange(tp // QKV_TILE) * QKV_TILE)
    qkv_p = _qkv(xp1, g1_q, g_kv, wq, wkv, rope_p, QKV_TILE, keep, ATTN_DILATIONS, True)
    os_p, stats_p = [], []
    for gi, dil in enumerate(ATTN_DILATIONS):
        o, stat = _band_attention(qkv_p[gi], qkv_p[N_GROUPS + gi], qkv_p[2 * N_GROUPS + gi],
                                  ATTN_BLOCK_ROWS)
        os_p.append(o)
        stats_p.append(stat)
    y_prompt = _layer1(xp1, os_p, stats_p, g1, wo, wup, wdn, 1, LAYER1_TILE)
    pool_prompt = utail[:, POOL_HALO - POOL_BUF:][None]
    k_prompt = qkv_p[-2].reshape(bp, N_KV_HEADS, HEAD_DIM, keep).transpose(0, 3, 1, 2)
    v_prompt = qkv_p[-1].reshape(bp, N_KV_HEADS, HEAD_DIM, keep).transpose(0, 3, 1, 2)

    n_tok = ts * bs
    xs_tm = jnp.swapaxes(x_sample, 0, 1)
    buf_tm = jnp.swapaxes(cache_pool[0], 0, 1)
    xs1_tm, us_tm = _layer0_sample(xs_tm, buf_tm, PAST_LEN, g0, wp, ps, wup, wdn, 0)
    xs1 = xs1_tm.reshape(1, n_tok, d)
    rope_s = _rope_tables(jnp.arange(n_tok) // bs, jnp.full((1,), PAST_LEN))
    qkv_s = _qkv(xs1, g1_q, g_kv, wq, wkv, rope_s, n_tok, n_tok, (1,) * N_GROUPS, False)
    q_s = jnp.concatenate([q[0, 0] for q in qkv_s[:N_GROUPS]], axis=-1)
    q16 = q_s.reshape(ts, bs, N_GROUPS, Q_PER_KV, KV_PER_GROUP, HEAD_DIM)
    q16 = q16.transpose(1, 2, 4, 3, 0, 5).reshape(bs, N_KV_HEADS, Q_PER_KV, ts, HEAD_DIM)
    q16 = jnp.pad(q16, ((0, 0), (0, 0), (0, SAMPLE_Q_ROWS // ts - Q_PER_KV), (0, 0), (0, 0)))
    q16 = q16.reshape(bs, N_KV_HEADS, SAMPLE_Q_ROWS, HEAD_DIM)
    k_s = jnp.swapaxes(qkv_s[-2].reshape(ts, bs, KV_WIDTH), 0, 1)
    v_s = jnp.swapaxes(qkv_s[-1].reshape(ts, bs, KV_WIDTH), 0, 1)
    pad_new = ((0, 0), (0, NEW_KEY_ROWS - ts), (0, 0))
    o16, lse16 = _sample_attention(
        q16, jnp.pad(k_s, pad_new), jnp.pad(v_s, pad_new),
        cache_k.transpose(0, 2, 3, 1), cache_v.transpose(0, 2, 3, 1), ts)
    heads_padded = SAMPLE_Q_ROWS // ts
    o_s = o16.reshape(bs, N_GROUPS, KV_PER_GROUP, heads_padded, ts, HEAD_DIM)[:, :, :, :Q_PER_KV]
    o_s = o_s.transpose(4, 0, 1, 3, 2, 5).reshape(1, 1, n_tok, N_GROUPS, GROUP_Q_WIDTH).astype(BF16)
    lse_s = lse16.reshape(bs, N_GROUPS, KV_PER_GROUP, heads_padded, ts)[:, :, :, :Q_PER_KV]
    lse_s = lse_s.transpose(4, 0, 1, 2, 3).reshape(1, 1, n_tok, N_GROUPS, KV_PER_GROUP, Q_PER_KV)
    ones = lambda n: jnp.ones((1, 1, n_tok, N_GROUPS, n), F32)
    max_lane = [_stat_lane(c, 0) + STAT_MAX_OFFSET for c in range(KV_PER_GROUP)]
    assert max_lane[1] < max_lane[0]
    stat_s = jnp.concatenate(
        [ones(max_lane[1]), lse_s[..., 1, :],
         ones(max_lane[0] - max_lane[1] - Q_PER_KV), lse_s[..., 0, :],
         ones(STAT_LANES - max_lane[0] - Q_PER_KV)], axis=-1)
    ys_tm = _layer1(xs1, [o_s[:, :, :, gi] for gi in range(N_GROUPS)],
                    [stat_s[:, :, :, gi] for gi in range(N_GROUPS)], g1, wo, wup, wdn, 1, n_tok)
    y_sample = jnp.swapaxes(ys_tm.reshape(ts, bs, d), 0, 1)
    u_s = jnp.swapaxes(us_tm, 0, 1)
    pool_sample = jnp.concatenate([cache_pool[0], u_s], axis=1)[:, -POOL_BUF:][None]
    k_sample = k_s.reshape(bs, ts, N_KV_HEADS, HEAD_DIM)
    v_sample = v_s.reshape(bs, ts, N_KV_HEADS, HEAD_DIM)

    return (y_prompt, y_sample, pool_prompt, k_prompt, v_prompt, pool_sample, k_sample, v_sample)
```

```python
import functools

import jax
import jax.numpy as jnp
from jax import lax
from jax.experimental import pallas as pl
from jax.experimental.pallas import tpu as pltpu

F32 = jnp.float32
BF16 = jnp.bfloat16

EPS = 1e-6
ROPE_THETA = 10000.0
PAST_LEN = 16384
POOL_WINDOWS = (2, 4, 8, 16)
POOL_BUF = max(POOL_WINDOWS) - 1
POOL_HALO = 16
HEAD_DIM = 64
ATTN_WINDOWS = (128, 512, 2048)
ATTN_DILATIONS = (1, 4, 16)
N_GROUPS = len(ATTN_WINDOWS)
KV_PER_GROUP = 2
Q_PER_KV = 3
HEADS_PER_GROUP = KV_PER_GROUP * Q_PER_KV
N_KV_HEADS = N_GROUPS * KV_PER_GROUP
GROUP_Q_WIDTH = HEADS_PER_GROUP * HEAD_DIM
GROUP_KV_WIDTH = KV_PER_GROUP * HEAD_DIM
Q_WIDTH = N_GROUPS * GROUP_Q_WIDTH
KV_WIDTH = N_GROUPS * GROUP_KV_WIDTH
KV_WINDOW = max(ATTN_WINDOWS)
BAND = 128
NEG_BIG = -1e30

LANES = 128
STAT_LANES = LANES
STAT_MAX_OFFSET = 8
VMEM_LIMIT = 56 * 1024 * 1024
FF_CHUNK = 512
NORM_ROW_BLOCKS = 2
POOL_NORM_BLOCKS = 1
POOL_FF_CHUNK = 1024
SAMPLE_FF_CHUNK = 2048

for _w, _d in zip(ATTN_WINDOWS, ATTN_DILATIONS):
    assert _w // _d == BAND and _w % _d == 0
assert GROUP_KV_WIDTH == LANES


def _rms(x, g):
    return x * lax.rsqrt(jnp.mean(x * x, axis=-1, keepdims=True) + EPS) * g


def _mlp_matmul_steps(load_h, wup_ref, wdn_ref, emit_acc, chunk=FF_CHUNK):
    n_chunks = wup_ref.shape[1] // chunk
    yield n_chunks
    h = load_h()
    acc = None
    for c in range(n_chunks):
        cs = slice(c * chunk, (c + 1) * chunk)
        a = jnp.dot(h, wup_ref[:, cs], preferred_element_type=F32)
        a = jnp.square(jnp.maximum(a, 0.0)).astype(BF16)
        part = jnp.dot(a, wdn_ref[cs, :], preferred_element_type=F32)
        acc = part if acc is None else acc + part
        if c + 1 == n_chunks:
            emit_acc(acc)
        yield


def _row_blocks(n_rows, n_blocks=NORM_ROW_BLOCKS):
    size = n_rows // n_blocks if n_rows % (8 * n_blocks) == 0 else n_rows
    return [slice(lo, lo + size) for lo in range(0, n_rows, size)]


def _residual_norm_steps(dst_ref, load_x, load_v, g, n_blocks=NORM_ROW_BLOCKS):
    blocks = _row_blocks(dst_ref.shape[0], n_blocks)
    yield len(blocks)
    for rows in blocks:
        dst_ref[rows, :] = load_x(rows) + _rms(load_v(rows), g)
        yield


def _norm_cast_steps(dst_ref, load_v, g, n_blocks=NORM_ROW_BLOCKS):
    blocks = _row_blocks(dst_ref.shape[0], n_blocks)
    yield len(blocks)
    for rows in blocks:
        dst_ref[rows, :] = _rms(load_v(rows), g).astype(dst_ref.dtype)
        yield


def _chain(*generators):
    counts = [next(gen) for gen in generators]
    yield sum(counts)
    for gen in generators:
        yield from gen


def _interleave(*generators):
    totals = [next(gen) for gen in generators]
    done = [0] * len(generators)
    while any(d < t for d, t in zip(done, totals)):
        i = min((i for i in range(len(generators)) if done[i] < totals[i]),
                key=lambda i: (done[i] + 1) / totals[i])
        next(generators[i])
        done[i] += 1


def _const_spec(shape):
    zeros = (0,) * len(shape)
    return pl.BlockSpec(shape, lambda *_: zeros, pipeline_mode=pl.Buffered(1))


def _layer_spec(w, layer):
    index = (layer,) + (0,) * (w.ndim - 1)
    return pl.BlockSpec((None,) + w.shape[1:], lambda *_: index, pipeline_mode=pl.Buffered(1))


def _subseq_spec(dilation, rows, width):
    return pl.BlockSpec((None, dilation, rows // dilation, width), lambda bi, i: (bi, 0, i, 0))


def _subseq_rows(n_rows, r, dilation):
    n = n_rows // dilation
    return pl.ds(r, n, stride=dilation) if dilation > 1 else pl.ds(0, n)


def _pool_mixer_steps(x_ref, xh_ref, tile_in_seq, g, wp_ref, ps_ref, ext_ref, x1_ref, h_ref,
                      utail_ref=None):
    tm = x_ref.shape[0]
    pool_ch = wp_ref.shape[1]
    blocks = _row_blocks(tm, POOL_NORM_BLOCKS)
    yield 3 * len(blocks) + len(POOL_WINDOWS)
    uh = _rms(xh_ref[0:POOL_HALO, :], g[0:1]) * jnp.where(tile_in_seq > 0, 1.0, 0.0)
    ext_ref[0:POOL_HALO, :] = uh
    for rows in blocks:
        u = _rms(x_ref[rows, :], g[0:1])
        ext_ref[POOL_HALO + rows.start:POOL_HALO + rows.stop, :] = u
        if utail_ref is not None and rows.stop == tm:
            utail_ref[0] = u[u.shape[0] - POOL_HALO:, :]
        yield
    pos = tile_in_seq * tm + lax.broadcasted_iota(jnp.int32, (tm, 1), 0)
    parts = []
    for gi, w in enumerate(POOL_WINDOWS):
        cs = slice(gi * pool_ch, (gi + 1) * pool_ch)
        s = ext_ref[POOL_HALO:, cs]
        for j in range(1, w):
            s = s + ext_ref[POOL_HALO - j:POOL_HALO - j + tm, cs]
        cnt = jnp.minimum(w, pos + 1).astype(F32)
        pooled = s / cnt - ext_ref[POOL_HALO:, cs]
        parts.append(jnp.dot(pooled.astype(BF16), wp_ref[gi], preferred_element_type=F32))
        yield
    mix = jnp.concatenate(parts, axis=1) * ps_ref[...]
    tail = _chain(
        _residual_norm_steps(x1_ref, lambda rows: x_ref[rows, :], lambda rows: mix[rows], g[1:2],
                             POOL_NORM_BLOCKS),
        _norm_cast_steps(h_ref, lambda rows: x1_ref[rows, :], g[2:3], POOL_NORM_BLOCKS))
    next(tail)
    yield from tail


def _layer0_prompt_kernel(tiles_per_seq, x0_ref, xa_ref, xha_ref, xb_ref, xhb_ref, g_ref,
                          wp_ref, ps_ref, wup_ref, wdn_ref, y_ref, utail_ref,
                          x1_scr, h_scr, ext_scr):
    s = pl.program_id(0)
    n_tiles = 2 * pl.num_programs(0)
    tm = xa_ref.shape[0]
    g = g_ref[...]

    def prepare(x_ref, xh_ref, tile, slot, tail_ref):
        return _pool_mixer_steps(x_ref, xh_ref, lax.rem(tile, tiles_per_seq), g, wp_ref, ps_ref,
                                 ext_scr.at[slot], x1_scr.at[slot], h_scr.at[slot], tail_ref)

    @pl.when(s == 0)
    def _():
        _interleave(prepare(x0_ref, x0_ref, 0, 0, None))

    def mlp(slot):
        acc = []
        return _chain(
            _mlp_matmul_steps(lambda: h_scr[slot], wup_ref, wdn_ref, acc.append, POOL_FF_CHUNK),
            _residual_norm_steps(y_ref.at[pl.ds(slot * tm, tm)],
                                 lambda rows: x1_scr[slot, rows, :],
                                 lambda rows: acc[0][rows], g[3:4], POOL_NORM_BLOCKS))

    _interleave(mlp(0), prepare(xa_ref, xha_ref, 2 * s + 1, 1, utail_ref))
    _interleave(mlp(1), prepare(xb_ref, xhb_ref, jnp.minimum(2 * s + 2, n_tiles - 1), 0, None))


def _layer0_prompt(x, gains, w_pool, pool_scale, w_up, w_down, layer, tm):
    b, t, d = x.shape
    tiles_per_seq = t // tm
    n_tiles = b * tiles_per_seq
    assert t % tm == 0 and tm % POOL_HALO == 0 and tiles_per_seq % 2 == 0
    halo_per_tile = tm // POOL_HALO
    x2 = x.reshape(b * t, d)
    tile_a = lambda s: 2 * s + 1
    tile_b = lambda s: jnp.minimum(2 * s + 2, n_tiles - 1)
    tile_spec = lambda tile: pl.BlockSpec((tm, d), lambda s: (tile(s), 0))
    halo_spec = lambda tile: pl.BlockSpec((POOL_HALO, d),
                                          lambda s: (tile(s) * halo_per_tile - 1, 0))
    y, utail = pl.pallas_call(
        functools.partial(_layer0_prompt_kernel, tiles_per_seq),
        grid=(n_tiles // 2,),
        in_specs=[
            pl.BlockSpec((tm, d), lambda s: (0, 0), pipeline_mode=pl.Buffered(1)),
            tile_spec(tile_a), halo_spec(tile_a), tile_spec(tile_b), halo_spec(tile_b),
            _const_spec(gains.shape), _const_spec(w_pool.shape), _const_spec(pool_scale.shape),
            _layer_spec(w_up, layer), _layer_spec(w_down, layer),
        ],
        out_specs=[
            pl.BlockSpec((2 * tm, d), lambda s: (s, 0)),
            pl.BlockSpec((1, POOL_HALO, d), lambda s: (tile_a(s) // tiles_per_seq, 0, 0)),
        ],
        out_shape=[
            jax.ShapeDtypeStruct((b * t, d), F32),
            jax.ShapeDtypeStruct((b, POOL_HALO, d), F32),
        ],
        scratch_shapes=[pltpu.VMEM((2, tm, d), F32), pltpu.VMEM((2, tm, d), BF16),
                        pltpu.VMEM((2, tm + POOL_HALO, d), F32)],
        compiler_params=pltpu.CompilerParams(
            dimension_semantics=("arbitrary",), vmem_limit_bytes=VMEM_LIMIT),
        name="layer0_prompt",
    )(x2, x2, x2, x2, x2, gains, w_pool, pool_scale, w_up, w_down)
    return y.reshape(b, t, d), utail


def _streamed_mlp_step(g, wup_ref, wdn_ref, x1_scr, h_scr, acc_scr, emit):
    a = jnp.dot(h_scr[...], wup_ref[...], preferred_element_type=F32)
    a = jnp.square(jnp.maximum(a, 0.0)).astype(BF16)
    acc_scr[...] += jnp.dot(a, wdn_ref[...], preferred_element_type=F32)

    @pl.when(pl.program_id(0) == pl.num_programs(0) - 1)
    def _():
        emit(x1_scr[...] + _rms(acc_scr[...], g[3:4]))


def _streamed_mlp_specs(w_up, w_down, layer):
    assert w_up.shape[2] % SAMPLE_FF_CHUNK == 0
    return (w_up.shape[2] // SAMPLE_FF_CHUNK,
            [pl.BlockSpec((None, w_up.shape[1], SAMPLE_FF_CHUNK), lambda c: (layer, 0, c)),
             pl.BlockSpec((None, SAMPLE_FF_CHUNK, w_down.shape[2]), lambda c: (layer, c, 0))])


def _layer0_sample_kernel(start_pos, x_ref, buf_ref, g_ref, wp_ref, ps_ref, wup_ref, wdn_ref,
                          y_ref, u_ref, x1_scr, h_scr, acc_scr):
    n_t, n_b, _ = x_ref.shape
    g = g_ref[...]

    @pl.when(pl.program_id(0) == 0)
    def _():
        _pool_mixer_time_major(start_pos, x_ref, buf_ref, g, wp_ref, ps_ref, u_ref, x1_scr, h_scr)
        acc_scr[...] = jnp.zeros(acc_scr.shape, F32)

    def emit(y):
        for t in range(n_t):
            y_ref[t] = y[t * n_b:(t + 1) * n_b]

    _streamed_mlp_step(g, wup_ref, wdn_ref, x1_scr, h_scr, acc_scr, emit)


def _pool_mixer_time_major(start_pos, x_ref, buf_ref, g, wp_ref, ps_ref, u_ref, x1_ref, h_ref):
    n_t = x_ref.shape[0]
    pool_ch = wp_ref.shape[1]
    xs = [x_ref[t] for t in range(n_t)]
    us = [_rms(xt, g[0:1]) for xt in xs]
    ext = [buf_ref[j] for j in range(POOL_BUF)] + us
    parts = []
    for gi, w in enumerate(POOL_WINDOWS):
        cs = slice(gi * pool_ch, (gi + 1) * pool_ch)
        rows = []
        for t in range(n_t):
            s = us[t][:, cs]
            for j in range(1, w):
                s = s + ext[POOL_BUF + t - j][:, cs]
            cnt = float(min(w, start_pos + t + 1))
            rows.append(s / cnt - us[t][:, cs])
        pooled = jnp.concatenate(rows, axis=0)
        parts.append(jnp.dot(pooled.astype(BF16), wp_ref[gi], preferred_element_type=F32))
    mix = jnp.concatenate(parts, axis=1) * ps_ref[...]
    x = jnp.concatenate(xs, axis=0)
    x1 = x + _rms(mix, g[1:2])
    x1_ref[...] = x1
    h_ref[...] = _rms(x1, g[2:3]).astype(BF16)
    for t in range(n_t):
        u_ref[t] = us[t]


def _layer0_sample(x_tm, buf_tm, start_pos, gains, w_pool, pool_scale, w_up, w_down, layer):
    assert start_pos + 1 >= max(POOL_WINDOWS) and buf_tm.shape[0] == POOL_BUF
    n_tok, d = x_tm.shape[0] * x_tm.shape[1], x_tm.shape[2]
    n_chunks, weight_specs = _streamed_mlp_specs(w_up, w_down, layer)
    return pl.pallas_call(
        functools.partial(_layer0_sample_kernel, start_pos),
        grid=(n_chunks,),
        in_specs=[_const_spec(a.shape) for a in (x_tm, buf_tm, gains, w_pool, pool_scale)]
        + weight_specs,
        out_specs=[pl.BlockSpec(x_tm.shape, lambda c: (0, 0, 0))] * 2,
        out_shape=[jax.ShapeDtypeStruct(x_tm.shape, F32), jax.ShapeDtypeStruct(x_tm.shape, F32)],
        scratch_shapes=[pltpu.VMEM((n_tok, d), F32), pltpu.VMEM((n_tok, d), BF16),
                        pltpu.VMEM((n_tok, d), F32)],
        compiler_params=pltpu.CompilerParams(
            dimension_semantics=("arbitrary",), vmem_limit_bytes=VMEM_LIMIT),
        name="layer0_sample",
    )(x_tm, buf_tm, gains, w_pool, pool_scale, w_up, w_down)


def _rope(x, cos, sin_signed):
    lane = lax.broadcasted_iota(jnp.int32, (x.shape[0], LANES), 1)
    first_half = (lane & (HEAD_DIM // 2)) == 0
    out = []
    for c in range(x.shape[1] // LANES):
        xc = x[:, c * LANES:(c + 1) * LANES]
        partner = jnp.where(first_half,
                            pltpu.roll(xc, LANES - HEAD_DIM // 2, 1),
                            pltpu.roll(xc, HEAD_DIM // 2, 1))
        out.append(xc * cos + partner * sin_signed)
    return out


def _qkv_kernel(dilations, first_kept_tile, transpose_kept,
                x_ref, gq_ref, gkv_ref, wq_ref, wkv_ref, cos_ref, sin_ref, cos_step_ref, sin_step_ref,
                q0_ref, q1_ref, q2_ref, k0_ref, k1_ref, k2_ref, v0_ref, v1_ref, v2_ref,
                kf_ref, vf_ref, stage_scr):
    i = pl.program_id(1)
    x = x_ref[0]
    cos_in, sin_in = cos_ref[...], sin_ref[...]
    cos_at, sin_at = cos_step_ref[0:1, :], sin_step_ref[0:1, :]
    cos = cos_in * cos_at - sin_in * sin_at
    sin_signed = sin_in * cos_at + cos_in * sin_at
    chunks_per_group = GROUP_Q_WIDTH // LANES

    def emit(out_ref, dilation, chunks):
        if dilation == 1:
            for c, chunk in enumerate(chunks):
                out_ref[0, :, c * LANES:(c + 1) * LANES] = chunk.astype(BF16)
            return
        for c, chunk in enumerate(chunks):
            stage_scr[c] = chunk
        for r in range(dilation):
            rows = _subseq_rows(stage_scr.shape[1], r, dilation)
            for c in range(len(chunks)):
                out_ref[r, :, c * LANES:(c + 1) * LANES] = stage_scr[c, rows, :].astype(BF16)

    u = _rms(x, gq_ref[...]).astype(BF16)
    q = jnp.dot(u, wq_ref[...], preferred_element_type=F32)
    scale = HEAD_DIM ** -0.5
    q_chunks = _rope(q, cos * scale, sin_signed * scale)
    for gi, q_ref in enumerate((q0_ref, q1_ref, q2_ref)):
        emit(q_ref, dilations[gi], q_chunks[gi * chunks_per_group:(gi + 1) * chunks_per_group])

    un = _rms(x, gkv_ref[...]).astype(BF16)
    kv = jnp.dot(un, wkv_ref[...], preferred_element_type=F32)
    k_chunks = _rope(kv[:, :KV_WIDTH], cos, sin_signed)
    v_chunks = [kv[:, KV_WIDTH + gi * LANES:KV_WIDTH + (gi + 1) * LANES] for gi in range(N_GROUPS)]
    for gi, (k_ref, v_ref) in enumerate(((k0_ref, v0_ref), (k1_ref, v1_ref), (k2_ref, v2_ref))):
        emit(k_ref, dilations[gi], [k_chunks[gi]])
        emit(v_ref, dilations[gi], [v_chunks[gi]])

    @pl.when(i >= first_kept_tile)
    def _():
        for gi in range(N_GROUPS):
            cs = slice(gi * LANES, (gi + 1) * LANES)
            if transpose_kept:
                kf_ref[0, cs, :] = k_chunks[gi].T
                vf_ref[0, cs, :] = v_chunks[gi].T
            else:
                kf_ref[0, :, cs] = k_chunks[gi]
                vf_ref[0, :, cs] = v_chunks[gi]


def _qkv(x, g_q, g_kv, w_q, w_kv, rope, tm, keep_rows, dilations, transpose_kept):
    b, t, d = x.shape
    cos, sin_signed, cos_step, sin_step = rope
    assert t % tm == 0 and keep_rows % tm == 0 and all(tm % (16 * dil) == 0 for dil in dilations)
    first_kept_tile = (t - keep_rows) // tm
    kept_block = lambda bi, i: jnp.maximum(i - first_kept_tile, 0)
    if transpose_kept:
        kept = pl.BlockSpec((1, KV_WIDTH, tm), lambda bi, i: (bi, 0, kept_block(bi, i)))
        kept_shape = jax.ShapeDtypeStruct((b, KV_WIDTH, keep_rows), F32)
    else:
        kept = pl.BlockSpec((1, tm, KV_WIDTH), lambda bi, i: (bi, kept_block(bi, i), 0))
        kept_shape = jax.ShapeDtypeStruct((b, keep_rows, KV_WIDTH), F32)
    sub_shape = lambda dil, width: jax.ShapeDtypeStruct((b, dil, t // dil, width), BF16)
    q_specs = [_subseq_spec(dil, tm, GROUP_Q_WIDTH) for dil in dilations]
    kv_specs = [_subseq_spec(dil, tm, GROUP_KV_WIDTH) for dil in dilations]
    q_shapes = [sub_shape(dil, GROUP_Q_WIDTH) for dil in dilations]
    kv_shapes = [sub_shape(dil, GROUP_KV_WIDTH) for dil in dilations]
    return pl.pallas_call(
        functools.partial(_qkv_kernel, tuple(dilations), first_kept_tile, transpose_kept),
        grid=(b, t // tm),
        in_specs=[
            pl.BlockSpec((1, tm, d), lambda bi, i: (bi, i, 0)),
            _const_spec(g_q.shape), _const_spec(g_kv.shape),
            _const_spec(w_q.shape), _const_spec(w_kv.shape),
            _const_spec(cos.shape), _const_spec(sin_signed.shape),
            pl.BlockSpec((None,) + cos_step.shape[1:], lambda bi, i: (i, 0, 0)),
            pl.BlockSpec((None,) + sin_step.shape[1:], lambda bi, i: (i, 0, 0)),
        ],
        out_specs=q_specs + kv_specs + kv_specs + [kept, kept],
        out_shape=q_shapes + kv_shapes + kv_shapes + [kept_shape, kept_shape],
        scratch_shapes=[pltpu.VMEM((GROUP_Q_WIDTH // LANES, tm, LANES), F32)],
        compiler_params=pltpu.CompilerParams(
            dimension_semantics=("arbitrary", "arbitrary"), vmem_limit_bytes=VMEM_LIMIT),
        name="qkv_rope",
    )(x, g_q, g_kv, w_q, w_kv, cos, sin_signed, cos_step, sin_step)


def _rope_tables(offsets, tile_starts):
    half = HEAD_DIM // 2
    inv = jnp.tile(ROPE_THETA ** (-jnp.arange(0, HEAD_DIM, 2, dtype=F32) / HEAD_DIM), LANES // half)
    sign = jnp.where((jnp.arange(LANES) // half) % 2 == 0, -1.0, 1.0).astype(F32)
    tables = []
    for positions in (offsets, tile_starts):
        ang = positions.astype(F32)[:, None] * inv[None, :]
        tables += [jnp.cos(ang), jnp.sin(ang) * sign[None, :]]
    rows8 = lambda a: jnp.broadcast_to(a[:, None, :], (a.shape[0], 8, LANES))
    return tables[0], tables[1], rows8(tables[2]), rows8(tables[3])


def _stat_lane(c, k):
    return (1 - c) * HEAD_DIM + k


def _band_attention_kernel(seg_rows, tiles_per_subseq, q_ref, kp_ref, kc_ref, vp_ref, vc_ref,
                           o_ref, stat_ref, k_scr, v_scr, bias_scr, s_scr, p_scr):
    n_seg = q_ref.shape[0] // seg_rows
    bands_per_seg = seg_rows // BAND
    n_bands = n_seg * bands_per_seg
    step = pl.program_id(1)
    lane = lax.broadcasted_iota(jnp.int32, (1, LANES), 1)
    for c in range(KV_PER_GROUP):
        own = (lane // HEAD_DIM) == c
        for u in range(n_seg):
            base = u * (seg_rows + BAND)
            for dst, k_rows, v_rows in (
                    (slice(base, base + BAND), kp_ref[...], vp_ref[...]),
                    (slice(base + BAND, base + BAND + seg_rows),
                     kc_ref[u * seg_rows:(u + 1) * seg_rows, :],
                     vc_ref[u * seg_rows:(u + 1) * seg_rows, :])):
                k_scr[c, dst, :] = jnp.where(own, k_rows, jnp.zeros((), BF16))
                v_scr[c, dst, :] = jnp.where(own, v_rows, jnp.ones((), BF16))
    stat_ref[...] = jnp.ones(stat_ref.shape, F32)

    row = lax.broadcasted_iota(jnp.int32, (BAND, 2 * BAND), 0)
    col = lax.broadcasted_iota(jnp.int32, (BAND, 2 * BAND), 1)
    band_bias = jnp.where((col >= row) & (col <= row + BAND), 0.0, NEG_BIG)
    bias_scr[0] = band_bias
    bias_scr[1] = band_bias + jnp.where(col < BAND, NEG_BIG, 0.0)
    starts_subseq = True if n_seg > 1 else lax.rem(step, tiles_per_subseq) == 0

    def rows_of(g):
        return pl.ds(g * BAND if isinstance(g, int) else pl.multiple_of(g * BAND, BAND), BAND)

    def key_rows_of(g):
        staged = g + g // bands_per_seg
        start = staged * BAND if isinstance(g, int) else pl.multiple_of(staged * BAND, BAND)
        return pl.ds(start, 2 * BAND)

    def scores(g, slot):
        yield KV_PER_GROUP
        q = jnp.concatenate(
            [q_ref[rows_of(g), k * LANES:(k + 1) * LANES] for k in range(Q_PER_KV)], axis=0)
        for c in range(KV_PER_GROUP):
            s_scr[slot, c] = lax.dot_general(q, k_scr[c, key_rows_of(g), :],
                                             (((1,), (1,)), ((), ())),
                                             preferred_element_type=F32)
            yield

    def softmax(g, slot):
        yield KV_PER_GROUP * Q_PER_KV
        first = jnp.logical_and(starts_subseq, g % bands_per_seg == 0).astype(jnp.int32)
        for c in range(KV_PER_GROUP):
            for k in range(Q_PER_KV):
                part = pl.ds(k * BAND, BAND)
                s = s_scr[slot, c, part, :] + bias_scr[first]
                m = jnp.max(s, axis=1, keepdims=True)
                p_scr[slot, c, part, :] = jnp.exp(s - m).astype(BF16)
                stat_ref[rows_of(g), pl.ds(_stat_lane(c, k) + STAT_MAX_OFFSET, 1)] = m
                yield

    def weighted_values(g, slot):
        yield KV_PER_GROUP + Q_PER_KV + 1
        ov = []
        for c in range(KV_PER_GROUP):
            ov.append(jnp.dot(p_scr[slot, c], v_scr[c, key_rows_of(g), :],
                              preferred_element_type=F32))
            yield
        low = lane < HEAD_DIM
        sums = None
        for k in range(Q_PER_KV):
            part = slice(k * BAND, (k + 1) * BAND)
            o_ref[rows_of(g), k * LANES:(k + 1) * LANES] = (
                jnp.where(low, ov[0][part], ov[1][part]).astype(o_ref.dtype))
            sums_k = jnp.where(low, ov[1][part], ov[0][part])
            sums = sums_k if sums is None else jnp.where((lane % HEAD_DIM) == k, sums_k, sums)
            yield
        for c in range(KV_PER_GROUP):
            lanes = pl.ds(_stat_lane(c, 0), Q_PER_KV)
            stat_ref[rows_of(g), lanes] = sums[:, _stat_lane(c, 0):_stat_lane(c, 0) + Q_PER_KV]
        yield

    _interleave(scores(0, 0))
    _interleave(softmax(0, 0), scores(1, 1))

    def steady(t, carry):
        for g, slot in ((2 * t + 1, 1), (2 * t + 2, 0)):
            _interleave(weighted_values(g - 1, 1 - slot), softmax(g, slot),
                        scores(g + 1, 1 - slot))
        return carry

    assert n_bands % 2 == 0
    lax.fori_loop(0, (n_bands - 2) // 2, steady, 0)
    last = n_bands - 1
    _interleave(weighted_values(last - 1, (last - 1) % 2), softmax(last, last % 2))
    _interleave(weighted_values(last, last % 2))


def _band_attention(q, k, v, block_rows):
    b, dilation, sub, _ = q.shape
    seg_rows = min(sub, block_rows)
    total = dilation * sub
    assert total % block_rows == 0 and block_rows % seg_rows == 0 and sub % seg_rows == 0
    assert seg_rows % BAND == 0 and block_rows >= 2 * BAND
    n_seg = block_rows // seg_rows
    flat = lambda a: a.reshape(b, total, a.shape[-1])
    cur = lambda width: pl.BlockSpec((None, block_rows, width), lambda bi, i: (bi, i, 0))
    prev = pl.BlockSpec((None, BAND, GROUP_KV_WIDTH),
                        lambda bi, i: (bi, jnp.maximum(i * (block_rows // BAND) - 1, 0), 0))
    staged_rows = n_seg * (seg_rows + BAND)
    o, stat = pl.pallas_call(
        functools.partial(_band_attention_kernel, seg_rows, sub // seg_rows),
        grid=(b, total // block_rows),
        in_specs=[cur(GROUP_Q_WIDTH), prev, cur(GROUP_KV_WIDTH), prev, cur(GROUP_KV_WIDTH)],
        out_specs=[cur(GROUP_Q_WIDTH), cur(STAT_LANES)],
        out_shape=[jax.ShapeDtypeStruct((b, total, GROUP_Q_WIDTH), BF16),
                   jax.ShapeDtypeStruct((b, total, STAT_LANES), F32)],
        scratch_shapes=[pltpu.VMEM((KV_PER_GROUP, staged_rows, GROUP_KV_WIDTH), BF16),
                        pltpu.VMEM((KV_PER_GROUP, staged_rows, GROUP_KV_WIDTH), BF16),
                        pltpu.VMEM((2, BAND, 2 * BAND), F32),
                        pltpu.VMEM((2, KV_PER_GROUP, Q_PER_KV * BAND, 2 * BAND), F32),
                        pltpu.VMEM((2, KV_PER_GROUP, Q_PER_KV * BAND, 2 * BAND), BF16)],
        compiler_params=pltpu.CompilerParams(
            dimension_semantics=("arbitrary", "arbitrary"), vmem_limit_bytes=VMEM_LIMIT),
        name=f"band_attention_d{dilation}",
    )(flat(q), flat(k), flat(k), flat(v), flat(v))
    return (o.reshape(b, dilation, sub, GROUP_Q_WIDTH), stat.reshape(b, dilation, sub, STAT_LANES))


SAMPLE_Q_ROWS = 16
NEW_KEY_ROWS = 8
SAMPLE_SEQS_PER_STEP = 4


def _sample_attention_kernel(n_new, q_ref, kn_ref, vn_ref, kc0, kc1, kc2, vc0, vc1, vc2,
                             o_ref, lse_ref):
    n_seq = q_ref.shape[0]
    n_rows = q_ref.shape[2]
    contract_last = (((1,), (1,)), ((), ()))
    cache_refs = ((kc0, vc0), (kc1, vc1), (kc2, vc2))
    bias_c, bias_n = [], []
    for gi, (kc_ref, _) in enumerate(cache_refs):
        dmask = ATTN_DILATIONS[gi] - 1
        n_cache = kc_ref.shape[3]
        assert n_cache == ATTN_WINDOWS[gi]
        t_c = lax.broadcasted_iota(jnp.int32, (n_rows, n_cache), 0) & (n_new - 1)
        j_c = lax.broadcasted_iota(jnp.int32, (n_rows, n_cache), 1)
        valid_c = (j_c >= t_c) & (((j_c - t_c) & dmask) == 0)
        t_n = lax.broadcasted_iota(jnp.int32, (n_rows, NEW_KEY_ROWS), 0) & (n_new - 1)
        j_n = lax.broadcasted_iota(jnp.int32, (n_rows, NEW_KEY_ROWS), 1)
        valid_n = (j_n <= t_n) & (((t_n - j_n) & dmask) == 0)
        bias_c.append(jnp.where(valid_c, 0.0, NEG_BIG))
        bias_n.append(jnp.where(valid_n, 0.0, NEG_BIG))

    def one_sequence(b):
        yield len(cache_refs) * KV_PER_GROUP
        kn_all = kn_ref[b]
        vn_all = vn_ref[b]
        for gi, (kc_ref, vc_ref) in enumerate(cache_refs):
            for c in range(KV_PER_GROUP):
                kvh = gi * KV_PER_GROUP + c
                ns = slice(kvh * HEAD_DIM, (kvh + 1) * HEAD_DIM)
                q = q_ref[b, kvh]
                kt = kc_ref[b, c].astype(BF16)
                vt = vc_ref[b, c].astype(BF16)
                s_c = jnp.dot(q, kt, preferred_element_type=F32) + bias_c[gi]
                s_n = lax.dot_general(q, kn_all[:, ns].astype(BF16), contract_last,
                                      preferred_element_type=F32) + bias_n[gi]
                m = jnp.maximum(jnp.max(s_c, axis=1, keepdims=True),
                                jnp.max(s_n, axis=1, keepdims=True))
                p_c = jnp.exp(s_c - m)
                p_n = jnp.exp(s_n - m)
                l = jnp.sum(p_c, axis=1, keepdims=True) + jnp.sum(p_n, axis=1, keepdims=True)
                o = (lax.dot_general(p_c.astype(BF16), vt, contract_last,
                                     preferred_element_type=F32)
                     + jnp.dot(p_n.astype(BF16), vn_all[:, ns].astype(BF16),
                               preferred_element_type=F32))
                o_ref[b, kvh] = o / l
                lse_ref[b, kvh] = m + jnp.log(l)
                yield

    _interleave(*[one_sequence(b) for b in range(n_seq)])


def _sample_attention(q16, k_new, v_new, cache_kt, cache_vt, n_new):
    b = q16.shape[0]
    n_kv = q16.shape[1]
    assert cache_kt.shape[3] == KV_WINDOW and n_new & (n_new - 1) == 0 and n_new <= NEW_KEY_ROWS

    per_step = SAMPLE_SEQS_PER_STEP
    assert b % per_step == 0

    def cache_spec(gi):
        cols = ATTN_WINDOWS[gi]
        last = KV_WINDOW // cols - 1
        return pl.BlockSpec((per_step, KV_PER_GROUP, HEAD_DIM, cols),
                            lambda bi: (bi, gi, 0, last))

    whole = lambda a: pl.BlockSpec((per_step,) + a.shape[1:],
                                   lambda bi: (bi,) + (0,) * (a.ndim - 1))
    o_shape = jax.ShapeDtypeStruct((b, n_kv, SAMPLE_Q_ROWS, HEAD_DIM), F32)
    lse_shape = jax.ShapeDtypeStruct((b, n_kv, SAMPLE_Q_ROWS, 1), F32)
    return pl.pallas_call(
        functools.partial(_sample_attention_kernel, n_new),
        grid=(b // per_step,),
        in_specs=[whole(q16), whole(k_new), whole(v_new)]
        + [cache_spec(gi) for gi in range(N_GROUPS)] * 2,
        out_specs=[whole(o_shape), whole(lse_shape)],
        out_shape=[o_shape, lse_shape],
        compiler_params=pltpu.CompilerParams(
            dimension_semantics=("arbitrary",), vmem_limit_bytes=VMEM_LIMIT),
        name="sample_attention",
    )(q16, k_new, v_new, cache_kt, cache_kt, cache_kt, cache_vt, cache_vt, cache_vt)


def _attn_mix_steps(dilations, x_ref, o_refs, s_refs, g, wo_ref, o_scr, stat_scr, a_scr,
                    x1_ref, h_ref):
    tm = x_ref.shape[0]
    chunks_per_group = GROUP_Q_WIDTH // LANES
    yield 2 * N_GROUPS + 1 + 2 * len(_row_blocks(tm))
    for gi, (o_ref, s_ref) in enumerate(zip(o_refs, s_refs)):
        for r in range(dilations[gi]):
            rows = _subseq_rows(tm, r, dilations[gi])
            stat_scr[gi, rows, :] = s_ref[r]
            for c in range(chunks_per_group):
                o_scr[gi * chunks_per_group + c, rows, :] = (
                    o_ref[r, :, c * LANES:(c + 1) * LANES].astype(F32))
        yield
    sums = [stat_scr[gi] for gi in range(N_GROUPS)]
    maxes = [pltpu.roll(sm, STAT_LANES - STAT_MAX_OFFSET, 1) for sm in sums]
    top = jnp.maximum(jnp.maximum(maxes[0], maxes[1]), maxes[2])
    es = [jnp.exp(mx - top) for mx in maxes]
    den = sums[0] * es[0] + sums[1] * es[1] + sums[2] * es[2]
    slot_id = lax.broadcasted_iota(jnp.int32, (tm, GROUP_Q_WIDTH), 1) // HEAD_DIM
    for gi in range(N_GROUPS):
        scale = es[gi] / den
        wide = jnp.zeros((tm, GROUP_Q_WIDTH), F32)
        for slot in range(HEADS_PER_GROUP):
            stat_lane = _stat_lane(slot % KV_PER_GROUP, slot // KV_PER_GROUP)
            wide = jnp.where(slot_id == slot, scale[:, stat_lane:stat_lane + 1], wide)
        for c in range(chunks_per_group):
            ci = gi * chunks_per_group + c
            a_scr[:, ci * LANES:(ci + 1) * LANES] = (
                o_scr[ci] * wide[:, c * LANES:(c + 1) * LANES]).astype(BF16)
        yield
    mix = jnp.dot(a_scr[...], wo_ref[...], preferred_element_type=F32)
    yield
    tail = _chain(
        _residual_norm_steps(x1_ref, lambda rows: x_ref[rows, :], lambda rows: mix[rows], g[1:2]),
        _norm_cast_steps(h_ref, lambda rows: x1_ref[rows, :], g[2:3]))
    next(tail)
    yield from tail


def _attn_mix_scratch(tm):
    return [pltpu.VMEM((Q_WIDTH // LANES, tm, LANES), F32),
            pltpu.VMEM((N_GROUPS, tm, STAT_LANES), F32),
            pltpu.VMEM((tm, Q_WIDTH), BF16)]


def _layer1_single_kernel(dilations, x_ref, o0_ref, o1_ref, o2_ref, s0_ref, s1_ref, s2_ref, g_ref,
                          wo_ref, wup_ref, wdn_ref, y_ref, o_scr, stat_scr, a_scr,
                          x1_scr, h_scr, acc_scr):
    g = g_ref[...]

    @pl.when(pl.program_id(0) == 0)
    def _():
        _interleave(_attn_mix_steps(dilations, x_ref, (o0_ref, o1_ref, o2_ref),
                                    (s0_ref, s1_ref, s2_ref), g, wo_ref, o_scr, stat_scr, a_scr,
                                    x1_scr, h_scr))
        acc_scr[...] = jnp.zeros(acc_scr.shape, F32)

    def emit(y):
        y_ref[...] = y

    _streamed_mlp_step(g, wup_ref, wdn_ref, x1_scr, h_scr, acc_scr, emit)


def _layer1_prompt_kernel(dilations, x_ref, o0_ref, o1_ref, o2_ref, s0_ref, s1_ref, s2_ref,
                          g_ref, wo_ref, wup_ref, wdn_ref, y_ref,
                          o_scr, stat_scr, a_scr, x1_scr, h_scr, acc_scr):
    s = pl.program_id(0)
    n_tiles = pl.num_programs(0) - 2
    g = g_ref[...]

    def prepare():
        return _attn_mix_steps(dilations, x_ref, (o0_ref, o1_ref, o2_ref),
                               (s0_ref, s1_ref, s2_ref), g, wo_ref, o_scr, stat_scr, a_scr,
                               x1_scr.at[lax.rem(s, 3)], h_scr.at[lax.rem(s, 2)])

    def finish():
        x1_slot, acc_slot = lax.rem(s + 1, 3), lax.rem(s, 2)
        return _residual_norm_steps(y_ref, lambda rows: x1_scr[x1_slot, rows, :],
                                    lambda rows: acc_scr[acc_slot, rows, :], g[3:4])

    @pl.when(s == 0)
    def _():
        _interleave(prepare())
        x1_scr[2] = jnp.zeros(x1_scr.shape[1:], F32)
        acc_scr[1] = jnp.zeros(acc_scr.shape[1:], F32)

    @pl.when(jnp.logical_and(s > 0, s <= n_tiles))
    def _():
        mlp_slot = lax.rem(s + 1, 2)

        def store_acc(acc):
            acc_scr[mlp_slot] = acc

        _interleave(
            finish(),
            _mlp_matmul_steps(lambda: h_scr[mlp_slot], wup_ref, wdn_ref, store_acc),
            prepare())

    @pl.when(s == n_tiles + 1)
    def _():
        _interleave(finish())


def _layer1(x, os, stats, gains, w_o, w_up, w_down, layer, tm):
    b, t, d = x.shape
    dilations = tuple(o.shape[1] for o in os)
    tiles_per_seq = t // tm
    n_tiles = b * tiles_per_seq
    assert t % tm == 0 and all(tm % (16 * dil) == 0 for dil in dilations)
    x2 = x.reshape(b * t, d)
    weights = (gains, w_o, w_up, w_down)
    weight_specs = [_const_spec(gains.shape), _const_spec(w_o.shape),
                    _layer_spec(w_up, layer), _layer_spec(w_down, layer)]

    def tile_specs(tile):
        sub = lambda dil, width: pl.BlockSpec(
            (None, dil, tm // dil, width),
            lambda s: (tile(s) // tiles_per_seq, 0, tile(s) % tiles_per_seq, 0))
        return ([pl.BlockSpec((tm, d), lambda s: (tile(s), 0))]
                + [sub(dil, GROUP_Q_WIDTH) for dil in dilations]
                + [sub(dil, STAT_LANES) for dil in dilations])

    if n_tiles == 1:
        n_chunks, mlp_specs = _streamed_mlp_specs(w_up, w_down, layer)
        y = pl.pallas_call(
            functools.partial(_layer1_single_kernel, dilations),
            grid=(n_chunks,),
            in_specs=tile_specs(lambda s: 0) + weight_specs[:2] + mlp_specs,
            out_specs=pl.BlockSpec((tm, d), lambda s: (0, 0)),
            out_shape=jax.ShapeDtypeStruct((b * t, d), F32),
            scratch_shapes=_attn_mix_scratch(tm) + [
                pltpu.VMEM((tm, d), F32), pltpu.VMEM((tm, d), BF16), pltpu.VMEM((tm, d), F32)],
            compiler_params=pltpu.CompilerParams(
                dimension_semantics=("arbitrary",), vmem_limit_bytes=VMEM_LIMIT),
            name="layer1_single_tile",
        )(x2, *os, *stats, *weights)
        return y.reshape(b, t, d)

    scratch = _attn_mix_scratch(tm) + [
        pltpu.VMEM((3, tm, d), F32), pltpu.VMEM((2, tm, d), BF16), pltpu.VMEM((2, tm, d), F32)]
    y = pl.pallas_call(
        functools.partial(_layer1_prompt_kernel, dilations),
        grid=(n_tiles + 2,),
        in_specs=tile_specs(lambda s: jnp.minimum(s, n_tiles - 1)) + weight_specs,
        out_specs=pl.BlockSpec((tm, d), lambda s: (jnp.maximum(s - 2, 0), 0)),
        out_shape=jax.ShapeDtypeStruct((b * t, d), F32),
        scratch_shapes=scratch,
        compiler_params=pltpu.CompilerParams(
            dimension_semantics=("arbitrary",), vmem_limit_bytes=VMEM_LIMIT),
        name="layer1_prompt",
    )(x2, *os, *stats, *weights)
    return y.reshape(b, t, d)


PROMPT_TILE = 512
LAYER1_TILE = 512
QKV_TILE = 1024
ATTN_BLOCK_ROWS = 2048


def kernel(x_prompt, x_sample, cache_pool, cache_k, cache_v, norm_gains, kv_norm_gain, w_pool,
           pool_scale, w_q, w_o, w_kv, w_up, w_down):
    depth = norm_gains.shape[0]
    assert depth == 2 and cache_pool.shape[0] == 1 and w_q.shape[0] == 1
    bp, tp, d = x_prompt.shape
    bs, ts, _ = x_sample.shape

    g0, g1 = norm_gains[0], norm_gains[1]
    g1_q = g1[0:1]
    g_kv = kv_norm_gain[None, :]
    wp = w_pool[0].astype(BF16)
    ps = pool_scale[0][None, :]
    wq = w_q[0].reshape(d, N_GROUPS, KV_PER_GROUP, Q_PER_KV, HEAD_DIM).transpose(0, 1, 3, 2, 4)
    wq = wq.reshape(d, Q_WIDTH).astype(BF16)
    wo = w_o[0].reshape(N_GROUPS, KV_PER_GROUP, Q_PER_KV, HEAD_DIM, d).transpose(0, 2, 1, 3, 4)
    wo = wo.reshape(Q_WIDTH, d).astype(BF16)
    wkv = w_kv.astype(BF16)
    wup, wdn = w_up.astype(BF16), w_down.astype(BF16)

    keep = min(KV_WINDOW, tp)
    xp1, utail = _layer0_prompt(x_prompt, g0, wp, ps, wup, wdn, 0, PROMPT_TILE)
    rope_p = _rope_tables(jnp.arange(QKV_TILE), jnp.arange(tp // QKV_TILE) * QKV_TILE)
    qkv_p = _qkv(xp1, g1_q, g_kv, wq, wkv, rope_p, QKV_TILE, keep, ATTN_DILATIONS, True)
    os_p, stats_p = [], []
    for gi, dil in enumerate(ATTN_DILATIONS):
        o, stat = _band_attention(qkv_p[gi], qkv_p[N_GROUPS + gi], qkv_p[2 * N_GROUPS + gi],
                                  ATTN_BLOCK_ROWS)
        os_p.append(o)
        stats_p.append(stat)
    y_prompt = _layer1(xp1, os_p, stats_p, g1, wo, wup, wdn, 1, LAYER1_TILE)
    pool_prompt = utail[:, POOL_HALO - POOL_BUF:][None]
    k_prompt = qkv_p[-2].reshape(bp, N_KV_HEADS, HEAD_DIM, keep).transpose(0, 3, 1, 2)
    v_prompt = qkv_p[-1].reshape(bp, N_KV_HEADS, HEAD_DIM, keep).transpose(0, 3, 1, 2)

    n_tok = ts * bs
    xs_tm = jnp.swapaxes(x_sample, 0, 1)
    buf_tm = jnp.swapaxes(cache_pool[0], 0, 1)
    xs1_tm, us_tm = _layer0_sample(xs_tm, buf_tm, PAST_LEN, g0, wp, ps, wup, wdn, 0)
    xs1 = xs1_tm.reshape(1, n_tok, d)
    rope_s = _rope_tables(jnp.arange(n_tok) // bs, jnp.full((1,), PAST_LEN))
    qkv_s = _qkv(xs1, g1_q, g_kv, wq, wkv, rope_s, n_tok, n_tok, (1,) * N_GROUPS, False)
    q_s = jnp.concatenate([q[0, 0] for q in qkv_s[:N_GROUPS]], axis=-1)
    q16 = q_s.reshape(ts, bs, N_GROUPS, Q_PER_KV, KV_PER_GROUP, HEAD_DIM)
    q16 = q16.transpose(1, 2, 4, 3, 0, 5).reshape(bs, N_KV_HEADS, Q_PER_KV, ts, HEAD_DIM)
    q16 = jnp.pad(q16, ((0, 0), (0, 0), (0, SAMPLE_Q_ROWS // ts - Q_PER_KV), (0, 0), (0, 0)))
    q16 = q16.reshape(bs, N_KV_HEADS, SAMPLE_Q_ROWS, HEAD_DIM)
    k_s = jnp.swapaxes(qkv_s[-2].reshape(ts, bs, KV_WIDTH), 0, 1)
    v_s = jnp.swapaxes(qkv_s[-1].reshape(ts, bs, KV_WIDTH), 0, 1)
    pad_new = ((0, 0), (0, NEW_KEY_ROWS - ts), (0, 0))
    o16, lse16 = _sample_attention(
        q16, jnp.pad(k_s, pad_new), jnp.pad(v_s, pad_new),
        cache_k.transpose(0, 2, 3, 1), cache_v.transpose(0, 2, 3, 1), ts)
    heads_padded = SAMPLE_Q_ROWS // ts
    o_s = o16.reshape(bs, N_GROUPS, KV_PER_GROUP, heads_padded, ts, HEAD_DIM)[:, :, :, :Q_PER_KV]
    o_s = o_s.transpose(4, 0, 1, 3, 2, 5).reshape(1, 1, n_tok, N_GROUPS, GROUP_Q_WIDTH).astype(BF16)
    lse_s = lse16.reshape(bs, N_GROUPS, KV_PER_GROUP, heads_padded, ts)[:, :, :, :Q_PER_KV]
    lse_s = lse_s.transpose(4, 0, 1, 2, 3).reshape(1, 1, n_tok, N_GROUPS, KV_PER_GROUP, Q_PER_KV)
    ones = lambda n: jnp.ones((1, 1, n_tok, N_GROUPS, n), F32)
    max_lane = [_stat_lane(c, 0) + STAT_MAX_OFFSET for c in range(KV_PER_GROUP)]
    assert max_lane[1] < max_lane[0]
    stat_s = jnp.concatenate(
        [ones(max_lane[1]), lse_s[..., 1, :],
         ones(max_lane[0] - max_lane[1] - Q_PER_KV), lse_s[..., 0, :],
         ones(STAT_LANES - max_lane[0] - Q_PER_KV)], axis=-1)
    ys_tm = _layer1(xs1, [o_s[:, :, :, gi] for gi in range(N_GROUPS)],
                    [stat_s[:, :, :, gi] for gi in range(N_GROUPS)], g1, wo, wup, wdn, 1, n_tok)
    y_sample = jnp.swapaxes(ys_tm.reshape(ts, bs, d), 0, 1)
    u_s = jnp.swapaxes(us_tm, 0, 1)
    pool_sample = jnp.concatenate([cache_pool[0], u_s], axis=1)[:, -POOL_BUF:][None]
    k_sample = k_s.reshape(bs, ts, N_KV_HEADS, HEAD_DIM)
    v_sample = v_s.reshape(bs, ts, N_KV_HEADS, HEAD_DIM)

    return (y_prompt, y_sample, pool_prompt, k_prompt, v_prompt, pool_sample, k_sample, v_sample)
```

```python
import functools

import jax
import jax.numpy as jnp
from jax import lax
from jax.experimental import pallas as pl
from jax.experimental.pallas import tpu as pltpu

F32 = jnp.float32
BF16 = jnp.bfloat16

EPS = 1e-6
ROPE_THETA = 10000.0
PAST_LEN = 16384
POOL_WINDOWS = (2, 4, 8, 16)
POOL_BUF = max(POOL_WINDOWS) - 1
POOL_HALO = 16
HEAD_DIM = 64
ATTN_WINDOWS = (128, 512, 2048)
ATTN_DILATIONS = (1, 4, 16)
N_GROUPS = len(ATTN_WINDOWS)
KV_PER_GROUP = 2
Q_PER_KV = 3
HEADS_PER_GROUP = KV_PER_GROUP * Q_PER_KV
N_KV_HEADS = N_GROUPS * KV_PER_GROUP
GROUP_Q_WIDTH = HEADS_PER_GROUP * HEAD_DIM
GROUP_KV_WIDTH = KV_PER_GROUP * HEAD_DIM
Q_WIDTH = N_GROUPS * GROUP_Q_WIDTH
KV_WIDTH = N_GROUPS * GROUP_KV_WIDTH
KV_WINDOW = max(ATTN_WINDOWS)
BAND = 128
NEG_BIG = -1e30

LANES = 128
STAT_LANES = LANES
STAT_MAX_OFFSET = 8
VMEM_LIMIT = 56 * 1024 * 1024
FF_CHUNK = 512
NORM_ROW_BLOCKS = 4
POOL_NORM_BLOCKS = 1
POOL_FF_CHUNK = 1024
SAMPLE_FF_CHUNK = 1024

for _w, _d in zip(ATTN_WINDOWS, ATTN_DILATIONS):
    assert _w // _d == BAND and _w % _d == 0
assert GROUP_KV_WIDTH == LANES


def _rms(x, g):
    return x * lax.rsqrt(jnp.mean(x * x, axis=-1, keepdims=True) + EPS) * g


def _mlp_matmul_steps(load_h, wup_ref, wdn_ref, emit_acc, chunk=FF_CHUNK):
    n_chunks = wup_ref.shape[1] // chunk
    yield n_chunks
    h = load_h()
    acc = None
    for c in range(n_chunks):
        cs = slice(c * chunk, (c + 1) * chunk)
        a = jnp.dot(h, wup_ref[:, cs], preferred_element_type=F32)
        a = jnp.square(jnp.maximum(a, 0.0)).astype(BF16)
        part = jnp.dot(a, wdn_ref[cs, :], preferred_element_type=F32)
        acc = part if acc is None else acc + part
        if c + 1 == n_chunks:
            emit_acc(acc)
        yield


def _row_blocks(n_rows, n_blocks=NORM_ROW_BLOCKS):
    size = n_rows // n_blocks if n_rows % (8 * n_blocks) == 0 else n_rows
    return [slice(lo, lo + size) for lo in range(0, n_rows, size)]


def _residual_norm_steps(dst_ref, load_x, load_v, g, n_blocks=NORM_ROW_BLOCKS):
    blocks = _row_blocks(dst_ref.shape[0], n_blocks)
    yield len(blocks)
    for rows in blocks:
        dst_ref[rows, :] = load_x(rows) + _rms(load_v(rows), g)
        yield


def _norm_cast_steps(dst_ref, load_v, g, n_blocks=NORM_ROW_BLOCKS):
    blocks = _row_blocks(dst_ref.shape[0], n_blocks)
    yield len(blocks)
    for rows in blocks:
        dst_ref[rows, :] = _rms(load_v(rows), g).astype(dst_ref.dtype)
        yield


def _chain(*generators):
    counts = [next(gen) for gen in generators]
    yield sum(counts)
    for gen in generators:
        yield from gen


def _interleave(*generators):
    totals = [next(gen) for gen in generators]
    done = [0] * len(generators)
    while any(d < t for d, t in zip(done, totals)):
        i = min((i for i in range(len(generators)) if done[i] < totals[i]),
                key=lambda i: (done[i] + 1) / totals[i])
        next(generators[i])
        done[i] += 1


def _fuse_mlp_weights(n_inputs):
    return [False] * (n_inputs - 2) + [True, True]


def _const_spec(shape):
    zeros = (0,) * len(shape)
    return pl.BlockSpec(shape, lambda *_: zeros, pipeline_mode=pl.Buffered(1))


def _layer_spec(w, layer):
    index = (layer,) + (0,) * (w.ndim - 1)
    return pl.BlockSpec((None,) + w.shape[1:], lambda *_: index, pipeline_mode=pl.Buffered(1))


def _subseq_spec(dilation, rows, width):
    return pl.BlockSpec((None, dilation, rows // dilation, width), lambda bi, i: (bi, 0, i, 0))


def _subseq_rows(n_rows, r, dilation):
    n = n_rows // dilation
    return pl.ds(r, n, stride=dilation) if dilation > 1 else pl.ds(0, n)


def _pool_mixer_steps(x_ref, xh_ref, tile_in_seq, g, wp_ref, ps_ref, ext_ref, x1_ref, h_ref,
                      utail_ref=None):
    tm = x_ref.shape[0]
    pool_ch = wp_ref.shape[1]
    blocks = _row_blocks(tm, POOL_NORM_BLOCKS)
    yield 3 * len(blocks) + len(POOL_WINDOWS)
    uh = _rms(xh_ref[0:POOL_HALO, :], g[0:1]) * jnp.where(tile_in_seq > 0, 1.0, 0.0)
    ext_ref[0:POOL_HALO, :] = uh
    for rows in blocks:
        u = _rms(x_ref[rows, :], g[0:1])
        ext_ref[POOL_HALO + rows.start:POOL_HALO + rows.stop, :] = u
        if utail_ref is not None and rows.stop == tm:
            utail_ref[0] = u[u.shape[0] - POOL_HALO:, :]
        yield
    pos = tile_in_seq * tm + lax.broadcasted_iota(jnp.int32, (tm, 1), 0)
    parts = []
    for gi, w in enumerate(POOL_WINDOWS):
        cs = slice(gi * pool_ch, (gi + 1) * pool_ch)
        s = ext_ref[POOL_HALO:, cs]
        for j in range(1, w):
            s = s + ext_ref[POOL_HALO - j:POOL_HALO - j + tm, cs]
        cnt = jnp.minimum(w, pos + 1).astype(F32)
        pooled = s / cnt - ext_ref[POOL_HALO:, cs]
        parts.append(jnp.dot(pooled.astype(BF16), wp_ref[gi], preferred_element_type=F32))
        yield
    mix = jnp.concatenate(parts, axis=1) * ps_ref[...]
    tail = _chain(
        _residual_norm_steps(x1_ref, lambda rows: x_ref[rows, :], lambda rows: mix[rows], g[1:2],
                             POOL_NORM_BLOCKS),
        _norm_cast_steps(h_ref, lambda rows: x1_ref[rows, :], g[2:3], POOL_NORM_BLOCKS))
    next(tail)
    yield from tail


def _layer0_prompt_kernel(tiles_per_seq, x0_ref, xa_ref, xha_ref, xb_ref, xhb_ref, g_ref,
                          wp_ref, ps_ref, wup_ref, wdn_ref, y_ref, utail_ref,
                          x1_scr, h_scr, ext_scr):
    s = pl.program_id(0)
    n_tiles = 2 * pl.num_programs(0)
    tm = xa_ref.shape[0]
    g = g_ref[...]

    def prepare(x_ref, xh_ref, tile, slot, tail_ref):
        return _pool_mixer_steps(x_ref, xh_ref, lax.rem(tile, tiles_per_seq), g, wp_ref, ps_ref,
                                 ext_scr.at[slot], x1_scr.at[slot], h_scr.at[slot], tail_ref)

    @pl.when(s == 0)
    def _():
        _interleave(prepare(x0_ref, x0_ref, 0, 0, None))

    def mlp(slot):
        acc = []
        return _chain(
            _mlp_matmul_steps(lambda: h_scr[slot], wup_ref, wdn_ref, acc.append, POOL_FF_CHUNK),
            _residual_norm_steps(y_ref.at[pl.ds(slot * tm, tm)],
                                 lambda rows: x1_scr[slot, rows, :],
                                 lambda rows: acc[0][rows], g[3:4], POOL_NORM_BLOCKS))

    _interleave(mlp(0), prepare(xa_ref, xha_ref, 2 * s + 1, 1, utail_ref))
    _interleave(mlp(1), prepare(xb_ref, xhb_ref, jnp.minimum(2 * s + 2, n_tiles - 1), 0, None))


def _layer0_prompt(x, gains, w_pool, pool_scale, w_up, w_down, layer, tm):
    b, t, d = x.shape
    tiles_per_seq = t // tm
    n_tiles = b * tiles_per_seq
    assert t % tm == 0 and tm % POOL_HALO == 0 and tiles_per_seq % 2 == 0
    halo_per_tile = tm // POOL_HALO
    x2 = x.reshape(b * t, d)
    tile_a = lambda s: 2 * s + 1
    tile_b = lambda s: jnp.minimum(2 * s + 2, n_tiles - 1)
    tile_spec = lambda tile: pl.BlockSpec((tm, d), lambda s: (tile(s), 0))
    halo_spec = lambda tile: pl.BlockSpec((POOL_HALO, d),
                                          lambda s: (tile(s) * halo_per_tile - 1, 0))
    y, utail = pl.pallas_call(
        functools.partial(_layer0_prompt_kernel, tiles_per_seq),
        grid=(n_tiles // 2,),
        in_specs=[
            pl.BlockSpec((tm, d), lambda s: (0, 0), pipeline_mode=pl.Buffered(1)),
            tile_spec(tile_a), halo_spec(tile_a), tile_spec(tile_b), halo_spec(tile_b),
            _const_spec(gains.shape), _const_spec(w_pool.shape), _const_spec(pool_scale.shape),
            _layer_spec(w_up, layer), _layer_spec(w_down, layer),
        ],
        out_specs=[
            pl.BlockSpec((2 * tm, d), lambda s: (s, 0)),
            pl.BlockSpec((1, POOL_HALO, d), lambda s: (tile_a(s) // tiles_per_seq, 0, 0)),
        ],
        out_shape=[
            jax.ShapeDtypeStruct((b * t, d), F32),
            jax.ShapeDtypeStruct((b, POOL_HALO, d), F32),
        ],
        scratch_shapes=[pltpu.VMEM((2, tm, d), F32), pltpu.VMEM((2, tm, d), BF16),
                        pltpu.VMEM((2, tm + POOL_HALO, d), F32)],
        compiler_params=pltpu.CompilerParams(
            dimension_semantics=("arbitrary",), vmem_limit_bytes=VMEM_LIMIT,
            allow_input_fusion=_fuse_mlp_weights(10)),
        name="layer0_prompt",
    )(x2, x2, x2, x2, x2, gains, w_pool, pool_scale, w_up, w_down)
    return y.reshape(b, t, d), utail


def _streamed_mlp_step(g, wup_ref, wdn_ref, x1_scr, h_scr, acc_scr, emit):
    a = jnp.dot(h_scr[...], wup_ref[...], preferred_element_type=F32)
    a = jnp.square(jnp.maximum(a, 0.0)).astype(BF16)
    acc_scr[...] += jnp.dot(a, wdn_ref[...], preferred_element_type=F32)

    @pl.when(pl.program_id(0) == pl.num_programs(0) - 1)
    def _():
        emit(x1_scr[...] + _rms(acc_scr[...], g[3:4]))


def _streamed_mlp_specs(w_up, w_down, layer):
    assert w_up.shape[2] % SAMPLE_FF_CHUNK == 0
    return (w_up.shape[2] // SAMPLE_FF_CHUNK,
            [pl.BlockSpec((None, w_up.shape[1], SAMPLE_FF_CHUNK), lambda c: (layer, 0, c)),
             pl.BlockSpec((None, SAMPLE_FF_CHUNK, w_down.shape[2]), lambda c: (layer, c, 0))])


def _layer0_sample_kernel(start_pos, x_ref, buf_ref, g_ref, wp_ref, ps_ref, wup_ref, wdn_ref,
                          y_ref, u_ref, x1_scr, h_scr, acc_scr):
    n_t, n_b, _ = x_ref.shape
    g = g_ref[...]

    @pl.when(pl.program_id(0) == 0)
    def _():
        _pool_mixer_time_major(start_pos, x_ref, buf_ref, g, wp_ref, ps_ref, u_ref, x1_scr, h_scr)
        acc_scr[...] = jnp.zeros(acc_scr.shape, F32)

    def emit(y):
        for t in range(n_t):
            y_ref[t] = y[t * n_b:(t + 1) * n_b]

    _streamed_mlp_step(g, wup_ref, wdn_ref, x1_scr, h_scr, acc_scr, emit)


def _pool_mixer_time_major(start_pos, x_ref, buf_ref, g, wp_ref, ps_ref, u_ref, x1_ref, h_ref):
    n_t = x_ref.shape[0]
    pool_ch = wp_ref.shape[1]
    xs = [x_ref[t] for t in range(n_t)]
    us = [_rms(xt, g[0:1]) for xt in xs]
    ext = [buf_ref[j] for j in range(POOL_BUF)] + us
    parts = []
    for gi, w in enumerate(POOL_WINDOWS):
        cs = slice(gi * pool_ch, (gi + 1) * pool_ch)
        rows = []
        for t in range(n_t):
            s = us[t][:, cs]
            for j in range(1, w):
                s = s + ext[POOL_BUF + t - j][:, cs]
            cnt = float(min(w, start_pos + t + 1))
            rows.append(s / cnt - us[t][:, cs])
        pooled = jnp.concatenate(rows, axis=0)
        parts.append(jnp.dot(pooled.astype(BF16), wp_ref[gi], preferred_element_type=F32))
    mix = jnp.concatenate(parts, axis=1) * ps_ref[...]
    x = jnp.concatenate(xs, axis=0)
    x1 = x + _rms(mix, g[1:2])
    x1_ref[...] = x1
    h_ref[...] = _rms(x1, g[2:3]).astype(BF16)
    for t in range(n_t):
        u_ref[t] = us[t]


def _layer0_sample(x_tm, buf_tm, start_pos, gains, w_pool, pool_scale, w_up, w_down, layer):
    assert start_pos + 1 >= max(POOL_WINDOWS) and buf_tm.shape[0] == POOL_BUF
    n_tok, d = x_tm.shape[0] * x_tm.shape[1], x_tm.shape[2]
    n_chunks, weight_specs = _streamed_mlp_specs(w_up, w_down, layer)
    return pl.pallas_call(
        functools.partial(_layer0_sample_kernel, start_pos),
        grid=(n_chunks,),
        in_specs=[_const_spec(a.shape) for a in (x_tm, buf_tm, gains, w_pool, pool_scale)]
        + weight_specs,
        out_specs=[pl.BlockSpec(x_tm.shape, lambda c: (0, 0, 0))] * 2,
        out_shape=[jax.ShapeDtypeStruct(x_tm.shape, F32), jax.ShapeDtypeStruct(x_tm.shape, F32)],
        scratch_shapes=[pltpu.VMEM((n_tok, d), F32), pltpu.VMEM((n_tok, d), BF16),
                        pltpu.VMEM((n_tok, d), F32)],
        compiler_params=pltpu.CompilerParams(
            dimension_semantics=("arbitrary",), vmem_limit_bytes=VMEM_LIMIT,
            allow_input_fusion=_fuse_mlp_weights(7)),
        name="layer0_sample",
    )(x_tm, buf_tm, gains, w_pool, pool_scale, w_up, w_down)


def _rope(x, cos, sin_signed):
    lane = lax.broadcasted_iota(jnp.int32, (x.shape[0], LANES), 1)
    first_half = (lane & (HEAD_DIM // 2)) == 0
    out = []
    for c in range(x.shape[1] // LANES):
        xc = x[:, c * LANES:(c + 1) * LANES]
        partner = jnp.where(first_half,
                            pltpu.roll(xc, LANES - HEAD_DIM // 2, 1),
                            pltpu.roll(xc, HEAD_DIM // 2, 1))
        out.append(xc * cos + partner * sin_signed)
    return out


def _qkv_kernel(dilations, first_kept_tile, transpose_kept,
                x_ref, gq_ref, gkv_ref, wq_ref, wkv_ref, cos_ref, sin_ref, cos_step_ref, sin_step_ref,
                q0_ref, q1_ref, q2_ref, k0_ref, k1_ref, k2_ref, v0_ref, v1_ref, v2_ref,
                kf_ref, vf_ref, stage_scr):
    i = pl.program_id(1)
    x = x_ref[0]
    cos_in, sin_in = cos_ref[...], sin_ref[...]
    cos_at, sin_at = cos_step_ref[0:1, :], sin_step_ref[0:1, :]
    cos = cos_in * cos_at - sin_in * sin_at
    sin_signed = sin_in * cos_at + cos_in * sin_at
    chunks_per_group = GROUP_Q_WIDTH // LANES

    def emit(out_ref, dilation, chunks):
        if dilation == 1:
            for c, chunk in enumerate(chunks):
                out_ref[0, :, c * LANES:(c + 1) * LANES] = chunk.astype(BF16)
            return
        for c, chunk in enumerate(chunks):
            stage_scr[c] = chunk
        for r in range(dilation):
            rows = _subseq_rows(stage_scr.shape[1], r, dilation)
            for c in range(len(chunks)):
                out_ref[r, :, c * LANES:(c + 1) * LANES] = stage_scr[c, rows, :].astype(BF16)

    u = _rms(x, gq_ref[...]).astype(BF16)
    q = jnp.dot(u, wq_ref[...], preferred_element_type=F32)
    scale = HEAD_DIM ** -0.5
    q_chunks = _rope(q, cos * scale, sin_signed * scale)
    for gi, q_ref in enumerate((q0_ref, q1_ref, q2_ref)):
        emit(q_ref, dilations[gi], q_chunks[gi * chunks_per_group:(gi + 1) * chunks_per_group])

    un = _rms(x, gkv_ref[...]).astype(BF16)
    kv = jnp.dot(un, wkv_ref[...], preferred_element_type=F32)
    k_chunks = _rope(kv[:, :KV_WIDTH], cos, sin_signed)
    v_chunks = [kv[:, KV_WIDTH + gi * LANES:KV_WIDTH + (gi + 1) * LANES] for gi in range(N_GROUPS)]
    for gi, (k_ref, v_ref) in enumerate(((k0_ref, v0_ref), (k1_ref, v1_ref), (k2_ref, v2_ref))):
        emit(k_ref, dilations[gi], [k_chunks[gi]])
        emit(v_ref, dilations[gi], [v_chunks[gi]])

    @pl.when(i >= first_kept_tile)
    def _():
        for gi in range(N_GROUPS):
            cs = slice(gi * LANES, (gi + 1) * LANES)
            if transpose_kept:
                kf_ref[0, cs, :] = k_chunks[gi].T
                vf_ref[0, cs, :] = v_chunks[gi].T
            else:
                kf_ref[0, :, cs] = k_chunks[gi]
                vf_ref[0, :, cs] = v_chunks[gi]


def _qkv(x, g_q, g_kv, w_q, w_kv, rope, tm, keep_rows, dilations, transpose_kept):
    b, t, d = x.shape
    cos, sin_signed, cos_step, sin_step = rope
    assert t % tm == 0 and keep_rows % tm == 0 and all(tm % (16 * dil) == 0 for dil in dilations)
    first_kept_tile = (t - keep_rows) // tm
    kept_block = lambda bi, i: jnp.maximum(i - first_kept_tile, 0)
    if transpose_kept:
        kept = pl.BlockSpec((1, KV_WIDTH, tm), lambda bi, i: (bi, 0, kept_block(bi, i)))
        kept_shape = jax.ShapeDtypeStruct((b, KV_WIDTH, keep_rows), F32)
    else:
        kept = pl.BlockSpec((1, tm, KV_WIDTH), lambda bi, i: (bi, kept_block(bi, i), 0))
        kept_shape = jax.ShapeDtypeStruct((b, keep_rows, KV_WIDTH), F32)
    sub_shape = lambda dil, width: jax.ShapeDtypeStruct((b, dil, t // dil, width), BF16)
    q_specs = [_subseq_spec(dil, tm, GROUP_Q_WIDTH) for dil in dilations]
    kv_specs = [_subseq_spec(dil, tm, GROUP_KV_WIDTH) for dil in dilations]
    q_shapes = [sub_shape(dil, GROUP_Q_WIDTH) for dil in dilations]
    kv_shapes = [sub_shape(dil, GROUP_KV_WIDTH) for dil in dilations]
    return pl.pallas_call(
        functools.partial(_qkv_kernel, tuple(dilations), first_kept_tile, transpose_kept),
        grid=(b, t // tm),
        in_specs=[
            pl.BlockSpec((1, tm, d), lambda bi, i: (bi, i, 0)),
            _const_spec(g_q.shape), _const_spec(g_kv.shape),
            _const_spec(w_q.shape), _const_spec(w_kv.shape),
            _const_spec(cos.shape), _const_spec(sin_signed.shape),
            pl.BlockSpec((None,) + cos_step.shape[1:], lambda bi, i: (i, 0, 0)),
            pl.BlockSpec((None,) + sin_step.shape[1:], lambda bi, i: (i, 0, 0)),
        ],
        out_specs=q_specs + kv_specs + kv_specs + [kept, kept],
        out_shape=q_shapes + kv_shapes + kv_shapes + [kept_shape, kept_shape],
        scratch_shapes=[pltpu.VMEM((GROUP_Q_WIDTH // LANES, tm, LANES), F32)],
        compiler_params=pltpu.CompilerParams(
            dimension_semantics=("arbitrary", "arbitrary"), vmem_limit_bytes=VMEM_LIMIT),
        name="qkv_rope",
    )(x, g_q, g_kv, w_q, w_kv, cos, sin_signed, cos_step, sin_step)


def _rope_tables(offsets, tile_starts):
    half = HEAD_DIM // 2
    inv = jnp.tile(ROPE_THETA ** (-jnp.arange(0, HEAD_DIM, 2, dtype=F32) / HEAD_DIM), LANES // half)
    sign = jnp.where((jnp.arange(LANES) // half) % 2 == 0, -1.0, 1.0).astype(F32)
    tables = []
    for positions in (offsets, tile_starts):
        ang = positions.astype(F32)[:, None] * inv[None, :]
        tables += [jnp.cos(ang), jnp.sin(ang) * sign[None, :]]
    rows8 = lambda a: jnp.broadcast_to(a[:, None, :], (a.shape[0], 8, LANES))
    return tables[0], tables[1], rows8(tables[2]), rows8(tables[3])


def _stat_lane(c, k):
    return (1 - c) * HEAD_DIM + k


def _band_attention_kernel(seg_rows, tiles_per_subseq, q_ref, kp_ref, kc_ref, vp_ref, vc_ref,
                           o_ref, stat_ref, k_scr, v_scr, bias_scr, s_scr, p_scr):
    n_seg = q_ref.shape[0] // seg_rows
    bands_per_seg = seg_rows // BAND
    n_bands = n_seg * bands_per_seg
    step = pl.program_id(1)
    lane = lax.broadcasted_iota(jnp.int32, (1, LANES), 1)
    for c in range(KV_PER_GROUP):
        own = (lane // HEAD_DIM) == c
        for u in range(n_seg):
            base = u * (seg_rows + BAND)
            for dst, k_rows, v_rows in (
                    (slice(base, base + BAND), kp_ref[...], vp_ref[...]),
                    (slice(base + BAND, base + BAND + seg_rows),
                     kc_ref[u * seg_rows:(u + 1) * seg_rows, :],
                     vc_ref[u * seg_rows:(u + 1) * seg_rows, :])):
                k_scr[c, dst, :] = jnp.where(own, k_rows, jnp.zeros((), BF16))
                v_scr[c, dst, :] = jnp.where(own, v_rows, jnp.ones((), BF16))
    stat_ref[...] = jnp.ones(stat_ref.shape, F32)

    row = lax.broadcasted_iota(jnp.int32, (BAND, 2 * BAND), 0)
    col = lax.broadcasted_iota(jnp.int32, (BAND, 2 * BAND), 1)
    band_bias = jnp.where((col >= row) & (col <= row + BAND), 0.0, NEG_BIG)
    bias_scr[0] = band_bias
    bias_scr[1] = band_bias + jnp.where(col < BAND, NEG_BIG, 0.0)
    starts_subseq = True if n_seg > 1 else lax.rem(step, tiles_per_subseq) == 0

    def rows_of(g):
        return pl.ds(g * BAND if isinstance(g, int) else pl.multiple_of(g * BAND, BAND), BAND)

    def key_rows_of(g):
        staged = g + g // bands_per_seg
        start = staged * BAND if isinstance(g, int) else pl.multiple_of(staged * BAND, BAND)
        return pl.ds(start, 2 * BAND)

    def scores(g, slot):
        yield KV_PER_GROUP
        q = jnp.concatenate(
            [q_ref[rows_of(g), k * LANES:(k + 1) * LANES] for k in range(Q_PER_KV)], axis=0)
        for c in range(KV_PER_GROUP):
            s_scr[slot, c] = lax.dot_general(q, k_scr[c, key_rows_of(g), :],
                                             (((1,), (1,)), ((), ())),
                                             preferred_element_type=F32)
            yield

    def softmax(g, slot):
        yield KV_PER_GROUP * Q_PER_KV
        first = jnp.logical_and(starts_subseq, g % bands_per_seg == 0).astype(jnp.int32)
        for c in range(KV_PER_GROUP):
            for k in range(Q_PER_KV):
                part = pl.ds(k * BAND, BAND)
                s = s_scr[slot, c, part, :] + bias_scr[first]
                m = jnp.max(s, axis=1, keepdims=True)
                p_scr[slot, c, part, :] = jnp.exp(s - m).astype(BF16)
                stat_ref[rows_of(g), pl.ds(_stat_lane(c, k) + STAT_MAX_OFFSET, 1)] = m
                yield

    def weighted_values(g, slot):
        yield KV_PER_GROUP + Q_PER_KV + 1
        ov = []
        for c in range(KV_PER_GROUP):
            ov.append(jnp.dot(p_scr[slot, c], v_scr[c, key_rows_of(g), :],
                              preferred_element_type=F32))
            yield
        low = lane < HEAD_DIM
        sums = None
        for k in range(Q_PER_KV):
            part = slice(k * BAND, (k + 1) * BAND)
            o_ref[rows_of(g), k * LANES:(k + 1) * LANES] = (
                jnp.where(low, ov[0][part], ov[1][part]).astype(o_ref.dtype))
            sums_k = jnp.where(low, ov[1][part], ov[0][part])
            sums = sums_k if sums is None else jnp.where((lane % HEAD_DIM) == k, sums_k, sums)
            yield
        for c in range(KV_PER_GROUP):
            lanes = pl.ds(_stat_lane(c, 0), Q_PER_KV)
            stat_ref[rows_of(g), lanes] = sums[:, _stat_lane(c, 0):_stat_lane(c, 0) + Q_PER_KV]
        yield

    _interleave(scores(0, 0))
    _interleave(softmax(0, 0), scores(1, 1))

    def steady(t, carry):
        for g, slot in ((2 * t + 1, 1), (2 * t + 2, 0)):
            _interleave(weighted_values(g - 1, 1 - slot), softmax(g, slot),
                        scores(g + 1, 1 - slot))
        return carry

    assert n_bands % 2 == 0
    lax.fori_loop(0, (n_bands - 2) // 2, steady, 0)
    last = n_bands - 1
    _interleave(weighted_values(last - 1, (last - 1) % 2), softmax(last, last % 2))
    _interleave(weighted_values(last, last % 2))


def _band_attention(q, k, v, block_rows):
    b, dilation, sub, _ = q.shape
    seg_rows = min(sub, block_rows)
    total = dilation * sub
    assert total % block_rows == 0 and block_rows % seg_rows == 0 and sub % seg_rows == 0
    assert seg_rows % BAND == 0 and block_rows >= 2 * BAND
    n_seg = block_rows // seg_rows
    flat = lambda a: a.reshape(b, total, a.shape[-1])
    cur = lambda width: pl.BlockSpec((None, block_rows, width), lambda bi, i: (bi, i, 0))
    prev = pl.BlockSpec((None, BAND, GROUP_KV_WIDTH),
                        lambda bi, i: (bi, jnp.maximum(i * (block_rows // BAND) - 1, 0), 0))
    staged_rows = n_seg * (seg_rows + BAND)
    o, stat = pl.pallas_call(
        functools.partial(_band_attention_kernel, seg_rows, sub // seg_rows),
        grid=(b, total // block_rows),
        in_specs=[cur(GROUP_Q_WIDTH), prev, cur(GROUP_KV_WIDTH), prev, cur(GROUP_KV_WIDTH)],
        out_specs=[cur(GROUP_Q_WIDTH), cur(STAT_LANES)],
        out_shape=[jax.ShapeDtypeStruct((b, total, GROUP_Q_WIDTH), BF16),
                   jax.ShapeDtypeStruct((b, total, STAT_LANES), F32)],
        scratch_shapes=[pltpu.VMEM((KV_PER_GROUP, staged_rows, GROUP_KV_WIDTH), BF16),
                        pltpu.VMEM((KV_PER_GROUP, staged_rows, GROUP_KV_WIDTH), BF16),
                        pltpu.VMEM((2, BAND, 2 * BAND), F32),
                        pltpu.VMEM((2, KV_PER_GROUP, Q_PER_KV * BAND, 2 * BAND), F32),
                        pltpu.VMEM((2, KV_PER_GROUP, Q_PER_KV * BAND, 2 * BAND), BF16)],
        compiler_params=pltpu.CompilerParams(
            dimension_semantics=("arbitrary", "arbitrary"), vmem_limit_bytes=VMEM_LIMIT),
        name=f"band_attention_d{dilation}",
    )(flat(q), flat(k), flat(k), flat(v), flat(v))
    return (o.reshape(b, dilation, sub, GROUP_Q_WIDTH), stat.reshape(b, dilation, sub, STAT_LANES))


SAMPLE_Q_ROWS = 16
NEW_KEY_ROWS = 8
SAMPLE_SEQS_PER_STEP = 4


def _sample_attention_kernel(n_new, q_ref, kn_ref, vn_ref, kc0, kc1, kc2, vc0, vc1, vc2,
                             o_ref, lse_ref):
    n_seq = q_ref.shape[0]
    n_rows = q_ref.shape[2]
    contract_last = (((1,), (1,)), ((), ()))
    cache_refs = ((kc0, vc0), (kc1, vc1), (kc2, vc2))
    bias_c, bias_n = [], []
    for gi, (kc_ref, _) in enumerate(cache_refs):
        dmask = ATTN_DILATIONS[gi] - 1
        n_cache = kc_ref.shape[3]
        assert n_cache == ATTN_WINDOWS[gi]
        t_c = lax.broadcasted_iota(jnp.int32, (n_rows, n_cache), 0) & (n_new - 1)
        j_c = lax.broadcasted_iota(jnp.int32, (n_rows, n_cache), 1)
        valid_c = (j_c >= t_c) & (((j_c - t_c) & dmask) == 0)
        t_n = lax.broadcasted_iota(jnp.int32, (n_rows, NEW_KEY_ROWS), 0) & (n_new - 1)
        j_n = lax.broadcasted_iota(jnp.int32, (n_rows, NEW_KEY_ROWS), 1)
        valid_n = (j_n <= t_n) & (((t_n - j_n) & dmask) == 0)
        bias_c.append(jnp.where(valid_c, 0.0, NEG_BIG))
        bias_n.append(jnp.where(valid_n, 0.0, NEG_BIG))

    def one_sequence(b):
        yield len(cache_refs) * KV_PER_GROUP
        kn_all = kn_ref[b]
        vn_all = vn_ref[b]
        for gi, (kc_ref, vc_ref) in enumerate(cache_refs):
            for c in range(KV_PER_GROUP):
                kvh = gi * KV_PER_GROUP + c
                ns = slice(kvh * HEAD_DIM, (kvh + 1) * HEAD_DIM)
                q = q_ref[b, kvh]
                kt = kc_ref[b, c].astype(BF16)
                vt = vc_ref[b, c].astype(BF16)
                s_c = jnp.dot(q, kt, preferred_element_type=F32) + bias_c[gi]
                s_n = lax.dot_general(q, kn_all[:, ns].astype(BF16), contract_last,
                                      preferred_element_type=F32) + bias_n[gi]
                m = jnp.maximum(jnp.max(s_c, axis=1, keepdims=True),
                                jnp.max(s_n, axis=1, keepdims=True))
                p_c = jnp.exp(s_c - m)
                p_n = jnp.exp(s_n - m)
                l = jnp.sum(p_c, axis=1, keepdims=True) + jnp.sum(p_n, axis=1, keepdims=True)
                o = (lax.dot_general(p_c.astype(BF16), vt, contract_last,
                                     preferred_element_type=F32)
                     + jnp.dot(p_n.astype(BF16), vn_all[:, ns].astype(BF16),
                               preferred_element_type=F32))
                o_ref[b, kvh] = o / l
                lse_ref[b, kvh] = m + jnp.log(l)
                yield

    _interleave(*[one_sequence(b) for b in range(n_seq)])


def _sample_attention(q16, k_new, v_new, cache_kt, cache_vt, n_new):
    b = q16.shape[0]
    n_kv = q16.shape[1]
    assert cache_kt.shape[3] == KV_WINDOW and n_new & (n_new - 1) == 0 and n_new <= NEW_KEY_ROWS

    per_step = SAMPLE_SEQS_PER_STEP
    assert b % per_step == 0

    def cache_spec(gi):
        cols = ATTN_WINDOWS[gi]
        last = KV_WINDOW // cols - 1
        return pl.BlockSpec((per_step, KV_PER_GROUP, HEAD_DIM, cols),
                            lambda bi: (bi, gi, 0, last))

    whole = lambda a: pl.BlockSpec((per_step,) + a.shape[1:],
                                   lambda bi: (bi,) + (0,) * (a.ndim - 1))
    o_shape = jax.ShapeDtypeStruct((b, n_kv, SAMPLE_Q_ROWS, HEAD_DIM), F32)
    lse_shape = jax.ShapeDtypeStruct((b, n_kv, SAMPLE_Q_ROWS, 1), F32)
    return pl.pallas_call(
        functools.partial(_sample_attention_kernel, n_new),
        grid=(b // per_step,),
        in_specs=[whole(q16), whole(k_new), whole(v_new)]
        + [cache_spec(gi) for gi in range(N_GROUPS)] * 2,
        out_specs=[whole(o_shape), whole(lse_shape)],
        out_shape=[o_shape, lse_shape],
        compiler_params=pltpu.CompilerParams(
            dimension_semantics=("arbitrary",), vmem_limit_bytes=VMEM_LIMIT),
        name="sample_attention",
    )(q16, k_new, v_new, cache_kt, cache_kt, cache_kt, cache_vt, cache_vt, cache_vt)


def _attn_mix_steps(dilations, x_ref, o_refs, s_refs, g, wo_ref, o_scr, stat_scr, a_scr,
                    x1_ref, h_ref):
    tm = x_ref.shape[0]
    chunks_per_group = GROUP_Q_WIDTH // LANES
    yield 2 * N_GROUPS + 1 + 2 * len(_row_blocks(tm))
    for gi, (o_ref, s_ref) in enumerate(zip(o_refs, s_refs)):
        for r in range(dilations[gi]):
            rows = _subseq_rows(tm, r, dilations[gi])
            stat_scr[gi, rows, :] = s_ref[r]
            for c in range(chunks_per_group):
                o_scr[gi * chunks_per_group + c, rows, :] = (
                    o_ref[r, :, c * LANES:(c + 1) * LANES].astype(F32))
        yield
    sums = [stat_scr[gi] for gi in range(N_GROUPS)]
    maxes = [pltpu.roll(sm, STAT_LANES - STAT_MAX_OFFSET, 1) for sm in sums]
    top = jnp.maximum(jnp.maximum(maxes[0], maxes[1]), maxes[2])
    es = [jnp.exp(mx - top) for mx in maxes]
    den = sums[0] * es[0] + sums[1] * es[1] + sums[2] * es[2]
    slot_id = lax.broadcasted_iota(jnp.int32, (tm, GROUP_Q_WIDTH), 1) // HEAD_DIM
    for gi in range(N_GROUPS):
        scale = es[gi] / den
        wide = jnp.zeros((tm, GROUP_Q_WIDTH), F32)
        for slot in range(HEADS_PER_GROUP):
            stat_lane = _stat_lane(slot % KV_PER_GROUP, slot // KV_PER_GROUP)
            wide = jnp.where(slot_id == slot, scale[:, stat_lane:stat_lane + 1], wide)
        for c in range(chunks_per_group):
            ci = gi * chunks_per_group + c
            a_scr[:, ci * LANES:(ci + 1) * LANES] = (
                o_scr[ci] * wide[:, c * LANES:(c + 1) * LANES]).astype(BF16)
        yield
    mix = jnp.dot(a_scr[...], wo_ref[...], preferred_element_type=F32)
    yield
    tail = _chain(
        _residual_norm_steps(x1_ref, lambda rows: x_ref[rows, :], lambda rows: mix[rows], g[1:2]),
        _norm_cast_steps(h_ref, lambda rows: x1_ref[rows, :], g[2:3]))
    next(tail)
    yield from tail


def _attn_mix_scratch(tm):
    return [pltpu.VMEM((Q_WIDTH // LANES, tm, LANES), F32),
            pltpu.VMEM((N_GROUPS, tm, STAT_LANES), F32),
            pltpu.VMEM((tm, Q_WIDTH), BF16)]


def _layer1_single_kernel(dilations, x_ref, o0_ref, o1_ref, o2_ref, s0_ref, s1_ref, s2_ref, g_ref,
                          wo_ref, wup_ref, wdn_ref, y_ref, o_scr, stat_scr, a_scr,
                          x1_scr, h_scr, acc_scr):
    g = g_ref[...]

    @pl.when(pl.program_id(0) == 0)
    def _():
        _interleave(_attn_mix_steps(dilations, x_ref, (o0_ref, o1_ref, o2_ref),
                                    (s0_ref, s1_ref, s2_ref), g, wo_ref, o_scr, stat_scr, a_scr,
                                    x1_scr, h_scr))
        acc_scr[...] = jnp.zeros(acc_scr.shape, F32)

    def emit(y):
        y_ref[...] = y

    _streamed_mlp_step(g, wup_ref, wdn_ref, x1_scr, h_scr, acc_scr, emit)


def _layer1_prompt_kernel(dilations, x_ref, o0_ref, o1_ref, o2_ref, s0_ref, s1_ref, s2_ref,
                          g_ref, wo_ref, wup_ref, wdn_ref, y_ref,
                          o_scr, stat_scr, a_scr, x1_scr, h_scr, acc_scr):
    s = pl.program_id(0)
    n_tiles = pl.num_programs(0) - 2
    g = g_ref[...]

    def prepare():
        return _attn_mix_steps(dilations, x_ref, (o0_ref, o1_ref, o2_ref),
                               (s0_ref, s1_ref, s2_ref), g, wo_ref, o_scr, stat_scr, a_scr,
                               x1_scr.at[lax.rem(s, 3)], h_scr.at[lax.rem(s, 2)])

    def finish():
        x1_slot, acc_slot = lax.rem(s + 1, 3), lax.rem(s, 2)
        return _residual_norm_steps(y_ref, lambda rows: x1_scr[x1_slot, rows, :],
                                    lambda rows: acc_scr[acc_slot, rows, :], g[3:4])

    @pl.when(s == 0)
    def _():
        _interleave(prepare())
        x1_scr[2] = jnp.zeros(x1_scr.shape[1:], F32)
        acc_scr[1] = jnp.zeros(acc_scr.shape[1:], F32)

    @pl.when(jnp.logical_and(s > 0, s <= n_tiles))
    def _():
        mlp_slot = lax.rem(s + 1, 2)

        def store_acc(acc):
            acc_scr[mlp_slot] = acc

        _interleave(
            finish(),
            _mlp_matmul_steps(lambda: h_scr[mlp_slot], wup_ref, wdn_ref, store_acc),
            prepare())

    @pl.when(s == n_tiles + 1)
    def _():
        _interleave(finish())


def _layer1(x, os, stats, gains, w_o, w_up, w_down, layer, tm):
    b, t, d = x.shape
    dilations = tuple(o.shape[1] for o in os)
    tiles_per_seq = t // tm
    n_tiles = b * tiles_per_seq
    assert t % tm == 0 and all(tm % (16 * dil) == 0 for dil in dilations)
    x2 = x.reshape(b * t, d)
    weights = (gains, w_o, w_up, w_down)
    weight_specs = [_const_spec(gains.shape), _const_spec(w_o.shape),
                    _layer_spec(w_up, layer), _layer_spec(w_down, layer)]

    def tile_specs(tile):
        sub = lambda dil, width: pl.BlockSpec(
            (None, dil, tm // dil, width),
            lambda s: (tile(s) // tiles_per_seq, 0, tile(s) % tiles_per_seq, 0))
        return ([pl.BlockSpec((tm, d), lambda s: (tile(s), 0))]
                + [sub(dil, GROUP_Q_WIDTH) for dil in dilations]
                + [sub(dil, STAT_LANES) for dil in dilations])

    if n_tiles == 1:
        n_chunks, mlp_specs = _streamed_mlp_specs(w_up, w_down, layer)
        y = pl.pallas_call(
            functools.partial(_layer1_single_kernel, dilations),
            grid=(n_chunks,),
            in_specs=tile_specs(lambda s: 0) + weight_specs[:2] + mlp_specs,
            out_specs=pl.BlockSpec((tm, d), lambda s: (0, 0)),
            out_shape=jax.ShapeDtypeStruct((b * t, d), F32),
            scratch_shapes=_attn_mix_scratch(tm) + [
                pltpu.VMEM((tm, d), F32), pltpu.VMEM((tm, d), BF16), pltpu.VMEM((tm, d), F32)],
            compiler_params=pltpu.CompilerParams(
                dimension_semantics=("arbitrary",), vmem_limit_bytes=VMEM_LIMIT,
                allow_input_fusion=_fuse_mlp_weights(11)),
            name="layer1_single_tile",
        )(x2, *os, *stats, *weights)
        return y.reshape(b, t, d)

    scratch = _attn_mix_scratch(tm) + [
        pltpu.VMEM((3, tm, d), F32), pltpu.VMEM((2, tm, d), BF16), pltpu.VMEM((2, tm, d), F32)]
    y = pl.pallas_call(
        functools.partial(_layer1_prompt_kernel, dilations),
        grid=(n_tiles + 2,),
        in_specs=tile_specs(lambda s: jnp.minimum(s, n_tiles - 1)) + weight_specs,
        out_specs=pl.BlockSpec((tm, d), lambda s: (jnp.maximum(s - 2, 0), 0)),
        out_shape=jax.ShapeDtypeStruct((b * t, d), F32),
        scratch_shapes=scratch,
        compiler_params=pltpu.CompilerParams(
            dimension_semantics=("arbitrary",), vmem_limit_bytes=VMEM_LIMIT,
            allow_input_fusion=_fuse_mlp_weights(11)),
        name="layer1_prompt",
    )(x2, *os, *stats, *weights)
    return y.reshape(b, t, d)


PROMPT_TILE = 512
LAYER1_TILE = 512
QKV_TILE = 1024
ATTN_BLOCK_ROWS = 2048


def kernel(x_prompt, x_sample, cache_pool, cache_k, cache_v, norm_gains, kv_norm_gain, w_pool,
           pool_scale, w_q, w_o, w_kv, w_up, w_down):
    depth = norm_gains.shape[0]
    assert depth == 2 and cache_pool.shape[0] == 1 and w_q.shape[0] == 1
    bp, tp, d = x_prompt.shape
    bs, ts, _ = x_sample.shape

    g0, g1 = norm_gains[0], norm_gains[1]
    g1_q = g1[0:1]
    g_kv = kv_norm_gain[None, :]
    wp = w_pool[0].astype(BF16)
    ps = pool_scale[0][None, :]
    wq = w_q[0].reshape(d, N_GROUPS, KV_PER_GROUP, Q_PER_KV, HEAD_DIM).transpose(0, 1, 3, 2, 4)
    wq = wq.reshape(d, Q_WIDTH).astype(BF16)
    wo = w_o[0].reshape(N_GROUPS, KV_PER_GROUP, Q_PER_KV, HEAD_DIM, d).transpose(0, 2, 1, 3, 4)
    wo = wo.reshape(Q_WIDTH, d).astype(BF16)
    wkv = w_kv.astype(BF16)
    wup, wdn = w_up.astype(BF16), w_down.astype(BF16)

    keep = min(KV_WINDOW, tp)
    xp1, utail = _layer0_prompt(x_prompt, g0, wp, ps, wup, wdn, 0, PROMPT_TILE)
    rope_p = _rope_tables(jnp.arange(QKV_TILE), jnp.arange(tp // QKV_TILE) * QKV_TILE)
    qkv_p = _qkv(xp1, g1_q, g_kv, wq, wkv, rope_p, QKV_TILE, keep, ATTN_DILATIONS, True)
    os_p, stats_p = [], []
    for gi, dil in enumerate(ATTN_DILATIONS):
        o, stat = _band_attention(qkv_p[gi], qkv_p[N_GROUPS + gi], qkv_p[2 * N_GROUPS + gi],
                                  ATTN_BLOCK_ROWS)
        os_p.append(o)
        stats_p.append(stat)
    y_prompt = _layer1(xp1, os_p, stats_p, g1, wo, wup, wdn, 1, LAYER1_TILE)
    pool_prompt = utail[:, POOL_HALO - POOL_BUF:][None]
    k_prompt = qkv_p[-2].reshape(bp, N_KV_HEADS, HEAD_DIM, keep).transpose(0, 3, 1, 2)
    v_prompt = qkv_p[-1].reshape(bp, N_KV_HEADS, HEAD_DIM, keep).transpose(0, 3, 1, 2)

    n_tok = ts * bs
    xs_tm = jnp.swapaxes(x_sample, 0, 1)
    buf_tm = jnp.swapaxes(cache_pool[0], 0, 1)
    xs1_tm, us_tm = _layer0_sample(xs_tm, buf_tm, PAST_LEN, g0, wp, ps, wup, wdn, 0)
    xs1 = xs1_tm.reshape(1, n_tok, d)
    rope_s = _rope_tables(jnp.arange(n_tok) // bs, jnp.full((1,), PAST_LEN))
    qkv_s = _qkv(xs1, g1_q, g_kv, wq, wkv, rope_s, n_tok, n_tok, (1,) * N_GROUPS, False)
    q_s = jnp.concatenate([q[0, 0] for q in qkv_s[:N_GROUPS]], axis=-1)
    q16 = q_s.reshape(ts, bs, N_GROUPS, Q_PER_KV, KV_PER_GROUP, HEAD_DIM)
    q16 = q16.transpose(1, 2, 4, 3, 0, 5).reshape(bs, N_KV_HEADS, Q_PER_KV, ts, HEAD_DIM)
    q16 = jnp.pad(q16, ((0, 0), (0, 0), (0, SAMPLE_Q_ROWS // ts - Q_PER_KV), (0, 0), (0, 0)))
    q16 = q16.reshape(bs, N_KV_HEADS, SAMPLE_Q_ROWS, HEAD_DIM)
    k_s = jnp.swapaxes(qkv_s[-2].reshape(ts, bs, KV_WIDTH), 0, 1)
    v_s = jnp.swapaxes(qkv_s[-1].reshape(ts, bs, KV_WIDTH), 0, 1)
    pad_new = ((0, 0), (0, NEW_KEY_ROWS - ts), (0, 0))
    o16, lse16 = _sample_attention(
        q16, jnp.pad(k_s, pad_new), jnp.pad(v_s, pad_new),
        cache_k.transpose(0, 2, 3, 1), cache_v.transpose(0, 2, 3, 1), ts)
    heads_padded = SAMPLE_Q_ROWS // ts
    o_s = o16.reshape(bs, N_GROUPS, KV_PER_GROUP, heads_padded, ts, HEAD_DIM)[:, :, :, :Q_PER_KV]
    o_s = o_s.transpose(4, 0, 1, 3, 2, 5).reshape(1, 1, n_tok, N_GROUPS, GROUP_Q_WIDTH).astype(BF16)
    lse_s = lse16.reshape(bs, N_GROUPS, KV_PER_GROUP, heads_padded, ts)[:, :, :, :Q_PER_KV]
    lse_s = lse_s.transpose(4, 0, 1, 2, 3).reshape(1, 1, n_tok, N_GROUPS, KV_PER_GROUP, Q_PER_KV)
    ones = lambda n: jnp.ones((1, 1, n_tok, N_GROUPS, n), F32)
    max_lane = [_stat_lane(c, 0) + STAT_MAX_OFFSET for c in range(KV_PER_GROUP)]
    assert max_lane[1] < max_lane[0]
    stat_s = jnp.concatenate(
        [ones(max_lane[1]), lse_s[..., 1, :],
         ones(max_lane[0] - max_lane[1] - Q_PER_KV), lse_s[..., 0, :],
         ones(STAT_LANES - max_lane[0] - Q_PER_KV)], axis=-1)
    ys_tm = _layer1(xs1, [o_s[:, :, :, gi] for gi in range(N_GROUPS)],
                    [stat_s[:, :, :, gi] for gi in range(N_GROUPS)], g1, wo, wup, wdn, 1, n_tok)
    y_sample = jnp.swapaxes(ys_tm.reshape(ts, bs, d), 0, 1)
    u_s = jnp.swapaxes(us_tm, 0, 1)
    pool_sample = jnp.concatenate([cache_pool[0], u_s], axis=1)[:, -POOL_BUF:][None]
    k_sample = k_s.reshape(bs, ts, N_KV_HEADS, HEAD_DIM)
    v_sample = v_s.reshape(bs, ts, N_KV_HEADS, HEAD_DIM)

    return (y_prompt, y_sample, pool_prompt, k_prompt, v_prompt, pool_sample, k_sample, v_sample)
```
